```python
import jax
import jax.numpy as jnp
from jax import lax
import numpy as np

D_MODEL = 1024
BATCH = 8
SEQ = 8192
DEPTH = 2

PLE_DIM = 256
POOL_WINDOWS = (2, 4, 8, 16)
N_POOL_GROUPS = len(POOL_WINDOWS)
POOL_GROUP = D_MODEL // 8
POOL_WIDTH = N_POOL_GROUPS * POOL_GROUP
N_HEADS = 8
Q_LORA = D_MODEL // 2
KV_LORA = D_MODEL // 4
QK_NOPE = 128
QK_ROPE = 64
QK_HEAD = QK_NOPE + QK_ROPE
V_HEAD = 128
ATTN_WIDTH = N_HEADS * V_HEAD
D_FF = -(-8 * D_MODEL // (3 * 256)) * 256
ROPE_THETA = 10000.0
EPS = 1e-6
Q_BLOCK = 128

_OFF_Q = POOL_WIDTH
_OFF_KV = _OFF_Q + Q_LORA
_OFF_KR = _OFF_KV + KV_LORA
_OFF_GA = _OFF_KR + QK_ROPE
_OFF_GB = _OFF_GA + D_MODEL
IN_WIDTH = _OFF_GB + D_MODEL

kernel_name = 'hybrid_pool_mla_gated_block'


def rms_norm(x, g):
    xf = x.astype(jnp.float32)
    y = xf * lax.rsqrt(jnp.mean(xf * xf, axis=-1, keepdims=True) + EPS)
    return (y * g.astype(jnp.float32)).astype(x.dtype)


def rope_tables(positions):
    inv_freq = 1.0 / (ROPE_THETA ** (jnp.arange(0, QK_ROPE, 2, dtype=jnp.float32) / QK_ROPE))
    ang = positions.astype(jnp.float32)[..., None] * inv_freq
    return jnp.cos(ang)[:, :, None, :], jnp.sin(ang)[:, :, None, :]


def apply_rope(x, cos, sin):
    xf = x.astype(jnp.float32)
    half = QK_ROPE // 2
    x1, x2 = xf[..., :half], xf[..., half:]
    return jnp.concatenate([x1 * cos - x2 * sin, x2 * cos + x1 * sin], axis=-1).astype(x.dtype)


def multiscale_pool(u, w_pool, pool_scale):
    B, S, _ = u.shape
    uf = u.astype(jnp.float32).reshape(B, S, N_POOL_GROUPS, POOL_GROUP)
    csum = jnp.concatenate([jnp.zeros((B, 1, N_POOL_GROUPS, POOL_GROUP), jnp.float32),
                            jnp.cumsum(uf, axis=1)], axis=1)
    t = jnp.arange(S)
    pooled = []
    for g, w in enumerate(POOL_WINDOWS):
        hi = csum[:, 1:, g]
        lo = jnp.concatenate([jnp.zeros((B, w - 1, POOL_GROUP), jnp.float32),
                              csum[:, :S - w + 1, g]], axis=1)
        cnt = jnp.minimum(t + 1, w).astype(jnp.float32)[None, :, None]
        pooled.append((hi - lo) / cnt - uf[:, :, g])
    pooled = jnp.stack(pooled, axis=2).astype(u.dtype)
    mixed = jnp.einsum('bsgc,gcd->bsgd', pooled, w_pool)
    return mixed.reshape(B, S, POOL_WIDTH) * pool_scale


def causal_block_attention(q, k, v):
    B, S, H, Dq = q.shape
    nb = S // Q_BLOCK
    scale = QK_HEAD ** -0.5
    q_blocks = q.reshape(B, nb, Q_BLOCK, H, Dq).transpose(1, 0, 2, 3, 4)
    key_idx = jnp.arange(S)
    neg = jnp.finfo(jnp.float32).min

    def one_block(args):
        qb, bi = args
        s = jnp.einsum('bqhd,bkhd->bhqk', qb, k, preferred_element_type=jnp.float32) * scale
        q_idx = bi * Q_BLOCK + jnp.arange(Q_BLOCK)
        mask = key_idx[None, :] <= q_idx[:, None]
        s = jnp.where(mask[None, None], s, neg)
        pr = jax.nn.softmax(s, axis=-1)
        return jnp.einsum('bhqk,bkhd->bqhd', pr.astype(v.dtype), v)

    out = lax.map(one_block, (q_blocks, jnp.arange(nb)))
    return out.transpose(1, 0, 2, 3, 4).reshape(B, S, H * v.shape[-1])


def mla(c_q, c_kv, k_rope, q_norm_g, kv_norm_g, w_uq, w_ukv, cos, sin):
    B, S, _ = c_q.shape
    q = jnp.einsum('bsr,rhd->bshd', rms_norm(c_q, q_norm_g), w_uq)
    q = jnp.concatenate([q[..., :QK_NOPE], apply_rope(q[..., QK_NOPE:], cos, sin)], axis=-1)
    kv = jnp.einsum('bsr,rhd->bshd', rms_norm(c_kv, kv_norm_g), w_ukv)
    k_nope, v = kv[..., :QK_NOPE], kv[..., QK_NOPE:]
    k_pe = apply_rope(k_rope[:, :, None, :], cos, sin)
    k = jnp.concatenate([k_nope, jnp.broadcast_to(k_pe, (B, S, N_HEADS, QK_ROPE))], axis=-1)
    return causal_block_attention(q, k, v)


def _fwd_setup_inputs(seed: int = 0) -> dict:
    key = jax.random.key(seed)
    ks = jax.random.split(key, 24)
    f32 = jnp.float32

    def dense(k, shape, fan_in):
        return jax.random.normal(k, shape, f32) * fan_in ** -0.5

    def gain(k, shape, s=0.05):
        return 1.0 + s * jax.random.normal(k, shape, f32)

    L = DEPTH
    return {
        'x': jax.random.normal(ks[0], (BATCH, SEQ, D_MODEL), f32),
        'p': jax.random.normal(ks[1], (L, BATCH, SEQ, PLE_DIM), f32),
        'positions': jnp.tile(jnp.arange(SEQ, dtype=jnp.int32)[None, :], (BATCH, 1)),
        'norm_mix': gain(ks[2], (L, D_MODEL)),
        'w_in': dense(ks[3], (L, D_MODEL, IN_WIDTH), D_MODEL),
        'w_pool': dense(ks[4], (L, N_POOL_GROUPS, POOL_GROUP, POOL_GROUP), POOL_GROUP),
        'pool_scale': gain(ks[5], (L, POOL_WIDTH), 0.1),
        'q_norm': gain(ks[6], (L, Q_LORA)),
        'kv_norm': gain(ks[7], (L, KV_LORA)),
        'w_uq': dense(ks[8], (L, Q_LORA, N_HEADS, QK_HEAD), Q_LORA),
        'w_ukv': dense(ks[9], (L, KV_LORA, N_HEADS, QK_NOPE + V_HEAD), KV_LORA),
        'w_a': dense(ks[10], (L, POOL_WIDTH, D_MODEL), POOL_WIDTH),
        'w_b': dense(ks[11], (L, ATTN_WIDTH, D_MODEL), ATTN_WIDTH),
        'w_o': dense(ks[12], (L, D_MODEL, D_MODEL), D_MODEL),
        'norm_ffn': gain(ks[13], (L, D_MODEL)),
        'w_gate': dense(ks[14], (L, D_MODEL, D_FF), D_MODEL),
        'w_up': dense(ks[15], (L, D_MODEL, D_FF), D_MODEL),
        'w_down': dense(ks[16], (L, D_FF, D_MODEL), D_FF),
        'norm_ple': gain(ks[17], (L, D_MODEL)),
        'w_ple_gate': dense(ks[18], (L, D_MODEL, D_MODEL), D_MODEL),
        'w_ple': dense(ks[19], (L, PLE_DIM, D_MODEL), PLE_DIM),
        'final_norm': gain(ks[20], (D_MODEL,)),
    }


def _fwd_reference(x, p, positions, norm_mix, w_in, w_pool, pool_scale, q_norm, kv_norm, w_uq, w_ukv,
              w_a, w_b, w_o, norm_ffn, w_gate, w_up, w_down, norm_ple, w_ple_gate, w_ple,
              final_norm):
    cos, sin = rope_tables(positions)
    for i in range(DEPTH):
        h = rms_norm(x, norm_mix[i])
        z = h @ w_in[i]
        u = z[..., :_OFF_Q]
        c_q = z[..., _OFF_Q:_OFF_KV]
        c_kv = z[..., _OFF_KV:_OFF_KR]
        k_rope = z[..., _OFF_KR:_OFF_GA]
        gate_a = z[..., _OFF_GA:_OFF_GB]
        gate_b = z[..., _OFF_GB:]
        y_a = multiscale_pool(u, w_pool[i], pool_scale[i]) @ w_a[i]
        y_b = mla(c_q, c_kv, k_rope, q_norm[i], kv_norm[i], w_uq[i], w_ukv[i], cos, sin) @ w_b[i]
        merged = jax.nn.sigmoid(gate_a) * y_a + jax.nn.sigmoid(gate_b) * y_b
        x = x + merged @ w_o[i]
        h = rms_norm(x, norm_ffn[i])
        x = x + (jax.nn.silu(h @ w_gate[i]) * (h @ w_up[i])) @ w_down[i]
        g = jax.nn.sigmoid(rms_norm(x, norm_ple[i]) @ w_ple_gate[i])
        x = x + g * (p[i] @ w_ple[i])
    return rms_norm(x, final_norm)


import jax as _jax
import jax.numpy as _jnp

TWIN_FORMAT = 'train_step'
FWD_PARAMS = ['x', 'p', 'positions', 'norm_mix', 'w_in', 'w_pool', 'pool_scale', 'q_norm', 'kv_norm', 'w_uq', 'w_ukv', 'w_a', 'w_b', 'w_o', 'norm_ffn', 'w_gate', 'w_up', 'w_down', 'norm_ple', 'w_ple_gate', 'w_ple', 'final_norm']
TWIN_WEIGHTS = ['norm_mix', 'w_in', 'w_pool', 'pool_scale', 'q_norm', 'kv_norm', 'w_uq', 'w_ukv', 'w_a', 'w_b', 'w_o', 'norm_ffn', 'w_gate', 'w_up', 'w_down', 'norm_ple', 'w_ple_gate', 'w_ple', 'final_norm']
TWIN_DIFF_INPUT = 'x'
TWIN_INPUTS = ['x', 'p', 'positions', 'norm_mix', 'w_in', 'w_pool', 'pool_scale', 'q_norm', 'kv_norm', 'w_uq', 'w_ukv', 'w_a', 'w_b', 'w_o', 'norm_ffn', 'w_gate', 'w_up', 'w_down', 'norm_ple', 'w_ple_gate', 'w_ple', 'final_norm', 'loss_target', 'm_norm_mix', 'm_w_in', 'm_w_pool', 'm_pool_scale', 'm_q_norm', 'm_kv_norm', 'm_w_uq', 'm_w_ukv', 'm_w_a', 'm_w_b', 'm_w_o', 'm_norm_ffn', 'm_w_gate', 'm_w_up', 'm_w_down', 'm_norm_ple', 'm_w_ple_gate', 'm_w_ple', 'm_final_norm', 'v_norm_mix', 'v_w_in', 'v_w_pool', 'v_pool_scale', 'v_q_norm', 'v_kv_norm', 'v_w_uq', 'v_w_ukv', 'v_w_a', 'v_w_b', 'v_w_o', 'v_norm_ffn', 'v_w_gate', 'v_w_up', 'v_w_down', 'v_norm_ple', 'v_w_ple_gate', 'v_w_ple', 'v_final_norm']
TWIN_OUTPUTS = ['loss', 'grad_x', 'grad_norm_mix', 'grad_w_in', 'grad_w_pool', 'grad_pool_scale', 'grad_q_norm', 'grad_kv_norm', 'grad_w_uq', 'grad_w_ukv', 'grad_w_a', 'grad_w_b', 'grad_w_o', 'grad_norm_ffn', 'grad_w_gate', 'grad_w_up', 'grad_w_down', 'grad_norm_ple', 'grad_w_ple_gate', 'grad_w_ple', 'grad_final_norm', 'delta_norm_mix', 'delta_w_in', 'delta_w_pool', 'delta_pool_scale', 'delta_q_norm', 'delta_kv_norm', 'delta_w_uq', 'delta_w_ukv', 'delta_w_a', 'delta_w_b', 'delta_w_o', 'delta_norm_ffn', 'delta_w_gate', 'delta_w_up', 'delta_w_down', 'delta_norm_ple', 'delta_w_ple_gate', 'delta_w_ple', 'delta_final_norm', 'new_m_norm_mix', 'new_m_w_in', 'new_m_w_pool', 'new_m_pool_scale', 'new_m_q_norm', 'new_m_kv_norm', 'new_m_w_uq', 'new_m_w_ukv', 'new_m_w_a', 'new_m_w_b', 'new_m_w_o', 'new_m_norm_ffn', 'new_m_w_gate', 'new_m_w_up', 'new_m_w_down', 'new_m_norm_ple', 'new_m_w_ple_gate', 'new_m_w_ple', 'new_m_final_norm', 'new_v_norm_mix', 'new_v_w_in', 'new_v_w_pool', 'new_v_pool_scale', 'new_v_q_norm', 'new_v_kv_norm', 'new_v_w_uq', 'new_v_w_ukv', 'new_v_w_a', 'new_v_w_b', 'new_v_w_o', 'new_v_norm_ffn', 'new_v_w_gate', 'new_v_w_up', 'new_v_w_down', 'new_v_norm_ple', 'new_v_w_ple_gate', 'new_v_w_ple', 'new_v_final_norm']
TWIN_LEAF_KINDS = {'loss': 'loss', 'grad_x': 'grad_x', 'grad_norm_mix': 'grad_w', 'grad_w_in': 'grad_w', 'grad_w_pool': 'grad_w', 'grad_pool_scale': 'grad_w', 'grad_q_norm': 'grad_w', 'grad_kv_norm': 'grad_w', 'grad_w_uq': 'grad_w', 'grad_w_ukv': 'grad_w', 'grad_w_a': 'grad_w', 'grad_w_b': 'grad_w', 'grad_w_o': 'grad_w', 'grad_norm_ffn': 'grad_w', 'grad_w_gate': 'grad_w', 'grad_w_up': 'grad_w', 'grad_w_down': 'grad_w', 'grad_norm_ple': 'grad_w', 'grad_w_ple_gate': 'grad_w', 'grad_w_ple': 'grad_w', 'grad_final_norm': 'grad_w', 'delta_norm_mix': 'delta_w', 'delta_w_in': 'delta_w', 'delta_w_pool': 'delta_w', 'delta_pool_scale': 'delta_w', 'delta_q_norm': 'delta_w', 'delta_kv_norm': 'delta_w', 'delta_w_uq': 'delta_w', 'delta_w_ukv': 'delta_w', 'delta_w_a': 'delta_w', 'delta_w_b': 'delta_w', 'delta_w_o': 'delta_w', 'delta_norm_ffn': 'delta_w', 'delta_w_gate': 'delta_w', 'delta_w_up': 'delta_w', 'delta_w_down': 'delta_w', 'delta_norm_ple': 'delta_w', 'delta_w_ple_gate': 'delta_w', 'delta_w_ple': 'delta_w', 'delta_final_norm': 'delta_w', 'new_m_norm_mix': 'new_m', 'new_m_w_in': 'new_m', 'new_m_w_pool': 'new_m', 'new_m_pool_scale': 'new_m', 'new_m_q_norm': 'new_m', 'new_m_kv_norm': 'new_m', 'new_m_w_uq': 'new_m', 'new_m_w_ukv': 'new_m', 'new_m_w_a': 'new_m', 'new_m_w_b': 'new_m', 'new_m_w_o': 'new_m', 'new_m_norm_ffn': 'new_m', 'new_m_w_gate': 'new_m', 'new_m_w_up': 'new_m', 'new_m_w_down': 'new_m', 'new_m_norm_ple': 'new_m', 'new_m_w_ple_gate': 'new_m', 'new_m_w_ple': 'new_m', 'new_m_final_norm': 'new_m', 'new_v_norm_mix': 'new_v', 'new_v_w_in': 'new_v', 'new_v_w_pool': 'new_v', 'new_v_pool_scale': 'new_v', 'new_v_q_norm': 'new_v', 'new_v_kv_norm': 'new_v', 'new_v_w_uq': 'new_v', 'new_v_w_ukv': 'new_v', 'new_v_w_a': 'new_v', 'new_v_w_b': 'new_v', 'new_v_w_o': 'new_v', 'new_v_norm_ffn': 'new_v', 'new_v_w_gate': 'new_v', 'new_v_w_up': 'new_v', 'new_v_w_down': 'new_v', 'new_v_norm_ple': 'new_v', 'new_v_w_ple_gate': 'new_v', 'new_v_w_ple': 'new_v', 'new_v_final_norm': 'new_v'}


def _forward(args):
    return _fwd_reference(*[args[k] for k in FWD_PARAMS])


def _output_shape():
    def fwd():
        inp = _fwd_setup_inputs(0)
        return _fwd_reference(*[inp[k] for k in FWD_PARAMS])
    out = _jax.eval_shape(fwd)
    return out.shape, out.dtype

N_MICROBATCH = 1
ADAM_LR = 0.001
ADAM_B1 = 0.9
ADAM_B2 = 0.999
ADAM_EPS = 1e-08
ADAM_WD = 0.01
ADAM_STEP = 10
PER_EXAMPLE_BATCH_AXIS = {'x': 0, 'p': 1, 'positions': 0, 'loss_target': 0}
SHARED_INPUTS = []
_WEIGHT_DTYPES = {'norm_mix': _jnp.float32, 'w_in': _jnp.float32, 'w_pool': _jnp.float32, 'pool_scale': _jnp.float32, 'q_norm': _jnp.float32, 'kv_norm': _jnp.float32, 'w_uq': _jnp.float32, 'w_ukv': _jnp.float32, 'w_a': _jnp.float32, 'w_b': _jnp.float32, 'w_o': _jnp.float32, 'norm_ffn': _jnp.float32, 'w_gate': _jnp.float32, 'w_up': _jnp.float32, 'w_down': _jnp.float32, 'norm_ple': _jnp.float32, 'w_ple_gate': _jnp.float32, 'w_ple': _jnp.float32, 'final_norm': _jnp.float32}
MOMENT_SCALE = {'norm_mix': 1.163670e-01, 'w_in': 6.389832e-02, 'w_pool': 1.480307e-01, 'pool_scale': 1.567496e-01, 'q_norm': 3.063436e-02, 'kv_norm': 6.521416e-02, 'w_uq': 1.709621e-02, 'w_ukv': 2.190215e-02, 'w_a': 1.054222e-01, 'w_b': 2.573739e-02, 'w_o': 1.048743e-01, 'norm_ffn': 1.563738e-01, 'w_gate': 6.504172e-02, 'w_up': 6.351169e-02, 'w_down': 1.058050e-01, 'norm_ple': 3.708624e-02, 'w_ple_gate': 3.660039e-02, 'w_ple': 9.563529e-02, 'final_norm': 6.420652e+01}


def _to_microbatches(a, axis):
    t = _jnp.moveaxis(a, axis, 0)
    t = t.reshape((N_MICROBATCH, t.shape[0] // N_MICROBATCH) + t.shape[1:])
    return _jnp.moveaxis(t, 1, axis + 1)


def setup_inputs(seed: int = 0) -> dict:
    inp = _fwd_setup_inputs(seed)
    key = _jax.random.fold_in(_jax.random.key(seed), 7919)
    shape, _ = _output_shape()
    out = dict(inp)
    out["loss_target"] = _jax.random.normal(_jax.random.fold_in(key, 0), shape, _jnp.float32)
    for i, name in enumerate(TWIN_WEIGHTS):
        w = inp[name].astype(_jnp.float32)
        if MOMENT_SCALE is None:
            s = _jnp.sqrt(_jnp.mean(_jnp.square(w)) + 1e-30)
        else:
            s = MOMENT_SCALE[name]
        km, kv = _jax.random.split(_jax.random.fold_in(key, i + 1))
        out[name] = w
        out["m_" + name] = s * _jax.random.normal(km, w.shape, _jnp.float32)
        out["v_" + name] = (s * s) * _jax.random.uniform(kv, w.shape, _jnp.float32, 0.5, 1.5)
    if N_MICROBATCH > 1:
        for name, axis in PER_EXAMPLE_BATCH_AXIS.items():
            out[name] = _to_microbatches(out[name], axis)
    return {'x': out['x'], 'p': out['p'], 'positions': out['positions'], 'norm_mix': out['norm_mix'], 'w_in': out['w_in'], 'w_pool': out['w_pool'], 'pool_scale': out['pool_scale'], 'q_norm': out['q_norm'], 'kv_norm': out['kv_norm'], 'w_uq': out['w_uq'], 'w_ukv': out['w_ukv'], 'w_a': out['w_a'], 'w_b': out['w_b'], 'w_o': out['w_o'], 'norm_ffn': out['norm_ffn'], 'w_gate': out['w_gate'], 'w_up': out['w_up'], 'w_down': out['w_down'], 'norm_ple': out['norm_ple'], 'w_ple_gate': out['w_ple_gate'], 'w_ple': out['w_ple'], 'final_norm': out['final_norm'], 'loss_target': out['loss_target'], 'm_norm_mix': out['m_norm_mix'], 'm_w_in': out['m_w_in'], 'm_w_pool': out['m_w_pool'], 'm_pool_scale': out['m_pool_scale'], 'm_q_norm': out['m_q_norm'], 'm_kv_norm': out['m_kv_norm'], 'm_w_uq': out['m_w_uq'], 'm_w_ukv': out['m_w_ukv'], 'm_w_a': out['m_w_a'], 'm_w_b': out['m_w_b'], 'm_w_o': out['m_w_o'], 'm_norm_ffn': out['m_norm_ffn'], 'm_w_gate': out['m_w_gate'], 'm_w_up': out['m_w_up'], 'm_w_down': out['m_w_down'], 'm_norm_ple': out['m_norm_ple'], 'm_w_ple_gate': out['m_w_ple_gate'], 'm_w_ple': out['m_w_ple'], 'm_final_norm': out['m_final_norm'], 'v_norm_mix': out['v_norm_mix'], 'v_w_in': out['v_w_in'], 'v_w_pool': out['v_w_pool'], 'v_pool_scale': out['v_pool_scale'], 'v_q_norm': out['v_q_norm'], 'v_kv_norm': out['v_kv_norm'], 'v_w_uq': out['v_w_uq'], 'v_w_ukv': out['v_w_ukv'], 'v_w_a': out['v_w_a'], 'v_w_b': out['v_w_b'], 'v_w_o': out['v_w_o'], 'v_norm_ffn': out['v_norm_ffn'], 'v_w_gate': out['v_w_gate'], 'v_w_up': out['v_w_up'], 'v_w_down': out['v_w_down'], 'v_norm_ple': out['v_norm_ple'], 'v_w_ple_gate': out['v_w_ple_gate'], 'v_w_ple': out['v_w_ple'], 'v_final_norm': out['v_final_norm']}


def _loss(weights, diff, rest, loss_target):
    with _jax.named_scope("forward"):
        args = {**rest, TWIN_DIFF_INPUT: diff, **{k: w.astype(_WEIGHT_DTYPES[k]) for k, w in weights.items()}}
        y = _forward(args)
    with _jax.named_scope("loss_head"):
        err = _jnp.square(y.astype(_jnp.float32) - loss_target)
        return 0.5 * _jnp.sum(_jnp.mean(err, axis=-1)) if err.ndim else 0.5 * err


def _adamw(w, g, m, v):
    m = ADAM_B1 * m + (1.0 - ADAM_B1) * g
    v = ADAM_B2 * v + (1.0 - ADAM_B2) * _jnp.square(g)
    m_hat = m / (1.0 - ADAM_B1 ** ADAM_STEP)
    v_hat = v / (1.0 - ADAM_B2 ** ADAM_STEP)
    delta = -ADAM_LR * (m_hat / (_jnp.sqrt(v_hat) + ADAM_EPS) + ADAM_WD * w)
    return delta, m, v


def reference(x, p, positions, norm_mix, w_in, w_pool, pool_scale, q_norm, kv_norm, w_uq, w_ukv, w_a, w_b, w_o, norm_ffn, w_gate, w_up, w_down, norm_ple, w_ple_gate, w_ple, final_norm, loss_target, m_norm_mix, m_w_in, m_w_pool, m_pool_scale, m_q_norm, m_kv_norm, m_w_uq, m_w_ukv, m_w_a, m_w_b, m_w_o, m_norm_ffn, m_w_gate, m_w_up, m_w_down, m_norm_ple, m_w_ple_gate, m_w_ple, m_final_norm, v_norm_mix, v_w_in, v_w_pool, v_pool_scale, v_q_norm, v_kv_norm, v_w_uq, v_w_ukv, v_w_a, v_w_b, v_w_o, v_norm_ffn, v_w_gate, v_w_up, v_w_down, v_norm_ple, v_w_ple_gate, v_w_ple, v_final_norm):
    given = dict(x=x, p=p, positions=positions, norm_mix=norm_mix, w_in=w_in, w_pool=w_pool, pool_scale=pool_scale, q_norm=q_norm, kv_norm=kv_norm, w_uq=w_uq, w_ukv=w_ukv, w_a=w_a, w_b=w_b, w_o=w_o, norm_ffn=norm_ffn, w_gate=w_gate, w_up=w_up, w_down=w_down, norm_ple=norm_ple, w_ple_gate=w_ple_gate, w_ple=w_ple, final_norm=final_norm, loss_target=loss_target, m_norm_mix=m_norm_mix, m_w_in=m_w_in, m_w_pool=m_w_pool, m_pool_scale=m_pool_scale, m_q_norm=m_q_norm, m_kv_norm=m_kv_norm, m_w_uq=m_w_uq, m_w_ukv=m_w_ukv, m_w_a=m_w_a, m_w_b=m_w_b, m_w_o=m_w_o, m_norm_ffn=m_norm_ffn, m_w_gate=m_w_gate, m_w_up=m_w_up, m_w_down=m_w_down, m_norm_ple=m_norm_ple, m_w_ple_gate=m_w_ple_gate, m_w_ple=m_w_ple, m_final_norm=m_final_norm, v_norm_mix=v_norm_mix, v_w_in=v_w_in, v_w_pool=v_w_pool, v_pool_scale=v_pool_scale, v_q_norm=v_q_norm, v_kv_norm=v_kv_norm, v_w_uq=v_w_uq, v_w_ukv=v_w_ukv, v_w_a=v_w_a, v_w_b=v_w_b, v_w_o=v_w_o, v_norm_ffn=v_norm_ffn, v_w_gate=v_w_gate, v_w_up=v_w_up, v_w_down=v_w_down, v_norm_ple=v_norm_ple, v_w_ple_gate=v_w_ple_gate, v_w_ple=v_w_ple, v_final_norm=v_final_norm)
    weights = {n: given[n] for n in TWIN_WEIGHTS}
    shared = {n: given[n] for n in SHARED_INPUTS}
    per_example = {n: given[n] for n in ['x', 'p', 'positions']}
    grad_fn = _jax.value_and_grad(_loss, argnums=(0, 1))

    def one_microbatch(ex, loss_target):
        ex = dict(ex)
        diff = ex.pop(TWIN_DIFF_INPUT)
        return grad_fn(weights, diff, {**shared, **ex}, loss_target)

    if N_MICROBATCH == 1:
        loss, (grad_w, grad_x) = one_microbatch(per_example, given["loss_target"])
    else:
        def body(carry, xs):
            loss_sum, grad_sum = carry
            l_k, (gw_k, gx_k) = one_microbatch(xs[0], xs[1])
            with _jax.named_scope("update"):
                return (loss_sum + l_k, _jax.tree.map(_jnp.add, grad_sum, gw_k)), gx_k

        init = (_jnp.zeros((), _jnp.float32), _jax.tree.map(_jnp.zeros_like, weights))
        (loss, grad_w), grad_x = _jax.lax.scan(body, init, (per_example, given["loss_target"]))
    with _jax.named_scope("update"):
        delta_w, new_m, new_v = {}, {}, {}
        for n in TWIN_WEIGHTS:
            delta_w[n], new_m[n], new_v[n] = _adamw(weights[n], grad_w[n], given["m_" + n], given["v_" + n])
    return (loss, grad_x, *[grad_w[n] for n in TWIN_WEIGHTS], *[delta_w[n] for n in TWIN_WEIGHTS],
            *[new_m[n] for n in TWIN_WEIGHTS], *[new_v[n] for n in TWIN_WEIGHTS])
```

```python
import functools

import numpy as np
import jax
import jax.numpy as jnp
from jax import lax
from jax.experimental import pallas as pl
from jax.experimental.pallas import tpu as pltpu

F32 = jnp.float32
BF16 = jnp.bfloat16

D_MODEL = 1024
DEPTH = 2
PLE_DIM = 256
POOL_WINDOWS = (2, 4, 8, 16)
POOL_GROUP = 128
POOL_WIDTH = 512
N_HEADS = 8
Q_LORA = 512
KV_LORA = 256
QK_NOPE = 128
QK_ROPE = 64
QK_HEAD = 192
V_HEAD = 128
HEAD_PAD = 256
D_FF = 2816
ROPE_THETA = 10000.0
EPS = 1e-6
IN_WIDTH = 3392
ATTN_SCALE = QK_HEAD ** -0.5

OFF_GA, OFF_GB, OFF_U, OFF_CQ, OFF_CKV, OFF_KR = 0, 1024, 2048, 2560, 3072, 3328
IN_PAD = 3456

ADAM_LR = 0.001
ADAM_B1 = 0.9
ADAM_B2 = 0.999
ADAM_EPS = 1e-08
ADAM_WD = 0.01
ADAM_STEP = 10

N_DEV = 8
LANES = 128
CHUNK = 128
VMEM_LIMIT = 56 * 1024 * 1024

SHARDED = (("w_in", 2), ("w_uq", 1), ("w_ukv", 1), ("w_a", 2), ("w_b", 1), ("w_o", 1),
           ("w_gate", 2), ("w_up", 2), ("w_down", 1), ("w_ple_gate", 1), ("w_ple", 2))
REPLICATED = ("norm_mix", "w_pool", "pool_scale", "q_norm", "kv_norm", "norm_ffn", "norm_ple",
              "final_norm")
WEIGHTS = ("norm_mix", "w_in", "w_pool", "pool_scale", "q_norm", "kv_norm", "w_uq", "w_ukv",
           "w_a", "w_b", "w_o", "norm_ffn", "w_gate", "w_up", "w_down", "norm_ple",
           "w_ple_gate", "w_ple", "final_norm")
FLAT_COLS = 1024
FLAT_ROW_BLOCK = 192


def _params(*sem):
    return pltpu.CompilerParams(dimension_semantics=sem, vmem_limit_bytes=VMEM_LIMIT)


def _dot(a, b):
    return jnp.dot(a, b, preferred_element_type=F32)


def _dot_nt(a, b):
    return lax.dot_general(a, b, (((1,), (1,)), ((), ())), preferred_element_type=F32)


def _dot_tn(a, b):
    return lax.dot_general(a, b, (((0,), (0,)), ((), ())), preferred_element_type=F32)


def _rms_fwd(x, g):
    r = lax.rsqrt(jnp.mean(x * x, axis=-1, keepdims=True) + EPS)
    return x * r * g


def _rms_bwd(x, g, dy):
    r = lax.rsqrt(jnp.mean(x * x, axis=-1, keepdims=True) + EPS)
    xr = x * r
    gy = dy * g
    dx = r * (gy - xr * jnp.mean(gy * xr, axis=-1, keepdims=True))
    return dx, dy * xr


def _sigmoid(x):
    return 1.0 / (1.0 + jnp.exp(-x))


def _accum_rows(ref, rows):
    ref[...] += jnp.sum(rows, axis=0, keepdims=True)


def _band(band, x):
    h1 = x.astype(BF16)
    r1 = x - h1.astype(F32)
    h2 = r1.astype(BF16)
    h3 = (r1 - h2.astype(F32)).astype(BF16)
    return _dot(band, h1) + _dot(band, h2) + _dot(band, h3)


def _rope(x, c, s1, s2, sign):
    return x * c + sign * (pltpu.roll(x, 96, 1) * s1 + pltpu.roll(x, 32, 1) * s2)


def _full(shape):
    n = len(shape)
    return pl.BlockSpec(shape, lambda *_: (0,) * n)


def matmul_tn(a, b, *, tn, tk, name, tm=1024):
    T, M = a.shape
    N = b.shape[1]
    tm, tn, tk = min(tm, M), min(tn, N), min(tk, T)

    def body(a_ref, b_ref, o_ref):
        @pl.when(pl.program_id(2) == 0)
        def _():
            o_ref[...] = jnp.zeros_like(o_ref)

        o_ref[...] += _dot_tn(a_ref[...].astype(BF16), b_ref[...].astype(BF16))

    return pl.pallas_call(
        body, name=name, grid=(M // tm, N // tn, T // tk),
        in_specs=[pl.BlockSpec((tk, tm), lambda i, j, k: (k, i)),
                  pl.BlockSpec((tk, tn), lambda i, j, k: (k, j))],
        out_specs=pl.BlockSpec((tm, tn), lambda i, j, k: (i, j)),
        out_shape=jax.ShapeDtypeStruct((M, N), F32),
        compiler_params=_params("parallel", "parallel", "arbitrary"),
    )(a, b)


def rms_matmul(x, g, w, *, tm, tn, name):
    T, D = x.shape
    N = w.shape[1]
    tm = min(tm, T)

    def body(x_ref, g_ref, w_ref, o_ref, h_ref):
        @pl.when(pl.program_id(1) == 0)
        def _():
            h_ref[...] = _rms_fwd(x_ref[...], g_ref[...]).astype(BF16)

        o_ref[...] = _dot(h_ref[...], w_ref[...])

    return pl.pallas_call(
        body, name=name, grid=(T // tm, N // tn),
        in_specs=[pl.BlockSpec((tm, D), lambda i, j: (i, 0)), _full((1, D)),
                  pl.BlockSpec((D, tn), lambda i, j: (0, j))],
        out_specs=pl.BlockSpec((tm, tn), lambda i, j: (i, j)),
        out_shape=jax.ShapeDtypeStruct((T, N), F32),
        scratch_shapes=[pltpu.VMEM((tm, D), BF16)],
        compiler_params=_params("parallel", "arbitrary"),
    )(x, g, w)


def _band_matrices():
    s = np.arange(CHUNK)[:, None]
    t = np.arange(CHUNK)[None, :]
    low = np.stack([((s - t >= 0) & (s - t < w)) for w in POOL_WINDOWS]).astype(np.float32)
    up = np.stack([(t > s + CHUNK - w) for w in POOL_WINDOWS]).astype(np.float32)
    return low, up


def _window_count(row0, g):
    t = row0 + lax.broadcasted_iota(jnp.int32, (CHUNK, 1), 0)
    return jnp.minimum(t + 1, POOL_WINDOWS[g]).astype(F32)


def pool_fwd(z, low, up, w_pool, pool_scale, w_a, *, tm):
    T = z.shape[0]
    tm = min(tm, T)
    nch = tm // CHUNK
    ublk = OFF_U // POOL_WIDTH

    def body(u_ref, halo_ref, low_ref, up_ref, wp_ref, sc_ref, wa_ref, pooled_ref, ms_ref, ya_ref):
        i = pl.program_id(0)
        for c in range(nch):
            rows = slice(c * CHUNK, (c + 1) * CHUNK)
            for g in range(4):
                cols = slice(g * POOL_GROUP, (g + 1) * POOL_GROUP)
                cur = u_ref[rows, cols]
                if c == 0:
                    prev = jnp.where(i > 0, halo_ref[:, cols], 0.0)
                else:
                    prev = u_ref[(c - 1) * CHUNK:c * CHUNK, cols]
                s = _band(low_ref[g], cur) + _band(up_ref[g], prev)
                pooled = (s / _window_count(i * tm + c * CHUNK, g) - cur).astype(BF16)
                pooled_ref[rows, cols] = pooled
                ms_ref[rows, cols] = (_dot(pooled, wp_ref[g]) * sc_ref[:, cols]).astype(BF16)
        ya_ref[...] = _dot(ms_ref[...], wa_ref[...])

    return pl.pallas_call(
        body, name="pool_fwd", grid=(T // tm,),
        in_specs=[pl.BlockSpec((tm, POOL_WIDTH), lambda i: (i, ublk)),
                  pl.BlockSpec((CHUNK, POOL_WIDTH), lambda i: (jnp.maximum(i * nch - 1, 0), ublk)),
                  _full((4, CHUNK, CHUNK)), _full((4, CHUNK, CHUNK)),
                  _full((4, POOL_GROUP, POOL_GROUP)), _full((1, POOL_WIDTH)),
                  _full((POOL_WIDTH, D_MODEL))],
        out_specs=[pl.BlockSpec((tm, POOL_WIDTH), lambda i: (i, 0)),
                   pl.BlockSpec((tm, POOL_WIDTH), lambda i: (i, 0)),
                   pl.BlockSpec((tm, D_MODEL), lambda i: (i, 0))],
        out_shape=[jax.ShapeDtypeStruct((T, POOL_WIDTH), BF16),
                   jax.ShapeDtypeStruct((T, POOL_WIDTH), BF16),
                   jax.ShapeDtypeStruct((T, D_MODEL), F32)],
        compiler_params=_params("parallel"),
    )(z, z, low, up, w_pool, pool_scale, w_a)


def pool_bwd_a(dya, pooled, w_a, w_pool, pool_scale, *, tm):
    T = dya.shape[0]
    tm = min(tm, T)
    nch = tm // CHUNK

    def body(dya_ref, pooled_ref, wa_ref, wp_ref, sc_ref, dpc_ref, dsc_ref, dwp_ref):
        i = pl.program_id(0)

        @pl.when(i == 0)
        def _():
            dsc_ref[...] = jnp.zeros_like(dsc_ref)
            dwp_ref[...] = jnp.zeros_like(dwp_ref)

        dms = _dot_nt(dya_ref[...], wa_ref[...])
        for g in range(4):
            cols = slice(g * POOL_GROUP, (g + 1) * POOL_GROUP)
            pg = pooled_ref[:, cols]
            dmg = dms[:, cols]
            mixed = _dot(pg, wp_ref[g])
            dsc_ref[:, cols] += jnp.sum(dmg * mixed, axis=0, keepdims=True)
            dmixed = (dmg * sc_ref[:, cols]).astype(BF16)
            dwp_ref[g] += _dot_tn(pg, dmixed)
            dpooled = _dot_nt(dmixed, wp_ref[g])
            for c in range(nch):
                rows = slice(c * CHUNK, (c + 1) * CHUNK)
                dpc_ref[rows, cols] = dpooled[rows] / _window_count(i * tm + c * CHUNK, g)

    return pl.pallas_call(
        body, name="pool_bwd_a", grid=(T // tm,),
        in_specs=[pl.BlockSpec((tm, D_MODEL), lambda i: (i, 0)),
                  pl.BlockSpec((tm, POOL_WIDTH), lambda i: (i, 0)),
                  _full((POOL_WIDTH, D_MODEL)), _full((4, POOL_GROUP, POOL_GROUP)),
                  _full((1, POOL_WIDTH))],
        out_specs=[pl.BlockSpec((tm, POOL_WIDTH), lambda i: (i, 0)), _full((1, POOL_WIDTH)),
                   _full((4, POOL_GROUP, POOL_GROUP))],
        out_shape=[jax.ShapeDtypeStruct((T, POOL_WIDTH), F32),
                   jax.ShapeDtypeStruct((1, POOL_WIDTH), F32),
                   jax.ShapeDtypeStruct((4, POOL_GROUP, POOL_GROUP), F32)],
        compiler_params=_params("arbitrary"),
    )(dya, pooled, w_a, w_pool, pool_scale)


def pool_bwd_b(dpc, low_t, up_t, *, tm):
    T = dpc.shape[0]
    tm = min(tm, T)
    nch = tm // CHUNK
    last_chunk = T // CHUNK - 1

    def body(d_ref, halo_ref, low_ref, up_ref, du_ref):
        i = pl.program_id(0)
        for c in range(nch):
            rows = slice(c * CHUNK, (c + 1) * CHUNK)
            for g in range(4):
                cols = slice(g * POOL_GROUP, (g + 1) * POOL_GROUP)
                cur = d_ref[rows, cols]
                if c == nch - 1:
                    nxt = jnp.where(i < pl.num_programs(0) - 1, halo_ref[:, cols], 0.0)
                else:
                    nxt = d_ref[(c + 1) * CHUNK:(c + 2) * CHUNK, cols]
                s = _band(low_ref[g], cur) + _band(up_ref[g], nxt)
                du_ref[rows, cols] = (s - cur * _window_count(i * tm + c * CHUNK, g)).astype(BF16)

    return pl.pallas_call(
        body, name="pool_bwd_b", grid=(T // tm,),
        in_specs=[pl.BlockSpec((tm, POOL_WIDTH), lambda i: (i, 0)),
                  pl.BlockSpec((CHUNK, POOL_WIDTH),
                               lambda i: (jnp.minimum((i + 1) * nch, last_chunk), 0)),
                  _full((4, CHUNK, CHUNK)), _full((4, CHUNK, CHUNK))],
        out_specs=pl.BlockSpec((tm, POOL_WIDTH), lambda i: (i, 0)),
        out_shape=jax.ShapeDtypeStruct((T, POOL_WIDTH), BF16),
        compiler_params=_params("parallel"),
    )(dpc, dpc, low_t, up_t)


def mla_prep(z, cos_t, s1_t, s2_t, q_norm, kv_norm, w_uq, w_ukv, *, tm):
    T = z.shape[0]
    tm = min(tm, T)

    def body(cq_ref, ckv_ref, kr_ref, c_ref, s1_ref, s2_ref, qg_ref, kvg_ref, wuq_ref, wukv_ref,
             q_ref, k_ref, v_ref):
        c, s1, s2 = c_ref[...], s1_ref[...], s2_ref[...]
        qn = _rms_fwd(cq_ref[...], qg_ref[...]).astype(BF16)
        q = _dot(qn, wuq_ref[...])
        kvn = _rms_fwd(ckv_ref[...], kvg_ref[...]).astype(BF16)
        kv = _dot(kvn, wukv_ref[...])
        kpe = _rope(kr_ref[...], c, s1, s2, 1.0).astype(BF16)
        for h in range(N_HEADS):
            o = h * HEAD_PAD
            q_ref[h, :, 0:QK_NOPE] = (q[:, o:o + QK_NOPE] * ATTN_SCALE).astype(BF16)
            q_ref[h, :, QK_NOPE:HEAD_PAD] = (
                _rope(q[:, o + QK_NOPE:o + HEAD_PAD], c, s1, s2, 1.0) * ATTN_SCALE).astype(BF16)
            k_ref[h, :, 0:QK_NOPE] = kv[:, o:o + QK_NOPE].astype(BF16)
            k_ref[h, :, QK_NOPE:HEAD_PAD] = kpe
            v_ref[h] = kv[:, o + QK_NOPE:o + HEAD_PAD].astype(BF16)

    tok = lambda w: pl.BlockSpec((tm, w), lambda i: (i, 0))
    return pl.pallas_call(
        body, name="mla_prep", grid=(T // tm,),
        in_specs=[pl.BlockSpec((tm, Q_LORA), lambda i: (i, OFF_CQ // Q_LORA)),
                  pl.BlockSpec((tm, KV_LORA), lambda i: (i, OFF_CKV // KV_LORA)),
                  pl.BlockSpec((tm, LANES), lambda i: (i, OFF_KR // LANES)),
                  tok(LANES), tok(LANES), tok(LANES),
                  _full((1, Q_LORA)), _full((1, KV_LORA)),
                  _full((Q_LORA, N_HEADS * HEAD_PAD)), _full((KV_LORA, N_HEADS * HEAD_PAD))],
        out_specs=[pl.BlockSpec((N_HEADS, tm, HEAD_PAD), lambda i: (0, i, 0)),
                   pl.BlockSpec((N_HEADS, tm, HEAD_PAD), lambda i: (0, i, 0)),
                   pl.BlockSpec((N_HEADS, tm, V_HEAD), lambda i: (0, i, 0))],
        out_shape=[jax.ShapeDtypeStruct((N_HEADS, T, HEAD_PAD), BF16),
                   jax.ShapeDtypeStruct((N_HEADS, T, HEAD_PAD), BF16),
                   jax.ShapeDtypeStruct((N_HEADS, T, V_HEAD), BF16)],
        compiler_params=_params("parallel"),
    )(z, z, z, cos_t, s1_t, s2_t, q_norm, kv_norm, w_uq, w_ukv)


def _row_vector(col, n):
    return jnp.transpose(jnp.broadcast_to(col, (n, LANES)))[0:1, :]


def flash_fwd(q, k, v, *, tq):
    H, T, _ = q.shape
    tq = min(tq, T)
    nq = T // tq
    neg = -1e30

    def body(q_ref, k_ref, v_ref, o_ref, lse_ref):
        qi = pl.program_id(1)
        qb = q_ref[0]

        def tile(j, carry, masked):
            m, l, acc = carry
            rows = pl.ds(pl.multiple_of(j * tq, tq), tq)
            s = _dot_nt(qb, k_ref[0, rows, :])
            if masked:
                r = lax.broadcasted_iota(jnp.int32, (tq, tq), 0)
                c = lax.broadcasted_iota(jnp.int32, (tq, tq), 1)
                s = jnp.where(c <= r, s, neg)
            m_new = jnp.maximum(m, jnp.max(s, axis=-1, keepdims=True))
            alpha = jnp.exp(m - m_new)
            p = jnp.exp(s - m_new)
            l = alpha * l + jnp.sum(p, axis=-1, keepdims=True)
            acc = alpha * acc + _dot(p.astype(BF16), v_ref[0, rows, :])
            return m_new, l, acc

        init = (jnp.full((tq, 1), neg, F32), jnp.zeros((tq, 1), F32), jnp.zeros((tq, V_HEAD), F32))
        carry = lax.fori_loop(0, qi, lambda j, cr: tile(j, cr, False), init)
        m, l, acc = tile(qi, carry, True)
        o_ref[...] = (acc / l).astype(BF16)
        lse_ref[0, 0] = _row_vector(m + jnp.log(l), tq)

    return pl.pallas_call(
        body, name="flash_fwd", grid=(H, nq),
        in_specs=[pl.BlockSpec((1, tq, HEAD_PAD), lambda h, i: (h, i, 0)),
                  pl.BlockSpec((1, T, HEAD_PAD), lambda h, i: (h, 0, 0)),
                  pl.BlockSpec((1, T, V_HEAD), lambda h, i: (h, 0, 0))],
        out_specs=[pl.BlockSpec((tq, V_HEAD), lambda h, i: (i, h)),
                   pl.BlockSpec((1, 1, 1, tq), lambda h, i: (h, i, 0, 0))],
        out_shape=[jax.ShapeDtypeStruct((T, H * V_HEAD), BF16),
                   jax.ShapeDtypeStruct((H, nq, 1, tq), F32)],
        compiler_params=_params("parallel", "arbitrary"),
    )(q, k, v)


def attn_delta(do, o, *, tq):
    T = do.shape[0]
    tq = min(tq, T)
    nq = T // tq

    def body(do_ref, o_ref, d_ref):
        prod = do_ref[...].astype(F32) * o_ref[...].astype(F32)
        d_ref[0, 0] = jnp.sum(jnp.transpose(prod), axis=0, keepdims=True)

    return pl.pallas_call(
        body, name="attn_delta", grid=(N_HEADS, nq),
        in_specs=[pl.BlockSpec((tq, V_HEAD), lambda h, i: (i, h)),
                  pl.BlockSpec((tq, V_HEAD), lambda h, i: (i, h))],
        out_specs=pl.BlockSpec((1, 1, 1, tq), lambda h, i: (h, i, 0, 0)),
        out_shape=jax.ShapeDtypeStruct((N_HEADS, nq, 1, tq), F32),
        compiler_params=_params("parallel", "parallel"),
    )(do, o)


def flash_bwd(q, k, v, do, lse, delta, *, tq):
    H, T, _ = q.shape
    tq = min(tq, T)
    nq = T // tq
    neg = -1e30

    def body(q_ref, do_ref, lse_ref, dl_ref, k_ref, v_ref, dq_ref, dk_ref, dv_ref, dq_acc):
        j = pl.program_id(1)

        @pl.when(j == 0)
        def _():
            dq_acc[...] = jnp.zeros_like(dq_acc)

        kb = k_ref[0]
        vb = v_ref[0]

        def tile(i, carry, masked):
            dk, dv = carry
            rows = pl.ds(pl.multiple_of(i * tq, tq), tq)
            qb = q_ref[0, rows, :]
            dob = do_ref[rows, :]
            st = _dot_nt(kb, qb)
            if masked:
                r = lax.broadcasted_iota(jnp.int32, (tq, tq), 0)
                c = lax.broadcasted_iota(jnp.int32, (tq, tq), 1)
                st = jnp.where(r <= c, st, neg)
            pt = jnp.exp(st - lse_ref[0, i])
            dv = dv + _dot(pt.astype(BF16), dob)
            dpt = _dot_nt(vb, dob)
            dst = (pt * (dpt - dl_ref[0, i])).astype(BF16)
            dk = dk + _dot(dst, qb)
            dq_acc[rows, :] += _dot_tn(dst, kb)
            return dk, dv

        carry = tile(j, (jnp.zeros((tq, HEAD_PAD), F32), jnp.zeros((tq, V_HEAD), F32)), True)
        dk, dv = lax.fori_loop(j + 1, nq, lambda i, cr: tile(i, cr, False), carry)
        dk_ref[0] = dk.astype(BF16)
        dv_ref[0] = dv.astype(BF16)

        @pl.when(j == nq - 1)
        def _():
            dq_ref[0] = dq_acc[...].astype(BF16)

    return pl.pallas_call(
        body, name="flash_bwd", grid=(H, nq),
        in_specs=[pl.BlockSpec((1, T, HEAD_PAD), lambda h, j: (h, 0, 0)),
                  pl.BlockSpec((T, V_HEAD), lambda h, j: (0, h)),
                  pl.BlockSpec((1, nq, 1, tq), lambda h, j: (h, 0, 0, 0)),
                  pl.BlockSpec((1, nq, 1, tq), lambda h, j: (h, 0, 0, 0)),
                  pl.BlockSpec((1, tq, HEAD_PAD), lambda h, j: (h, j, 0)),
                  pl.BlockSpec((1, tq, V_HEAD), lambda h, j: (h, j, 0))],
        out_specs=[pl.BlockSpec((1, T, HEAD_PAD), lambda h, j: (h, 0, 0)),
                   pl.BlockSpec((1, tq, HEAD_PAD), lambda h, j: (h, j, 0)),
                   pl.BlockSpec((1, tq, V_HEAD), lambda h, j: (h, j, 0))],
        out_shape=[jax.ShapeDtypeStruct((H, T, HEAD_PAD), BF16),
                   jax.ShapeDtypeStruct((H, T, HEAD_PAD), BF16),
                   jax.ShapeDtypeStruct((H, T, V_HEAD), BF16)],
        scratch_shapes=[pltpu.VMEM((T, HEAD_PAD), F32)],
        compiler_params=_params("parallel", "arbitrary"),
    )(q, do, lse, delta, k, v)


def mla_bwd(dq, dk, dv, z, cos_t, s1_t, s2_t, q_norm, kv_norm, w_uq, w_ukv, *, tm):
    T = z.shape[0]
    tm = min(tm, T)
    HW = N_HEADS * HEAD_PAD

    def body(dq_ref, dk_ref, dv_ref, cq_ref, ckv_ref, c_ref, s1_ref, s2_ref, qg_ref, kvg_ref,
             wuq_ref, wukv_ref, dcq_ref, dckv_ref, dkr_ref, dqf_ref, dkvf_ref, qn_ref, kvn_ref,
             dqg_ref, dkvg_ref):
        @pl.when(pl.program_id(0) == 0)
        def _():
            dqg_ref[...] = jnp.zeros_like(dqg_ref)
            dkvg_ref[...] = jnp.zeros_like(dkvg_ref)

        c, s1, s2 = c_ref[...], s1_ref[...], s2_ref[...]
        dkpe = jnp.zeros((tm, LANES), F32)
        for h in range(N_HEADS):
            o = h * HEAD_PAD
            dqf_ref[:, o:o + QK_NOPE] = (
                dq_ref[h, :, 0:QK_NOPE].astype(F32) * ATTN_SCALE).astype(BF16)
            dqf_ref[:, o + QK_NOPE:o + HEAD_PAD] = (
                _rope(dq_ref[h, :, QK_NOPE:HEAD_PAD].astype(F32), c, s1, s2, -1.0)
                * ATTN_SCALE).astype(BF16)
            dkvf_ref[:, o:o + QK_NOPE] = dk_ref[h, :, 0:QK_NOPE]
            dkvf_ref[:, o + QK_NOPE:o + HEAD_PAD] = dv_ref[h]
            dkpe = dkpe + dk_ref[h, :, QK_NOPE:HEAD_PAD].astype(F32)
        dkr_ref[...] = _rope(dkpe, c, s1, s2, -1.0).astype(BF16)

        cq = cq_ref[...]
        qn_ref[...] = _rms_fwd(cq, qg_ref[...]).astype(BF16)
        dcq, dgrows = _rms_bwd(cq, qg_ref[...], _dot_nt(dqf_ref[...], wuq_ref[...]))
        dcq_ref[...] = dcq.astype(BF16)
        _accum_rows(dqg_ref, dgrows)

        ckv = ckv_ref[...]
        kvn_ref[...] = _rms_fwd(ckv, kvg_ref[...]).astype(BF16)
        dckv, dgrows = _rms_bwd(ckv, kvg_ref[...], _dot_nt(dkvf_ref[...], wukv_ref[...]))
        dckv_ref[...] = dckv.astype(BF16)
        _accum_rows(dkvg_ref, dgrows)

    tok = lambda w: pl.BlockSpec((tm, w), lambda i: (i, 0))
    head = lambda w: pl.BlockSpec((N_HEADS, tm, w), lambda i: (0, i, 0))
    return pl.pallas_call(
        body, name="mla_bwd", grid=(T // tm,),
        in_specs=[head(HEAD_PAD), head(HEAD_PAD), head(V_HEAD),
                  pl.BlockSpec((tm, Q_LORA), lambda i: (i, OFF_CQ // Q_LORA)),
                  pl.BlockSpec((tm, KV_LORA), lambda i: (i, OFF_CKV // KV_LORA)),
                  tok(LANES), tok(LANES), tok(LANES),
                  _full((1, Q_LORA)), _full((1, KV_LORA)),
                  _full((Q_LORA, HW)), _full((KV_LORA, HW))],
        out_specs=[tok(Q_LORA), tok(KV_LORA), tok(LANES), tok(HW), tok(HW), tok(Q_LORA),
                   tok(KV_LORA), _full((1, Q_LORA)), _full((1, KV_LORA))],
        out_shape=[jax.ShapeDtypeStruct((T, Q_LORA), BF16),
                   jax.ShapeDtypeStruct((T, KV_LORA), BF16),
                   jax.ShapeDtypeStruct((T, LANES), BF16),
                   jax.ShapeDtypeStruct((T, HW), BF16),
                   jax.ShapeDtypeStruct((T, HW), BF16),
                   jax.ShapeDtypeStruct((T, Q_LORA), BF16),
                   jax.ShapeDtypeStruct((T, KV_LORA), BF16),
                   jax.ShapeDtypeStruct((1, Q_LORA), F32),
                   jax.ShapeDtypeStruct((1, KV_LORA), F32)],
        compiler_params=_params("arbitrary"),
    )(dq, dk, dv, z, z, cos_t, s1_t, s2_t, q_norm, kv_norm, w_uq, w_ukv)


def merge_fwd(o, ya, z, x, w_b, w_o, *, tm):
    T = x.shape[0]
    tm = min(tm, T)

    def body(o_ref, ya_ref, ga_ref, gb_ref, x_ref, wb_ref, wo_ref, yb_ref, mg_ref, x1_ref):
        yb = _dot(o_ref[...], wb_ref[...])
        yb_ref[...] = yb
        merged = (_sigmoid(ga_ref[...]) * ya_ref[...] + _sigmoid(gb_ref[...]) * yb).astype(BF16)
        mg_ref[...] = merged
        x1_ref[...] = x_ref[...] + _dot(merged, wo_ref[...])

    tok = pl.BlockSpec((tm, D_MODEL), lambda i: (i, 0))
    return pl.pallas_call(
        body, name="merge_fwd", grid=(T // tm,),
        in_specs=[tok, tok, pl.BlockSpec((tm, D_MODEL), lambda i: (i, OFF_GA // D_MODEL)),
                  pl.BlockSpec((tm, D_MODEL), lambda i: (i, OFF_GB // D_MODEL)), tok,
                  _full((D_MODEL, D_MODEL)), _full((D_MODEL, D_MODEL))],
        out_specs=[tok, tok, tok],
        out_shape=[jax.ShapeDtypeStruct((T, D_MODEL), F32),
                   jax.ShapeDtypeStruct((T, D_MODEL), BF16),
                   jax.ShapeDtypeStruct((T, D_MODEL), F32)],
        compiler_params=_params("parallel"),
    )(o, ya, z, z, x, w_b, w_o)


def merge_bwd(dx1, ya, yb, z, w_o, w_b, *, tm):
    T = dx1.shape[0]
    tm = min(tm, T)

    def body(dx_ref, ya_ref, yb_ref, ga_ref, gb_ref, wo_ref, wb_ref, dya_ref, dyb_ref, do_ref,
             dga_ref, dgb_ref):
        dm = _dot_nt(dx_ref[...].astype(BF16), wo_ref[...])
        sa = _sigmoid(ga_ref[...])
        sb = _sigmoid(gb_ref[...])
        dya_ref[...] = (dm * sa).astype(BF16)
        dyb = (dm * sb).astype(BF16)
        dyb_ref[...] = dyb
        dga_ref[...] = (dm * ya_ref[...] * sa * (1.0 - sa)).astype(BF16)
        dgb_ref[...] = (dm * yb_ref[...] * sb * (1.0 - sb)).astype(BF16)
        do_ref[...] = _dot_nt(dyb, wb_ref[...]).astype(BF16)

    tok = pl.BlockSpec((tm, D_MODEL), lambda i: (i, 0))
    return pl.pallas_call(
        body, name="merge_bwd", grid=(T // tm,),
        in_specs=[tok, tok, tok, pl.BlockSpec((tm, D_MODEL), lambda i: (i, OFF_GA // D_MODEL)),
                  pl.BlockSpec((tm, D_MODEL), lambda i: (i, OFF_GB // D_MODEL)),
                  _full((D_MODEL, D_MODEL)), _full((D_MODEL, D_MODEL))],
        out_specs=[tok] * 5,
        out_shape=[jax.ShapeDtypeStruct((T, D_MODEL), BF16)] * 5,
        compiler_params=_params("parallel"),
    )(dx1, ya, yb, z, z, w_o, w_b)


def in_bwd(dx1, x, g, segs, w_in, *, tm):
    T = x.shape[0]
    tm = min(tm, T)
    widths = [s.shape[1] for s, _ in segs]
    offs = [o for _, o in segs]
    n = len(segs)

    def body(*refs):
        dx1_ref, x_ref, g_ref = refs[:3]
        seg_refs = refs[3:3 + n]
        w_ref = refs[3 + n]
        dx_ref, h_ref, dg_ref = refs[4 + n:]

        @pl.when(pl.program_id(0) == 0)
        def _():
            dg_ref[...] = jnp.zeros_like(dg_ref)

        dh = jnp.zeros((tm, D_MODEL), F32)
        for s_ref, off, w in zip(seg_refs, offs, widths):
            dh = dh + _dot_nt(s_ref[...], w_ref[:, off:off + w])
        xv = x_ref[...]
        h_ref[...] = _rms_fwd(xv, g_ref[...]).astype(BF16)
        dxn, dgrows = _rms_bwd(xv, g_ref[...], dh)
        dx_ref[...] = dx1_ref[...] + dxn
        _accum_rows(dg_ref, dgrows)

    tok = pl.BlockSpec((tm, D_MODEL), lambda i: (i, 0))
    return pl.pallas_call(
        body, name="in_bwd", grid=(T // tm,),
        in_specs=[tok, tok, _full((1, D_MODEL))]
        + [pl.BlockSpec((tm, w), lambda i: (i, 0)) for w in widths] + [_full((D_MODEL, IN_PAD))],
        out_specs=[tok, tok, _full((1, D_MODEL))],
        out_shape=[jax.ShapeDtypeStruct((T, D_MODEL), F32),
                   jax.ShapeDtypeStruct((T, D_MODEL), BF16),
                   jax.ShapeDtypeStruct((1, D_MODEL), F32)],
        compiler_params=_params("arbitrary"),
    )(dx1, x, g, *[s for s, _ in segs], w_in)


def ffn_fwd(x1, g, w_gate, w_up, w_down, *, tm, tf):
    T = x1.shape[0]
    tm = min(tm, T)
    nf = D_FF // tf

    def body(x_ref, g_ref, wg_ref, wu_ref, wd_ref, gp_ref, up_ref, x2_ref, h_ref, acc_ref):
        f = pl.program_id(1)

        @pl.when(f == 0)
        def _():
            h_ref[...] = _rms_fwd(x_ref[...], g_ref[...]).astype(BF16)
            acc_ref[...] = jnp.zeros_like(acc_ref)

        gp = _dot(h_ref[...], wg_ref[...])
        up = _dot(h_ref[...], wu_ref[...])
        gp_ref[...] = gp
        up_ref[...] = up
        act = (gp * _sigmoid(gp) * up).astype(BF16)
        acc_ref[...] += _dot(act, wd_ref[...])

        @pl.when(f == nf - 1)
        def _():
            x2_ref[...] = x_ref[...] + acc_ref[...]

    tok = pl.BlockSpec((tm, D_MODEL), lambda i, f: (i, 0))
    ff = pl.BlockSpec((tm, tf), lambda i, f: (i, f))
    return pl.pallas_call(
        body, name="ffn_fwd", grid=(T // tm, nf),
        in_specs=[tok, _full((1, D_MODEL)), pl.BlockSpec((D_MODEL, tf), lambda i, f: (0, f)),
                  pl.BlockSpec((D_MODEL, tf), lambda i, f: (0, f)),
                  pl.BlockSpec((tf, D_MODEL), lambda i, f: (f, 0))],
        out_specs=[ff, ff, tok],
        out_shape=[jax.ShapeDtypeStruct((T, D_FF), F32), jax.ShapeDtypeStruct((T, D_FF), F32),
                   jax.ShapeDtypeStruct((T, D_MODEL), F32)],
        scratch_shapes=[pltpu.VMEM((tm, D_MODEL), BF16), pltpu.VMEM((tm, D_MODEL), F32)],
        compiler_params=_params("parallel", "arbitrary"),
    )(x1, g, w_gate, w_up, w_down)


def ffn_bwd(dx2, x1, g, gpre, upre, w_gate, w_up, w_down, *, tm, tf):
    T = x1.shape[0]
    tm = min(tm, T)
    nf = D_FF // tf

    def body(dx_ref, x_ref, g_ref, gp_ref, up_ref, wg_ref, wu_ref, wd_ref,
             act_ref, dg_ref, du_ref, h_ref, dx1_ref, dgain_ref, dxb_ref, acc_ref):
        f = pl.program_id(1)

        @pl.when((pl.program_id(0) == 0) & (f == 0))
        def _():
            dgain_ref[...] = jnp.zeros_like(dgain_ref)

        @pl.when(f == 0)
        def _():
            dxb_ref[...] = dx_ref[...].astype(BF16)
            acc_ref[...] = jnp.zeros_like(acc_ref)

        gp = gp_ref[...]
        up = up_ref[...]
        dact = _dot_nt(dxb_ref[...], wd_ref[...])
        sg = _sigmoid(gp)
        silu = gp * sg
        act_ref[...] = (silu * up).astype(BF16)
        dgp = (dact * up * (sg * (1.0 + gp * (1.0 - sg)))).astype(BF16)
        dup = (dact * silu).astype(BF16)
        dg_ref[...] = dgp
        du_ref[...] = dup
        acc_ref[...] += _dot_nt(dgp, wg_ref[...]) + _dot_nt(dup, wu_ref[...])

        @pl.when(f == nf - 1)
        def _():
            xv = x_ref[...]
            h_ref[...] = _rms_fwd(xv, g_ref[...]).astype(BF16)
            dxn, dgrows = _rms_bwd(xv, g_ref[...], acc_ref[...])
            dx1_ref[...] = dx_ref[...] + dxn
            _accum_rows(dgain_ref, dgrows)

    tok = pl.BlockSpec((tm, D_MODEL), lambda i, f: (i, 0))
    ff = pl.BlockSpec((tm, tf), lambda i, f: (i, f))
    return pl.pallas_call(
        body, name="ffn_bwd", grid=(T // tm, nf),
        in_specs=[tok, tok, _full((1, D_MODEL)), ff, ff,
                  pl.BlockSpec((D_MODEL, tf), lambda i, f: (0, f)),
                  pl.BlockSpec((D_MODEL, tf), lambda i, f: (0, f)),
                  pl.BlockSpec((tf, D_MODEL), lambda i, f: (f, 0))],
        out_specs=[ff, ff, ff, tok, tok, _full((1, D_MODEL))],
        out_shape=[jax.ShapeDtypeStruct((T, D_FF), BF16)] * 3
        + [jax.ShapeDtypeStruct((T, D_MODEL), BF16), jax.ShapeDtypeStruct((T, D_MODEL), F32),
           jax.ShapeDtypeStruct((1, D_MODEL), F32)],
        scratch_shapes=[pltpu.VMEM((tm, D_MODEL), BF16), pltpu.VMEM((tm, D_MODEL), F32)],
        compiler_params=_params("arbitrary", "arbitrary"),
    )(dx2, x1, g, gpre, upre, w_gate, w_up, w_down)


def ple_fwd(x2, p, g, w_pg, w_ple, *, tm):
    T = x2.shape[0]
    tm = min(tm, T)

    def body(x_ref, p_ref, g_ref, wpg_ref, wple_ref, l_ref, pe_ref, x3_ref):
        xv = x_ref[...]
        logits = _dot(_rms_fwd(xv, g_ref[...]).astype(BF16), wpg_ref[...])
        pe = _dot(p_ref[...].astype(BF16), wple_ref[...])
        l_ref[...] = logits
        pe_ref[...] = pe
        x3_ref[...] = xv + _sigmoid(logits) * pe

    tok = pl.BlockSpec((tm, D_MODEL), lambda i: (i, 0))
    return pl.pallas_call(
        body, name="ple_fwd", grid=(T // tm,),
        in_specs=[tok, pl.BlockSpec((tm, PLE_DIM), lambda i: (i, 0)), _full((1, D_MODEL)),
                  _full((D_MODEL, D_MODEL)), _full((PLE_DIM, D_MODEL))],
        out_specs=[tok, tok, tok],
        out_shape=[jax.ShapeDtypeStruct((T, D_MODEL), F32)] * 3,
        compiler_params=_params("parallel"),
    )(x2, p, g, w_pg, w_ple)


def ple_bwd(dx3, logits, pe, x2, g, w_pg, *, tm):
    T = x2.shape[0]
    tm = min(tm, T)

    def body(dx_ref, l_ref, pe_ref, x_ref, g_ref, wpg_ref, dl_ref, dpe_ref, h_ref, dx2_ref,
             dg_ref):
        @pl.when(pl.program_id(0) == 0)
        def _():
            dg_ref[...] = jnp.zeros_like(dg_ref)

        dx = dx_ref[...]
        s = _sigmoid(l_ref[...])
        dpe_ref[...] = (dx * s).astype(BF16)
        dl = (dx * pe_ref[...] * s * (1.0 - s)).astype(BF16)
        dl_ref[...] = dl
        xv = x_ref[...]
        h_ref[...] = _rms_fwd(xv, g_ref[...]).astype(BF16)
        dxn, dgrows = _rms_bwd(xv, g_ref[...], _dot_nt(dl, wpg_ref[...]))
        dx2_ref[...] = dx + dxn
        _accum_rows(dg_ref, dgrows)

    tok = pl.BlockSpec((tm, D_MODEL), lambda i: (i, 0))
    return pl.pallas_call(
        body, name="ple_bwd", grid=(T // tm,),
        in_specs=[tok, tok, tok, tok, _full((1, D_MODEL)), _full((D_MODEL, D_MODEL))],
        out_specs=[tok, tok, tok, tok, _full((1, D_MODEL))],
        out_shape=[jax.ShapeDtypeStruct((T, D_MODEL), BF16)] * 3
        + [jax.ShapeDtypeStruct((T, D_MODEL), F32), jax.ShapeDtypeStruct((1, D_MODEL), F32)],
        compiler_params=_params("arbitrary"),
    )(dx3, logits, pe, x2, g, w_pg)


def loss_head(x, g, target, *, tm):
    T = x.shape[0]
    tm = min(tm, T)

    def body(x_ref, g_ref, t_ref, loss_ref, dx_ref, dg_ref):
        @pl.when(pl.program_id(0) == 0)
        def _():
            loss_ref[...] = jnp.zeros_like(loss_ref)
            dg_ref[...] = jnp.zeros_like(dg_ref)

        xv = x_ref[...]
        err = _rms_fwd(xv, g_ref[...]) - t_ref[...]
        part = 0.5 * jnp.sum(jnp.mean(err * err, axis=-1, keepdims=True), axis=0, keepdims=True)
        lane = lax.broadcasted_iota(jnp.int32, (1, LANES), 1)
        loss_ref[...] += jnp.where(lane == 0, part, 0.0)
        dxn, dgrows = _rms_bwd(xv, g_ref[...], err * (1.0 / D_MODEL))
        dx_ref[...] = dxn
        _accum_rows(dg_ref, dgrows)

    tok = pl.BlockSpec((tm, D_MODEL), lambda i: (i, 0))
    return pl.pallas_call(
        body, name="loss_head", grid=(T // tm,),
        in_specs=[tok, _full((1, D_MODEL)), tok],
        out_specs=[_full((1, LANES)), tok, _full((1, D_MODEL))],
        out_shape=[jax.ShapeDtypeStruct((1, LANES), F32), jax.ShapeDtypeStruct((T, D_MODEL), F32),
                   jax.ShapeDtypeStruct((1, D_MODEL), F32)],
        compiler_params=_params("arbitrary"),
    )(x, g, target)


def _peers():
    x, y, c = lax.axis_index("x"), lax.axis_index("y"), lax.axis_index("c")
    me = 4 * x + 2 * y + c
    out = []
    for d in range(1, N_DEV):
        px = 1 - x if d & 4 else x
        py = 1 - y if d & 2 else y
        pc = 1 - c if d & 1 else c
        out.append(((px, py, pc), 4 * px + 2 * py + pc))
    return me, out


def exchange(send, *, gather, name):
    shape = send.shape if not gather else (N_DEV,) + send.shape

    def body(s_ref, o_ref, send_sems, recv_sems, local_sem):
        me, peers = _peers()
        local = pltpu.make_async_copy(s_ref if gather else s_ref.at[me], o_ref.at[me], local_sem)
        local.start()
        copies = []
        for d, (pos, idx) in enumerate(peers):
            cp = pltpu.make_async_remote_copy(
                src_ref=s_ref if gather else s_ref.at[idx], dst_ref=o_ref.at[me],
                send_sem=send_sems.at[d], recv_sem=recv_sems.at[d],
                device_id=pos, device_id_type=pl.DeviceIdType.MESH)
            cp.start()
            copies.append(cp)
        for cp in copies:
            cp.wait()
        local.wait()

    return pl.pallas_call(
        body, name=name,
        in_specs=[pl.BlockSpec(memory_space=pl.ANY)],
        out_specs=pl.BlockSpec(memory_space=pl.ANY),
        out_shape=jax.ShapeDtypeStruct(shape, send.dtype),
        scratch_shapes=[pltpu.SemaphoreType.DMA((N_DEV - 1,)), pltpu.SemaphoreType.DMA((N_DEV - 1,)),
                        pltpu.SemaphoreType.DMA],
        compiler_params=pltpu.CompilerParams(has_side_effects=True),
    )(send)


def reduce_adamw(parts, w, m, v):
    rows = w.shape[0]
    rb = FLAT_ROW_BLOCK

    def body(p_ref, w_ref, m_ref, v_ref, g_ref, d_ref, m2_ref, v2_ref):
        g = p_ref[0]
        for k in range(1, N_DEV):
            g = g + p_ref[k]
        g_ref[...] = g
        m2 = ADAM_B1 * m_ref[...] + (1.0 - ADAM_B1) * g
        v2 = ADAM_B2 * v_ref[...] + (1.0 - ADAM_B2) * (g * g)
        m2_ref[...] = m2
        v2_ref[...] = v2
        m_hat = m2 / (1.0 - ADAM_B1 ** ADAM_STEP)
        v_hat = v2 / (1.0 - ADAM_B2 ** ADAM_STEP)
        d_ref[...] = -ADAM_LR * (m_hat / (jnp.sqrt(v_hat) + ADAM_EPS) + ADAM_WD * w_ref[...])

    blk = pl.BlockSpec((rb, FLAT_COLS), lambda i: (i, 0))
    return pl.pallas_call(
        body, name="reduce_adamw", grid=(rows // rb,),
        in_specs=[pl.BlockSpec((N_DEV, rb, FLAT_COLS), lambda i: (0, i, 0)), blk, blk, blk],
        out_specs=[blk] * 4,
        out_shape=[jax.ShapeDtypeStruct((rows, FLAT_COLS), F32)] * 4,
        compiler_params=_params("parallel"),
    )(parts, w, m, v)


def _to_pieces(full, axis):
    s = full.shape
    t = full.reshape(s[:axis] + (N_DEV, s[axis] // N_DEV) + s[axis + 1:])
    return jnp.moveaxis(t, axis, 0).reshape(N_DEV, -1)


def _from_pieces(pieces, shard_shape, axis):
    t = pieces.reshape((N_DEV,) + tuple(shard_shape))
    t = jnp.moveaxis(t, 0, axis)
    s = t.shape
    return t.reshape(s[:axis] + (s[axis] * s[axis + 1],) + s[axis + 2:])


def _pad_flat(flat, n):
    return jnp.pad(flat, [(0, 0)] * (flat.ndim - 1) + [(0, n - flat.shape[-1])])


def _rope_tables(positions):
    inv_freq = 1.0 / (ROPE_THETA ** (jnp.arange(0, QK_ROPE, 2, dtype=F32) / QK_ROPE))
    ang = positions.astype(F32)[:, None] * inv_freq
    cos, sin = jnp.cos(ang), jnp.sin(ang)
    zero = jnp.zeros_like(cos)
    return (jnp.concatenate([cos, cos, zero, zero], axis=1),
            jnp.concatenate([-sin, zero, zero, zero], axis=1),
            jnp.concatenate([zero, sin, zero, zero], axis=1))


def _pad_w_in(w):
    u, cq, ckv = w[:, 0:512], w[:, 512:1024], w[:, 1024:1280]
    kr, ga, gb = w[:, 1280:1344], w[:, 1344:2368], w[:, 2368:3392]
    return jnp.concatenate([ga, gb, u, cq, ckv, kr, jnp.zeros_like(kr)], axis=1)


def kernel(x, p, positions, norm_mix, w_in, w_pool, pool_scale, q_norm, kv_norm, w_uq, w_ukv, w_a, w_b, w_o, norm_ffn, w_gate, w_up, w_down, norm_ple, w_ple_gate, w_ple, final_norm, loss_target, m_norm_mix, m_w_in, m_w_pool, m_pool_scale, m_q_norm, m_kv_norm, m_w_uq, m_w_ukv, m_w_a, m_w_b, m_w_o, m_norm_ffn, m_w_gate, m_w_up, m_w_down, m_norm_ple, m_w_ple_gate, m_w_ple, m_final_norm, v_norm_mix, v_w_in, v_w_pool, v_pool_scale, v_q_norm, v_kv_norm, v_w_uq, v_w_ukv, v_w_a, v_w_b, v_w_o, v_norm_ffn, v_w_gate, v_w_up, v_w_down, v_norm_ple, v_w_ple_gate, v_w_ple, v_final_norm):
    given = dict(locals())
    shard = {n: given[n] for n in WEIGHTS}
    T = x.shape[1]
    TM = 512
    TQ = 512
    TF = 256

    sharded_names = [n for n, _ in SHARDED]
    sizes = [int(np.prod(shard[n].shape)) for n in sharded_names]
    send = jnp.concatenate([shard[n].astype(BF16).reshape(-1) for n in sharded_names])
    gathered = exchange(send.reshape(-1, FLAT_COLS), gather=True, name="gather_weights")
    gathered = gathered.reshape(N_DEV, -1)
    full = {}
    off = 0
    for (n, axis), size in zip(SHARDED, sizes):
        full[n] = _from_pieces(gathered[:, off:off + size], shard[n].shape, axis)
        off += size

    low_np, up_np = _band_matrices()
    low, up = jnp.asarray(low_np, BF16), jnp.asarray(up_np, BF16)
    low_t = jnp.asarray(low_np.transpose(0, 2, 1), BF16)
    up_t = jnp.asarray(up_np.transpose(0, 2, 1), BF16)
    cos_t, s1_t, s2_t = _rope_tables(positions[0])

    layers = []
    for i in range(DEPTH):
        w_uq_pad = jnp.pad(full["w_uq"][i], ((0, 0), (0, 0), (0, HEAD_PAD - QK_HEAD)))
        layers.append(dict(
            w_in=_pad_w_in(full["w_in"][i]),
            w_uq=w_uq_pad.reshape(Q_LORA, N_HEADS * HEAD_PAD),
            w_ukv=full["w_ukv"][i].reshape(KV_LORA, N_HEADS * HEAD_PAD),
            w_a=full["w_a"][i], w_b=full["w_b"][i], w_o=full["w_o"][i],
            w_gate=full["w_gate"][i], w_up=full["w_up"][i], w_down=full["w_down"][i],
            w_pg=full["w_ple_gate"][i], w_ple=full["w_ple"][i],
            w_pool=w_pool[i].astype(BF16),
            pool_scale=pool_scale[i][None], norm_mix=norm_mix[i][None], q_norm=q_norm[i][None],
            kv_norm=kv_norm[i][None], norm_ffn=norm_ffn[i][None], norm_ple=norm_ple[i][None]))

    xs = x[0]
    saved = []
    for i in range(DEPTH):
        L = layers[i]
        z = rms_matmul(xs, L["norm_mix"], L["w_in"], tm=TM, tn=1152, name="in_proj")
        pooled, ms, ya = pool_fwd(z, low, up, L["w_pool"], L["pool_scale"], L["w_a"], tm=TM)
        q, k, v = mla_prep(z, cos_t, s1_t, s2_t, L["q_norm"], L["kv_norm"], L["w_uq"], L["w_ukv"],
                           tm=TM)
        o, lse = flash_fwd(q, k, v, tq=TQ)
        yb, merged, x1 = merge_fwd(o, ya, z, xs, L["w_b"], L["w_o"], tm=TM)
        gpre, upre, x2 = ffn_fwd(x1, L["norm_ffn"], L["w_gate"], L["w_up"], L["w_down"],
                                 tm=2 * TM, tf=TF)
        logits, pe, x3 = ple_fwd(x2, p[i, 0], L["norm_ple"], L["w_pg"], L["w_ple"], tm=TM)
        saved.append(dict(x=xs, z=z, pooled=pooled, ms=ms, ya=ya, q=q, k=k, v=v, o=o, lse=lse,
                          yb=yb, merged=merged, x1=x1, gpre=gpre, upre=upre, x2=x2,
                          logits=logits, pe=pe))
        xs = x3

    loss_part, dx, d_final = loss_head(xs, final_norm[None], loss_target[0], tm=TM)

    grads = {n: [None] * DEPTH for n in WEIGHTS if n != "final_norm"}
    tn_mm = functools.partial(matmul_tn, tk=1024)
    for i in reversed(range(DEPTH)):
        L, S = layers[i], saved[i]
        dl, dpe, hn, dx2, d_norm_ple = ple_bwd(dx, S["logits"], S["pe"], S["x2"], L["norm_ple"],
                                               L["w_pg"], tm=TM)
        grads["norm_ple"][i] = d_norm_ple[0]
        grads["w_ple_gate"][i] = tn_mm(hn, dl, tn=1024, name="dw_ple_gate")
        grads["w_ple"][i] = tn_mm(p[i, 0], dpe, tn=1024, name="dw_ple")

        act, dgp, dup, h2, dx1, d_norm_ffn = ffn_bwd(dx2, S["x1"], L["norm_ffn"], S["gpre"],
                                                     S["upre"], L["w_gate"], L["w_up"],
                                                     L["w_down"], tm=TM, tf=TF)
        grads["norm_ffn"][i] = d_norm_ffn[0]
        grads["w_down"][i] = tn_mm(act, dx2, tm=1408, tn=1024, name="dw_down")
        grads["w_gate"][i] = tn_mm(h2, dgp, tn=1408, name="dw_gate")
        grads["w_up"][i] = tn_mm(h2, dup, tn=1408, name="dw_up")

        dya, dyb, do, dga, dgb = merge_bwd(dx1, S["ya"], S["yb"], S["z"], L["w_o"], L["w_b"],
                                           tm=TM)
        grads["w_o"][i] = tn_mm(S["merged"], dx1, tn=1024, name="dw_o")
        grads["w_b"][i] = tn_mm(S["o"], dyb, tn=1024, name="dw_b")

        delta = attn_delta(do, S["o"], tq=TQ)
        dq, dk, dv = flash_bwd(S["q"], S["k"], S["v"], do, S["lse"], delta, tq=TQ)
        dcq, dckv, dkr, dqf, dkvf, qn, kvn, d_q_norm, d_kv_norm = mla_bwd(
            dq, dk, dv, S["z"], cos_t, s1_t, s2_t, L["q_norm"], L["kv_norm"], L["w_uq"],
            L["w_ukv"], tm=TM)
        grads["q_norm"][i] = d_q_norm[0]
        grads["kv_norm"][i] = d_kv_norm[0]
        d_w_uq = tn_mm(qn, dqf, tn=1024, name="dw_uq")
        grads["w_uq"][i] = d_w_uq.reshape(Q_LORA, N_HEADS, HEAD_PAD)[:, :, :QK_HEAD]
        grads["w_ukv"][i] = tn_mm(kvn, dkvf, tn=1024, name="dw_ukv").reshape(
            KV_LORA, N_HEADS, QK_NOPE + V_HEAD)

        dpc, d_pool_scale, d_w_pool = pool_bwd_a(dya, S["pooled"], L["w_a"], L["w_pool"],
                                                 L["pool_scale"], tm=TM)
        grads["pool_scale"][i] = d_pool_scale[0]
        grads["w_pool"][i] = d_w_pool
        grads["w_a"][i] = tn_mm(S["ms"], dya, tn=1024, name="dw_a")
        du = pool_bwd_b(dpc, low_t, up_t, tm=TM)

        segs = [(dga, OFF_GA), (dgb, OFF_GB), (du, OFF_U), (dcq, OFF_CQ), (dckv, OFF_CKV),
                (dkr, OFF_KR)]
        dx, h, d_norm_mix = in_bwd(dx1, S["x"], L["norm_mix"], segs, L["w_in"], tm=TM)
        grads["norm_mix"][i] = d_norm_mix[0]
        dws = {o_: tn_mm(h, s_, tn=1024, name="dw_in") for s_, o_ in segs}
        grads["w_in"][i] = jnp.concatenate(
            [dws[OFF_U], dws[OFF_CQ], dws[OFF_CKV], dws[OFF_KR][:, :QK_ROPE], dws[OFF_GA],
             dws[OFF_GB]], axis=1)

    grad_full = {n: jnp.stack(g) for n, g in grads.items()}
    grad_full["final_norm"] = d_final[0]

    n_sh = sum(sizes)
    rep_sizes = [int(np.prod(shard[n].shape)) for n in REPLICATED]
    n_flat = n_sh + sum(rep_sizes) + LANES
    block = FLAT_COLS * FLAT_ROW_BLOCK
    n_pad = -(-n_flat // block) * block
    rep = jnp.concatenate([grad_full[n].reshape(-1) for n in REPLICATED] + [loss_part.reshape(-1)])
    pieces = jnp.concatenate(
        [_to_pieces(grad_full[n], axis) for n, axis in SHARDED]
        + [jnp.broadcast_to(rep[None], (N_DEV, rep.shape[0]))], axis=1)
    pieces = _pad_flat(pieces, n_pad).reshape(N_DEV, -1, FLAT_COLS)
    parts = exchange(pieces, gather=False, name="scatter_grads")

    def flat(prefix):
        arrs = [given[prefix + n].reshape(-1) for n in sharded_names + list(REPLICATED)]
        return _pad_flat(jnp.concatenate(arrs), n_pad).reshape(-1, FLAT_COLS)

    g_flat, d_flat, m_flat, v_flat = reduce_adamw(parts, flat(""), flat("m_"), flat("v_"))

    def unflat(a):
        a = a.reshape(-1)
        out = {}
        off = 0
        for n, size in zip(sharded_names + list(REPLICATED), sizes + rep_sizes):
            out[n] = a[off:off + size].reshape(shard[n].shape)
            off += size
        return out, a[off]

    g_out, loss = unflat(g_flat)
    d_out, _ = unflat(d_flat)
    m_out, _ = unflat(m_flat)
    v_out, _ = unflat(v_flat)
    return (loss, dx[None], *[g_out[n] for n in WEIGHTS], *[d_out[n] for n in WEIGHTS],
            *[m_out[n] for n in WEIGHTS], *[v_out[n] for n in WEIGHTS])
```

```python
import functools

import numpy as np
import jax
import jax.numpy as jnp
from jax import lax
from jax.experimental import pallas as pl
from jax.experimental.pallas import tpu as pltpu

F32 = jnp.float32
BF16 = jnp.bfloat16

D_MODEL = 1024
DEPTH = 2
PLE_DIM = 256
POOL_WINDOWS = (2, 4, 8, 16)
POOL_GROUP = 128
POOL_WIDTH = 512
N_HEADS = 8
Q_LORA = 512
KV_LORA = 256
QK_NOPE = 128
QK_ROPE = 64
QK_HEAD = 192
V_HEAD = 128
HEAD_PAD = 256
D_FF = 2816
ROPE_THETA = 10000.0
EPS = 1e-6
IN_WIDTH = 3392
ATTN_SCALE = QK_HEAD ** -0.5

OFF_GA, OFF_GB, OFF_U, OFF_CQ, OFF_CKV, OFF_KR = 0, 1024, 2048, 2560, 3072, 3328
IN_PAD = 3456
W_IN_PIECE = IN_WIDTH // 8
PIECE_PAD = 512
FF_PIECE = D_FF // 8

ADAM_LR = 0.001
ADAM_B1 = 0.9
ADAM_B2 = 0.999
ADAM_EPS = 1e-08
ADAM_WD = 0.01
ADAM_STEP = 10

N_DEV = 8
LANES = 128
CHUNK = 128
VMEM_LIMIT = 56 * 1024 * 1024

SHARDED = (("w_in", 2), ("w_uq", 1), ("w_ukv", 1), ("w_a", 2), ("w_b", 1), ("w_o", 1),
           ("w_gate", 2), ("w_up", 2), ("w_down", 1), ("w_ple_gate", 1), ("w_ple", 2))
REPLICATED = ("norm_mix", "w_pool", "pool_scale", "q_norm", "kv_norm", "norm_ffn", "norm_ple",
              "final_norm")
WEIGHTS = ("norm_mix", "w_in", "w_pool", "pool_scale", "q_norm", "kv_norm", "w_uq", "w_ukv",
           "w_a", "w_b", "w_o", "norm_ffn", "w_gate", "w_up", "w_down", "norm_ple",
           "w_ple_gate", "w_ple", "final_norm")
FLAT_COLS = 1024
FLAT_ROW_BLOCK = 192


def _params(*sem):
    return pltpu.CompilerParams(dimension_semantics=sem, vmem_limit_bytes=VMEM_LIMIT)


def _dot(a, b):
    return jnp.dot(a, b, preferred_element_type=F32)


def _dot_nt(a, b):
    return lax.dot_general(a, b, (((1,), (1,)), ((), ())), preferred_element_type=F32)


def _dot_tn(a, b):
    return lax.dot_general(a, b, (((0,), (0,)), ((), ())), preferred_element_type=F32)


def _rms_fwd(x, g):
    r = lax.rsqrt(jnp.mean(x * x, axis=-1, keepdims=True) + EPS)
    return x * r * g


def _rms_bwd(x, g, dy):
    r = lax.rsqrt(jnp.mean(x * x, axis=-1, keepdims=True) + EPS)
    xr = x * r
    gy = dy * g
    dx = r * (gy - xr * jnp.mean(gy * xr, axis=-1, keepdims=True))
    return dx, dy * xr


def _sigmoid(x):
    return 1.0 / (1.0 + jnp.exp(-x))


def _accum_rows(ref, rows):
    ref[...] += jnp.sum(rows, axis=0, keepdims=True)


def _band(band, x):
    h1 = x.astype(BF16)
    r1 = x - h1.astype(F32)
    h2 = r1.astype(BF16)
    h3 = (r1 - h2.astype(F32)).astype(BF16)
    return _dot(band, h1) + _dot(band, h2) + _dot(band, h3)


def _rope(x, c, s1, s2, sign):
    return x * c + sign * (pltpu.roll(x, 96, 1) * s1 + pltpu.roll(x, 32, 1) * s2)


def _full(shape):
    n = len(shape)
    return pl.BlockSpec(shape, lambda *_: (0,) * n)


def matmul_tn(a, b, *, tn, tk, name, tm=1024):
    T, M = a.shape
    N = b.shape[1]
    tm, tn, tk = min(tm, M), min(tn, N), min(tk, T)
    nk = T // tk

    def body(a_ref, b_ref, o_ref, acc_ref):
        k = pl.program_id(2)

        @pl.when(k == 0)
        def _():
            acc_ref[...] = jnp.zeros_like(acc_ref)

        acc_ref[...] += _dot_tn(a_ref[...].astype(BF16), b_ref[...].astype(BF16))

        @pl.when(k == nk - 1)
        def _():
            o_ref[...] = acc_ref[...].astype(BF16)

    return pl.pallas_call(
        body, name=name, grid=(M // tm, N // tn, nk),
        in_specs=[pl.BlockSpec((tk, tm), lambda i, j, k: (k, i)),
                  pl.BlockSpec((tk, tn), lambda i, j, k: (k, j))],
        out_specs=pl.BlockSpec((tm, tn), lambda i, j, k: (i, j)),
        out_shape=jax.ShapeDtypeStruct((M, N), BF16),
        scratch_shapes=[pltpu.VMEM((tm, tn), F32)],
        compiler_params=_params("parallel", "parallel", "arbitrary"),
    )(a, b)


def matmul_tn_pieces(a, b, *, tk, name):
    a3 = a.ndim == 3
    T = a.shape[-2]
    m, n = a.shape[-1], b.shape[-1]
    tk = min(tk, T)
    nk = T // tk

    def body(a_ref, b_ref, o_ref, acc_ref):
        k = pl.program_id(0)

        @pl.when(k == 0)
        def _():
            acc_ref[...] = jnp.zeros_like(acc_ref)

        whole = (b_ref if a3 else a_ref)[...].astype(BF16)
        for j in range(N_DEV):
            if a3:
                acc_ref[j] += _dot_tn(a_ref[j].astype(BF16), whole)
            else:
                acc_ref[j] += _dot_tn(whole, b_ref[j].astype(BF16))

        @pl.when(k == nk - 1)
        def _():
            o_ref[...] = acc_ref[...].astype(BF16)

    piece = lambda w: pl.BlockSpec((N_DEV, tk, w), lambda k: (0, k, 0))
    whole = lambda w: pl.BlockSpec((tk, w), lambda k: (k, 0))
    return pl.pallas_call(
        body, name=name, grid=(nk,),
        in_specs=[piece(m) if a3 else whole(m), whole(n) if a3 else piece(n)],
        out_specs=_full((N_DEV, m, n)),
        out_shape=jax.ShapeDtypeStruct((N_DEV, m, n), BF16),
        scratch_shapes=[pltpu.VMEM((N_DEV, m, n), F32)],
        compiler_params=_params("arbitrary"),
    )(a, b)


def dw_in_proj(h, segs, *, tk):
    T = h.shape[0]
    tk = min(tk, T)
    nk = T // tk
    widths = [s.shape[1] for s, _ in segs]
    offs = [o for _, o in segs]

    def body(*refs):
        h_ref, seg_refs, o_ref, acc_ref = refs[0], refs[1:-2], refs[-2], refs[-1]
        k = pl.program_id(0)

        @pl.when(k == 0)
        def _():
            acc_ref[...] = jnp.zeros_like(acc_ref)

        hv = h_ref[...]
        for s_ref, off, w in zip(seg_refs, offs, widths):
            acc_ref[:, off:off + w] += _dot_tn(hv, s_ref[...])

        @pl.when(k == nk - 1)
        def _():
            o_ref[...] = acc_ref[...].astype(BF16)

    return pl.pallas_call(
        body, name="dw_in", grid=(nk,),
        in_specs=[pl.BlockSpec((tk, D_MODEL), lambda k: (k, 0))]
        + [pl.BlockSpec((tk, w), lambda k: (k, 0)) for w in widths],
        out_specs=_full((D_MODEL, IN_PAD)),
        out_shape=jax.ShapeDtypeStruct((D_MODEL, IN_PAD), BF16),
        scratch_shapes=[pltpu.VMEM((D_MODEL, IN_PAD), F32)],
        compiler_params=_params("arbitrary"),
    )(h, *[s for s, _ in segs])


def _piece_row_map():
    src = np.full(IN_PAD, -1, np.int64)
    for orig, pad, width in ((0, OFF_U, 512), (512, OFF_CQ, 512), (1024, OFF_CKV, 256),
                             (1280, OFF_KR, 64), (1344, OFF_GA, 1024), (2368, OFF_GB, 1024)):
        src[pad:pad + width] = np.arange(orig, orig + width)
    rows = np.where(src >= 0, (src // W_IN_PIECE) * PIECE_PAD + src % W_IN_PIECE, -1)
    return rows.astype(np.int32)[None, :]


def _selector(j, map_ref):
    rid = j * PIECE_PAD + lax.broadcasted_iota(jnp.int32, (PIECE_PAD, IN_PAD), 0)
    return jnp.where(rid == map_ref[...], 1.0, 0.0).astype(BF16)


def assemble_w_in(pieces, row_map, *, tr):
    def body(p_ref, map_ref, o_ref, acc_ref):
        j = pl.program_id(1)

        @pl.when(j == 0)
        def _():
            acc_ref[...] = jnp.zeros_like(acc_ref)

        acc_ref[...] += _dot(p_ref[0], _selector(j, map_ref))

        @pl.when(j == N_DEV - 1)
        def _():
            o_ref[...] = acc_ref[...].astype(BF16)

    return pl.pallas_call(
        body, name="assemble_w_in", grid=(D_MODEL // tr, N_DEV),
        in_specs=[pl.BlockSpec((1, tr, PIECE_PAD), lambda i, j: (j, i, 0)), _full((1, IN_PAD))],
        out_specs=pl.BlockSpec((tr, IN_PAD), lambda i, j: (i, 0)),
        out_shape=jax.ShapeDtypeStruct((D_MODEL, IN_PAD), BF16),
        scratch_shapes=[pltpu.VMEM((tr, IN_PAD), F32)],
        compiler_params=_params("parallel", "arbitrary"),
    )(pieces, row_map)


def split_dw_in(dwp, row_map):
    def body(d_ref, map_ref, o_ref):
        o_ref[0] = _dot_nt(d_ref[...], _selector(pl.program_id(0), map_ref)).astype(BF16)

    return pl.pallas_call(
        body, name="split_dw_in", grid=(N_DEV,),
        in_specs=[_full((D_MODEL, IN_PAD)), _full((1, IN_PAD))],
        out_specs=pl.BlockSpec((1, D_MODEL, PIECE_PAD), lambda j: (j, 0, 0)),
        out_shape=jax.ShapeDtypeStruct((N_DEV, D_MODEL, PIECE_PAD), BF16),
        compiler_params=_params("parallel"),
    )(dwp, row_map)


def rms_matmul(x, g, w, *, tm, tn, name):
    T, D = x.shape
    N = w.shape[1]
    tm = min(tm, T)

    def body(x_ref, g_ref, w_ref, o_ref, h_ref):
        @pl.when(pl.program_id(1) == 0)
        def _():
            h_ref[...] = _rms_fwd(x_ref[...], g_ref[...]).astype(BF16)

        o_ref[...] = _dot(h_ref[...], w_ref[...])

    return pl.pallas_call(
        body, name=name, grid=(T // tm, N // tn),
        in_specs=[pl.BlockSpec((tm, D), lambda i, j: (i, 0)), _full((1, D)),
                  pl.BlockSpec((D, tn), lambda i, j: (0, j))],
        out_specs=pl.BlockSpec((tm, tn), lambda i, j: (i, j)),
        out_shape=jax.ShapeDtypeStruct((T, N), F32),
        scratch_shapes=[pltpu.VMEM((tm, D), BF16)],
        compiler_params=_params("parallel", "arbitrary"),
    )(x, g, w)


def _band_matrices():
    s = np.arange(CHUNK)[:, None]
    t = np.arange(CHUNK)[None, :]
    low = np.stack([((s - t >= 0) & (s - t < w)) for w in POOL_WINDOWS]).astype(np.float32)
    up = np.stack([(t > s + CHUNK - w) for w in POOL_WINDOWS]).astype(np.float32)
    return low, up


def _window_count(row0, g):
    t = row0 + lax.broadcasted_iota(jnp.int32, (CHUNK, 1), 0)
    return jnp.minimum(t + 1, POOL_WINDOWS[g]).astype(F32)


def pool_fwd(z, low, up, w_pool, pool_scale, w_a, *, tm):
    T = z.shape[0]
    tm = min(tm, T)
    nch = tm // CHUNK
    ublk = OFF_U // POOL_WIDTH

    def body(u_ref, halo_ref, low_ref, up_ref, wp_ref, sc_ref, wa_ref, pooled_ref, ms_ref, ya_ref):
        i = pl.program_id(0)
        for c in range(nch):
            rows = slice(c * CHUNK, (c + 1) * CHUNK)
            for g in range(4):
                cols = slice(g * POOL_GROUP, (g + 1) * POOL_GROUP)
                cur = u_ref[rows, cols]
                if c == 0:
                    prev = jnp.where(i > 0, halo_ref[:, cols], 0.0)
                else:
                    prev = u_ref[(c - 1) * CHUNK:c * CHUNK, cols]
                s = _band(low_ref[g], cur) + _band(up_ref[g], prev)
                pooled = (s / _window_count(i * tm + c * CHUNK, g) - cur).astype(BF16)
                pooled_ref[rows, cols] = pooled
                ms_ref[rows, cols] = (_dot(pooled, wp_ref[g]) * sc_ref[:, cols]).astype(BF16)
        ya_ref[...] = _dot(ms_ref[...], wa_ref[...])

    return pl.pallas_call(
        body, name="pool_fwd", grid=(T // tm,),
        in_specs=[pl.BlockSpec((tm, POOL_WIDTH), lambda i: (i, ublk)),
                  pl.BlockSpec((CHUNK, POOL_WIDTH), lambda i: (jnp.maximum(i * nch - 1, 0), ublk)),
                  _full((4, CHUNK, CHUNK)), _full((4, CHUNK, CHUNK)),
                  _full((4, POOL_GROUP, POOL_GROUP)), _full((1, POOL_WIDTH)),
                  _full((POOL_WIDTH, D_MODEL))],
        out_specs=[pl.BlockSpec((tm, POOL_WIDTH), lambda i: (i, 0)),
                   pl.BlockSpec((tm, POOL_WIDTH), lambda i: (i, 0)),
                   pl.BlockSpec((tm, D_MODEL), lambda i: (i, 0))],
        out_shape=[jax.ShapeDtypeStruct((T, POOL_WIDTH), BF16),
                   jax.ShapeDtypeStruct((T, POOL_WIDTH), BF16),
                   jax.ShapeDtypeStruct((T, D_MODEL), F32)],
        compiler_params=_params("parallel"),
    )(z, z, low, up, w_pool, pool_scale, w_a)


def pool_bwd_a(dya, pooled, w_a, w_pool, pool_scale, *, tm):
    T = dya.shape[0]
    tm = min(tm, T)
    nch = tm // CHUNK

    def body(dya_ref, pooled_ref, wa_ref, wp_ref, sc_ref, dpc_ref, dsc_ref, dwp_ref):
        i = pl.program_id(0)

        @pl.when(i == 0)
        def _():
            dsc_ref[...] = jnp.zeros_like(dsc_ref)
            dwp_ref[...] = jnp.zeros_like(dwp_ref)

        dms = _dot_nt(dya_ref[...], wa_ref[...])
        for g in range(4):
            cols = slice(g * POOL_GROUP, (g + 1) * POOL_GROUP)
            pg = pooled_ref[:, cols]
            dmg = dms[:, cols]
            mixed = _dot(pg, wp_ref[g])
            dsc_ref[:, cols] += jnp.sum(dmg * mixed, axis=0, keepdims=True)
            dmixed = (dmg * sc_ref[:, cols]).astype(BF16)
            dwp_ref[g] += _dot_tn(pg, dmixed)
            dpooled = _dot_nt(dmixed, wp_ref[g])
            for c in range(nch):
                rows = slice(c * CHUNK, (c + 1) * CHUNK)
                dpc_ref[rows, cols] = dpooled[rows] / _window_count(i * tm + c * CHUNK, g)

    return pl.pallas_call(
        body, name="pool_bwd_a", grid=(T // tm,),
        in_specs=[pl.BlockSpec((tm, D_MODEL), lambda i: (i, 0)),
                  pl.BlockSpec((tm, POOL_WIDTH), lambda i: (i, 0)),
                  _full((POOL_WIDTH, D_MODEL)), _full((4, POOL_GROUP, POOL_GROUP)),
                  _full((1, POOL_WIDTH))],
        out_specs=[pl.BlockSpec((tm, POOL_WIDTH), lambda i: (i, 0)), _full((1, POOL_WIDTH)),
                   _full((4, POOL_GROUP, POOL_GROUP))],
        out_shape=[jax.ShapeDtypeStruct((T, POOL_WIDTH), F32),
                   jax.ShapeDtypeStruct((1, POOL_WIDTH), F32),
                   jax.ShapeDtypeStruct((4, POOL_GROUP, POOL_GROUP), F32)],
        compiler_params=_params("arbitrary"),
    )(dya, pooled, w_a, w_pool, pool_scale)


def pool_bwd_b(dpc, low_t, up_t, *, tm):
    T = dpc.shape[0]
    tm = min(tm, T)
    nch = tm // CHUNK
    last_chunk = T // CHUNK - 1

    def body(d_ref, halo_ref, low_ref, up_ref, du_ref):
        i = pl.program_id(0)
        for c in range(nch):
            rows = slice(c * CHUNK, (c + 1) * CHUNK)
            for g in range(4):
                cols = slice(g * POOL_GROUP, (g + 1) * POOL_GROUP)
                cur = d_ref[rows, cols]
                if c == nch - 1:
                    nxt = jnp.where(i < pl.num_programs(0) - 1, halo_ref[:, cols], 0.0)
                else:
                    nxt = d_ref[(c + 1) * CHUNK:(c + 2) * CHUNK, cols]
                s = _band(low_ref[g], cur) + _band(up_ref[g], nxt)
                du_ref[rows, cols] = (s - cur * _window_count(i * tm + c * CHUNK, g)).astype(BF16)

    return pl.pallas_call(
        body, name="pool_bwd_b", grid=(T // tm,),
        in_specs=[pl.BlockSpec((tm, POOL_WIDTH), lambda i: (i, 0)),
                  pl.BlockSpec((CHUNK, POOL_WIDTH),
                               lambda i: (jnp.minimum((i + 1) * nch, last_chunk), 0)),
                  _full((4, CHUNK, CHUNK)), _full((4, CHUNK, CHUNK))],
        out_specs=pl.BlockSpec((tm, POOL_WIDTH), lambda i: (i, 0)),
        out_shape=jax.ShapeDtypeStruct((T, POOL_WIDTH), BF16),
        compiler_params=_params("parallel"),
    )(dpc, dpc, low_t, up_t)


def mla_prep(z, cos_t, s1_t, s2_t, q_norm, kv_norm, w_uq, w_ukv, *, tm):
    T = z.shape[0]
    tm = min(tm, T)

    def body(cq_ref, ckv_ref, kr_ref, c_ref, s1_ref, s2_ref, qg_ref, kvg_ref, wuq_ref, wukv_ref,
             q_ref, k_ref, v_ref):
        c, s1, s2 = c_ref[...], s1_ref[...], s2_ref[...]
        qn = _rms_fwd(cq_ref[...], qg_ref[...]).astype(BF16)
        q = _dot(qn, wuq_ref[...])
        kvn = _rms_fwd(ckv_ref[...], kvg_ref[...]).astype(BF16)
        kv = _dot(kvn, wukv_ref[...])
        kpe = _rope(kr_ref[...], c, s1, s2, 1.0).astype(BF16)
        for h in range(N_HEADS):
            o = h * HEAD_PAD
            q_ref[h, :, 0:QK_NOPE] = (q[:, o:o + QK_NOPE] * ATTN_SCALE).astype(BF16)
            q_ref[h, :, QK_NOPE:HEAD_PAD] = (
                _rope(q[:, o + QK_NOPE:o + HEAD_PAD], c, s1, s2, 1.0) * ATTN_SCALE).astype(BF16)
            k_ref[h, :, 0:QK_NOPE] = kv[:, o:o + QK_NOPE].astype(BF16)
            k_ref[h, :, QK_NOPE:HEAD_PAD] = kpe
            v_ref[h] = kv[:, o + QK_NOPE:o + HEAD_PAD].astype(BF16)

    tok = lambda w: pl.BlockSpec((tm, w), lambda i: (i, 0))
    return pl.pallas_call(
        body, name="mla_prep", grid=(T // tm,),
        in_specs=[pl.BlockSpec((tm, Q_LORA), lambda i: (i, OFF_CQ // Q_LORA)),
                  pl.BlockSpec((tm, KV_LORA), lambda i: (i, OFF_CKV // KV_LORA)),
                  pl.BlockSpec((tm, LANES), lambda i: (i, OFF_KR // LANES)),
                  tok(LANES), tok(LANES), tok(LANES),
                  _full((1, Q_LORA)), _full((1, KV_LORA)),
                  _full((Q_LORA, N_HEADS * HEAD_PAD)), _full((KV_LORA, N_HEADS * HEAD_PAD))],
        out_specs=[pl.BlockSpec((N_HEADS, tm, HEAD_PAD), lambda i: (0, i, 0)),
                   pl.BlockSpec((N_HEADS, tm, HEAD_PAD), lambda i: (0, i, 0)),
                   pl.BlockSpec((N_HEADS, tm, V_HEAD), lambda i: (0, i, 0))],
        out_shape=[jax.ShapeDtypeStruct((N_HEADS, T, HEAD_PAD), BF16),
                   jax.ShapeDtypeStruct((N_HEADS, T, HEAD_PAD), BF16),
                   jax.ShapeDtypeStruct((N_HEADS, T, V_HEAD), BF16)],
        compiler_params=_params("parallel"),
    )(z, z, z, cos_t, s1_t, s2_t, q_norm, kv_norm, w_uq, w_ukv)


def _row_vector(col, n):
    return jnp.transpose(jnp.broadcast_to(col, (n, LANES)))[0:1, :]


def flash_fwd(q, k, v, *, tq):
    H, T, _ = q.shape
    tq = min(tq, T)
    nq = T // tq
    neg = -1e30

    def body(q_ref, k_ref, v_ref, o_ref, lse_ref):
        qi = pl.program_id(1)
        qb = q_ref[0]

        def tile(j, carry, masked):
            m, l, acc = carry
            rows = pl.ds(pl.multiple_of(j * tq, tq), tq)
            s = _dot_nt(qb, k_ref[0, rows, :])
            if masked:
                r = lax.broadcasted_iota(jnp.int32, (tq, tq), 0)
                c = lax.broadcasted_iota(jnp.int32, (tq, tq), 1)
                s = jnp.where(c <= r, s, neg)
            m_new = jnp.maximum(m, jnp.max(s, axis=-1, keepdims=True))
            alpha = jnp.exp(m - m_new)
            p = jnp.exp(s - m_new)
            l = alpha * l + jnp.sum(p, axis=-1, keepdims=True)
            acc = alpha * acc + _dot(p.astype(BF16), v_ref[0, rows, :])
            return m_new, l, acc

        init = (jnp.full((tq, 1), neg, F32), jnp.zeros((tq, 1), F32), jnp.zeros((tq, V_HEAD), F32))
        carry = lax.fori_loop(0, qi, lambda j, cr: tile(j, cr, False), init)
        m, l, acc = tile(qi, carry, True)
        o_ref[...] = (acc / l).astype(BF16)
        lse_ref[0, 0] = _row_vector(m + jnp.log(l), tq)

    return pl.pallas_call(
        body, name="flash_fwd", grid=(H, nq),
        in_specs=[pl.BlockSpec((1, tq, HEAD_PAD), lambda h, i: (h, i, 0)),
                  pl.BlockSpec((1, T, HEAD_PAD), lambda h, i: (h, 0, 0)),
                  pl.BlockSpec((1, T, V_HEAD), lambda h, i: (h, 0, 0))],
        out_specs=[pl.BlockSpec((tq, V_HEAD), lambda h, i: (i, h)),
                   pl.BlockSpec((1, 1, 1, tq), lambda h, i: (h, i, 0, 0))],
        out_shape=[jax.ShapeDtypeStruct((T, H * V_HEAD), BF16),
                   jax.ShapeDtypeStruct((H, nq, 1, tq), F32)],
        compiler_params=_params("parallel", "arbitrary"),
    )(q, k, v)


def attn_delta(do, o, *, tq):
    T = do.shape[0]
    tq = min(tq, T)
    nq = T // tq

    def body(do_ref, o_ref, d_ref):
        prod = do_ref[...].astype(F32) * o_ref[...].astype(F32)
        d_ref[0, 0] = jnp.sum(jnp.transpose(prod), axis=0, keepdims=True)

    return pl.pallas_call(
        body, name="attn_delta", grid=(N_HEADS, nq),
        in_specs=[pl.BlockSpec((tq, V_HEAD), lambda h, i: (i, h)),
                  pl.BlockSpec((tq, V_HEAD), lambda h, i: (i, h))],
        out_specs=pl.BlockSpec((1, 1, 1, tq), lambda h, i: (h, i, 0, 0)),
        out_shape=jax.ShapeDtypeStruct((N_HEADS, nq, 1, tq), F32),
        compiler_params=_params("parallel", "parallel"),
    )(do, o)


def flash_bwd(q, k, v, do, lse, delta, *, tq):
    H, T, _ = q.shape
    tq = min(tq, T)
    nq = T // tq
    neg = -1e30

    def body(q_ref, do_ref, lse_ref, dl_ref, k_ref, v_ref, dq_ref, dk_ref, dv_ref, dq_acc):
        j = pl.program_id(1)

        @pl.when(j == 0)
        def _():
            dq_acc[...] = jnp.zeros_like(dq_acc)

        kb = k_ref[0]
        vb = v_ref[0]

        def tile(i, carry, masked):
            dk, dv = carry
            rows = pl.ds(pl.multiple_of(i * tq, tq), tq)
            qb = q_ref[0, rows, :]
            dob = do_ref[rows, :]
            st = _dot_nt(kb, qb)
            if masked:
                r = lax.broadcasted_iota(jnp.int32, (tq, tq), 0)
                c = lax.broadcasted_iota(jnp.int32, (tq, tq), 1)
                st = jnp.where(r <= c, st, neg)
            pt = jnp.exp(st - lse_ref[0, i])
            dv = dv + _dot(pt.astype(BF16), dob)
            dpt = _dot_nt(vb, dob)
            dst = (pt * (dpt - dl_ref[0, i])).astype(BF16)
            dk = dk + _dot(dst, qb)
            dq_acc[rows, :] += _dot_tn(dst, kb)
            return dk, dv

        carry = tile(j, (jnp.zeros((tq, HEAD_PAD), F32), jnp.zeros((tq, V_HEAD), F32)), True)
        dk, dv = lax.fori_loop(j + 1, nq, lambda i, cr: tile(i, cr, False), carry)
        dk_ref[0] = dk.astype(BF16)
        dv_ref[0] = dv.astype(BF16)

        @pl.when(j == nq - 1)
        def _():
            dq_ref[0] = dq_acc[...].astype(BF16)

    return pl.pallas_call(
        body, name="flash_bwd", grid=(H, nq),
        in_specs=[pl.BlockSpec((1, T, HEAD_PAD), lambda h, j: (h, 0, 0)),
                  pl.BlockSpec((T, V_HEAD), lambda h, j: (0, h)),
                  pl.BlockSpec((1, nq, 1, tq), lambda h, j: (h, 0, 0, 0)),
                  pl.BlockSpec((1, nq, 1, tq), lambda h, j: (h, 0, 0, 0)),
                  pl.BlockSpec((1, tq, HEAD_PAD), lambda h, j: (h, j, 0)),
                  pl.BlockSpec((1, tq, V_HEAD), lambda h, j: (h, j, 0))],
        out_specs=[pl.BlockSpec((1, T, HEAD_PAD), lambda h, j: (h, 0, 0)),
                   pl.BlockSpec((1, tq, HEAD_PAD), lambda h, j: (h, j, 0)),
                   pl.BlockSpec((1, tq, V_HEAD), lambda h, j: (h, j, 0))],
        out_shape=[jax.ShapeDtypeStruct((H, T, HEAD_PAD), BF16),
                   jax.ShapeDtypeStruct((H, T, HEAD_PAD), BF16),
                   jax.ShapeDtypeStruct((H, T, V_HEAD), BF16)],
        scratch_shapes=[pltpu.VMEM((T, HEAD_PAD), F32)],
        compiler_params=_params("parallel", "arbitrary"),
    )(q, do, lse, delta, k, v)


def mla_bwd(dq, dk, dv, z, cos_t, s1_t, s2_t, q_norm, kv_norm, w_uq, w_ukv, *, tm):
    T = z.shape[0]
    tm = min(tm, T)
    HW = N_HEADS * HEAD_PAD

    def body(dq_ref, dk_ref, dv_ref, cq_ref, ckv_ref, c_ref, s1_ref, s2_ref, qg_ref, kvg_ref,
             wuq_ref, wukv_ref, dcq_ref, dckv_ref, dkr_ref, dqf_ref, dkvf_ref, qn_ref, kvn_ref,
             dqg_ref, dkvg_ref):
        @pl.when(pl.program_id(0) == 0)
        def _():
            dqg_ref[...] = jnp.zeros_like(dqg_ref)
            dkvg_ref[...] = jnp.zeros_like(dkvg_ref)

        c, s1, s2 = c_ref[...], s1_ref[...], s2_ref[...]
        dkpe = jnp.zeros((tm, LANES), F32)
        for h in range(N_HEADS):
            o = h * HEAD_PAD
            dqf_ref[:, o:o + QK_NOPE] = (
                dq_ref[h, :, 0:QK_NOPE].astype(F32) * ATTN_SCALE).astype(BF16)
            dqf_ref[:, o + QK_NOPE:o + HEAD_PAD] = (
                _rope(dq_ref[h, :, QK_NOPE:HEAD_PAD].astype(F32), c, s1, s2, -1.0)
                * ATTN_SCALE).astype(BF16)
            dkvf_ref[:, o:o + QK_NOPE] = dk_ref[h, :, 0:QK_NOPE]
            dkvf_ref[:, o + QK_NOPE:o + HEAD_PAD] = dv_ref[h]
            dkpe = dkpe + dk_ref[h, :, QK_NOPE:HEAD_PAD].astype(F32)
        dkr_ref[...] = _rope(dkpe, c, s1, s2, -1.0).astype(BF16)

        cq = cq_ref[...]
        qn_ref[...] = _rms_fwd(cq, qg_ref[...]).astype(BF16)
        dcq, dgrows = _rms_bwd(cq, qg_ref[...], _dot_nt(dqf_ref[...], wuq_ref[...]))
        dcq_ref[...] = dcq.astype(BF16)
        _accum_rows(dqg_ref, dgrows)

        ckv = ckv_ref[...]
        kvn_ref[...] = _rms_fwd(ckv, kvg_ref[...]).astype(BF16)
        dckv, dgrows = _rms_bwd(ckv, kvg_ref[...], _dot_nt(dkvf_ref[...], wukv_ref[...]))
        dckv_ref[...] = dckv.astype(BF16)
        _accum_rows(dkvg_ref, dgrows)

    tok = lambda w: pl.BlockSpec((tm, w), lambda i: (i, 0))
    head = lambda w: pl.BlockSpec((N_HEADS, tm, w), lambda i: (0, i, 0))
    return pl.pallas_call(
        body, name="mla_bwd", grid=(T // tm,),
        in_specs=[head(HEAD_PAD), head(HEAD_PAD), head(V_HEAD),
                  pl.BlockSpec((tm, Q_LORA), lambda i: (i, OFF_CQ // Q_LORA)),
                  pl.BlockSpec((tm, KV_LORA), lambda i: (i, OFF_CKV // KV_LORA)),
                  tok(LANES), tok(LANES), tok(LANES),
                  _full((1, Q_LORA)), _full((1, KV_LORA)),
                  _full((Q_LORA, HW)), _full((KV_LORA, HW))],
        out_specs=[tok(Q_LORA), tok(KV_LORA), tok(LANES), tok(HW), tok(HW), tok(Q_LORA),
                   tok(KV_LORA), _full((1, Q_LORA)), _full((1, KV_LORA))],
        out_shape=[jax.ShapeDtypeStruct((T, Q_LORA), BF16),
                   jax.ShapeDtypeStruct((T, KV_LORA), BF16),
                   jax.ShapeDtypeStruct((T, LANES), BF16),
                   jax.ShapeDtypeStruct((T, HW), BF16),
                   jax.ShapeDtypeStruct((T, HW), BF16),
                   jax.ShapeDtypeStruct((T, Q_LORA), BF16),
                   jax.ShapeDtypeStruct((T, KV_LORA), BF16),
                   jax.ShapeDtypeStruct((1, Q_LORA), F32),
                   jax.ShapeDtypeStruct((1, KV_LORA), F32)],
        compiler_params=_params("arbitrary"),
    )(dq, dk, dv, z, z, cos_t, s1_t, s2_t, q_norm, kv_norm, w_uq, w_ukv)


def merge_fwd(o, ya, z, x, w_b, w_o, *, tm):
    T = x.shape[0]
    tm = min(tm, T)

    def body(o_ref, ya_ref, ga_ref, gb_ref, x_ref, wb_ref, wo_ref, yb_ref, mg_ref, x1_ref):
        yb = _dot(o_ref[...], wb_ref[...])
        yb_ref[...] = yb
        merged = (_sigmoid(ga_ref[...]) * ya_ref[...] + _sigmoid(gb_ref[...]) * yb).astype(BF16)
        mg_ref[...] = merged
        x1_ref[...] = x_ref[...] + _dot(merged, wo_ref[...])

    tok = pl.BlockSpec((tm, D_MODEL), lambda i: (i, 0))
    return pl.pallas_call(
        body, name="merge_fwd", grid=(T // tm,),
        in_specs=[tok, tok, pl.BlockSpec((tm, D_MODEL), lambda i: (i, OFF_GA // D_MODEL)),
                  pl.BlockSpec((tm, D_MODEL), lambda i: (i, OFF_GB // D_MODEL)), tok,
                  _full((D_MODEL, D_MODEL)), _full((D_MODEL, D_MODEL))],
        out_specs=[tok, tok, tok],
        out_shape=[jax.ShapeDtypeStruct((T, D_MODEL), F32),
                   jax.ShapeDtypeStruct((T, D_MODEL), BF16),
                   jax.ShapeDtypeStruct((T, D_MODEL), F32)],
        compiler_params=_params("parallel"),
    )(o, ya, z, z, x, w_b, w_o)


def merge_bwd(dx1, ya, yb, z, w_o, w_b, *, tm):
    T = dx1.shape[0]
    tm = min(tm, T)

    def body(dx_ref, ya_ref, yb_ref, ga_ref, gb_ref, wo_ref, wb_ref, dya_ref, dyb_ref, do_ref,
             dga_ref, dgb_ref):
        dm = _dot_nt(dx_ref[...].astype(BF16), wo_ref[...])
        sa = _sigmoid(ga_ref[...])
        sb = _sigmoid(gb_ref[...])
        dya_ref[...] = (dm * sa).astype(BF16)
        dyb = (dm * sb).astype(BF16)
        dyb_ref[...] = dyb
        dga_ref[...] = (dm * ya_ref[...] * sa * (1.0 - sa)).astype(BF16)
        dgb_ref[...] = (dm * yb_ref[...] * sb * (1.0 - sb)).astype(BF16)
        do_ref[...] = _dot_nt(dyb, wb_ref[...]).astype(BF16)

    tok = pl.BlockSpec((tm, D_MODEL), lambda i: (i, 0))
    return pl.pallas_call(
        body, name="merge_bwd", grid=(T // tm,),
        in_specs=[tok, tok, tok, pl.BlockSpec((tm, D_MODEL), lambda i: (i, OFF_GA // D_MODEL)),
                  pl.BlockSpec((tm, D_MODEL), lambda i: (i, OFF_GB // D_MODEL)),
                  _full((D_MODEL, D_MODEL)), _full((D_MODEL, D_MODEL))],
        out_specs=[tok] * 5,
        out_shape=[jax.ShapeDtypeStruct((T, D_MODEL), BF16)] * 5,
        compiler_params=_params("parallel"),
    )(dx1, ya, yb, z, z, w_o, w_b)


def in_bwd(dx1, x, g, segs, w_in, *, tm):
    T = x.shape[0]
    tm = min(tm, T)
    widths = [s.shape[1] for s, _ in segs]
    offs = [o for _, o in segs]
    n = len(segs)

    def body(*refs):
        dx1_ref, x_ref, g_ref = refs[:3]
        seg_refs = refs[3:3 + n]
        w_ref = refs[3 + n]
        dx_ref, h_ref, dg_ref = refs[4 + n:]

        @pl.when(pl.program_id(0) == 0)
        def _():
            dg_ref[...] = jnp.zeros_like(dg_ref)

        dh = jnp.zeros((tm, D_MODEL), F32)
        for s_ref, off, w in zip(seg_refs, offs, widths):
            dh = dh + _dot_nt(s_ref[...], w_ref[:, off:off + w])
        xv = x_ref[...]
        h_ref[...] = _rms_fwd(xv, g_ref[...]).astype(BF16)
        dxn, dgrows = _rms_bwd(xv, g_ref[...], dh)
        dx_ref[...] = dx1_ref[...] + dxn
        _accum_rows(dg_ref, dgrows)

    tok = pl.BlockSpec((tm, D_MODEL), lambda i: (i, 0))
    return pl.pallas_call(
        body, name="in_bwd", grid=(T // tm,),
        in_specs=[tok, tok, _full((1, D_MODEL))]
        + [pl.BlockSpec((tm, w), lambda i: (i, 0)) for w in widths] + [_full((D_MODEL, IN_PAD))],
        out_specs=[tok, tok, _full((1, D_MODEL))],
        out_shape=[jax.ShapeDtypeStruct((T, D_MODEL), F32),
                   jax.ShapeDtypeStruct((T, D_MODEL), BF16),
                   jax.ShapeDtypeStruct((1, D_MODEL), F32)],
        compiler_params=_params("arbitrary"),
    )(dx1, x, g, *[s for s, _ in segs], w_in)


def ffn_fwd(x1, g, w_gate, w_up, w_down, *, tm):
    T = x1.shape[0]
    tm = min(tm, T)

    def body(x_ref, g_ref, wg_ref, wu_ref, wd_ref, gp_ref, up_ref, x2_ref, h_ref, acc_ref):
        f = pl.program_id(1)

        @pl.when(f == 0)
        def _():
            h_ref[...] = _rms_fwd(x_ref[...], g_ref[...]).astype(BF16)
            acc_ref[...] = jnp.zeros_like(acc_ref)

        gp = _dot(h_ref[...], wg_ref[0])
        up = _dot(h_ref[...], wu_ref[0])
        gp_ref[0] = gp
        up_ref[0] = up
        act = (gp * _sigmoid(gp) * up).astype(BF16)
        acc_ref[...] += _dot(act, wd_ref[0])

        @pl.when(f == N_DEV - 1)
        def _():
            x2_ref[...] = x_ref[...] + acc_ref[...]

    tok = pl.BlockSpec((tm, D_MODEL), lambda i, f: (i, 0))
    ff = pl.BlockSpec((1, tm, FF_PIECE), lambda i, f: (f, i, 0))
    w_col = pl.BlockSpec((1, D_MODEL, FF_PIECE), lambda i, f: (f, 0, 0))
    w_row = pl.BlockSpec((1, FF_PIECE, D_MODEL), lambda i, f: (f, 0, 0))
    return pl.pallas_call(
        body, name="ffn_fwd", grid=(T // tm, N_DEV),
        in_specs=[tok, _full((1, D_MODEL)), w_col, w_col, w_row],
        out_specs=[ff, ff, tok],
        out_shape=[jax.ShapeDtypeStruct((N_DEV, T, FF_PIECE), F32),
                   jax.ShapeDtypeStruct((N_DEV, T, FF_PIECE), F32),
                   jax.ShapeDtypeStruct((T, D_MODEL), F32)],
        scratch_shapes=[pltpu.VMEM((tm, D_MODEL), BF16), pltpu.VMEM((tm, D_MODEL), F32)],
        compiler_params=_params("parallel", "arbitrary"),
    )(x1, g, w_gate, w_up, w_down)


def ffn_bwd(dx2, x1, g, gpre, upre, w_gate, w_up, w_down, *, tm):
    T = x1.shape[0]
    tm = min(tm, T)

    def body(dx_ref, x_ref, g_ref, gp_ref, up_ref, wg_ref, wu_ref, wd_ref,
             act_ref, dg_ref, du_ref, h_ref, dx1_ref, dgain_ref, dxb_ref, acc_ref):
        f = pl.program_id(1)

        @pl.when((pl.program_id(0) == 0) & (f == 0))
        def _():
            dgain_ref[...] = jnp.zeros_like(dgain_ref)

        @pl.when(f == 0)
        def _():
            dxb_ref[...] = dx_ref[...].astype(BF16)
            acc_ref[...] = jnp.zeros_like(acc_ref)

        gp = gp_ref[0]
        up = up_ref[0]
        dact = _dot_nt(dxb_ref[...], wd_ref[0])
        sg = _sigmoid(gp)
        silu = gp * sg
        act_ref[0] = (silu * up).astype(BF16)
        dgp = (dact * up * (sg * (1.0 + gp * (1.0 - sg)))).astype(BF16)
        dup = (dact * silu).astype(BF16)
        dg_ref[0] = dgp
        du_ref[0] = dup
        acc_ref[...] += _dot_nt(dgp, wg_ref[0]) + _dot_nt(dup, wu_ref[0])

        @pl.when(f == N_DEV - 1)
        def _():
            xv = x_ref[...]
            h_ref[...] = _rms_fwd(xv, g_ref[...]).astype(BF16)
            dxn, dgrows = _rms_bwd(xv, g_ref[...], acc_ref[...])
            dx1_ref[...] = dx_ref[...] + dxn
            _accum_rows(dgain_ref, dgrows)

    tok = pl.BlockSpec((tm, D_MODEL), lambda i, f: (i, 0))
    ff = pl.BlockSpec((1, tm, FF_PIECE), lambda i, f: (f, i, 0))
    w_col = pl.BlockSpec((1, D_MODEL, FF_PIECE), lambda i, f: (f, 0, 0))
    w_row = pl.BlockSpec((1, FF_PIECE, D_MODEL), lambda i, f: (f, 0, 0))
    return pl.pallas_call(
        body, name="ffn_bwd", grid=(T // tm, N_DEV),
        in_specs=[tok, tok, _full((1, D_MODEL)), ff, ff, w_col, w_col, w_row],
        out_specs=[ff, ff, ff, tok, tok, _full((1, D_MODEL))],
        out_shape=[jax.ShapeDtypeStruct((N_DEV, T, FF_PIECE), BF16)] * 3
        + [jax.ShapeDtypeStruct((T, D_MODEL), BF16), jax.ShapeDtypeStruct((T, D_MODEL), F32),
           jax.ShapeDtypeStruct((1, D_MODEL), F32)],
        scratch_shapes=[pltpu.VMEM((tm, D_MODEL), BF16), pltpu.VMEM((tm, D_MODEL), F32)],
        compiler_params=_params("arbitrary", "arbitrary"),
    )(dx2, x1, g, gpre, upre, w_gate, w_up, w_down)


def ple_fwd(x2, p, g, w_pg, w_ple, *, tm):
    T = x2.shape[0]
    tm = min(tm, T)

    def body(x_ref, p_ref, g_ref, wpg_ref, wple_ref, l_ref, pe_ref, x3_ref):
        xv = x_ref[...]
        logits = _dot(_rms_fwd(xv, g_ref[...]).astype(BF16), wpg_ref[...])
        pe = _dot(p_ref[...].astype(BF16), wple_ref[...])
        l_ref[...] = logits
        pe_ref[...] = pe
        x3_ref[...] = xv + _sigmoid(logits) * pe

    tok = pl.BlockSpec((tm, D_MODEL), lambda i: (i, 0))
    return pl.pallas_call(
        body, name="ple_fwd", grid=(T // tm,),
        in_specs=[tok, pl.BlockSpec((tm, PLE_DIM), lambda i: (i, 0)), _full((1, D_MODEL)),
                  _full((D_MODEL, D_MODEL)), _full((PLE_DIM, D_MODEL))],
        out_specs=[tok, tok, tok],
        out_shape=[jax.ShapeDtypeStruct((T, D_MODEL), F32)] * 3,
        compiler_params=_params("parallel"),
    )(x2, p, g, w_pg, w_ple)


def ple_bwd(dx3, logits, pe, x2, g, w_pg, *, tm):
    T = x2.shape[0]
    tm = min(tm, T)

    def body(dx_ref, l_ref, pe_ref, x_ref, g_ref, wpg_ref, dl_ref, dpe_ref, h_ref, dx2_ref,
             dg_ref):
        @pl.when(pl.program_id(0) == 0)
        def _():
            dg_ref[...] = jnp.zeros_like(dg_ref)

        dx = dx_ref[...]
        s = _sigmoid(l_ref[...])
        dpe_ref[...] = (dx * s).astype(BF16)
        dl = (dx * pe_ref[...] * s * (1.0 - s)).astype(BF16)
        dl_ref[...] = dl
        xv = x_ref[...]
        h_ref[...] = _rms_fwd(xv, g_ref[...]).astype(BF16)
        dxn, dgrows = _rms_bwd(xv, g_ref[...], _dot_nt(dl, wpg_ref[...]))
        dx2_ref[...] = dx + dxn
        _accum_rows(dg_ref, dgrows)

    tok = pl.BlockSpec((tm, D_MODEL), lambda i: (i, 0))
    return pl.pallas_call(
        body, name="ple_bwd", grid=(T // tm,),
        in_specs=[tok, tok, tok, tok, _full((1, D_MODEL)), _full((D_MODEL, D_MODEL))],
        out_specs=[tok, tok, tok, tok, _full((1, D_MODEL))],
        out_shape=[jax.ShapeDtypeStruct((T, D_MODEL), BF16)] * 3
        + [jax.ShapeDtypeStruct((T, D_MODEL), F32), jax.ShapeDtypeStruct((1, D_MODEL), F32)],
        compiler_params=_params("arbitrary"),
    )(dx3, logits, pe, x2, g, w_pg)


def loss_head(x, g, target, *, tm):
    T = x.shape[0]
    tm = min(tm, T)

    def body(x_ref, g_ref, t_ref, loss_ref, dx_ref, dg_ref):
        @pl.when(pl.program_id(0) == 0)
        def _():
            loss_ref[...] = jnp.zeros_like(loss_ref)
            dg_ref[...] = jnp.zeros_like(dg_ref)

        xv = x_ref[...]
        err = _rms_fwd(xv, g_ref[...]) - t_ref[...]
        part = 0.5 * jnp.sum(jnp.mean(err * err, axis=-1, keepdims=True), axis=0, keepdims=True)
        lane = lax.broadcasted_iota(jnp.int32, (1, LANES), 1)
        loss_ref[...] += jnp.where(lane == 0, part, 0.0)
        dxn, dgrows = _rms_bwd(xv, g_ref[...], err * (1.0 / D_MODEL))
        dx_ref[...] = dxn
        _accum_rows(dg_ref, dgrows)

    tok = pl.BlockSpec((tm, D_MODEL), lambda i: (i, 0))
    return pl.pallas_call(
        body, name="loss_head", grid=(T // tm,),
        in_specs=[tok, _full((1, D_MODEL)), tok],
        out_specs=[_full((1, LANES)), tok, _full((1, D_MODEL))],
        out_shape=[jax.ShapeDtypeStruct((1, LANES), F32), jax.ShapeDtypeStruct((T, D_MODEL), F32),
                   jax.ShapeDtypeStruct((1, D_MODEL), F32)],
        compiler_params=_params("arbitrary"),
    )(x, g, target)


def _peers():
    x, y, c = lax.axis_index("x"), lax.axis_index("y"), lax.axis_index("c")
    me = 4 * x + 2 * y + c
    out = []
    for d in range(1, N_DEV):
        px = 1 - x if d & 4 else x
        py = 1 - y if d & 2 else y
        pc = 1 - c if d & 1 else c
        out.append(((px, py, pc), 4 * px + 2 * py + pc))
    return me, out


def exchange(gathers, scatters, *, name):
    ng = len(gathers)
    ins = list(gathers) + [a for pair in scatters for a in pair]
    out_shapes = [jax.ShapeDtypeStruct((N_DEV,) + a.shape, a.dtype) for a in gathers]
    out_shapes += [jax.ShapeDtypeStruct((N_DEV, 2) + a.shape[1:], a.dtype) for a, _ in scatters]
    n_moves = ng + 2 * len(scatters)

    def body(*refs):
        in_refs, out_refs = refs[:len(ins)], refs[len(ins):-3]
        send_sems, recv_sems, local_sems = refs[-3:]
        me, peers = _peers()
        moves = [(lambda idx, s=in_refs[t]: s, out_refs[t].at[me]) for t in range(ng)]
        for u in range(len(scatters)):
            for layer in range(2):
                s_ref = in_refs[ng + 2 * u + layer]
                moves.append((lambda idx, s=s_ref: s.at[idx], out_refs[ng + u].at[me, layer]))
        started = []
        for t, (src, dst) in enumerate(moves):
            local = pltpu.make_async_copy(src(me), dst, local_sems.at[t])
            local.start()
            started.append(local)
            for d, (pos, idx) in enumerate(peers):
                cp = pltpu.make_async_remote_copy(
                    src_ref=src(idx), dst_ref=dst,
                    send_sem=send_sems.at[t * (N_DEV - 1) + d],
                    recv_sem=recv_sems.at[t * (N_DEV - 1) + d],
                    device_id=pos, device_id_type=pl.DeviceIdType.MESH)
                cp.start()
                started.append(cp)
        for cp in started:
            cp.wait()

    return pl.pallas_call(
        body, name=name,
        in_specs=[pl.BlockSpec(memory_space=pl.ANY)] * len(ins),
        out_specs=[pl.BlockSpec(memory_space=pl.ANY)] * len(out_shapes),
        out_shape=out_shapes,
        scratch_shapes=[pltpu.SemaphoreType.DMA((n_moves * (N_DEV - 1),)),
                        pltpu.SemaphoreType.DMA((n_moves * (N_DEV - 1),)),
                        pltpu.SemaphoreType.DMA((n_moves,))],
        compiler_params=pltpu.CompilerParams(has_side_effects=True),
    )(*ins)


def _row_block(rows, cols):
    best = None
    for rb in range(16, rows + 1, 16):
        if rows % rb == 0 and rb * cols <= 256 * 1024:
            best = rb
    return best or rows


def reduce_adamw(parts, w, m, v, *, name):
    R, C = w.shape
    cp = parts.shape[2]
    rb = _row_block(R, cp)

    def body(p_ref, w_ref, m_ref, v_ref, g_ref, d_ref, m2_ref, v2_ref):
        g = p_ref[0, :, 0:C].astype(F32)
        for k in range(1, N_DEV):
            g = g + p_ref[k, :, 0:C].astype(F32)
        g_ref[...] = g
        m2 = ADAM_B1 * m_ref[...] + (1.0 - ADAM_B1) * g
        v2 = ADAM_B2 * v_ref[...] + (1.0 - ADAM_B2) * (g * g)
        m2_ref[...] = m2
        v2_ref[...] = v2
        m_hat = m2 / (1.0 - ADAM_B1 ** ADAM_STEP)
        v_hat = v2 / (1.0 - ADAM_B2 ** ADAM_STEP)
        d_ref[...] = -ADAM_LR * (m_hat / (jnp.sqrt(v_hat) + ADAM_EPS) + ADAM_WD * w_ref[...])

    blk = pl.BlockSpec((rb, C), lambda i: (i, 0))
    return pl.pallas_call(
        body, name=name, grid=(R // rb,),
        in_specs=[pl.BlockSpec((N_DEV, rb, cp), lambda i: (0, i, 0)), blk, blk, blk],
        out_specs=[blk] * 4,
        out_shape=[jax.ShapeDtypeStruct((R, C), F32)] * 4,
        compiler_params=_params("parallel"),
    )(parts, w, m, v)


def _cols_to_pieces(full):
    r = full.shape[0]
    return jnp.transpose(full.reshape(r, N_DEV, -1), (1, 0, 2))


def _pieces_to_cols(pieces):
    r = pieces.shape[1]
    return jnp.transpose(pieces, (1, 0, 2)).reshape(r, -1)


def _rope_tables(positions):
    inv_freq = 1.0 / (ROPE_THETA ** (jnp.arange(0, QK_ROPE, 2, dtype=F32) / QK_ROPE))
    ang = positions.astype(F32)[:, None] * inv_freq
    cos, sin = jnp.cos(ang), jnp.sin(ang)
    zero = jnp.zeros_like(cos)
    return (jnp.concatenate([cos, cos, zero, zero], axis=1),
            jnp.concatenate([-sin, zero, zero, zero], axis=1),
            jnp.concatenate([zero, sin, zero, zero], axis=1))


def kernel(x, p, positions, norm_mix, w_in, w_pool, pool_scale, q_norm, kv_norm, w_uq, w_ukv, w_a, w_b, w_o, norm_ffn, w_gate, w_up, w_down, norm_ple, w_ple_gate, w_ple, final_norm, loss_target, m_norm_mix, m_w_in, m_w_pool, m_pool_scale, m_q_norm, m_kv_norm, m_w_uq, m_w_ukv, m_w_a, m_w_b, m_w_o, m_norm_ffn, m_w_gate, m_w_up, m_w_down, m_norm_ple, m_w_ple_gate, m_w_ple, m_final_norm, v_norm_mix, v_w_in, v_w_pool, v_pool_scale, v_q_norm, v_kv_norm, v_w_uq, v_w_ukv, v_w_a, v_w_b, v_w_o, v_norm_ffn, v_w_gate, v_w_up, v_w_down, v_norm_ple, v_w_ple_gate, v_w_ple, v_final_norm):
    given = dict(locals())
    shard = {n: given[n] for n in WEIGHTS}
    TM = 512
    TQ = 512

    def send_form(n, i):
        w = shard[n][i].astype(BF16)
        if n == "w_in":
            return jnp.pad(w, ((0, 0), (0, PIECE_PAD - W_IN_PIECE)))
        if n == "w_uq":
            w = jnp.pad(w, ((0, 0), (0, 0), (0, HEAD_PAD - QK_HEAD)))
        if n in ("w_uq", "w_ukv"):
            return w.reshape(-1, N_HEADS * HEAD_PAD)
        return w

    names = [n for n, _ in SHARDED]
    gathered = exchange([send_form(n, i) for i in range(DEPTH) for n in names], [],
                        name="gather_weights")
    row_map = jnp.asarray(_piece_row_map())

    low_np, up_np = _band_matrices()
    low, up = jnp.asarray(low_np, BF16), jnp.asarray(up_np, BF16)
    low_t = jnp.asarray(low_np.transpose(0, 2, 1), BF16)
    up_t = jnp.asarray(up_np.transpose(0, 2, 1), BF16)
    cos_t, s1_t, s2_t = _rope_tables(positions[0])

    layers = []
    for i in range(DEPTH):
        g = dict(zip(names, gathered[i * len(names):(i + 1) * len(names)]))
        layers.append(dict(
            w_in=assemble_w_in(g["w_in"], row_map, tr=512),
            w_uq=g["w_uq"].reshape(Q_LORA, N_HEADS * HEAD_PAD),
            w_ukv=g["w_ukv"].reshape(KV_LORA, N_HEADS * HEAD_PAD),
            w_a=_pieces_to_cols(g["w_a"]), w_b=g["w_b"].reshape(D_MODEL, D_MODEL),
            w_o=g["w_o"].reshape(D_MODEL, D_MODEL),
            w_gate=g["w_gate"], w_up=g["w_up"], w_down=g["w_down"],
            w_pg=g["w_ple_gate"].reshape(D_MODEL, D_MODEL), w_ple=_pieces_to_cols(g["w_ple"]),
            w_pool=w_pool[i].astype(BF16),
            pool_scale=pool_scale[i][None], norm_mix=norm_mix[i][None], q_norm=q_norm[i][None],
            kv_norm=kv_norm[i][None], norm_ffn=norm_ffn[i][None], norm_ple=norm_ple[i][None]))

    xs = x[0]
    saved = []
    for i in range(DEPTH):
        L = layers[i]
        z = rms_matmul(xs, L["norm_mix"], L["w_in"], tm=TM, tn=1152, name="in_proj")
        pooled, ms, ya = pool_fwd(z, low, up, L["w_pool"], L["pool_scale"], L["w_a"], tm=TM)
        q, k, v = mla_prep(z, cos_t, s1_t, s2_t, L["q_norm"], L["kv_norm"], L["w_uq"], L["w_ukv"],
                           tm=TM)
        o, lse = flash_fwd(q, k, v, tq=TQ)
        yb, merged, x1 = merge_fwd(o, ya, z, xs, L["w_b"], L["w_o"], tm=TM)
        gpre, upre, x2 = ffn_fwd(x1, L["norm_ffn"], L["w_gate"], L["w_up"], L["w_down"],
                                 tm=2 * TM)
        logits, pe, x3 = ple_fwd(x2, p[i, 0], L["norm_ple"], L["w_pg"], L["w_ple"], tm=TM)
        saved.append(dict(x=xs, z=z, pooled=pooled, ms=ms, ya=ya, q=q, k=k, v=v, o=o, lse=lse,
                          yb=yb, merged=merged, x1=x1, gpre=gpre, upre=upre, x2=x2,
                          logits=logits, pe=pe))
        xs = x3

    loss_part, dx, d_final = loss_head(xs, final_norm[None], loss_target[0], tm=TM)

    grads = {n: [None] * DEPTH for n in REPLICATED if n != "final_norm"}
    pieces = {n: [None] * DEPTH for n in names}
    tn_mm = functools.partial(matmul_tn, tn=1024, tk=1024)
    row_pieces = lambda a: a.reshape(N_DEV, -1, a.shape[-1])
    for i in reversed(range(DEPTH)):
        L, S = layers[i], saved[i]
        dl, dpe, hn, dx2, d_norm_ple = ple_bwd(dx, S["logits"], S["pe"], S["x2"], L["norm_ple"],
                                               L["w_pg"], tm=TM)
        grads["norm_ple"][i] = d_norm_ple[0]
        pieces["w_ple_gate"][i] = row_pieces(tn_mm(hn, dl, name="dw_ple_gate"))
        pieces["w_ple"][i] = _cols_to_pieces(tn_mm(p[i, 0], dpe, name="dw_ple"))

        act, dgp, dup, h2, dx1, d_norm_ffn = ffn_bwd(dx2, S["x1"], L["norm_ffn"], S["gpre"],
                                                     S["upre"], L["w_gate"], L["w_up"],
                                                     L["w_down"], tm=TM)
        grads["norm_ffn"][i] = d_norm_ffn[0]
        pieces["w_down"][i] = matmul_tn_pieces(act, dx2, tk=512, name="dw_down")
        pieces["w_gate"][i] = matmul_tn_pieces(h2, dgp, tk=512, name="dw_gate")
        pieces["w_up"][i] = matmul_tn_pieces(h2, dup, tk=512, name="dw_up")

        dya, dyb, do, dga, dgb = merge_bwd(dx1, S["ya"], S["yb"], S["z"], L["w_o"], L["w_b"],
                                           tm=TM)
        pieces["w_o"][i] = row_pieces(tn_mm(S["merged"], dx1, name="dw_o"))
        pieces["w_b"][i] = row_pieces(tn_mm(S["o"], dyb, name="dw_b"))

        delta = attn_delta(do, S["o"], tq=TQ)
        dq, dk, dv = flash_bwd(S["q"], S["k"], S["v"], do, S["lse"], delta, tq=TQ)
        dcq, dckv, dkr, dqf, dkvf, qn, kvn, d_q_norm, d_kv_norm = mla_bwd(
            dq, dk, dv, S["z"], cos_t, s1_t, s2_t, L["q_norm"], L["kv_norm"], L["w_uq"],
            L["w_ukv"], tm=TM)
        grads["q_norm"][i] = d_q_norm[0]
        grads["kv_norm"][i] = d_kv_norm[0]
        d_w_uq = tn_mm(qn, dqf, name="dw_uq")
        pieces["w_uq"][i] = d_w_uq.reshape(N_DEV, -1, N_HEADS, HEAD_PAD)[..., :QK_HEAD]
        pieces["w_ukv"][i] = tn_mm(kvn, dkvf, name="dw_ukv").reshape(
            N_DEV, -1, N_HEADS, QK_NOPE + V_HEAD)

        dpc, d_pool_scale, d_w_pool = pool_bwd_a(dya, S["pooled"], L["w_a"], L["w_pool"],
                                                 L["pool_scale"], tm=TM)
        grads["pool_scale"][i] = d_pool_scale[0]
        grads["w_pool"][i] = d_w_pool
        pieces["w_a"][i] = _cols_to_pieces(tn_mm(S["ms"], dya, name="dw_a"))
        du = pool_bwd_b(dpc, low_t, up_t, tm=TM)

        segs = [(dga, OFF_GA), (dgb, OFF_GB), (du, OFF_U), (dcq, OFF_CQ), (dckv, OFF_CKV),
                (dkr, OFF_KR)]
        dx, h, d_norm_mix = in_bwd(dx1, S["x"], L["norm_mix"], segs, L["w_in"], tm=TM)
        grads["norm_mix"][i] = d_norm_mix[0]
        pieces["w_in"][i] = split_dw_in(dw_in_proj(h, segs, tk=512), row_map)

    small = {n: jnp.stack(g) for n, g in grads.items()}
    small["final_norm"] = d_final[0]
    rep = jnp.concatenate([small[n].reshape(-1) for n in REPLICATED] + [loss_part.reshape(-1)])
    n_small = -(-rep.shape[0] // (8 * FLAT_COLS)) * (8 * FLAT_COLS)

    def small_flat(parts):
        flat = jnp.concatenate([a.reshape(-1) for a in parts])
        return jnp.pad(flat, (0, n_small - flat.shape[0])).reshape(-1, FLAT_COLS)

    received = exchange([small_flat([rep])], [tuple(pieces[n]) for n in names],
                        name="scatter_grads")

    results = {}
    for n, parts in zip(names, received[1:]):
        w = shard[n]
        rows2d = lambda a: a.reshape(-1, w.shape[-1])
        w2 = rows2d(w)
        out = reduce_adamw(parts.reshape(N_DEV, w2.shape[0], -1), w2, rows2d(given["m_" + n]),
                           rows2d(given["v_" + n]), name="adamw_" + n)
        results[n] = [a.reshape(w.shape) for a in out]

    out = reduce_adamw(received[0], *[small_flat([given[pre + n] for n in REPLICATED])
                                      for pre in ("", "m_", "v_")], name="adamw_small")
    loss = None
    for n in REPLICATED:
        results[n] = []
    for a in out:
        a = a.reshape(-1)
        off = 0
        for n in REPLICATED:
            size = int(np.prod(shard[n].shape))
            results[n].append(a[off:off + size].reshape(shard[n].shape))
            off += size
        if loss is None:
            loss = a[off]
    return (loss, dx[None], *[results[n][0] for n in WEIGHTS], *[results[n][1] for n in WEIGHTS],
            *[results[n][2] for n in WEIGHTS], *[results[n][3] for n in WEIGHTS])
```

```python
import functools

import numpy as np
import jax
import jax.numpy as jnp
from jax import lax
from jax.experimental import pallas as pl
from jax.experimental.pallas import tpu as pltpu

F32 = jnp.float32
BF16 = jnp.bfloat16

D_MODEL = 1024
DEPTH = 2
PLE_DIM = 256
POOL_WINDOWS = (2, 4, 8, 16)
POOL_GROUP = 128
POOL_WIDTH = 512
N_HEADS = 8
Q_LORA = 512
KV_LORA = 256
QK_NOPE = 128
QK_ROPE = 64
QK_HEAD = 192
V_HEAD = 128
HEAD_PAD = 256
D_FF = 2816
ROPE_THETA = 10000.0
EPS = 1e-6
IN_WIDTH = 3392
ATTN_SCALE = QK_HEAD ** -0.5

OFF_GA, OFF_GB, OFF_U, OFF_CQ, OFF_CKV, OFF_KR = 0, 1024, 2048, 2560, 3072, 3328
IN_PAD = 3456
W_IN_PIECE = IN_WIDTH // 8
PIECE_PAD = 512
FF_PIECE = D_FF // 8

ADAM_LR = 0.001
ADAM_B1 = 0.9
ADAM_B2 = 0.999
ADAM_EPS = 1e-08
ADAM_WD = 0.01
ADAM_STEP = 10

N_DEV = 8
LANES = 128
CHUNK = 128
VMEM_LIMIT = 56 * 1024 * 1024

SHARDED = (("w_in", 2), ("w_uq", 1), ("w_ukv", 1), ("w_a", 2), ("w_b", 1), ("w_o", 1),
           ("w_gate", 2), ("w_up", 2), ("w_down", 1), ("w_ple_gate", 1), ("w_ple", 2))
REPLICATED = ("norm_mix", "w_pool", "pool_scale", "q_norm", "kv_norm", "norm_ffn", "norm_ple",
              "final_norm")
WEIGHTS = ("norm_mix", "w_in", "w_pool", "pool_scale", "q_norm", "kv_norm", "w_uq", "w_ukv",
           "w_a", "w_b", "w_o", "norm_ffn", "w_gate", "w_up", "w_down", "norm_ple",
           "w_ple_gate", "w_ple", "final_norm")
FLAT_COLS = 1024
FLAT_ROW_BLOCK = 192


def _params(*sem):
    return pltpu.CompilerParams(dimension_semantics=sem, vmem_limit_bytes=VMEM_LIMIT)


def _dot(a, b):
    return jnp.dot(a, b, preferred_element_type=F32)


def _dot_nt(a, b):
    return lax.dot_general(a, b, (((1,), (1,)), ((), ())), preferred_element_type=F32)


def _dot_tn(a, b):
    return lax.dot_general(a, b, (((0,), (0,)), ((), ())), preferred_element_type=F32)


def _rms_fwd(x, g):
    r = lax.rsqrt(jnp.mean(x * x, axis=-1, keepdims=True) + EPS)
    return x * r * g


def _rms_bwd(x, g, dy):
    r = lax.rsqrt(jnp.mean(x * x, axis=-1, keepdims=True) + EPS)
    xr = x * r
    gy = dy * g
    dx = r * (gy - xr * jnp.mean(gy * xr, axis=-1, keepdims=True))
    return dx, dy * xr


def _sigmoid(x):
    return 1.0 / (1.0 + jnp.exp(-x))


def _accum_rows(ref, rows):
    ref[...] += jnp.sum(rows, axis=0, keepdims=True)


def _band(band, x):
    h1 = x.astype(BF16)
    r1 = x - h1.astype(F32)
    h2 = r1.astype(BF16)
    h3 = (r1 - h2.astype(F32)).astype(BF16)
    return _dot(band, h1) + _dot(band, h2) + _dot(band, h3)


def _rope(x, c, s1, s2, sign):
    return x * c + sign * (pltpu.roll(x, 96, 1) * s1 + pltpu.roll(x, 32, 1) * s2)


def _full(shape):
    n = len(shape)
    return pl.BlockSpec(shape, lambda *_: (0,) * n)


def matmul_tn(a, b, *, tn, tk, name, tm=1024):
    T, M = a.shape
    N = b.shape[1]
    tm, tn, tk = min(tm, M), min(tn, N), min(tk, T)
    nk = T // tk

    def body(a_ref, b_ref, o_ref, acc_ref):
        k = pl.program_id(2)

        @pl.when(k == 0)
        def _():
            acc_ref[...] = jnp.zeros_like(acc_ref)

        acc_ref[...] += _dot_tn(a_ref[...].astype(BF16), b_ref[...].astype(BF16))

        @pl.when(k == nk - 1)
        def _():
            o_ref[...] = acc_ref[...].astype(BF16)

    return pl.pallas_call(
        body, name=name, grid=(M // tm, N // tn, nk),
        in_specs=[pl.BlockSpec((tk, tm), lambda i, j, k: (k, i)),
                  pl.BlockSpec((tk, tn), lambda i, j, k: (k, j))],
        out_specs=pl.BlockSpec((tm, tn), lambda i, j, k: (i, j)),
        out_shape=jax.ShapeDtypeStruct((M, N), BF16),
        scratch_shapes=[pltpu.VMEM((tm, tn), F32)],
        compiler_params=_params("parallel", "parallel", "arbitrary"),
    )(a, b)


def matmul_tn_pieces(a, b, *, tk, name):
    a3 = a.ndim == 3
    T = a.shape[-2]
    m, n = a.shape[-1], b.shape[-1]
    tk = min(tk, T)
    nk = T // tk

    def body(a_ref, b_ref, o_ref, acc_ref):
        k = pl.program_id(0)

        @pl.when(k == 0)
        def _():
            acc_ref[...] = jnp.zeros_like(acc_ref)

        whole = (b_ref if a3 else a_ref)[...].astype(BF16)
        for j in range(N_DEV):
            if a3:
                acc_ref[j] += _dot_tn(a_ref[j].astype(BF16), whole)
            else:
                acc_ref[j] += _dot_tn(whole, b_ref[j].astype(BF16))

        @pl.when(k == nk - 1)
        def _():
            o_ref[...] = acc_ref[...].astype(BF16)

    piece = lambda w: pl.BlockSpec((N_DEV, tk, w), lambda k: (0, k, 0))
    whole = lambda w: pl.BlockSpec((tk, w), lambda k: (k, 0))
    return pl.pallas_call(
        body, name=name, grid=(nk,),
        in_specs=[piece(m) if a3 else whole(m), whole(n) if a3 else piece(n)],
        out_specs=_full((N_DEV, m, n)),
        out_shape=jax.ShapeDtypeStruct((N_DEV, m, n), BF16),
        scratch_shapes=[pltpu.VMEM((N_DEV, m, n), F32)],
        compiler_params=_params("arbitrary"),
    )(a, b)


def dw_in_proj(h, segs, *, tk):
    T = h.shape[0]
    tk = min(tk, T)
    nk = T // tk
    widths = [s.shape[1] for s, _ in segs]
    offs = [o for _, o in segs]

    def body(*refs):
        h_ref, seg_refs, o_ref, acc_ref = refs[0], refs[1:-2], refs[-2], refs[-1]
        k = pl.program_id(0)

        @pl.when(k == 0)
        def _():
            acc_ref[...] = jnp.zeros_like(acc_ref)

        hv = h_ref[...]
        for s_ref, off, w in zip(seg_refs, offs, widths):
            acc_ref[:, off:off + w] += _dot_tn(hv, s_ref[...])

        @pl.when(k == nk - 1)
        def _():
            o_ref[...] = acc_ref[...].astype(BF16)

    return pl.pallas_call(
        body, name="dw_in", grid=(nk,),
        in_specs=[pl.BlockSpec((tk, D_MODEL), lambda k: (k, 0))]
        + [pl.BlockSpec((tk, w), lambda k: (k, 0)) for w in widths],
        out_specs=_full((D_MODEL, IN_PAD)),
        out_shape=jax.ShapeDtypeStruct((D_MODEL, IN_PAD), BF16),
        scratch_shapes=[pltpu.VMEM((D_MODEL, IN_PAD), F32)],
        compiler_params=_params("arbitrary"),
    )(h, *[s for s, _ in segs])


def _piece_row_map():
    src = np.full(IN_PAD, -1, np.int64)
    for orig, pad, width in ((0, OFF_U, 512), (512, OFF_CQ, 512), (1024, OFF_CKV, 256),
                             (1280, OFF_KR, 64), (1344, OFF_GA, 1024), (2368, OFF_GB, 1024)):
        src[pad:pad + width] = np.arange(orig, orig + width)
    rows = np.where(src >= 0, (src // W_IN_PIECE) * PIECE_PAD + src % W_IN_PIECE, -1)
    return rows.astype(np.int32)[None, :]


def _selector(j, map_ref):
    rid = j * PIECE_PAD + lax.broadcasted_iota(jnp.int32, (PIECE_PAD, IN_PAD), 0)
    return jnp.where(rid == map_ref[...], 1.0, 0.0).astype(BF16)


def assemble_w_in(pieces, row_map, *, tr):
    def body(p_ref, map_ref, o_ref, acc_ref):
        j = pl.program_id(1)

        @pl.when(j == 0)
        def _():
            acc_ref[...] = jnp.zeros_like(acc_ref)

        acc_ref[...] += _dot(p_ref[0], _selector(j, map_ref))

        @pl.when(j == N_DEV - 1)
        def _():
            o_ref[...] = acc_ref[...].astype(BF16)

    return pl.pallas_call(
        body, name="assemble_w_in", grid=(D_MODEL // tr, N_DEV),
        in_specs=[pl.BlockSpec((1, tr, PIECE_PAD), lambda i, j: (j, i, 0)), _full((1, IN_PAD))],
        out_specs=pl.BlockSpec((tr, IN_PAD), lambda i, j: (i, 0)),
        out_shape=jax.ShapeDtypeStruct((D_MODEL, IN_PAD), BF16),
        scratch_shapes=[pltpu.VMEM((tr, IN_PAD), F32)],
        compiler_params=_params("parallel", "arbitrary"),
    )(pieces, row_map)


def split_dw_in(dwp, row_map):
    def body(d_ref, map_ref, o_ref):
        o_ref[0] = _dot_nt(d_ref[...], _selector(pl.program_id(0), map_ref)).astype(BF16)

    return pl.pallas_call(
        body, name="split_dw_in", grid=(N_DEV,),
        in_specs=[_full((D_MODEL, IN_PAD)), _full((1, IN_PAD))],
        out_specs=pl.BlockSpec((1, D_MODEL, PIECE_PAD), lambda j: (j, 0, 0)),
        out_shape=jax.ShapeDtypeStruct((N_DEV, D_MODEL, PIECE_PAD), BF16),
        compiler_params=_params("parallel"),
    )(dwp, row_map)


def rms_matmul(x, g, w, *, tm, tn, name):
    T, D = x.shape
    N = w.shape[1]
    tm = min(tm, T)

    def body(x_ref, g_ref, w_ref, o_ref, h_ref):
        @pl.when(pl.program_id(1) == 0)
        def _():
            h_ref[...] = _rms_fwd(x_ref[...], g_ref[...]).astype(BF16)

        o_ref[...] = _dot(h_ref[...], w_ref[...])

    return pl.pallas_call(
        body, name=name, grid=(T // tm, N // tn),
        in_specs=[pl.BlockSpec((tm, D), lambda i, j: (i, 0)), _full((1, D)),
                  pl.BlockSpec((D, tn), lambda i, j: (0, j))],
        out_specs=pl.BlockSpec((tm, tn), lambda i, j: (i, j)),
        out_shape=jax.ShapeDtypeStruct((T, N), F32),
        scratch_shapes=[pltpu.VMEM((tm, D), BF16)],
        compiler_params=_params("parallel", "arbitrary"),
    )(x, g, w)


def _band_matrices():
    s = np.arange(CHUNK)[:, None]
    t = np.arange(CHUNK)[None, :]
    low = np.stack([((s - t >= 0) & (s - t < w)) for w in POOL_WINDOWS]).astype(np.float32)
    up = np.stack([(t > s + CHUNK - w) for w in POOL_WINDOWS]).astype(np.float32)
    return low, up


def _window_count(row0, g):
    t = row0 + lax.broadcasted_iota(jnp.int32, (CHUNK, 1), 0)
    return jnp.minimum(t + 1, POOL_WINDOWS[g]).astype(F32)


def pool_fwd(z, low, up, w_pool, pool_scale, w_a, *, tm):
    T = z.shape[0]
    tm = min(tm, T)
    nch = tm // CHUNK
    ublk = OFF_U // POOL_WIDTH

    def body(u_ref, halo_ref, low_ref, up_ref, wp_ref, sc_ref, wa_ref, pooled_ref, ms_ref, ya_ref):
        i = pl.program_id(0)
        for c in range(nch):
            rows = slice(c * CHUNK, (c + 1) * CHUNK)
            for g in range(4):
                cols = slice(g * POOL_GROUP, (g + 1) * POOL_GROUP)
                cur = u_ref[rows, cols]
                if c == 0:
                    prev = jnp.where(i > 0, halo_ref[:, cols], 0.0)
                else:
                    prev = u_ref[(c - 1) * CHUNK:c * CHUNK, cols]
                s = _band(low_ref[g], cur) + _band(up_ref[g], prev)
                pooled = (s / _window_count(i * tm + c * CHUNK, g) - cur).astype(BF16)
                pooled_ref[rows, cols] = pooled
                ms_ref[rows, cols] = (_dot(pooled, wp_ref[g]) * sc_ref[:, cols]).astype(BF16)
        ya_ref[...] = _dot(ms_ref[...], wa_ref[...])

    return pl.pallas_call(
        body, name="pool_fwd", grid=(T // tm,),
        in_specs=[pl.BlockSpec((tm, POOL_WIDTH), lambda i: (i, ublk)),
                  pl.BlockSpec((CHUNK, POOL_WIDTH), lambda i: (jnp.maximum(i * nch - 1, 0), ublk)),
                  _full((4, CHUNK, CHUNK)), _full((4, CHUNK, CHUNK)),
                  _full((4, POOL_GROUP, POOL_GROUP)), _full((1, POOL_WIDTH)),
                  _full((POOL_WIDTH, D_MODEL))],
        out_specs=[pl.BlockSpec((tm, POOL_WIDTH), lambda i: (i, 0)),
                   pl.BlockSpec((tm, POOL_WIDTH), lambda i: (i, 0)),
                   pl.BlockSpec((tm, D_MODEL), lambda i: (i, 0))],
        out_shape=[jax.ShapeDtypeStruct((T, POOL_WIDTH), BF16),
                   jax.ShapeDtypeStruct((T, POOL_WIDTH), BF16),
                   jax.ShapeDtypeStruct((T, D_MODEL), F32)],
        compiler_params=_params("parallel"),
    )(z, z, low, up, w_pool, pool_scale, w_a)


def pool_bwd_a(dya, pooled, w_a, w_pool, pool_scale, *, tm):
    T = dya.shape[0]
    tm = min(tm, T)
    nch = tm // CHUNK

    def body(dya_ref, pooled_ref, wa_ref, wp_ref, sc_ref, dpc_ref, dsc_ref, dwp_ref):
        i = pl.program_id(0)

        @pl.when(i == 0)
        def _():
            dsc_ref[...] = jnp.zeros_like(dsc_ref)
            dwp_ref[...] = jnp.zeros_like(dwp_ref)

        dms = _dot_nt(dya_ref[...], wa_ref[...])
        for g in range(4):
            cols = slice(g * POOL_GROUP, (g + 1) * POOL_GROUP)
            pg = pooled_ref[:, cols]
            dmg = dms[:, cols]
            mixed = _dot(pg, wp_ref[g])
            dsc_ref[:, cols] += jnp.sum(dmg * mixed, axis=0, keepdims=True)
            dmixed = (dmg * sc_ref[:, cols]).astype(BF16)
            dwp_ref[g] += _dot_tn(pg, dmixed)
            dpooled = _dot_nt(dmixed, wp_ref[g])
            for c in range(nch):
                rows = slice(c * CHUNK, (c + 1) * CHUNK)
                dpc_ref[rows, cols] = dpooled[rows] / _window_count(i * tm + c * CHUNK, g)

    return pl.pallas_call(
        body, name="pool_bwd_a", grid=(T // tm,),
        in_specs=[pl.BlockSpec((tm, D_MODEL), lambda i: (i, 0)),
                  pl.BlockSpec((tm, POOL_WIDTH), lambda i: (i, 0)),
                  _full((POOL_WIDTH, D_MODEL)), _full((4, POOL_GROUP, POOL_GROUP)),
                  _full((1, POOL_WIDTH))],
        out_specs=[pl.BlockSpec((tm, POOL_WIDTH), lambda i: (i, 0)), _full((1, POOL_WIDTH)),
                   _full((4, POOL_GROUP, POOL_GROUP))],
        out_shape=[jax.ShapeDtypeStruct((T, POOL_WIDTH), F32),
                   jax.ShapeDtypeStruct((1, POOL_WIDTH), F32),
                   jax.ShapeDtypeStruct((4, POOL_GROUP, POOL_GROUP), F32)],
        compiler_params=_params("arbitrary"),
    )(dya, pooled, w_a, w_pool, pool_scale)


def pool_bwd_b(dpc, low_t, up_t, *, tm):
    T = dpc.shape[0]
    tm = min(tm, T)
    nch = tm // CHUNK
    last_chunk = T // CHUNK - 1

    def body(d_ref, halo_ref, low_ref, up_ref, du_ref):
        i = pl.program_id(0)
        for c in range(nch):
            rows = slice(c * CHUNK, (c + 1) * CHUNK)
            for g in range(4):
                cols = slice(g * POOL_GROUP, (g + 1) * POOL_GROUP)
                cur = d_ref[rows, cols]
                if c == nch - 1:
                    nxt = jnp.where(i < pl.num_programs(0) - 1, halo_ref[:, cols], 0.0)
                else:
                    nxt = d_ref[(c + 1) * CHUNK:(c + 2) * CHUNK, cols]
                s = _band(low_ref[g], cur) + _band(up_ref[g], nxt)
                du_ref[rows, cols] = (s - cur * _window_count(i * tm + c * CHUNK, g)).astype(BF16)

    return pl.pallas_call(
        body, name="pool_bwd_b", grid=(T // tm,),
        in_specs=[pl.BlockSpec((tm, POOL_WIDTH), lambda i: (i, 0)),
                  pl.BlockSpec((CHUNK, POOL_WIDTH),
                               lambda i: (jnp.minimum((i + 1) * nch, last_chunk), 0)),
                  _full((4, CHUNK, CHUNK)), _full((4, CHUNK, CHUNK))],
        out_specs=pl.BlockSpec((tm, POOL_WIDTH), lambda i: (i, 0)),
        out_shape=jax.ShapeDtypeStruct((T, POOL_WIDTH), BF16),
        compiler_params=_params("parallel"),
    )(dpc, dpc, low_t, up_t)


def mla_prep(z, cos_t, s1_t, s2_t, q_norm, kv_norm, w_uq, w_ukv, *, tm):
    T = z.shape[0]
    tm = min(tm, T)

    def body(cq_ref, ckv_ref, kr_ref, c_ref, s1_ref, s2_ref, qg_ref, kvg_ref, wuq_ref, wukv_ref,
             q_ref, k_ref, v_ref):
        c, s1, s2 = c_ref[...], s1_ref[...], s2_ref[...]
        qn = _rms_fwd(cq_ref[...], qg_ref[...]).astype(BF16)
        q = _dot(qn, wuq_ref[...])
        kvn = _rms_fwd(ckv_ref[...], kvg_ref[...]).astype(BF16)
        kv = _dot(kvn, wukv_ref[...])
        kpe = _rope(kr_ref[...], c, s1, s2, 1.0).astype(BF16)
        for h in range(N_HEADS):
            o = h * HEAD_PAD
            q_ref[h, :, 0:QK_NOPE] = (q[:, o:o + QK_NOPE] * ATTN_SCALE).astype(BF16)
            q_ref[h, :, QK_NOPE:HEAD_PAD] = (
                _rope(q[:, o + QK_NOPE:o + HEAD_PAD], c, s1, s2, 1.0) * ATTN_SCALE).astype(BF16)
            k_ref[h, :, 0:QK_NOPE] = kv[:, o:o + QK_NOPE].astype(BF16)
            k_ref[h, :, QK_NOPE:HEAD_PAD] = kpe
            v_ref[h] = kv[:, o + QK_NOPE:o + HEAD_PAD].astype(BF16)

    tok = lambda w: pl.BlockSpec((tm, w), lambda i: (i, 0))
    return pl.pallas_call(
        body, name="mla_prep", grid=(T // tm,),
        in_specs=[pl.BlockSpec((tm, Q_LORA), lambda i: (i, OFF_CQ // Q_LORA)),
                  pl.BlockSpec((tm, KV_LORA), lambda i: (i, OFF_CKV // KV_LORA)),
                  pl.BlockSpec((tm, LANES), lambda i: (i, OFF_KR // LANES)),
                  tok(LANES), tok(LANES), tok(LANES),
                  _full((1, Q_LORA)), _full((1, KV_LORA)),
                  _full((Q_LORA, N_HEADS * HEAD_PAD)), _full((KV_LORA, N_HEADS * HEAD_PAD))],
        out_specs=[pl.BlockSpec((N_HEADS, tm, HEAD_PAD), lambda i: (0, i, 0)),
                   pl.BlockSpec((N_HEADS, tm, HEAD_PAD), lambda i: (0, i, 0)),
                   pl.BlockSpec((N_HEADS, tm, V_HEAD), lambda i: (0, i, 0))],
        out_shape=[jax.ShapeDtypeStruct((N_HEADS, T, HEAD_PAD), BF16),
                   jax.ShapeDtypeStruct((N_HEADS, T, HEAD_PAD), BF16),
                   jax.ShapeDtypeStruct((N_HEADS, T, V_HEAD), BF16)],
        compiler_params=_params("parallel"),
    )(z, z, z, cos_t, s1_t, s2_t, q_norm, kv_norm, w_uq, w_ukv)


def _row_vector(col, n):
    return jnp.transpose(jnp.broadcast_to(col, (n, LANES)))[0:1, :]


def flash_fwd(q, k, v, *, tq):
    H, T, _ = q.shape
    tq = min(tq, T)
    nq = T // tq
    neg = -1e30

    def body(q_ref, k_ref, v_ref, o_ref, lse_ref):
        qi = pl.program_id(1)
        qb = q_ref[0]

        def tile(j, carry, masked):
            m, l, acc = carry
            rows = pl.ds(pl.multiple_of(j * tq, tq), tq)
            s = _dot_nt(qb, k_ref[0, rows, :])
            if masked:
                r = lax.broadcasted_iota(jnp.int32, (tq, tq), 0)
                c = lax.broadcasted_iota(jnp.int32, (tq, tq), 1)
                s = jnp.where(c <= r, s, neg)
            m_new = jnp.maximum(m, jnp.max(s, axis=-1, keepdims=True))
            alpha = jnp.exp(m - m_new)
            p = jnp.exp(s - m_new)
            l = alpha * l + jnp.sum(p, axis=-1, keepdims=True)
            acc = alpha * acc + _dot(p.astype(BF16), v_ref[0, rows, :])
            return m_new, l, acc

        init = (jnp.full((tq, 1), neg, F32), jnp.zeros((tq, 1), F32), jnp.zeros((tq, V_HEAD), F32))
        carry = lax.fori_loop(0, qi, lambda j, cr: tile(j, cr, False), init)
        m, l, acc = tile(qi, carry, True)
        o_ref[...] = (acc / l).astype(BF16)
        lse_ref[0, 0] = _row_vector(m + jnp.log(l), tq)

    return pl.pallas_call(
        body, name="flash_fwd", grid=(H, nq),
        in_specs=[pl.BlockSpec((1, tq, HEAD_PAD), lambda h, i: (h, i, 0)),
                  pl.BlockSpec((1, T, HEAD_PAD), lambda h, i: (h, 0, 0)),
                  pl.BlockSpec((1, T, V_HEAD), lambda h, i: (h, 0, 0))],
        out_specs=[pl.BlockSpec((tq, V_HEAD), lambda h, i: (i, h)),
                   pl.BlockSpec((1, 1, 1, tq), lambda h, i: (h, i, 0, 0))],
        out_shape=[jax.ShapeDtypeStruct((T, H * V_HEAD), BF16),
                   jax.ShapeDtypeStruct((H, nq, 1, tq), F32)],
        compiler_params=_params("parallel", "arbitrary"),
    )(q, k, v)


def attn_delta(do, o, *, tq):
    T = do.shape[0]
    tq = min(tq, T)
    nq = T // tq

    def body(do_ref, o_ref, d_ref):
        prod = do_ref[...].astype(F32) * o_ref[...].astype(F32)
        d_ref[0, 0] = jnp.sum(jnp.transpose(prod), axis=0, keepdims=True)

    return pl.pallas_call(
        body, name="attn_delta", grid=(N_HEADS, nq),
        in_specs=[pl.BlockSpec((tq, V_HEAD), lambda h, i: (i, h)),
                  pl.BlockSpec((tq, V_HEAD), lambda h, i: (i, h))],
        out_specs=pl.BlockSpec((1, 1, 1, tq), lambda h, i: (h, i, 0, 0)),
        out_shape=jax.ShapeDtypeStruct((N_HEADS, nq, 1, tq), F32),
        compiler_params=_params("parallel", "parallel"),
    )(do, o)


def flash_bwd(q, k, v, do, lse, delta, *, tq):
    H, T, _ = q.shape
    tq = min(tq, T)
    nq = T // tq
    neg = -1e30

    def body(q_ref, do_ref, lse_ref, dl_ref, k_ref, v_ref, dq_ref, dk_ref, dv_ref, dq_acc):
        j = pl.program_id(1)

        @pl.when(j == 0)
        def _():
            dq_acc[...] = jnp.zeros_like(dq_acc)

        kb = k_ref[0]
        vb = v_ref[0]

        def tile(i, carry, masked):
            dk, dv = carry
            rows = pl.ds(pl.multiple_of(i * tq, tq), tq)
            qb = q_ref[0, rows, :]
            dob = do_ref[rows, :]
            st = _dot_nt(kb, qb)
            if masked:
                r = lax.broadcasted_iota(jnp.int32, (tq, tq), 0)
                c = lax.broadcasted_iota(jnp.int32, (tq, tq), 1)
                st = jnp.where(r <= c, st, neg)
            pt = jnp.exp(st - lse_ref[0, i])
            dv = dv + _dot(pt.astype(BF16), dob)
            dpt = _dot_nt(vb, dob)
            dst = (pt * (dpt - dl_ref[0, i])).astype(BF16)
            dk = dk + _dot(dst, qb)
            dq_acc[rows, :] += _dot_tn(dst, kb)
            return dk, dv

        carry = tile(j, (jnp.zeros((tq, HEAD_PAD), F32), jnp.zeros((tq, V_HEAD), F32)), True)
        dk, dv = lax.fori_loop(j + 1, nq, lambda i, cr: tile(i, cr, False), carry)
        dk_ref[0] = dk.astype(BF16)
        dv_ref[0] = dv.astype(BF16)

        @pl.when(j == nq - 1)
        def _():
            dq_ref[0] = dq_acc[...].astype(BF16)

    return pl.pallas_call(
        body, name="flash_bwd", grid=(H, nq),
        in_specs=[pl.BlockSpec((1, T, HEAD_PAD), lambda h, j: (h, 0, 0)),
                  pl.BlockSpec((T, V_HEAD), lambda h, j: (0, h)),
                  pl.BlockSpec((1, nq, 1, tq), lambda h, j: (h, 0, 0, 0)),
                  pl.BlockSpec((1, nq, 1, tq), lambda h, j: (h, 0, 0, 0)),
                  pl.BlockSpec((1, tq, HEAD_PAD), lambda h, j: (h, j, 0)),
                  pl.BlockSpec((1, tq, V_HEAD), lambda h, j: (h, j, 0))],
        out_specs=[pl.BlockSpec((1, T, HEAD_PAD), lambda h, j: (h, 0, 0)),
                   pl.BlockSpec((1, tq, HEAD_PAD), lambda h, j: (h, j, 0)),
                   pl.BlockSpec((1, tq, V_HEAD), lambda h, j: (h, j, 0))],
        out_shape=[jax.ShapeDtypeStruct((H, T, HEAD_PAD), BF16),
                   jax.ShapeDtypeStruct((H, T, HEAD_PAD), BF16),
                   jax.ShapeDtypeStruct((H, T, V_HEAD), BF16)],
        scratch_shapes=[pltpu.VMEM((T, HEAD_PAD), F32)],
        compiler_params=_params("parallel", "arbitrary"),
    )(q, do, lse, delta, k, v)


def mla_bwd(dq, dk, dv, z, cos_t, s1_t, s2_t, q_norm, kv_norm, w_uq, w_ukv, *, tm):
    T = z.shape[0]
    tm = min(tm, T)
    HW = N_HEADS * HEAD_PAD

    def body(dq_ref, dk_ref, dv_ref, cq_ref, ckv_ref, c_ref, s1_ref, s2_ref, qg_ref, kvg_ref,
             wuq_ref, wukv_ref, dcq_ref, dckv_ref, dkr_ref, dqf_ref, dkvf_ref, qn_ref, kvn_ref,
             dqg_ref, dkvg_ref):
        @pl.when(pl.program_id(0) == 0)
        def _():
            dqg_ref[...] = jnp.zeros_like(dqg_ref)
            dkvg_ref[...] = jnp.zeros_like(dkvg_ref)

        c, s1, s2 = c_ref[...], s1_ref[...], s2_ref[...]
        dkpe = jnp.zeros((tm, LANES), F32)
        for h in range(N_HEADS):
            o = h * HEAD_PAD
            dqf_ref[:, o:o + QK_NOPE] = (
                dq_ref[h, :, 0:QK_NOPE].astype(F32) * ATTN_SCALE).astype(BF16)
            dqf_ref[:, o + QK_NOPE:o + HEAD_PAD] = (
                _rope(dq_ref[h, :, QK_NOPE:HEAD_PAD].astype(F32), c, s1, s2, -1.0)
                * ATTN_SCALE).astype(BF16)
            dkvf_ref[:, o:o + QK_NOPE] = dk_ref[h, :, 0:QK_NOPE]
            dkvf_ref[:, o + QK_NOPE:o + HEAD_PAD] = dv_ref[h]
            dkpe = dkpe + dk_ref[h, :, QK_NOPE:HEAD_PAD].astype(F32)
        dkr_ref[...] = _rope(dkpe, c, s1, s2, -1.0).astype(BF16)

        cq = cq_ref[...]
        qn_ref[...] = _rms_fwd(cq, qg_ref[...]).astype(BF16)
        dcq, dgrows = _rms_bwd(cq, qg_ref[...], _dot_nt(dqf_ref[...], wuq_ref[...]))
        dcq_ref[...] = dcq.astype(BF16)
        _accum_rows(dqg_ref, dgrows)

        ckv = ckv_ref[...]
        kvn_ref[...] = _rms_fwd(ckv, kvg_ref[...]).astype(BF16)
        dckv, dgrows = _rms_bwd(ckv, kvg_ref[...], _dot_nt(dkvf_ref[...], wukv_ref[...]))
        dckv_ref[...] = dckv.astype(BF16)
        _accum_rows(dkvg_ref, dgrows)

    tok = lambda w: pl.BlockSpec((tm, w), lambda i: (i, 0))
    head = lambda w: pl.BlockSpec((N_HEADS, tm, w), lambda i: (0, i, 0))
    return pl.pallas_call(
        body, name="mla_bwd", grid=(T // tm,),
        in_specs=[head(HEAD_PAD), head(HEAD_PAD), head(V_HEAD),
                  pl.BlockSpec((tm, Q_LORA), lambda i: (i, OFF_CQ // Q_LORA)),
                  pl.BlockSpec((tm, KV_LORA), lambda i: (i, OFF_CKV // KV_LORA)),
                  tok(LANES), tok(LANES), tok(LANES),
                  _full((1, Q_LORA)), _full((1, KV_LORA)),
                  _full((Q_LORA, HW)), _full((KV_LORA, HW))],
        out_specs=[tok(Q_LORA), tok(KV_LORA), tok(LANES), tok(HW), tok(HW), tok(Q_LORA),
                   tok(KV_LORA), _full((1, Q_LORA)), _full((1, KV_LORA))],
        out_shape=[jax.ShapeDtypeStruct((T, Q_LORA), BF16),
                   jax.ShapeDtypeStruct((T, KV_LORA), BF16),
                   jax.ShapeDtypeStruct((T, LANES), BF16),
                   jax.ShapeDtypeStruct((T, HW), BF16),
                   jax.ShapeDtypeStruct((T, HW), BF16),
                   jax.ShapeDtypeStruct((T, Q_LORA), BF16),
                   jax.ShapeDtypeStruct((T, KV_LORA), BF16),
                   jax.ShapeDtypeStruct((1, Q_LORA), F32),
                   jax.ShapeDtypeStruct((1, KV_LORA), F32)],
        compiler_params=_params("arbitrary"),
    )(dq, dk, dv, z, z, cos_t, s1_t, s2_t, q_norm, kv_norm, w_uq, w_ukv)


def merge_fwd(o, ya, z, x, w_b, w_o, *, tm):
    T = x.shape[0]
    tm = min(tm, T)

    def body(o_ref, ya_ref, ga_ref, gb_ref, x_ref, wb_ref, wo_ref, yb_ref, mg_ref, x1_ref):
        yb = _dot(o_ref[...], wb_ref[...])
        yb_ref[...] = yb
        merged = (_sigmoid(ga_ref[...]) * ya_ref[...] + _sigmoid(gb_ref[...]) * yb).astype(BF16)
        mg_ref[...] = merged
        x1_ref[...] = x_ref[...] + _dot(merged, wo_ref[...])

    tok = pl.BlockSpec((tm, D_MODEL), lambda i: (i, 0))
    return pl.pallas_call(
        body, name="merge_fwd", grid=(T // tm,),
        in_specs=[tok, tok, pl.BlockSpec((tm, D_MODEL), lambda i: (i, OFF_GA // D_MODEL)),
                  pl.BlockSpec((tm, D_MODEL), lambda i: (i, OFF_GB // D_MODEL)), tok,
                  _full((D_MODEL, D_MODEL)), _full((D_MODEL, D_MODEL))],
        out_specs=[tok, tok, tok],
        out_shape=[jax.ShapeDtypeStruct((T, D_MODEL), F32),
                   jax.ShapeDtypeStruct((T, D_MODEL), BF16),
                   jax.ShapeDtypeStruct((T, D_MODEL), F32)],
        compiler_params=_params("parallel"),
    )(o, ya, z, z, x, w_b, w_o)


def merge_bwd(dx1, ya, yb, z, w_o, w_b, dep, *, tm):
    T = dx1.shape[0]
    tm = min(tm, T)

    def body(dx_ref, ya_ref, yb_ref, ga_ref, gb_ref, wo_ref, wb_ref, dep_ref, dya_ref, dyb_ref,
             do_ref, dga_ref, dgb_ref):
        dm = _dot_nt(dx_ref[...].astype(BF16), wo_ref[...])
        sa = _sigmoid(ga_ref[...])
        sb = _sigmoid(gb_ref[...])
        dya_ref[...] = (dm * sa).astype(BF16)
        dyb = (dm * sb).astype(BF16)
        dyb_ref[...] = dyb
        dga_ref[...] = (dm * ya_ref[...] * sa * (1.0 - sa)).astype(BF16)
        dgb_ref[...] = (dm * yb_ref[...] * sb * (1.0 - sb)).astype(BF16)
        do_ref[...] = _dot_nt(dyb, wb_ref[...]).astype(BF16)

    tok = pl.BlockSpec((tm, D_MODEL), lambda i: (i, 0))
    return pl.pallas_call(
        body, name="merge_bwd", grid=(T // tm,),
        in_specs=[tok, tok, tok, pl.BlockSpec((tm, D_MODEL), lambda i: (i, OFF_GA // D_MODEL)),
                  pl.BlockSpec((tm, D_MODEL), lambda i: (i, OFF_GB // D_MODEL)),
                  _full((D_MODEL, D_MODEL)), _full((D_MODEL, D_MODEL)),
                  pl.BlockSpec(memory_space=pl.ANY)],
        out_specs=[tok] * 5,
        out_shape=[jax.ShapeDtypeStruct((T, D_MODEL), BF16)] * 5,
        compiler_params=_params("parallel"),
    )(dx1, ya, yb, z, z, w_o, w_b, dep)


def in_bwd(dx1, x, g, segs, w_in, *, tm):
    T = x.shape[0]
    tm = min(tm, T)
    widths = [s.shape[1] for s, _ in segs]
    offs = [o for _, o in segs]
    n = len(segs)

    def body(*refs):
        dx1_ref, x_ref, g_ref = refs[:3]
        seg_refs = refs[3:3 + n]
        w_ref = refs[3 + n]
        dx_ref, h_ref, dg_ref = refs[4 + n:]

        @pl.when(pl.program_id(0) == 0)
        def _():
            dg_ref[...] = jnp.zeros_like(dg_ref)

        dh = jnp.zeros((tm, D_MODEL), F32)
        for s_ref, off, w in zip(seg_refs, offs, widths):
            dh = dh + _dot_nt(s_ref[...], w_ref[:, off:off + w])
        xv = x_ref[...]
        h_ref[...] = _rms_fwd(xv, g_ref[...]).astype(BF16)
        dxn, dgrows = _rms_bwd(xv, g_ref[...], dh)
        dx_ref[...] = dx1_ref[...] + dxn
        _accum_rows(dg_ref, dgrows)

    tok = pl.BlockSpec((tm, D_MODEL), lambda i: (i, 0))
    return pl.pallas_call(
        body, name="in_bwd", grid=(T // tm,),
        in_specs=[tok, tok, _full((1, D_MODEL))]
        + [pl.BlockSpec((tm, w), lambda i: (i, 0)) for w in widths] + [_full((D_MODEL, IN_PAD))],
        out_specs=[tok, tok, _full((1, D_MODEL))],
        out_shape=[jax.ShapeDtypeStruct((T, D_MODEL), F32),
                   jax.ShapeDtypeStruct((T, D_MODEL), BF16),
                   jax.ShapeDtypeStruct((1, D_MODEL), F32)],
        compiler_params=_params("arbitrary"),
    )(dx1, x, g, *[s for s, _ in segs], w_in)


def ffn_fwd(x1, g, w_gate, w_up, w_down, *, tm):
    T = x1.shape[0]
    tm = min(tm, T)

    def body(x_ref, g_ref, wg_ref, wu_ref, wd_ref, gp_ref, up_ref, x2_ref, h_ref, acc_ref):
        f = pl.program_id(1)

        @pl.when(f == 0)
        def _():
            h_ref[...] = _rms_fwd(x_ref[...], g_ref[...]).astype(BF16)
            acc_ref[...] = jnp.zeros_like(acc_ref)

        gp = _dot(h_ref[...], wg_ref[0])
        up = _dot(h_ref[...], wu_ref[0])
        gp_ref[0] = gp
        up_ref[0] = up
        act = (gp * _sigmoid(gp) * up).astype(BF16)
        acc_ref[...] += _dot(act, wd_ref[0])

        @pl.when(f == N_DEV - 1)
        def _():
            x2_ref[...] = x_ref[...] + acc_ref[...]

    tok = pl.BlockSpec((tm, D_MODEL), lambda i, f: (i, 0))
    ff = pl.BlockSpec((1, tm, FF_PIECE), lambda i, f: (f, i, 0))
    w_col = pl.BlockSpec((1, D_MODEL, FF_PIECE), lambda i, f: (f, 0, 0))
    w_row = pl.BlockSpec((1, FF_PIECE, D_MODEL), lambda i, f: (f, 0, 0))
    return pl.pallas_call(
        body, name="ffn_fwd", grid=(T // tm, N_DEV),
        in_specs=[tok, _full((1, D_MODEL)), w_col, w_col, w_row],
        out_specs=[ff, ff, tok],
        out_shape=[jax.ShapeDtypeStruct((N_DEV, T, FF_PIECE), F32),
                   jax.ShapeDtypeStruct((N_DEV, T, FF_PIECE), F32),
                   jax.ShapeDtypeStruct((T, D_MODEL), F32)],
        scratch_shapes=[pltpu.VMEM((tm, D_MODEL), BF16), pltpu.VMEM((tm, D_MODEL), F32)],
        compiler_params=_params("parallel", "arbitrary"),
    )(x1, g, w_gate, w_up, w_down)


def ffn_bwd(dx2, x1, g, gpre, upre, w_gate, w_up, w_down, *, tm):
    T = x1.shape[0]
    tm = min(tm, T)

    def body(dx_ref, x_ref, g_ref, gp_ref, up_ref, wg_ref, wu_ref, wd_ref,
             act_ref, dg_ref, du_ref, h_ref, dx1_ref, dgain_ref, dxb_ref, acc_ref):
        f = pl.program_id(1)

        @pl.when((pl.program_id(0) == 0) & (f == 0))
        def _():
            dgain_ref[...] = jnp.zeros_like(dgain_ref)

        @pl.when(f == 0)
        def _():
            dxb_ref[...] = dx_ref[...].astype(BF16)
            acc_ref[...] = jnp.zeros_like(acc_ref)

        gp = gp_ref[0]
        up = up_ref[0]
        dact = _dot_nt(dxb_ref[...], wd_ref[0])
        sg = _sigmoid(gp)
        silu = gp * sg
        act_ref[0] = (silu * up).astype(BF16)
        dgp = (dact * up * (sg * (1.0 + gp * (1.0 - sg)))).astype(BF16)
        dup = (dact * silu).astype(BF16)
        dg_ref[0] = dgp
        du_ref[0] = dup
        acc_ref[...] += _dot_nt(dgp, wg_ref[0]) + _dot_nt(dup, wu_ref[0])

        @pl.when(f == N_DEV - 1)
        def _():
            xv = x_ref[...]
            h_ref[...] = _rms_fwd(xv, g_ref[...]).astype(BF16)
            dxn, dgrows = _rms_bwd(xv, g_ref[...], acc_ref[...])
            dx1_ref[...] = dx_ref[...] + dxn
            _accum_rows(dgain_ref, dgrows)

    tok = pl.BlockSpec((tm, D_MODEL), lambda i, f: (i, 0))
    ff = pl.BlockSpec((1, tm, FF_PIECE), lambda i, f: (f, i, 0))
    w_col = pl.BlockSpec((1, D_MODEL, FF_PIECE), lambda i, f: (f, 0, 0))
    w_row = pl.BlockSpec((1, FF_PIECE, D_MODEL), lambda i, f: (f, 0, 0))
    return pl.pallas_call(
        body, name="ffn_bwd", grid=(T // tm, N_DEV),
        in_specs=[tok, tok, _full((1, D_MODEL)), ff, ff, w_col, w_col, w_row],
        out_specs=[ff, ff, ff, tok, tok, _full((1, D_MODEL))],
        out_shape=[jax.ShapeDtypeStruct((N_DEV, T, FF_PIECE), BF16)] * 3
        + [jax.ShapeDtypeStruct((T, D_MODEL), BF16), jax.ShapeDtypeStruct((T, D_MODEL), F32),
           jax.ShapeDtypeStruct((1, D_MODEL), F32)],
        scratch_shapes=[pltpu.VMEM((tm, D_MODEL), BF16), pltpu.VMEM((tm, D_MODEL), F32)],
        compiler_params=_params("arbitrary", "arbitrary"),
    )(dx2, x1, g, gpre, upre, w_gate, w_up, w_down)


def ple_fwd(x2, p, g, w_pg, w_ple, *, tm):
    T = x2.shape[0]
    tm = min(tm, T)

    def body(x_ref, p_ref, g_ref, wpg_ref, wple_ref, l_ref, pe_ref, x3_ref):
        xv = x_ref[...]
        logits = _dot(_rms_fwd(xv, g_ref[...]).astype(BF16), wpg_ref[...])
        pe = _dot(p_ref[...].astype(BF16), wple_ref[...])
        l_ref[...] = logits
        pe_ref[...] = pe
        x3_ref[...] = xv + _sigmoid(logits) * pe

    tok = pl.BlockSpec((tm, D_MODEL), lambda i: (i, 0))
    return pl.pallas_call(
        body, name="ple_fwd", grid=(T // tm,),
        in_specs=[tok, pl.BlockSpec((tm, PLE_DIM), lambda i: (i, 0)), _full((1, D_MODEL)),
                  _full((D_MODEL, D_MODEL)), _full((PLE_DIM, D_MODEL))],
        out_specs=[tok, tok, tok],
        out_shape=[jax.ShapeDtypeStruct((T, D_MODEL), F32)] * 3,
        compiler_params=_params("parallel"),
    )(x2, p, g, w_pg, w_ple)


def ple_bwd(dx3, logits, pe, x2, g, w_pg, *, tm):
    T = x2.shape[0]
    tm = min(tm, T)

    def body(dx_ref, l_ref, pe_ref, x_ref, g_ref, wpg_ref, dl_ref, dpe_ref, h_ref, dx2_ref,
             dg_ref):
        @pl.when(pl.program_id(0) == 0)
        def _():
            dg_ref[...] = jnp.zeros_like(dg_ref)

        dx = dx_ref[...]
        s = _sigmoid(l_ref[...])
        dpe_ref[...] = (dx * s).astype(BF16)
        dl = (dx * pe_ref[...] * s * (1.0 - s)).astype(BF16)
        dl_ref[...] = dl
        xv = x_ref[...]
        h_ref[...] = _rms_fwd(xv, g_ref[...]).astype(BF16)
        dxn, dgrows = _rms_bwd(xv, g_ref[...], _dot_nt(dl, wpg_ref[...]))
        dx2_ref[...] = dx + dxn
        _accum_rows(dg_ref, dgrows)

    tok = pl.BlockSpec((tm, D_MODEL), lambda i: (i, 0))
    return pl.pallas_call(
        body, name="ple_bwd", grid=(T // tm,),
        in_specs=[tok, tok, tok, tok, _full((1, D_MODEL)), _full((D_MODEL, D_MODEL))],
        out_specs=[tok, tok, tok, tok, _full((1, D_MODEL))],
        out_shape=[jax.ShapeDtypeStruct((T, D_MODEL), BF16)] * 3
        + [jax.ShapeDtypeStruct((T, D_MODEL), F32), jax.ShapeDtypeStruct((1, D_MODEL), F32)],
        compiler_params=_params("arbitrary"),
    )(dx3, logits, pe, x2, g, w_pg)


def loss_head(x, g, target, *, tm):
    T = x.shape[0]
    tm = min(tm, T)

    def body(x_ref, g_ref, t_ref, loss_ref, dx_ref, dg_ref):
        @pl.when(pl.program_id(0) == 0)
        def _():
            loss_ref[...] = jnp.zeros_like(loss_ref)
            dg_ref[...] = jnp.zeros_like(dg_ref)

        xv = x_ref[...]
        err = _rms_fwd(xv, g_ref[...]) - t_ref[...]
        part = 0.5 * jnp.sum(jnp.mean(err * err, axis=-1, keepdims=True), axis=0, keepdims=True)
        lane = lax.broadcasted_iota(jnp.int32, (1, LANES), 1)
        loss_ref[...] += jnp.where(lane == 0, part, 0.0)
        dxn, dgrows = _rms_bwd(xv, g_ref[...], err * (1.0 / D_MODEL))
        dx_ref[...] = dxn
        _accum_rows(dg_ref, dgrows)

    tok = pl.BlockSpec((tm, D_MODEL), lambda i: (i, 0))
    return pl.pallas_call(
        body, name="loss_head", grid=(T // tm,),
        in_specs=[tok, _full((1, D_MODEL)), tok],
        out_specs=[_full((1, LANES)), tok, _full((1, D_MODEL))],
        out_shape=[jax.ShapeDtypeStruct((1, LANES), F32), jax.ShapeDtypeStruct((T, D_MODEL), F32),
                   jax.ShapeDtypeStruct((1, D_MODEL), F32)],
        compiler_params=_params("arbitrary"),
    )(x, g, target)


def _peers():
    x, y, c = lax.axis_index("x"), lax.axis_index("y"), lax.axis_index("c")
    me = 4 * x + 2 * y + c
    out = []
    for d in range(1, N_DEV):
        px = 1 - x if d & 4 else x
        py = 1 - y if d & 2 else y
        pc = 1 - c if d & 1 else c
        out.append(((px, py, pc), 4 * px + 2 * py + pc))
    return me, out


_HBM = pl.BlockSpec(memory_space=pltpu.HBM)
_SEM = pl.BlockSpec(memory_space=pltpu.SEMAPHORE)
_ANY = pl.BlockSpec(memory_space=pl.ANY)
_DATAFLOW = pltpu.SideEffectType.DATAFLOW_SIDE_EFFECTING


def _exchange_copy(t, d, pos, idx, me, src_ref, scatter, land_ref, send_sems, recv_sems):
    k = t * (N_DEV - 1) + d
    return pltpu.make_async_remote_copy(
        src_ref=src_ref.at[idx] if scatter else src_ref, dst_ref=land_ref.at[me],
        send_sem=send_sems.at[k], recv_sem=recv_sems.at[k],
        device_id=pos, device_id_type=pl.DeviceIdType.MESH)


def exchange_start(gathers, scatters, dep, *, name):
    srcs = [pltpu.with_memory_space_constraint(a, pltpu.HBM) for a in list(gathers) + list(scatters)]
    kinds = [False] * len(gathers) + [True] * len(scatters)
    lands = [pltpu.with_memory_space_constraint(
        lax.empty((N_DEV,) + (a.shape[1:] if sc else a.shape), a.dtype), pltpu.HBM)
        for a, sc in zip(srcs, kinds)]
    n = len(srcs)

    def body(*refs):
        src_refs, land_refs = refs[:n], refs[n:2 * n]
        send_sems, recv_sems = refs[-2 * n - 4], refs[-2 * n - 3]
        token, local_sems = refs[-2], refs[-1]
        me, peers = _peers()
        locals_ = []
        for t in range(n):
            local = pltpu.make_async_copy(src_refs[t].at[me] if kinds[t] else src_refs[t],
                                          land_refs[t].at[me], local_sems.at[t])
            local.start()
            locals_.append(local)
            for d, (pos, idx) in enumerate(peers):
                _exchange_copy(t, d, pos, idx, me, src_refs[t], kinds[t], land_refs[t],
                               send_sems, recv_sems).start()
        for local in locals_:
            local.wait()
        token[...] = jnp.zeros_like(token)

    sem = pltpu.SemaphoreType.DMA((n * (N_DEV - 1),))
    thru = [pltpu.HBM(a.shape, a.dtype) for a in srcs + lands]
    outs = pl.pallas_call(
        body, name=name,
        in_specs=[_HBM] * (2 * n) + ([_ANY] if dep is not None else []),
        out_specs=[_SEM, _SEM] + [_HBM] * (2 * n) + [pl.BlockSpec(memory_space=pltpu.VMEM)],
        out_shape=[sem, sem] + thru + [jax.ShapeDtypeStruct((8, LANES), F32)],
        input_output_aliases={i: 2 + i for i in range(2 * n)},
        scratch_shapes=[pltpu.SemaphoreType.DMA((n,))],
        compiler_params=pltpu.CompilerParams(has_side_effects=_DATAFLOW),
    )(*srcs, *lands, *([dep] if dep is not None else []))
    return dict(sems=outs[:2], srcs=outs[2:2 + n], lands=outs[2 + n:2 + 2 * n], token=outs[-1],
                kinds=kinds)


def exchange_wait(handle, after, *, name):
    kinds = handle["kinds"]
    n = len(kinds)

    def body(*refs):
        src_refs, land_refs = refs[:n], refs[n:2 * n]
        send_sems, recv_sems = refs[2 * n], refs[2 * n + 1]
        me, peers = _peers()
        for t in range(n):
            for d, (pos, idx) in enumerate(peers):
                cp = _exchange_copy(t, d, pos, idx, me, src_refs[t], kinds[t], land_refs[t],
                                    send_sems, recv_sems)
                cp.wait_send()
                cp.wait_recv()

    arrays = list(handle["srcs"]) + list(handle["lands"])
    outs = pl.pallas_call(
        body, name=name,
        in_specs=[_HBM] * (2 * n) + [_SEM, _SEM] + ([_ANY] if after is not None else []),
        out_specs=[_HBM] * (2 * n),
        out_shape=[pltpu.HBM(a.shape, a.dtype) for a in arrays],
        input_output_aliases={i: i for i in range(2 * n)},
        compiler_params=pltpu.CompilerParams(has_side_effects=_DATAFLOW),
    )(*arrays, *handle["sems"], *([after] if after is not None else []))
    return outs[n:]


def _row_block(rows, cols):
    best = None
    for rb in range(16, rows + 1, 16):
        if rows % rb == 0 and rb * cols <= 256 * 1024:
            best = rb
    return best or rows


def reduce_adamw(parts, w, m, v, *, name):
    nl = len(parts)
    C = w.shape[1]
    R, cp = parts[0].shape[1:]
    rb = _row_block(R, cp)
    nb = R // rb

    def body(*refs):
        p_refs = refs[:nl]
        w_ref, m_ref, v_ref, g_ref, d_ref, m2_ref, v2_ref = refs[nl:]

        def total(p_ref):
            g = p_ref[0, :, 0:C].astype(F32)
            for k in range(1, N_DEV):
                g = g + p_ref[k, :, 0:C].astype(F32)
            return g

        g = total(p_refs[0])
        for layer in range(1, nl):
            g = jnp.where(pl.program_id(0) == layer, total(p_refs[layer]), g)
        g_ref[...] = g
        m2 = ADAM_B1 * m_ref[...] + (1.0 - ADAM_B1) * g
        v2 = ADAM_B2 * v_ref[...] + (1.0 - ADAM_B2) * (g * g)
        m2_ref[...] = m2
        v2_ref[...] = v2
        m_hat = m2 / (1.0 - ADAM_B1 ** ADAM_STEP)
        v_hat = v2 / (1.0 - ADAM_B2 ** ADAM_STEP)
        d_ref[...] = -ADAM_LR * (m_hat / (jnp.sqrt(v_hat) + ADAM_EPS) + ADAM_WD * w_ref[...])

    def part_spec(layer):
        return pl.BlockSpec((N_DEV, rb, cp), lambda l, i: (0, jnp.where(l == layer, i, 0), 0))

    blk = pl.BlockSpec((rb, C), lambda l, i: (l * nb + i, 0))
    return pl.pallas_call(
        body, name=name, grid=(nl, nb),
        in_specs=[part_spec(layer) for layer in range(nl)] + [blk, blk, blk],
        out_specs=[blk] * 4,
        out_shape=[jax.ShapeDtypeStruct((nl * R, C), F32)] * 4,
        compiler_params=_params("arbitrary", "arbitrary"),
    )(*parts, w, m, v)


def _cols_to_pieces(full):
    r = full.shape[0]
    return jnp.transpose(full.reshape(r, N_DEV, -1), (1, 0, 2))


def _pieces_to_cols(pieces):
    r = pieces.shape[1]
    return jnp.transpose(pieces, (1, 0, 2)).reshape(r, -1)


def _rope_tables(positions):
    inv_freq = 1.0 / (ROPE_THETA ** (jnp.arange(0, QK_ROPE, 2, dtype=F32) / QK_ROPE))
    ang = positions.astype(F32)[:, None] * inv_freq
    cos, sin = jnp.cos(ang), jnp.sin(ang)
    zero = jnp.zeros_like(cos)
    return (jnp.concatenate([cos, cos, zero, zero], axis=1),
            jnp.concatenate([-sin, zero, zero, zero], axis=1),
            jnp.concatenate([zero, sin, zero, zero], axis=1))


def kernel(x, p, positions, norm_mix, w_in, w_pool, pool_scale, q_norm, kv_norm, w_uq, w_ukv, w_a, w_b, w_o, norm_ffn, w_gate, w_up, w_down, norm_ple, w_ple_gate, w_ple, final_norm, loss_target, m_norm_mix, m_w_in, m_w_pool, m_pool_scale, m_q_norm, m_kv_norm, m_w_uq, m_w_ukv, m_w_a, m_w_b, m_w_o, m_norm_ffn, m_w_gate, m_w_up, m_w_down, m_norm_ple, m_w_ple_gate, m_w_ple, m_final_norm, v_norm_mix, v_w_in, v_w_pool, v_pool_scale, v_q_norm, v_kv_norm, v_w_uq, v_w_ukv, v_w_a, v_w_b, v_w_o, v_norm_ffn, v_w_gate, v_w_up, v_w_down, v_norm_ple, v_w_ple_gate, v_w_ple, v_final_norm):
    given = dict(locals())
    shard = {n: given[n] for n in WEIGHTS}
    TM = 512
    TQ = 512

    def send_form(n, i):
        w = shard[n][i].astype(BF16)
        if n == "w_in":
            return jnp.pad(w, ((0, 0), (0, PIECE_PAD - W_IN_PIECE)))
        if n == "w_uq":
            w = jnp.pad(w, ((0, 0), (0, 0), (0, HEAD_PAD - QK_HEAD)))
        if n in ("w_uq", "w_ukv"):
            return w.reshape(-1, N_HEADS * HEAD_PAD)
        return w

    names = [n for n, _ in SHARDED]
    FIRST = ("w_in", "w_uq", "w_ukv", "w_a", "w_b", "w_o")
    LATER = ("w_gate", "w_up", "w_down", "w_ple_gate", "w_ple")
    start_a = exchange_start([send_form(n, 0) for n in FIRST], [], None, name="gather_start_a")
    got_a = exchange_wait(start_a, None, name="gather_wait_a")
    rest = [(n, 0) for n in LATER] + [(n, 1) for n in names]
    start_b = exchange_start([send_form(n, i) for n, i in rest], [], got_a[0],
                             name="gather_start_b")
    gathered = {(n, 0): a for n, a in zip(FIRST, got_a)}
    row_map = jnp.asarray(_piece_row_map())

    low_np, up_np = _band_matrices()
    low, up = jnp.asarray(low_np, BF16), jnp.asarray(up_np, BF16)
    low_t = jnp.asarray(low_np.transpose(0, 2, 1), BF16)
    up_t = jnp.asarray(up_np.transpose(0, 2, 1), BF16)
    cos_t, s1_t, s2_t = _rope_tables(positions[0])

    def mix_weights(i):
        g = lambda n: gathered[n, i]
        return dict(
            w_in=assemble_w_in(g("w_in"), row_map, tr=512),
            w_uq=g("w_uq").reshape(Q_LORA, N_HEADS * HEAD_PAD),
            w_ukv=g("w_ukv").reshape(KV_LORA, N_HEADS * HEAD_PAD),
            w_a=_pieces_to_cols(g("w_a")), w_b=g("w_b").reshape(D_MODEL, D_MODEL),
            w_o=g("w_o").reshape(D_MODEL, D_MODEL),
            w_pool=w_pool[i].astype(BF16),
            pool_scale=pool_scale[i][None], norm_mix=norm_mix[i][None], q_norm=q_norm[i][None],
            kv_norm=kv_norm[i][None], norm_ffn=norm_ffn[i][None], norm_ple=norm_ple[i][None])

    def channel_weights(i):
        g = lambda n: gathered[n, i]
        return dict(w_gate=g("w_gate"), w_up=g("w_up"), w_down=g("w_down"),
                    w_pg=g("w_ple_gate").reshape(D_MODEL, D_MODEL),
                    w_ple=_pieces_to_cols(g("w_ple")))

    def forward_mix(L, xs):
        z = rms_matmul(xs, L["norm_mix"], L["w_in"], tm=TM, tn=1152, name="in_proj")
        pooled, ms, ya = pool_fwd(z, low, up, L["w_pool"], L["pool_scale"], L["w_a"], tm=TM)
        q, k, v = mla_prep(z, cos_t, s1_t, s2_t, L["q_norm"], L["kv_norm"], L["w_uq"], L["w_ukv"],
                           tm=TM)
        o, lse = flash_fwd(q, k, v, tq=TQ)
        return dict(x=xs, z=z, pooled=pooled, ms=ms, ya=ya, q=q, k=k, v=v, o=o, lse=lse)

    def forward_rest(L, S, i):
        yb, merged, x1 = merge_fwd(S["o"], S["ya"], S["z"], S["x"], L["w_b"], L["w_o"], tm=TM)
        gpre, upre, x2 = ffn_fwd(x1, L["norm_ffn"], L["w_gate"], L["w_up"], L["w_down"],
                                 tm=2 * TM)
        logits, pe, x3 = ple_fwd(x2, p[i, 0], L["norm_ple"], L["w_pg"], L["w_ple"], tm=TM)
        S.update(yb=yb, merged=merged, x1=x1, gpre=gpre, upre=upre, x2=x2, logits=logits, pe=pe)
        return x3

    layers, saved = [], []
    L = mix_weights(0)
    first = dict(L, norm_mix=L["norm_mix"] + start_b["token"][0, 0])
    S = forward_mix(first, x[0])
    got_b = exchange_wait(start_b, S["o"], name="gather_wait_b")
    gathered.update(dict(zip(rest, got_b)))
    L.update(channel_weights(0))
    xs = forward_rest(L, S, 0)
    layers.append(L)
    saved.append(S)
    L = dict(mix_weights(1), **channel_weights(1))
    S = forward_mix(L, xs)
    xs = forward_rest(L, S, 1)
    layers.append(L)
    saved.append(S)

    loss_part, dx, d_final = loss_head(xs, final_norm[None], loss_target[0], tm=TM)

    grads = {n: [None] * DEPTH for n in REPLICATED if n != "final_norm"}
    pieces = {n: [None] * DEPTH for n in names}
    tn_mm = functools.partial(matmul_tn, tn=1024, tk=1024)
    row_pieces = lambda a: a.reshape(N_DEV, -1, a.shape[-1])
    start_1 = None
    for i in reversed(range(DEPTH)):
        L, S = layers[i], saved[i]
        g_ple = L["norm_ple"] if start_1 is None else L["norm_ple"] + start_1["token"][0, 0]
        dl, dpe, hn, dx2, d_norm_ple = ple_bwd(dx, S["logits"], S["pe"], S["x2"], g_ple,
                                               L["w_pg"], tm=TM)
        grads["norm_ple"][i] = d_norm_ple[0]
        pieces["w_ple_gate"][i] = row_pieces(tn_mm(hn, dl, name="dw_ple_gate"))
        pieces["w_ple"][i] = _cols_to_pieces(tn_mm(p[i, 0], dpe, name="dw_ple"))

        act, dgp, dup, h2, dx1, d_norm_ffn = ffn_bwd(dx2, S["x1"], L["norm_ffn"], S["gpre"],
                                                     S["upre"], L["w_gate"], L["w_up"],
                                                     L["w_down"], tm=TM)
        grads["norm_ffn"][i] = d_norm_ffn[0]
        pieces["w_down"][i] = matmul_tn_pieces(act, dx2, tk=512, name="dw_down")
        pieces["w_gate"][i] = matmul_tn_pieces(h2, dgp, tk=512, name="dw_gate")
        pieces["w_up"][i] = matmul_tn_pieces(h2, dup, tk=512, name="dw_up")
        if i == 0:
            start_0a = exchange_start([], [pieces[n][0] for n in LATER], None,
                                      name="scatter_start_0a")
        dep = start_0a["token"] if i == 0 else loss_part

        dya, dyb, do, dga, dgb = merge_bwd(dx1, S["ya"], S["yb"], S["z"], L["w_o"], L["w_b"],
                                           dep, tm=TM)
        pieces["w_o"][i] = row_pieces(tn_mm(S["merged"], dx1, name="dw_o"))
        pieces["w_b"][i] = row_pieces(tn_mm(S["o"], dyb, name="dw_b"))

        delta = attn_delta(do, S["o"], tq=TQ)
        dq, dk, dv = flash_bwd(S["q"], S["k"], S["v"], do, S["lse"], delta, tq=TQ)
        dcq, dckv, dkr, dqf, dkvf, qn, kvn, d_q_norm, d_kv_norm = mla_bwd(
            dq, dk, dv, S["z"], cos_t, s1_t, s2_t, L["q_norm"], L["kv_norm"], L["w_uq"],
            L["w_ukv"], tm=TM)
        grads["q_norm"][i] = d_q_norm[0]
        grads["kv_norm"][i] = d_kv_norm[0]
        d_w_uq = tn_mm(qn, dqf, name="dw_uq")
        pieces["w_uq"][i] = d_w_uq.reshape(N_DEV, -1, N_HEADS, HEAD_PAD)[..., :QK_HEAD]
        pieces["w_ukv"][i] = tn_mm(kvn, dkvf, name="dw_ukv").reshape(
            N_DEV, -1, N_HEADS, QK_NOPE + V_HEAD)

        dpc, d_pool_scale, d_w_pool = pool_bwd_a(dya, S["pooled"], L["w_a"], L["w_pool"],
                                                 L["pool_scale"], tm=TM)
        grads["pool_scale"][i] = d_pool_scale[0]
        grads["w_pool"][i] = d_w_pool
        pieces["w_a"][i] = _cols_to_pieces(tn_mm(S["ms"], dya, name="dw_a"))
        du = pool_bwd_b(dpc, low_t, up_t, tm=TM)

        segs = [(dga, OFF_GA), (dgb, OFF_GB), (du, OFF_U), (dcq, OFF_CQ), (dckv, OFF_CKV),
                (dkr, OFF_KR)]
        dx, h, d_norm_mix = in_bwd(dx1, S["x"], L["norm_mix"], segs, L["w_in"], tm=TM)
        grads["norm_mix"][i] = d_norm_mix[0]
        pieces["w_in"][i] = split_dw_in(dw_in_proj(h, segs, tk=512), row_map)
        if i == 1:
            start_1 = exchange_start([], [pieces[n][1] for n in names], None,
                                     name="scatter_start_1")

    small = {n: jnp.stack(g) for n, g in grads.items()}
    small["final_norm"] = d_final[0]
    rep = jnp.concatenate([small[n].reshape(-1) for n in REPLICATED] + [loss_part.reshape(-1)])
    n_small = -(-rep.shape[0] // (8 * FLAT_COLS)) * (8 * FLAT_COLS)

    def small_flat(parts):
        flat = jnp.concatenate([a.reshape(-1) for a in parts])
        return jnp.pad(flat, (0, n_small - flat.shape[0])).reshape(-1, FLAT_COLS)

    start_0b = exchange_start([small_flat([rep])], [pieces[n][0] for n in FIRST], None,
                              name="scatter_start_0b")
    got_1 = exchange_wait(start_1, start_0b["token"], name="scatter_wait_1")
    got_0a = exchange_wait(start_0a, start_0b["token"], name="scatter_wait_0a")
    got_0b = exchange_wait(start_0b, None, name="scatter_wait_0b")
    parts = {(n, 1): a for n, a in zip(names, got_1)}
    parts.update({(n, 0): a for n, a in zip(LATER, got_0a)})
    parts.update({(n, 0): a for n, a in zip(FIRST, got_0b[1:])})

    results = {}
    for n in names:
        w = shard[n]
        rows2d = lambda a: a.reshape(-1, w.shape[-1])
        w2 = rows2d(w)
        per_layer = [parts[n, i].reshape(N_DEV, w2.shape[0] // DEPTH, -1) for i in range(DEPTH)]
        out = reduce_adamw(per_layer, w2, rows2d(given["m_" + n]), rows2d(given["v_" + n]),
                           name="adamw_" + n)
        results[n] = [a.reshape(w.shape) for a in out]

    out = reduce_adamw([got_0b[0]], *[small_flat([given[pre + n] for n in REPLICATED])
                                      for pre in ("", "m_", "v_")], name="adamw_small")
    loss = None
    for n in REPLICATED:
        results[n] = []
    for a in out:
        a = a.reshape(-1)
        off = 0
        for n in REPLICATED:
            size = int(np.prod(shard[n].shape))
            results[n].append(a[off:off + size].reshape(shard[n].shape))
            off += size
        if loss is None:
            loss = a[off]
    return (loss, dx[None], *[results[n][0] for n in WEIGHTS], *[results[n][1] for n in WEIGHTS],
            *[results[n][2] for n in WEIGHTS], *[results[n][3] for n in WEIGHTS])
```

```python
import functools

import numpy as np
import jax
import jax.numpy as jnp
from jax import lax
from jax.experimental import pallas as pl
from jax.experimental.pallas import tpu as pltpu

F32 = jnp.float32
BF16 = jnp.bfloat16

D_MODEL = 1024
DEPTH = 2
PLE_DIM = 256
POOL_WINDOWS = (2, 4, 8, 16)
POOL_GROUP = 128
POOL_WIDTH = 512
N_HEADS = 8
Q_LORA = 512
KV_LORA = 256
QK_NOPE = 128
QK_ROPE = 64
QK_HEAD = 192
V_HEAD = 128
HEAD_PAD = 256
D_FF = 2816
ROPE_THETA = 10000.0
EPS = 1e-6
IN_WIDTH = 3392
ATTN_SCALE = QK_HEAD ** -0.5

OFF_GA, OFF_GB, OFF_U, OFF_CQ, OFF_CKV, OFF_KR = 0, 1024, 2048, 2560, 3072, 3328
IN_PAD = 3456
W_IN_PIECE = IN_WIDTH // 8
PIECE_PAD = 512
FF_PIECE = D_FF // 8
FF_PAD = 384
FF_PAIRS = 4
FF_STEP = 2 * FF_PAD

ADAM_LR = 0.001
ADAM_B1 = 0.9
ADAM_B2 = 0.999
ADAM_EPS = 1e-08
ADAM_WD = 0.01
ADAM_STEP = 10

N_DEV = 8
LANES = 128
CHUNK = 128
VMEM_LIMIT = 56 * 1024 * 1024

SHARDED = (("w_in", 2), ("w_uq", 1), ("w_ukv", 1), ("w_a", 2), ("w_b", 1), ("w_o", 1),
           ("w_gate", 2), ("w_up", 2), ("w_down", 1), ("w_ple_gate", 1), ("w_ple", 2))
REPLICATED = ("norm_mix", "w_pool", "pool_scale", "q_norm", "kv_norm", "norm_ffn", "norm_ple",
              "final_norm")
WEIGHTS = ("norm_mix", "w_in", "w_pool", "pool_scale", "q_norm", "kv_norm", "w_uq", "w_ukv",
           "w_a", "w_b", "w_o", "norm_ffn", "w_gate", "w_up", "w_down", "norm_ple",
           "w_ple_gate", "w_ple", "final_norm")
FLAT_COLS = 1024
FLAT_ROW_BLOCK = 192


def _params(*sem):
    return pltpu.CompilerParams(dimension_semantics=sem, vmem_limit_bytes=VMEM_LIMIT)


def _dot(a, b):
    return jnp.dot(a, b, preferred_element_type=F32)


def _dot_nt(a, b):
    return lax.dot_general(a, b, (((1,), (1,)), ((), ())), preferred_element_type=F32)


def _dot_tn(a, b):
    return lax.dot_general(a, b, (((0,), (0,)), ((), ())), preferred_element_type=F32)


def _rms_fwd(x, g):
    r = lax.rsqrt(jnp.mean(x * x, axis=-1, keepdims=True) + EPS)
    return x * r * g


def _rms_bwd(x, g, dy):
    r = lax.rsqrt(jnp.mean(x * x, axis=-1, keepdims=True) + EPS)
    xr = x * r
    gy = dy * g
    dx = r * (gy - xr * jnp.mean(gy * xr, axis=-1, keepdims=True))
    return dx, dy * xr


def _sigmoid(x):
    return 1.0 / (1.0 + jnp.exp(-x))


def _accum_rows(ref, rows):
    ref[...] += jnp.sum(rows, axis=0, keepdims=True)


def _band(band, x):
    h1 = x.astype(BF16)
    r1 = x - h1.astype(F32)
    h2 = r1.astype(BF16)
    h3 = (r1 - h2.astype(F32)).astype(BF16)
    return _dot(band, h1) + _dot(band, h2) + _dot(band, h3)


def _rope(x, c, s1, s2, sign):
    return x * c + sign * (pltpu.roll(x, 96, 1) * s1 + pltpu.roll(x, 32, 1) * s2)


def _full(shape):
    n = len(shape)
    return pl.BlockSpec(shape, lambda *_: (0,) * n)


def matmul_tn(a, b, *, tn, tk, name, tm=1024):
    T, M = a.shape
    N = b.shape[1]
    tm, tn, tk = min(tm, M), min(tn, N), min(tk, T)
    nk = T // tk

    def body(a_ref, b_ref, o_ref, acc_ref):
        k = pl.program_id(2)

        @pl.when(k == 0)
        def _():
            acc_ref[...] = jnp.zeros_like(acc_ref)

        acc_ref[...] += _dot_tn(a_ref[...].astype(BF16), b_ref[...].astype(BF16))

        @pl.when(k == nk - 1)
        def _():
            o_ref[...] = acc_ref[...].astype(BF16)

    return pl.pallas_call(
        body, name=name, grid=(M // tm, N // tn, nk),
        in_specs=[pl.BlockSpec((tk, tm), lambda i, j, k: (k, i)),
                  pl.BlockSpec((tk, tn), lambda i, j, k: (k, j))],
        out_specs=pl.BlockSpec((tm, tn), lambda i, j, k: (i, j)),
        out_shape=jax.ShapeDtypeStruct((M, N), BF16),
        scratch_shapes=[pltpu.VMEM((tm, tn), F32)],
        compiler_params=_params("parallel", "parallel", "arbitrary"),
    )(a, b)


def matmul_tn_pieces(a, b, *, tk, name):
    a3 = a.ndim == 3
    T = a.shape[-2]
    m, n = a.shape[-1], b.shape[-1]
    tk = min(tk, T)
    nk = T // tk
    out = (N_DEV, FF_PAD, n) if a3 else (N_DEV, m, FF_PAD)

    def body(a_ref, b_ref, o_ref, acc_ref):
        k = pl.program_id(0)

        @pl.when(k == 0)
        def _():
            acc_ref[...] = jnp.zeros_like(acc_ref)

        whole = (b_ref if a3 else a_ref)[...].astype(BF16)
        for j in range(FF_PAIRS):
            if a3:
                acc_ref[j] += _dot_tn(a_ref[j].astype(BF16), whole)
            else:
                acc_ref[j] += _dot_tn(whole, b_ref[j].astype(BF16))

        @pl.when(k == nk - 1)
        def _():
            for j in range(FF_PAIRS):
                for half in range(2):
                    cut = slice(half * FF_PAD, (half + 1) * FF_PAD)
                    piece = acc_ref[j, cut, :] if a3 else acc_ref[j, :, cut]
                    o_ref[2 * j + half] = piece.astype(BF16)

    pairs = lambda w: pl.BlockSpec((FF_PAIRS, tk, w), lambda k: (0, k, 0))
    whole = lambda w: pl.BlockSpec((tk, w), lambda k: (k, 0))
    return pl.pallas_call(
        body, name=name, grid=(nk,),
        in_specs=[pairs(m) if a3 else whole(m), whole(n) if a3 else pairs(n)],
        out_specs=_full(out),
        out_shape=jax.ShapeDtypeStruct(out, BF16),
        scratch_shapes=[pltpu.VMEM((FF_PAIRS, m, n), F32)],
        compiler_params=_params("arbitrary"),
    )(a, b)


def dw_in_proj(h, segs, *, tk):
    T = h.shape[0]
    tk = min(tk, T)
    nk = T // tk
    widths = [s.shape[1] for s, _ in segs]
    offs = [o for _, o in segs]

    def body(*refs):
        h_ref, seg_refs, o_ref, acc_ref = refs[0], refs[1:-2], refs[-2], refs[-1]
        k = pl.program_id(0)

        @pl.when(k == 0)
        def _():
            acc_ref[...] = jnp.zeros_like(acc_ref)

        hv = h_ref[...]
        for s_ref, off, w in zip(seg_refs, offs, widths):
            acc_ref[:, off:off + w] += _dot_tn(hv, s_ref[...])

        @pl.when(k == nk - 1)
        def _():
            o_ref[...] = acc_ref[...].astype(BF16)

    return pl.pallas_call(
        body, name="dw_in", grid=(nk,),
        in_specs=[pl.BlockSpec((tk, D_MODEL), lambda k: (k, 0))]
        + [pl.BlockSpec((tk, w), lambda k: (k, 0)) for w in widths],
        out_specs=_full((D_MODEL, IN_PAD)),
        out_shape=jax.ShapeDtypeStruct((D_MODEL, IN_PAD), BF16),
        scratch_shapes=[pltpu.VMEM((D_MODEL, IN_PAD), F32)],
        compiler_params=_params("arbitrary"),
    )(h, *[s for s, _ in segs])


def _piece_row_map():
    src = np.full(IN_PAD, -1, np.int64)
    for orig, pad, width in ((0, OFF_U, 512), (512, OFF_CQ, 512), (1024, OFF_CKV, 256),
                             (1280, OFF_KR, 64), (1344, OFF_GA, 1024), (2368, OFF_GB, 1024)):
        src[pad:pad + width] = np.arange(orig, orig + width)
    rows = np.where(src >= 0, (src // W_IN_PIECE) * PIECE_PAD + src % W_IN_PIECE, -1)
    return rows.astype(np.int32)[None, :]


def _selector(j, map_ref):
    rid = j * PIECE_PAD + lax.broadcasted_iota(jnp.int32, (PIECE_PAD, IN_PAD), 0)
    return jnp.where(rid == map_ref[...], 1.0, 0.0).astype(BF16)


def assemble_w_in(pieces, row_map, *, tr):
    def body(p_ref, map_ref, o_ref, acc_ref):
        j = pl.program_id(1)

        @pl.when(j == 0)
        def _():
            acc_ref[...] = jnp.zeros_like(acc_ref)

        acc_ref[...] += _dot(p_ref[0], _selector(j, map_ref))

        @pl.when(j == N_DEV - 1)
        def _():
            o_ref[...] = acc_ref[...].astype(BF16)

    return pl.pallas_call(
        body, name="assemble_w_in", grid=(D_MODEL // tr, N_DEV),
        in_specs=[pl.BlockSpec((1, tr, PIECE_PAD), lambda i, j: (j, i, 0)), _full((1, IN_PAD))],
        out_specs=pl.BlockSpec((tr, IN_PAD), lambda i, j: (i, 0)),
        out_shape=jax.ShapeDtypeStruct((D_MODEL, IN_PAD), BF16),
        scratch_shapes=[pltpu.VMEM((tr, IN_PAD), F32)],
        compiler_params=_params("parallel", "arbitrary"),
    )(pieces, row_map)


def split_dw_in(dwp, row_map):
    def body(d_ref, map_ref, o_ref):
        o_ref[0] = _dot_nt(d_ref[...], _selector(pl.program_id(0), map_ref)).astype(BF16)

    return pl.pallas_call(
        body, name="split_dw_in", grid=(N_DEV,),
        in_specs=[_full((D_MODEL, IN_PAD)), _full((1, IN_PAD))],
        out_specs=pl.BlockSpec((1, D_MODEL, PIECE_PAD), lambda j: (j, 0, 0)),
        out_shape=jax.ShapeDtypeStruct((N_DEV, D_MODEL, PIECE_PAD), BF16),
        compiler_params=_params("parallel"),
    )(dwp, row_map)


def rms_matmul(x, g, w, *, tm, tn, name):
    T, D = x.shape
    N = w.shape[1]
    tm = min(tm, T)

    def body(x_ref, g_ref, w_ref, o_ref, h_ref):
        @pl.when(pl.program_id(1) == 0)
        def _():
            h_ref[...] = _rms_fwd(x_ref[...], g_ref[...]).astype(BF16)

        o_ref[...] = _dot(h_ref[...], w_ref[...])

    return pl.pallas_call(
        body, name=name, grid=(T // tm, N // tn),
        in_specs=[pl.BlockSpec((tm, D), lambda i, j: (i, 0)), _full((1, D)),
                  pl.BlockSpec((D, tn), lambda i, j: (0, j))],
        out_specs=pl.BlockSpec((tm, tn), lambda i, j: (i, j)),
        out_shape=jax.ShapeDtypeStruct((T, N), F32),
        scratch_shapes=[pltpu.VMEM((tm, D), BF16)],
        compiler_params=_params("parallel", "arbitrary"),
    )(x, g, w)


def _band_matrices():
    s = np.arange(CHUNK)[:, None]
    t = np.arange(CHUNK)[None, :]
    low = np.stack([((s - t >= 0) & (s - t < w)) for w in POOL_WINDOWS]).astype(np.float32)
    up = np.stack([(t > s + CHUNK - w) for w in POOL_WINDOWS]).astype(np.float32)
    return low, up


def _window_count(row0, g):
    t = row0 + lax.broadcasted_iota(jnp.int32, (CHUNK, 1), 0)
    return jnp.minimum(t + 1, POOL_WINDOWS[g]).astype(F32)


def pool_fwd(z, low, up, w_pool, pool_scale, w_a, *, tm):
    T = z.shape[0]
    tm = min(tm, T)
    nch = tm // CHUNK
    ublk = OFF_U // POOL_WIDTH

    def body(u_ref, halo_ref, low_ref, up_ref, wp_ref, sc_ref, wa_ref, pooled_ref, ms_ref, ya_ref):
        i = pl.program_id(0)
        for c in range(nch):
            rows = slice(c * CHUNK, (c + 1) * CHUNK)
            for g in range(4):
                cols = slice(g * POOL_GROUP, (g + 1) * POOL_GROUP)
                cur = u_ref[rows, cols]
                if c == 0:
                    prev = jnp.where(i > 0, halo_ref[:, cols], 0.0)
                else:
                    prev = u_ref[(c - 1) * CHUNK:c * CHUNK, cols]
                s = _band(low_ref[g], cur) + _band(up_ref[g], prev)
                pooled = (s / _window_count(i * tm + c * CHUNK, g) - cur).astype(BF16)
                pooled_ref[rows, cols] = pooled
                ms_ref[rows, cols] = (_dot(pooled, wp_ref[g]) * sc_ref[:, cols]).astype(BF16)
        ya_ref[...] = _dot(ms_ref[...], wa_ref[...])

    return pl.pallas_call(
        body, name="pool_fwd", grid=(T // tm,),
        in_specs=[pl.BlockSpec((tm, POOL_WIDTH), lambda i: (i, ublk)),
                  pl.BlockSpec((CHUNK, POOL_WIDTH), lambda i: (jnp.maximum(i * nch - 1, 0), ublk)),
                  _full((4, CHUNK, CHUNK)), _full((4, CHUNK, CHUNK)),
                  _full((4, POOL_GROUP, POOL_GROUP)), _full((1, POOL_WIDTH)),
                  _full((POOL_WIDTH, D_MODEL))],
        out_specs=[pl.BlockSpec((tm, POOL_WIDTH), lambda i: (i, 0)),
                   pl.BlockSpec((tm, POOL_WIDTH), lambda i: (i, 0)),
                   pl.BlockSpec((tm, D_MODEL), lambda i: (i, 0))],
        out_shape=[jax.ShapeDtypeStruct((T, POOL_WIDTH), BF16),
                   jax.ShapeDtypeStruct((T, POOL_WIDTH), BF16),
                   jax.ShapeDtypeStruct((T, D_MODEL), F32)],
        compiler_params=_params("parallel"),
    )(z, z, low, up, w_pool, pool_scale, w_a)


def pool_bwd_a(dya, pooled, w_a, w_pool, pool_scale, *, tm):
    T = dya.shape[0]
    tm = min(tm, T)
    nch = tm // CHUNK

    def body(dya_ref, pooled_ref, wa_ref, wp_ref, sc_ref, dpc_ref, dsc_ref, dwp_ref):
        i = pl.program_id(0)

        @pl.when(i == 0)
        def _():
            dsc_ref[...] = jnp.zeros_like(dsc_ref)
            dwp_ref[...] = jnp.zeros_like(dwp_ref)

        dms = _dot_nt(dya_ref[...], wa_ref[...])
        for g in range(4):
            cols = slice(g * POOL_GROUP, (g + 1) * POOL_GROUP)
            pg = pooled_ref[:, cols]
            dmg = dms[:, cols]
            mixed = _dot(pg, wp_ref[g])
            dsc_ref[:, cols] += jnp.sum(dmg * mixed, axis=0, keepdims=True)
            dmixed = (dmg * sc_ref[:, cols]).astype(BF16)
            dwp_ref[g] += _dot_tn(pg, dmixed)
            dpooled = _dot_nt(dmixed, wp_ref[g])
            for c in range(nch):
                rows = slice(c * CHUNK, (c + 1) * CHUNK)
                dpc_ref[rows, cols] = dpooled[rows] / _window_count(i * tm + c * CHUNK, g)

    return pl.pallas_call(
        body, name="pool_bwd_a", grid=(T // tm,),
        in_specs=[pl.BlockSpec((tm, D_MODEL), lambda i: (i, 0)),
                  pl.BlockSpec((tm, POOL_WIDTH), lambda i: (i, 0)),
                  _full((POOL_WIDTH, D_MODEL)), _full((4, POOL_GROUP, POOL_GROUP)),
                  _full((1, POOL_WIDTH))],
        out_specs=[pl.BlockSpec((tm, POOL_WIDTH), lambda i: (i, 0)), _full((1, POOL_WIDTH)),
                   _full((4, POOL_GROUP, POOL_GROUP))],
        out_shape=[jax.ShapeDtypeStruct((T, POOL_WIDTH), F32),
                   jax.ShapeDtypeStruct((1, POOL_WIDTH), F32),
                   jax.ShapeDtypeStruct((4, POOL_GROUP, POOL_GROUP), F32)],
        compiler_params=_params("arbitrary"),
    )(dya, pooled, w_a, w_pool, pool_scale)


def pool_bwd_b(dpc, low_t, up_t, *, tm):
    T = dpc.shape[0]
    tm = min(tm, T)
    nch = tm // CHUNK
    last_chunk = T // CHUNK - 1

    def body(d_ref, halo_ref, low_ref, up_ref, du_ref):
        i = pl.program_id(0)
        for c in range(nch):
            rows = slice(c * CHUNK, (c + 1) * CHUNK)
            for g in range(4):
                cols = slice(g * POOL_GROUP, (g + 1) * POOL_GROUP)
                cur = d_ref[rows, cols]
                if c == nch - 1:
                    nxt = jnp.where(i < pl.num_programs(0) - 1, halo_ref[:, cols], 0.0)
                else:
                    nxt = d_ref[(c + 1) * CHUNK:(c + 2) * CHUNK, cols]
                s = _band(low_ref[g], cur) + _band(up_ref[g], nxt)
                du_ref[rows, cols] = (s - cur * _window_count(i * tm + c * CHUNK, g)).astype(BF16)

    return pl.pallas_call(
        body, name="pool_bwd_b", grid=(T // tm,),
        in_specs=[pl.BlockSpec((tm, POOL_WIDTH), lambda i: (i, 0)),
                  pl.BlockSpec((CHUNK, POOL_WIDTH),
                               lambda i: (jnp.minimum((i + 1) * nch, last_chunk), 0)),
                  _full((4, CHUNK, CHUNK)), _full((4, CHUNK, CHUNK))],
        out_specs=pl.BlockSpec((tm, POOL_WIDTH), lambda i: (i, 0)),
        out_shape=jax.ShapeDtypeStruct((T, POOL_WIDTH), BF16),
        compiler_params=_params("parallel"),
    )(dpc, dpc, low_t, up_t)


def mla_prep(z, cos_t, s1_t, s2_t, q_norm, kv_norm, w_uq, w_ukv, *, tm):
    T = z.shape[0]
    tm = min(tm, T)

    def body(cq_ref, ckv_ref, kr_ref, c_ref, s1_ref, s2_ref, qg_ref, kvg_ref, wuq_ref, wukv_ref,
             q_ref, k_ref, v_ref):
        c, s1, s2 = c_ref[...], s1_ref[...], s2_ref[...]
        qn = _rms_fwd(cq_ref[...], qg_ref[...]).astype(BF16)
        q = _dot(qn, wuq_ref[...])
        kvn = _rms_fwd(ckv_ref[...], kvg_ref[...]).astype(BF16)
        kv = _dot(kvn, wukv_ref[...])
        kpe = _rope(kr_ref[...], c, s1, s2, 1.0).astype(BF16)
        for h in range(N_HEADS):
            o = h * HEAD_PAD
            q_ref[h, :, 0:QK_NOPE] = (q[:, o:o + QK_NOPE] * ATTN_SCALE).astype(BF16)
            q_ref[h, :, QK_NOPE:HEAD_PAD] = (
                _rope(q[:, o + QK_NOPE:o + HEAD_PAD], c, s1, s2, 1.0) * ATTN_SCALE).astype(BF16)
            k_ref[h, :, 0:QK_NOPE] = kv[:, o:o + QK_NOPE].astype(BF16)
            k_ref[h, :, QK_NOPE:HEAD_PAD] = kpe
            v_ref[h] = kv[:, o + QK_NOPE:o + HEAD_PAD].astype(BF16)

    tok = lambda w: pl.BlockSpec((tm, w), lambda i: (i, 0))
    return pl.pallas_call(
        body, name="mla_prep", grid=(T // tm,),
        in_specs=[pl.BlockSpec((tm, Q_LORA), lambda i: (i, OFF_CQ // Q_LORA)),
                  pl.BlockSpec((tm, KV_LORA), lambda i: (i, OFF_CKV // KV_LORA)),
                  pl.BlockSpec((tm, LANES), lambda i: (i, OFF_KR // LANES)),
                  tok(LANES), tok(LANES), tok(LANES),
                  _full((1, Q_LORA)), _full((1, KV_LORA)),
                  _full((Q_LORA, N_HEADS * HEAD_PAD)), _full((KV_LORA, N_HEADS * HEAD_PAD))],
        out_specs=[pl.BlockSpec((N_HEADS, tm, HEAD_PAD), lambda i: (0, i, 0)),
                   pl.BlockSpec((N_HEADS, tm, HEAD_PAD), lambda i: (0, i, 0)),
                   pl.BlockSpec((N_HEADS, tm, V_HEAD), lambda i: (0, i, 0))],
        out_shape=[jax.ShapeDtypeStruct((N_HEADS, T, HEAD_PAD), BF16),
                   jax.ShapeDtypeStruct((N_HEADS, T, HEAD_PAD), BF16),
                   jax.ShapeDtypeStruct((N_HEADS, T, V_HEAD), BF16)],
        compiler_params=_params("parallel"),
    )(z, z, z, cos_t, s1_t, s2_t, q_norm, kv_norm, w_uq, w_ukv)


def _row_vector(col, n):
    return jnp.transpose(jnp.broadcast_to(col, (n, LANES)))[0:1, :]


def flash_fwd(q, k, v, *, tq):
    H, T, _ = q.shape
    tq = min(tq, T)
    nq = T // tq
    neg = -1e30

    def body(q_ref, k_ref, v_ref, o_ref, lse_ref):
        qi = pl.program_id(1)
        qb = q_ref[0]

        def tile(j, carry, masked):
            m, l, acc = carry
            rows = pl.ds(pl.multiple_of(j * tq, tq), tq)
            s = _dot_nt(qb, k_ref[0, rows, :])
            if masked:
                r = lax.broadcasted_iota(jnp.int32, (tq, tq), 0)
                c = lax.broadcasted_iota(jnp.int32, (tq, tq), 1)
                s = jnp.where(c <= r, s, neg)
            m_new = jnp.maximum(m, jnp.max(s, axis=-1, keepdims=True))
            alpha = jnp.exp(m - m_new)
            p = jnp.exp(s - m_new)
            l = alpha * l + jnp.sum(p, axis=-1, keepdims=True)
            acc = alpha * acc + _dot(p.astype(BF16), v_ref[0, rows, :])
            return m_new, l, acc

        init = (jnp.full((tq, 1), neg, F32), jnp.zeros((tq, 1), F32), jnp.zeros((tq, V_HEAD), F32))
        carry = lax.fori_loop(0, qi, lambda j, cr: tile(j, cr, False), init)
        m, l, acc = tile(qi, carry, True)
        o_ref[...] = (acc / l).astype(BF16)
        lse_ref[0, 0] = _row_vector(m + jnp.log(l), tq)

    return pl.pallas_call(
        body, name="flash_fwd", grid=(H, nq),
        in_specs=[pl.BlockSpec((1, tq, HEAD_PAD), lambda h, i: (h, i, 0)),
                  pl.BlockSpec((1, T, HEAD_PAD), lambda h, i: (h, 0, 0)),
                  pl.BlockSpec((1, T, V_HEAD), lambda h, i: (h, 0, 0))],
        out_specs=[pl.BlockSpec((tq, V_HEAD), lambda h, i: (i, h)),
                   pl.BlockSpec((1, 1, 1, tq), lambda h, i: (h, i, 0, 0))],
        out_shape=[jax.ShapeDtypeStruct((T, H * V_HEAD), BF16),
                   jax.ShapeDtypeStruct((H, nq, 1, tq), F32)],
        compiler_params=_params("parallel", "arbitrary"),
    )(q, k, v)


def attn_delta(do, o, *, tq):
    T = do.shape[0]
    tq = min(tq, T)
    nq = T // tq
    group = min(4, nq)

    def body(do_ref, o_ref, d_ref):
        for b in range(group):
            rows = slice(b * tq, (b + 1) * tq)
            prod = do_ref[rows, :].astype(F32) * o_ref[rows, :].astype(F32)
            d_ref[0, b] = jnp.sum(jnp.transpose(prod), axis=0, keepdims=True)

    return pl.pallas_call(
        body, name="attn_delta", grid=(N_HEADS, nq // group),
        in_specs=[pl.BlockSpec((group * tq, V_HEAD), lambda h, i: (i, h)),
                  pl.BlockSpec((group * tq, V_HEAD), lambda h, i: (i, h))],
        out_specs=pl.BlockSpec((1, group, 1, tq), lambda h, i: (h, i, 0, 0)),
        out_shape=jax.ShapeDtypeStruct((N_HEADS, nq, 1, tq), F32),
        compiler_params=_params("parallel", "parallel"),
    )(do, o)


def flash_bwd(q, k, v, do, lse, delta, *, tq):
    H, T, _ = q.shape
    tq = min(tq, T)
    nq = T // tq
    neg = -1e30

    def body(q_ref, do_ref, lse_ref, dl_ref, k_ref, v_ref, dq_ref, dk_ref, dv_ref, dq_acc):
        j = pl.program_id(1)

        @pl.when(j == 0)
        def _():
            dq_acc[...] = jnp.zeros_like(dq_acc)

        kb = k_ref[0]
        vb = v_ref[0]

        def tile(i, carry, masked):
            dk, dv = carry
            rows = pl.ds(pl.multiple_of(i * tq, tq), tq)
            qb = q_ref[0, rows, :]
            dob = do_ref[rows, :]
            st = _dot_nt(kb, qb)
            if masked:
                r = lax.broadcasted_iota(jnp.int32, (tq, tq), 0)
                c = lax.broadcasted_iota(jnp.int32, (tq, tq), 1)
                st = jnp.where(r <= c, st, neg)
            pt = jnp.exp(st - lse_ref[0, i])
            dv = dv + _dot(pt.astype(BF16), dob)
            dpt = _dot_nt(vb, dob)
            dst = (pt * (dpt - dl_ref[0, i])).astype(BF16)
            dk = dk + _dot(dst, qb)
            dq_acc[rows, :] += _dot_tn(dst, kb)
            return dk, dv

        carry = tile(j, (jnp.zeros((tq, HEAD_PAD), F32), jnp.zeros((tq, V_HEAD), F32)), True)
        dk, dv = lax.fori_loop(j + 1, nq, lambda i, cr: tile(i, cr, False), carry)
        dk_ref[0] = dk.astype(BF16)
        dv_ref[0] = dv.astype(BF16)

        @pl.when(j == nq - 1)
        def _():
            dq_ref[0] = dq_acc[...].astype(BF16)

    return pl.pallas_call(
        body, name="flash_bwd", grid=(H, nq),
        in_specs=[pl.BlockSpec((1, T, HEAD_PAD), lambda h, j: (h, 0, 0)),
                  pl.BlockSpec((T, V_HEAD), lambda h, j: (0, h)),
                  pl.BlockSpec((1, nq, 1, tq), lambda h, j: (h, 0, 0, 0)),
                  pl.BlockSpec((1, nq, 1, tq), lambda h, j: (h, 0, 0, 0)),
                  pl.BlockSpec((1, tq, HEAD_PAD), lambda h, j: (h, j, 0)),
                  pl.BlockSpec((1, tq, V_HEAD), lambda h, j: (h, j, 0))],
        out_specs=[pl.BlockSpec((1, T, HEAD_PAD), lambda h, j: (h, 0, 0)),
                   pl.BlockSpec((1, tq, HEAD_PAD), lambda h, j: (h, j, 0)),
                   pl.BlockSpec((1, tq, V_HEAD), lambda h, j: (h, j, 0))],
        out_shape=[jax.ShapeDtypeStruct((H, T, HEAD_PAD), BF16),
                   jax.ShapeDtypeStruct((H, T, HEAD_PAD), BF16),
                   jax.ShapeDtypeStruct((H, T, V_HEAD), BF16)],
        scratch_shapes=[pltpu.VMEM((T, HEAD_PAD), F32)],
        compiler_params=_params("parallel", "arbitrary"),
    )(q, do, lse, delta, k, v)


def mla_bwd(dq, dk, dv, z, cos_t, s1_t, s2_t, q_norm, kv_norm, w_uq, w_ukv, *, tm):
    T = z.shape[0]
    tm = min(tm, T)
    HW = N_HEADS * HEAD_PAD

    def body(dq_ref, dk_ref, dv_ref, cq_ref, ckv_ref, c_ref, s1_ref, s2_ref, qg_ref, kvg_ref,
             wuq_ref, wukv_ref, dcq_ref, dckv_ref, dkr_ref, dqf_ref, dkvf_ref, qn_ref, kvn_ref,
             dqg_ref, dkvg_ref):
        @pl.when(pl.program_id(0) == 0)
        def _():
            dqg_ref[...] = jnp.zeros_like(dqg_ref)
            dkvg_ref[...] = jnp.zeros_like(dkvg_ref)

        c, s1, s2 = c_ref[...], s1_ref[...], s2_ref[...]
        dkpe = jnp.zeros((tm, LANES), F32)
        for h in range(N_HEADS):
            o = h * HEAD_PAD
            dqf_ref[:, o:o + QK_NOPE] = (
                dq_ref[h, :, 0:QK_NOPE].astype(F32) * ATTN_SCALE).astype(BF16)
            dqf_ref[:, o + QK_NOPE:o + HEAD_PAD] = (
                _rope(dq_ref[h, :, QK_NOPE:HEAD_PAD].astype(F32), c, s1, s2, -1.0)
                * ATTN_SCALE).astype(BF16)
            dkvf_ref[:, o:o + QK_NOPE] = dk_ref[h, :, 0:QK_NOPE]
            dkvf_ref[:, o + QK_NOPE:o + HEAD_PAD] = dv_ref[h]
            dkpe = dkpe + dk_ref[h, :, QK_NOPE:HEAD_PAD].astype(F32)
        dkr_ref[...] = _rope(dkpe, c, s1, s2, -1.0).astype(BF16)

        cq = cq_ref[...]
        qn_ref[...] = _rms_fwd(cq, qg_ref[...]).astype(BF16)
        dcq, dgrows = _rms_bwd(cq, qg_ref[...], _dot_nt(dqf_ref[...], wuq_ref[...]))
        dcq_ref[...] = dcq.astype(BF16)
        _accum_rows(dqg_ref, dgrows)

        ckv = ckv_ref[...]
        kvn_ref[...] = _rms_fwd(ckv, kvg_ref[...]).astype(BF16)
        dckv, dgrows = _rms_bwd(ckv, kvg_ref[...], _dot_nt(dkvf_ref[...], wukv_ref[...]))
        dckv_ref[...] = dckv.astype(BF16)
        _accum_rows(dkvg_ref, dgrows)

    tok = lambda w: pl.BlockSpec((tm, w), lambda i: (i, 0))
    head = lambda w: pl.BlockSpec((N_HEADS, tm, w), lambda i: (0, i, 0))
    return pl.pallas_call(
        body, name="mla_bwd", grid=(T // tm,),
        in_specs=[head(HEAD_PAD), head(HEAD_PAD), head(V_HEAD),
                  pl.BlockSpec((tm, Q_LORA), lambda i: (i, OFF_CQ // Q_LORA)),
                  pl.BlockSpec((tm, KV_LORA), lambda i: (i, OFF_CKV // KV_LORA)),
                  tok(LANES), tok(LANES), tok(LANES),
                  _full((1, Q_LORA)), _full((1, KV_LORA)),
                  _full((Q_LORA, HW)), _full((KV_LORA, HW))],
        out_specs=[tok(Q_LORA), tok(KV_LORA), tok(LANES), tok(HW), tok(HW), tok(Q_LORA),
                   tok(KV_LORA), _full((1, Q_LORA)), _full((1, KV_LORA))],
        out_shape=[jax.ShapeDtypeStruct((T, Q_LORA), BF16),
                   jax.ShapeDtypeStruct((T, KV_LORA), BF16),
                   jax.ShapeDtypeStruct((T, LANES), BF16),
                   jax.ShapeDtypeStruct((T, HW), BF16),
                   jax.ShapeDtypeStruct((T, HW), BF16),
                   jax.ShapeDtypeStruct((T, Q_LORA), BF16),
                   jax.ShapeDtypeStruct((T, KV_LORA), BF16),
                   jax.ShapeDtypeStruct((1, Q_LORA), F32),
                   jax.ShapeDtypeStruct((1, KV_LORA), F32)],
        compiler_params=_params("arbitrary"),
    )(dq, dk, dv, z, z, cos_t, s1_t, s2_t, q_norm, kv_norm, w_uq, w_ukv)


def merge_fwd(o, ya, z, x, w_b, w_o, *, tm):
    T = x.shape[0]
    tm = min(tm, T)

    def body(o_ref, ya_ref, ga_ref, gb_ref, x_ref, wb_ref, wo_ref, yb_ref, mg_ref, x1_ref):
        yb = _dot(o_ref[...], wb_ref[...])
        yb_ref[...] = yb
        merged = (_sigmoid(ga_ref[...]) * ya_ref[...] + _sigmoid(gb_ref[...]) * yb).astype(BF16)
        mg_ref[...] = merged
        x1_ref[...] = x_ref[...] + _dot(merged, wo_ref[...])

    tok = pl.BlockSpec((tm, D_MODEL), lambda i: (i, 0))
    return pl.pallas_call(
        body, name="merge_fwd", grid=(T // tm,),
        in_specs=[tok, tok, pl.BlockSpec((tm, D_MODEL), lambda i: (i, OFF_GA // D_MODEL)),
                  pl.BlockSpec((tm, D_MODEL), lambda i: (i, OFF_GB // D_MODEL)), tok,
                  _full((D_MODEL, D_MODEL)), _full((D_MODEL, D_MODEL))],
        out_specs=[tok, tok, tok],
        out_shape=[jax.ShapeDtypeStruct((T, D_MODEL), F32),
                   jax.ShapeDtypeStruct((T, D_MODEL), BF16),
                   jax.ShapeDtypeStruct((T, D_MODEL), F32)],
        compiler_params=_params("parallel"),
    )(o, ya, z, z, x, w_b, w_o)


def merge_bwd(dx1, ya, yb, z, w_o, w_b, dep, *, tm):
    T = dx1.shape[0]
    tm = min(tm, T)

    def body(dx_ref, ya_ref, yb_ref, ga_ref, gb_ref, wo_ref, wb_ref, dep_ref, dya_ref, dyb_ref,
             do_ref, dga_ref, dgb_ref):
        dm = _dot_nt(dx_ref[...].astype(BF16), wo_ref[...])
        sa = _sigmoid(ga_ref[...])
        sb = _sigmoid(gb_ref[...])
        dya_ref[...] = (dm * sa).astype(BF16)
        dyb = (dm * sb).astype(BF16)
        dyb_ref[...] = dyb
        dga_ref[...] = (dm * ya_ref[...] * sa * (1.0 - sa)).astype(BF16)
        dgb_ref[...] = (dm * yb_ref[...] * sb * (1.0 - sb)).astype(BF16)
        do_ref[...] = _dot_nt(dyb, wb_ref[...]).astype(BF16)

    tok = pl.BlockSpec((tm, D_MODEL), lambda i: (i, 0))
    return pl.pallas_call(
        body, name="merge_bwd", grid=(T // tm,),
        in_specs=[tok, tok, tok, pl.BlockSpec((tm, D_MODEL), lambda i: (i, OFF_GA // D_MODEL)),
                  pl.BlockSpec((tm, D_MODEL), lambda i: (i, OFF_GB // D_MODEL)),
                  _full((D_MODEL, D_MODEL)), _full((D_MODEL, D_MODEL)),
                  pl.BlockSpec(memory_space=pl.ANY)],
        out_specs=[tok] * 5,
        out_shape=[jax.ShapeDtypeStruct((T, D_MODEL), BF16)] * 5,
        compiler_params=_params("parallel"),
    )(dx1, ya, yb, z, z, w_o, w_b, dep)


def in_bwd(dx1, x, g, segs, w_in, *, tm):
    T = x.shape[0]
    tm = min(tm, T)
    widths = [s.shape[1] for s, _ in segs]
    offs = [o for _, o in segs]
    n = len(segs)

    def body(*refs):
        dx1_ref, x_ref, g_ref = refs[:3]
        seg_refs = refs[3:3 + n]
        w_ref = refs[3 + n]
        dx_ref, h_ref, dg_ref = refs[4 + n:]

        @pl.when(pl.program_id(0) == 0)
        def _():
            dg_ref[...] = jnp.zeros_like(dg_ref)

        dh = jnp.zeros((tm, D_MODEL), F32)
        for s_ref, off, w in zip(seg_refs, offs, widths):
            dh = dh + _dot_nt(s_ref[...], w_ref[:, off:off + w])
        xv = x_ref[...]
        h_ref[...] = _rms_fwd(xv, g_ref[...]).astype(BF16)
        dxn, dgrows = _rms_bwd(xv, g_ref[...], dh)
        dx_ref[...] = dx1_ref[...] + dxn
        _accum_rows(dg_ref, dgrows)

    tok = pl.BlockSpec((tm, D_MODEL), lambda i: (i, 0))
    return pl.pallas_call(
        body, name="in_bwd", grid=(T // tm,),
        in_specs=[tok, tok, _full((1, D_MODEL))]
        + [pl.BlockSpec((tm, w), lambda i: (i, 0)) for w in widths] + [_full((D_MODEL, IN_PAD))],
        out_specs=[tok, tok, _full((1, D_MODEL))],
        out_shape=[jax.ShapeDtypeStruct((T, D_MODEL), F32),
                   jax.ShapeDtypeStruct((T, D_MODEL), BF16),
                   jax.ShapeDtypeStruct((1, D_MODEL), F32)],
        compiler_params=_params("arbitrary"),
    )(dx1, x, g, *[s for s, _ in segs], w_in)


def _pair_cols(w_ref):
    return jnp.concatenate([w_ref[0], w_ref[1]], axis=1)


def _pair_rows(w_ref):
    return jnp.concatenate([w_ref[0], w_ref[1]], axis=0)


def ffn_fwd(x1, g, w_gate, w_up, w_down, *, tm):
    T = x1.shape[0]
    tm = min(tm, T)

    def body(x_ref, g_ref, wg_ref, wu_ref, wd_ref, gp_ref, up_ref, x2_ref, h_ref, acc_ref):
        f = pl.program_id(1)

        @pl.when(f == 0)
        def _():
            h_ref[...] = _rms_fwd(x_ref[...], g_ref[...]).astype(BF16)
            acc_ref[...] = jnp.zeros_like(acc_ref)

        gp = _dot(h_ref[...], _pair_cols(wg_ref))
        up = _dot(h_ref[...], _pair_cols(wu_ref))
        gp_ref[0] = gp
        up_ref[0] = up
        act = (gp * _sigmoid(gp) * up).astype(BF16)
        acc_ref[...] += _dot(act, _pair_rows(wd_ref))

        @pl.when(f == FF_PAIRS - 1)
        def _():
            x2_ref[...] = x_ref[...] + acc_ref[...]

    tok = pl.BlockSpec((tm, D_MODEL), lambda i, f: (i, 0))
    ff = pl.BlockSpec((1, tm, FF_STEP), lambda i, f: (f, i, 0))
    w_col = pl.BlockSpec((2, D_MODEL, FF_PAD), lambda i, f: (f, 0, 0))
    w_row = pl.BlockSpec((2, FF_PAD, D_MODEL), lambda i, f: (f, 0, 0))
    return pl.pallas_call(
        body, name="ffn_fwd", grid=(T // tm, FF_PAIRS),
        in_specs=[tok, _full((1, D_MODEL)), w_col, w_col, w_row],
        out_specs=[ff, ff, tok],
        out_shape=[jax.ShapeDtypeStruct((FF_PAIRS, T, FF_STEP), F32),
                   jax.ShapeDtypeStruct((FF_PAIRS, T, FF_STEP), F32),
                   jax.ShapeDtypeStruct((T, D_MODEL), F32)],
        scratch_shapes=[pltpu.VMEM((tm, D_MODEL), BF16), pltpu.VMEM((tm, D_MODEL), F32)],
        compiler_params=_params("parallel", "arbitrary"),
    )(x1, g, w_gate, w_up, w_down)


def ffn_bwd(dx2, x1, g, gpre, upre, w_gate, w_up, w_down, *, tm):
    T = x1.shape[0]
    tm = min(tm, T)

    def body(dx_ref, x_ref, g_ref, gp_ref, up_ref, wg_ref, wu_ref, wd_ref,
             act_ref, dg_ref, du_ref, h_ref, dx1_ref, dgain_ref, dxb_ref, acc_ref):
        f = pl.program_id(1)

        @pl.when((pl.program_id(0) == 0) & (f == 0))
        def _():
            dgain_ref[...] = jnp.zeros_like(dgain_ref)

        @pl.when(f == 0)
        def _():
            dxb_ref[...] = dx_ref[...].astype(BF16)
            acc_ref[...] = jnp.zeros_like(acc_ref)

        gp = gp_ref[0]
        up = up_ref[0]
        dact = _dot_nt(dxb_ref[...], _pair_rows(wd_ref))
        sg = _sigmoid(gp)
        silu = gp * sg
        act_ref[0] = (silu * up).astype(BF16)
        dgp = (dact * up * (sg * (1.0 + gp * (1.0 - sg)))).astype(BF16)
        dup = (dact * silu).astype(BF16)
        dg_ref[0] = dgp
        du_ref[0] = dup
        acc_ref[...] += _dot_nt(dgp, _pair_cols(wg_ref)) + _dot_nt(dup, _pair_cols(wu_ref))

        @pl.when(f == FF_PAIRS - 1)
        def _():
            xv = x_ref[...]
            h_ref[...] = _rms_fwd(xv, g_ref[...]).astype(BF16)
            dxn, dgrows = _rms_bwd(xv, g_ref[...], acc_ref[...])
            dx1_ref[...] = dx_ref[...] + dxn
            _accum_rows(dgain_ref, dgrows)

    tok = pl.BlockSpec((tm, D_MODEL), lambda i, f: (i, 0))
    ff = pl.BlockSpec((1, tm, FF_STEP), lambda i, f: (f, i, 0))
    w_col = pl.BlockSpec((2, D_MODEL, FF_PAD), lambda i, f: (f, 0, 0))
    w_row = pl.BlockSpec((2, FF_PAD, D_MODEL), lambda i, f: (f, 0, 0))
    return pl.pallas_call(
        body, name="ffn_bwd", grid=(T // tm, FF_PAIRS),
        in_specs=[tok, tok, _full((1, D_MODEL)), ff, ff, w_col, w_col, w_row],
        out_specs=[ff, ff, ff, tok, tok, _full((1, D_MODEL))],
        out_shape=[jax.ShapeDtypeStruct((FF_PAIRS, T, FF_STEP), BF16)] * 3
        + [jax.ShapeDtypeStruct((T, D_MODEL), BF16), jax.ShapeDtypeStruct((T, D_MODEL), F32),
           jax.ShapeDtypeStruct((1, D_MODEL), F32)],
        scratch_shapes=[pltpu.VMEM((tm, D_MODEL), BF16), pltpu.VMEM((tm, D_MODEL), F32)],
        compiler_params=_params("arbitrary", "arbitrary"),
    )(dx2, x1, g, gpre, upre, w_gate, w_up, w_down)


def ple_fwd(x2, p, g, w_pg, w_ple, *, tm):
    T = x2.shape[0]
    tm = min(tm, T)

    def body(x_ref, p_ref, g_ref, wpg_ref, wple_ref, l_ref, pe_ref, x3_ref):
        xv = x_ref[...]
        logits = _dot(_rms_fwd(xv, g_ref[...]).astype(BF16), wpg_ref[...])
        pe = _dot(p_ref[...].astype(BF16), wple_ref[...])
        l_ref[...] = logits
        pe_ref[...] = pe
        x3_ref[...] = xv + _sigmoid(logits) * pe

    tok = pl.BlockSpec((tm, D_MODEL), lambda i: (i, 0))
    return pl.pallas_call(
        body, name="ple_fwd", grid=(T // tm,),
        in_specs=[tok, pl.BlockSpec((tm, PLE_DIM), lambda i: (i, 0)), _full((1, D_MODEL)),
                  _full((D_MODEL, D_MODEL)), _full((PLE_DIM, D_MODEL))],
        out_specs=[tok, tok, tok],
        out_shape=[jax.ShapeDtypeStruct((T, D_MODEL), F32)] * 3,
        compiler_params=_params("parallel"),
    )(x2, p, g, w_pg, w_ple)


def ple_bwd(dx3, logits, pe, x2, g, w_pg, *, tm):
    T = x2.shape[0]
    tm = min(tm, T)

    def body(dx_ref, l_ref, pe_ref, x_ref, g_ref, wpg_ref, dl_ref, dpe_ref, h_ref, dx2_ref,
             dg_ref):
        @pl.when(pl.program_id(0) == 0)
        def _():
            dg_ref[...] = jnp.zeros_like(dg_ref)

        dx = dx_ref[...]
        s = _sigmoid(l_ref[...])
        dpe_ref[...] = (dx * s).astype(BF16)
        dl = (dx * pe_ref[...] * s * (1.0 - s)).astype(BF16)
        dl_ref[...] = dl
        xv = x_ref[...]
        h_ref[...] = _rms_fwd(xv, g_ref[...]).astype(BF16)
        dxn, dgrows = _rms_bwd(xv, g_ref[...], _dot_nt(dl, wpg_ref[...]))
        dx2_ref[...] = dx + dxn
        _accum_rows(dg_ref, dgrows)

    tok = pl.BlockSpec((tm, D_MODEL), lambda i: (i, 0))
    return pl.pallas_call(
        body, name="ple_bwd", grid=(T // tm,),
        in_specs=[tok, tok, tok, tok, _full((1, D_MODEL)), _full((D_MODEL, D_MODEL))],
        out_specs=[tok, tok, tok, tok, _full((1, D_MODEL))],
        out_shape=[jax.ShapeDtypeStruct((T, D_MODEL), BF16)] * 3
        + [jax.ShapeDtypeStruct((T, D_MODEL), F32), jax.ShapeDtypeStruct((1, D_MODEL), F32)],
        compiler_params=_params("arbitrary"),
    )(dx3, logits, pe, x2, g, w_pg)


def loss_head(x, g, target, *, tm):
    T = x.shape[0]
    tm = min(tm, T)

    def body(x_ref, g_ref, t_ref, loss_ref, dx_ref, dg_ref):
        @pl.when(pl.program_id(0) == 0)
        def _():
            loss_ref[...] = jnp.zeros_like(loss_ref)
            dg_ref[...] = jnp.zeros_like(dg_ref)

        xv = x_ref[...]
        err = _rms_fwd(xv, g_ref[...]) - t_ref[...]
        part = 0.5 * jnp.sum(jnp.mean(err * err, axis=-1, keepdims=True), axis=0, keepdims=True)
        lane = lax.broadcasted_iota(jnp.int32, (1, LANES), 1)
        loss_ref[...] += jnp.where(lane == 0, part, 0.0)
        dxn, dgrows = _rms_bwd(xv, g_ref[...], err * (1.0 / D_MODEL))
        dx_ref[...] = dxn
        _accum_rows(dg_ref, dgrows)

    tok = pl.BlockSpec((tm, D_MODEL), lambda i: (i, 0))
    return pl.pallas_call(
        body, name="loss_head", grid=(T // tm,),
        in_specs=[tok, _full((1, D_MODEL)), tok],
        out_specs=[_full((1, LANES)), tok, _full((1, D_MODEL))],
        out_shape=[jax.ShapeDtypeStruct((1, LANES), F32), jax.ShapeDtypeStruct((T, D_MODEL), F32),
                   jax.ShapeDtypeStruct((1, D_MODEL), F32)],
        compiler_params=_params("arbitrary"),
    )(x, g, target)


def _peers():
    x, y, c = lax.axis_index("x"), lax.axis_index("y"), lax.axis_index("c")
    me = 4 * x + 2 * y + c
    out = []
    for d in range(1, N_DEV):
        px = 1 - x if d & 4 else x
        py = 1 - y if d & 2 else y
        pc = 1 - c if d & 1 else c
        out.append(((px, py, pc), 4 * px + 2 * py + pc))
    return me, out


_HBM = pl.BlockSpec(memory_space=pltpu.HBM)
_SEM = pl.BlockSpec(memory_space=pltpu.SEMAPHORE)
_ANY = pl.BlockSpec(memory_space=pl.ANY)
_DATAFLOW = pltpu.SideEffectType.DATAFLOW_SIDE_EFFECTING


def _exchange_copy(t, d, pos, idx, me, src_ref, scatter, land_ref, send_sems, recv_sems):
    k = t * (N_DEV - 1) + d
    return pltpu.make_async_remote_copy(
        src_ref=src_ref.at[idx] if scatter else src_ref, dst_ref=land_ref.at[me],
        send_sem=send_sems.at[k], recv_sem=recv_sems.at[k],
        device_id=pos, device_id_type=pl.DeviceIdType.MESH)


def exchange_start(gathers, scatters, dep, *, name):
    srcs = [pltpu.with_memory_space_constraint(a, pltpu.HBM) for a in list(gathers) + list(scatters)]
    kinds = [False] * len(gathers) + [True] * len(scatters)
    lands = [pltpu.with_memory_space_constraint(
        lax.empty((N_DEV,) + (a.shape[1:] if sc else a.shape), a.dtype), pltpu.HBM)
        for a, sc in zip(srcs, kinds)]
    n = len(srcs)

    def body(*refs):
        src_refs, land_refs = refs[:n], refs[n:2 * n]
        send_sems, recv_sems = refs[-2 * n - 4], refs[-2 * n - 3]
        token, local_sems = refs[-2], refs[-1]
        me, peers = _peers()
        locals_ = []
        for t in range(n):
            local = pltpu.make_async_copy(src_refs[t].at[me] if kinds[t] else src_refs[t],
                                          land_refs[t].at[me], local_sems.at[t])
            local.start()
            locals_.append(local)
        for t in range(n):
            for d, (pos, idx) in enumerate(peers):
                _exchange_copy(t, d, pos, idx, me, src_refs[t], kinds[t], land_refs[t],
                               send_sems, recv_sems).start()
        for local in locals_:
            local.wait()
        token[...] = jnp.zeros_like(token)

    sem = pltpu.SemaphoreType.DMA((n * (N_DEV - 1),))
    thru = [pltpu.HBM(a.shape, a.dtype) for a in srcs + lands]
    outs = pl.pallas_call(
        body, name=name,
        in_specs=[_HBM] * (2 * n) + ([_ANY] if dep is not None else []),
        out_specs=[_SEM, _SEM] + [_HBM] * (2 * n) + [pl.BlockSpec(memory_space=pltpu.VMEM)],
        out_shape=[sem, sem] + thru + [jax.ShapeDtypeStruct((8, LANES), F32)],
        input_output_aliases={i: 2 + i for i in range(2 * n)},
        scratch_shapes=[pltpu.SemaphoreType.DMA((n,))],
        compiler_params=pltpu.CompilerParams(has_side_effects=_DATAFLOW),
    )(*srcs, *lands, *([dep] if dep is not None else []))
    return dict(sems=outs[:2], srcs=outs[2:2 + n], lands=outs[2 + n:2 + 2 * n], token=outs[-1],
                kinds=kinds)


def exchange_wait(handle, after, *, name):
    kinds = handle["kinds"]
    n = len(kinds)

    def body(*refs):
        src_refs, land_refs = refs[:n], refs[n:2 * n]
        send_sems, recv_sems = refs[2 * n], refs[2 * n + 1]
        me, peers = _peers()
        for t in range(n):
            for d, (pos, idx) in enumerate(peers):
                cp = _exchange_copy(t, d, pos, idx, me, src_refs[t], kinds[t], land_refs[t],
                                    send_sems, recv_sems)
                cp.wait_send()
                cp.wait_recv()

    arrays = list(handle["srcs"]) + list(handle["lands"])
    outs = pl.pallas_call(
        body, name=name,
        in_specs=[_HBM] * (2 * n) + [_SEM, _SEM] + ([_ANY] if after is not None else []),
        out_specs=[_HBM] * (2 * n),
        out_shape=[pltpu.HBM(a.shape, a.dtype) for a in arrays],
        input_output_aliases={i: i for i in range(2 * n)},
        compiler_params=pltpu.CompilerParams(has_side_effects=_DATAFLOW),
    )(*arrays, *handle["sems"], *([after] if after is not None else []))
    return outs[n:]


def _row_block(rows, cols):
    best = None
    for rb in range(16, rows + 1, 16):
        if rows % rb == 0 and rb * cols <= 256 * 1024:
            best = rb
    return best or rows


def reduce_adamw(parts, w, m, v, *, name):
    nl = len(parts)
    C = w.shape[1]
    R, cp = parts[0].shape[1:]
    rb = _row_block(R, cp)
    nb = R // rb

    def body(*refs):
        p_refs = refs[:nl]
        w_ref, m_ref, v_ref, g_ref, d_ref, m2_ref, v2_ref = refs[nl:]

        def total(p_ref):
            g = p_ref[0, :, 0:C].astype(F32)
            for k in range(1, N_DEV):
                g = g + p_ref[k, :, 0:C].astype(F32)
            return g

        g = total(p_refs[0])
        for layer in range(1, nl):
            g = jnp.where(pl.program_id(0) == layer, total(p_refs[layer]), g)
        g_ref[...] = g
        m2 = ADAM_B1 * m_ref[...] + (1.0 - ADAM_B1) * g
        v2 = ADAM_B2 * v_ref[...] + (1.0 - ADAM_B2) * (g * g)
        m2_ref[...] = m2
        v2_ref[...] = v2
        m_hat = m2 / (1.0 - ADAM_B1 ** ADAM_STEP)
        v_hat = v2 / (1.0 - ADAM_B2 ** ADAM_STEP)
        d_ref[...] = -ADAM_LR * (m_hat / (jnp.sqrt(v_hat) + ADAM_EPS) + ADAM_WD * w_ref[...])

    def part_spec(layer):
        return pl.BlockSpec((N_DEV, rb, cp), lambda l, i: (0, jnp.where(l == layer, i, 0), 0))

    blk = pl.BlockSpec((rb, C), lambda l, i: (l * nb + i, 0))
    return pl.pallas_call(
        body, name=name, grid=(nl, nb),
        in_specs=[part_spec(layer) for layer in range(nl)] + [blk, blk, blk],
        out_specs=[blk] * 4,
        out_shape=[jax.ShapeDtypeStruct((nl * R, C), F32)] * 4,
        compiler_params=_params("arbitrary", "arbitrary"),
    )(*parts, w, m, v)


def _cols_to_pieces(full):
    r = full.shape[0]
    return jnp.transpose(full.reshape(r, N_DEV, -1), (1, 0, 2))


def _pieces_to_cols(pieces):
    r = pieces.shape[1]
    return jnp.transpose(pieces, (1, 0, 2)).reshape(r, -1)


def _rope_tables(positions):
    inv_freq = 1.0 / (ROPE_THETA ** (jnp.arange(0, QK_ROPE, 2, dtype=F32) / QK_ROPE))
    ang = positions.astype(F32)[:, None] * inv_freq
    cos, sin = jnp.cos(ang), jnp.sin(ang)
    zero = jnp.zeros_like(cos)
    return (jnp.concatenate([cos, cos, zero, zero], axis=1),
            jnp.concatenate([-sin, zero, zero, zero], axis=1),
            jnp.concatenate([zero, sin, zero, zero], axis=1))


def kernel(x, p, positions, norm_mix, w_in, w_pool, pool_scale, q_norm, kv_norm, w_uq, w_ukv, w_a, w_b, w_o, norm_ffn, w_gate, w_up, w_down, norm_ple, w_ple_gate, w_ple, final_norm, loss_target, m_norm_mix, m_w_in, m_w_pool, m_pool_scale, m_q_norm, m_kv_norm, m_w_uq, m_w_ukv, m_w_a, m_w_b, m_w_o, m_norm_ffn, m_w_gate, m_w_up, m_w_down, m_norm_ple, m_w_ple_gate, m_w_ple, m_final_norm, v_norm_mix, v_w_in, v_w_pool, v_pool_scale, v_q_norm, v_kv_norm, v_w_uq, v_w_ukv, v_w_a, v_w_b, v_w_o, v_norm_ffn, v_w_gate, v_w_up, v_w_down, v_norm_ple, v_w_ple_gate, v_w_ple, v_final_norm):
    given = dict(locals())
    shard = {n: given[n] for n in WEIGHTS}
    TM = 512
    TQ = 512

    def send_form(n, i):
        w = shard[n][i].astype(BF16)
        if n == "w_in":
            return jnp.pad(w, ((0, 0), (0, PIECE_PAD - W_IN_PIECE)))
        if n == "w_uq":
            w = jnp.pad(w, ((0, 0), (0, 0), (0, HEAD_PAD - QK_HEAD)))
        if n in ("w_uq", "w_ukv"):
            return w.reshape(-1, N_HEADS * HEAD_PAD)
        if n in ("w_gate", "w_up"):
            return jnp.pad(w, ((0, 0), (0, FF_PAD - FF_PIECE)))
        if n == "w_down":
            return jnp.pad(w, ((0, FF_PAD - FF_PIECE), (0, 0)))
        return w

    names = [n for n, _ in SHARDED]
    FIRST = ("w_in", "w_uq", "w_ukv", "w_a", "w_b", "w_o")
    LATER = ("w_gate", "w_up", "w_down", "w_ple_gate", "w_ple")
    start_a = exchange_start([send_form(n, 0) for n in FIRST], [], None, name="gather_start_a")
    got_a = exchange_wait(start_a, None, name="gather_wait_a")
    rest = [(n, 0) for n in LATER] + [(n, 1) for n in names]
    start_b = exchange_start([send_form(n, i) for n, i in rest], [], got_a[0],
                             name="gather_start_b")
    gathered = {(n, 0): a for n, a in zip(FIRST, got_a)}
    row_map = jnp.asarray(_piece_row_map())

    low_np, up_np = _band_matrices()
    low, up = jnp.asarray(low_np, BF16), jnp.asarray(up_np, BF16)
    low_t = jnp.asarray(low_np.transpose(0, 2, 1), BF16)
    up_t = jnp.asarray(up_np.transpose(0, 2, 1), BF16)
    cos_t, s1_t, s2_t = _rope_tables(positions[0])

    def mix_weights(i):
        g = lambda n: gathered[n, i]
        return dict(
            w_in=assemble_w_in(g("w_in"), row_map, tr=512),
            w_uq=g("w_uq").reshape(Q_LORA, N_HEADS * HEAD_PAD),
            w_ukv=g("w_ukv").reshape(KV_LORA, N_HEADS * HEAD_PAD),
            w_a=_pieces_to_cols(g("w_a")), w_b=g("w_b").reshape(D_MODEL, D_MODEL),
            w_o=g("w_o").reshape(D_MODEL, D_MODEL),
            w_pool=w_pool[i].astype(BF16),
            pool_scale=pool_scale[i][None], norm_mix=norm_mix[i][None], q_norm=q_norm[i][None],
            kv_norm=kv_norm[i][None], norm_ffn=norm_ffn[i][None], norm_ple=norm_ple[i][None])

    def channel_weights(i):
        g = lambda n: gathered[n, i]
        return dict(w_gate=g("w_gate"), w_up=g("w_up"), w_down=g("w_down"),
                    w_pg=g("w_ple_gate").reshape(D_MODEL, D_MODEL),
                    w_ple=_pieces_to_cols(g("w_ple")))

    def forward_mix(L, xs):
        z = rms_matmul(xs, L["norm_mix"], L["w_in"], tm=TM, tn=IN_PAD, name="in_proj")
        pooled, ms, ya = pool_fwd(z, low, up, L["w_pool"], L["pool_scale"], L["w_a"], tm=TM)
        q, k, v = mla_prep(z, cos_t, s1_t, s2_t, L["q_norm"], L["kv_norm"], L["w_uq"], L["w_ukv"],
                           tm=TM)
        o, lse = flash_fwd(q, k, v, tq=TQ)
        return dict(x=xs, z=z, pooled=pooled, ms=ms, ya=ya, q=q, k=k, v=v, o=o, lse=lse)

    def forward_rest(L, S, i):
        yb, merged, x1 = merge_fwd(S["o"], S["ya"], S["z"], S["x"], L["w_b"], L["w_o"], tm=TM)
        gpre, upre, x2 = ffn_fwd(x1, L["norm_ffn"], L["w_gate"], L["w_up"], L["w_down"], tm=TM)
        logits, pe, x3 = ple_fwd(x2, p[i, 0], L["norm_ple"], L["w_pg"], L["w_ple"], tm=TM)
        S.update(yb=yb, merged=merged, x1=x1, gpre=gpre, upre=upre, x2=x2, logits=logits, pe=pe)
        return x3

    layers, saved = [], []
    L = mix_weights(0)
    first = dict(L, norm_mix=L["norm_mix"] + start_b["token"][0, 0])
    S = forward_mix(first, x[0])
    got_b = exchange_wait(start_b, S["o"], name="gather_wait_b")
    gathered.update(dict(zip(rest, got_b)))
    L.update(channel_weights(0))
    xs = forward_rest(L, S, 0)
    layers.append(L)
    saved.append(S)
    L = dict(mix_weights(1), **channel_weights(1))
    S = forward_mix(L, xs)
    xs = forward_rest(L, S, 1)
    layers.append(L)
    saved.append(S)

    loss_part, dx, d_final = loss_head(xs, final_norm[None], loss_target[0], tm=TM)

    grads = {n: [None] * DEPTH for n in REPLICATED if n != "final_norm"}
    pieces = {n: [None] * DEPTH for n in names}
    tn_mm = functools.partial(matmul_tn, tn=1024, tk=1024)
    row_pieces = lambda a: a.reshape(N_DEV, -1, a.shape[-1])
    start_1 = None
    for i in reversed(range(DEPTH)):
        L, S = layers[i], saved[i]
        g_ple = L["norm_ple"] if start_1 is None else L["norm_ple"] + start_1["token"][0, 0]
        dl, dpe, hn, dx2, d_norm_ple = ple_bwd(dx, S["logits"], S["pe"], S["x2"], g_ple,
                                               L["w_pg"], tm=TM)
        grads["norm_ple"][i] = d_norm_ple[0]
        pieces["w_ple_gate"][i] = row_pieces(tn_mm(hn, dl, name="dw_ple_gate"))
        pieces["w_ple"][i] = _cols_to_pieces(tn_mm(p[i, 0], dpe, name="dw_ple"))

        act, dgp, dup, h2, dx1, d_norm_ffn = ffn_bwd(dx2, S["x1"], L["norm_ffn"], S["gpre"],
                                                     S["upre"], L["w_gate"], L["w_up"],
                                                     L["w_down"], tm=TM)
        grads["norm_ffn"][i] = d_norm_ffn[0]
        pieces["w_down"][i] = matmul_tn_pieces(act, dx2, tk=512, name="dw_down")
        pieces["w_gate"][i] = matmul_tn_pieces(h2, dgp, tk=512, name="dw_gate")
        pieces["w_up"][i] = matmul_tn_pieces(h2, dup, tk=512, name="dw_up")
        if i == 0:
            start_0a = exchange_start([], [pieces[n][0] for n in LATER], None,
                                      name="scatter_start_0a")
        dep = start_0a["token"] if i == 0 else loss_part

        dya, dyb, do, dga, dgb = merge_bwd(dx1, S["ya"], S["yb"], S["z"], L["w_o"], L["w_b"],
                                           dep, tm=TM)
        pieces["w_o"][i] = row_pieces(tn_mm(S["merged"], dx1, name="dw_o"))
        pieces["w_b"][i] = row_pieces(tn_mm(S["o"], dyb, name="dw_b"))

        delta = attn_delta(do, S["o"], tq=TQ)
        dq, dk, dv = flash_bwd(S["q"], S["k"], S["v"], do, S["lse"], delta, tq=TQ)
        dcq, dckv, dkr, dqf, dkvf, qn, kvn, d_q_norm, d_kv_norm = mla_bwd(
            dq, dk, dv, S["z"], cos_t, s1_t, s2_t, L["q_norm"], L["kv_norm"], L["w_uq"],
            L["w_ukv"], tm=TM)
        grads["q_norm"][i] = d_q_norm[0]
        grads["kv_norm"][i] = d_kv_norm[0]
        d_w_uq = tn_mm(qn, dqf, name="dw_uq")
        pieces["w_uq"][i] = d_w_uq.reshape(N_DEV, -1, N_HEADS, HEAD_PAD)[..., :QK_HEAD]
        pieces["w_ukv"][i] = tn_mm(kvn, dkvf, name="dw_ukv").reshape(
            N_DEV, -1, N_HEADS, QK_NOPE + V_HEAD)

        dpc, d_pool_scale, d_w_pool = pool_bwd_a(dya, S["pooled"], L["w_a"], L["w_pool"],
                                                 L["pool_scale"], tm=TM)
        grads["pool_scale"][i] = d_pool_scale[0]
        grads["w_pool"][i] = d_w_pool
        pieces["w_a"][i] = _cols_to_pieces(tn_mm(S["ms"], dya, name="dw_a"))
        du = pool_bwd_b(dpc, low_t, up_t, tm=TM)

        segs = [(dga, OFF_GA), (dgb, OFF_GB), (du, OFF_U), (dcq, OFF_CQ), (dckv, OFF_CKV),
                (dkr, OFF_KR)]
        dx, h, d_norm_mix = in_bwd(dx1, S["x"], L["norm_mix"], segs, L["w_in"], tm=TM)
        grads["norm_mix"][i] = d_norm_mix[0]
        pieces["w_in"][i] = split_dw_in(dw_in_proj(h, segs, tk=512), row_map)
        if i == 1:
            start_1 = exchange_start([], [pieces[n][1] for n in names], None,
                                     name="scatter_start_1")

    small = {n: jnp.stack(g) for n, g in grads.items()}
    small["final_norm"] = d_final[0]
    rep = jnp.concatenate([small[n].reshape(-1) for n in REPLICATED] + [loss_part.reshape(-1)])
    n_small = -(-rep.shape[0] // (8 * FLAT_COLS)) * (8 * FLAT_COLS)

    def small_flat(parts):
        flat = jnp.concatenate([a.reshape(-1) for a in parts])
        return jnp.pad(flat, (0, n_small - flat.shape[0])).reshape(-1, FLAT_COLS)

    start_0b = exchange_start([small_flat([rep])], [pieces[n][0] for n in FIRST], None,
                              name="scatter_start_0b")
    got_1 = exchange_wait(start_1, start_0b["token"], name="scatter_wait_1")
    got_0a = exchange_wait(start_0a, start_0b["token"], name="scatter_wait_0a")
    got_0b = exchange_wait(start_0b, None, name="scatter_wait_0b")
    parts = {(n, 1): a for n, a in zip(names, got_1)}
    parts.update({(n, 0): a for n, a in zip(LATER, got_0a)})
    parts.update({(n, 0): a for n, a in zip(FIRST, got_0b[1:])})
    for i in range(DEPTH):
        parts["w_down", i] = parts["w_down", i][:, :FF_PIECE]

    results = {}
    for n in names:
        w = shard[n]
        rows2d = lambda a: a.reshape(-1, w.shape[-1])
        w2 = rows2d(w)
        per_layer = [parts[n, i].reshape(N_DEV, w2.shape[0] // DEPTH, -1) for i in range(DEPTH)]
        out = reduce_adamw(per_layer, w2, rows2d(given["m_" + n]), rows2d(given["v_" + n]),
                           name="adamw_" + n)
        results[n] = [a.reshape(w.shape) for a in out]

    out = reduce_adamw([got_0b[0]], *[small_flat([given[pre + n] for n in REPLICATED])
                                      for pre in ("", "m_", "v_")], name="adamw_small")
    loss = None
    for n in REPLICATED:
        results[n] = []
    for a in out:
        a = a.reshape(-1)
        off = 0
        for n in REPLICATED:
            size = int(np.prod(shard[n].shape))
            results[n].append(a[off:off + size].reshape(shard[n].shape))
            off += size
        if loss is None:
            loss = a[off]
    return (loss, dx[None], *[results[n][0] for n in WEIGHTS], *[results[n][1] for n in WEIGHTS],
            *[results[n][2] for n in WEIGHTS], *[results[n][3] for n in WEIGHTS])
```

```python
import functools

import numpy as np
import jax
import jax.numpy as jnp
from jax import lax
from jax.experimental import pallas as pl
from jax.experimental.pallas import tpu as pltpu

F32 = jnp.float32
BF16 = jnp.bfloat16

D_MODEL = 1024
DEPTH = 2
PLE_DIM = 256
POOL_WINDOWS = (2, 4, 8, 16)
POOL_GROUP = 128
POOL_WIDTH = 512
N_HEADS = 8
Q_LORA = 512
KV_LORA = 256
QK_NOPE = 128
QK_ROPE = 64
QK_HEAD = 192
V_HEAD = 128
HEAD_PAD = 256
D_FF = 2816
ROPE_THETA = 10000.0
EPS = 1e-6
IN_WIDTH = 3392
ATTN_SCALE = QK_HEAD ** -0.5

OFF_GA, OFF_GB, OFF_U, OFF_CQ, OFF_CKV, OFF_KR = 0, 1024, 2048, 2560, 3072, 3328
IN_PAD = 3456
W_IN_PIECE = IN_WIDTH // 8
PIECE_PAD = 512
FF_PIECE = D_FF // 8
FF_PAD = 384
FF_PAIRS = 4
FF_STEP = 2 * FF_PAD

ADAM_LR = 0.001
ADAM_B1 = 0.9
ADAM_B2 = 0.999
ADAM_EPS = 1e-08
ADAM_WD = 0.01
ADAM_STEP = 10

N_DEV = 8
LANES = 128
CHUNK = 128
VMEM_LIMIT = 56 * 1024 * 1024

SHARDED = (("w_in", 2), ("w_uq", 1), ("w_ukv", 1), ("w_a", 2), ("w_b", 1), ("w_o", 1),
           ("w_gate", 2), ("w_up", 2), ("w_down", 1), ("w_ple_gate", 1), ("w_ple", 2))
REPLICATED = ("norm_mix", "w_pool", "pool_scale", "q_norm", "kv_norm", "norm_ffn", "norm_ple",
              "final_norm")
WEIGHTS = ("norm_mix", "w_in", "w_pool", "pool_scale", "q_norm", "kv_norm", "w_uq", "w_ukv",
           "w_a", "w_b", "w_o", "norm_ffn", "w_gate", "w_up", "w_down", "norm_ple",
           "w_ple_gate", "w_ple", "final_norm")
FLAT_COLS = 1024
FLAT_ROW_BLOCK = 192


def _params(*sem):
    return pltpu.CompilerParams(dimension_semantics=sem, vmem_limit_bytes=VMEM_LIMIT)


def _dot(a, b):
    return jnp.dot(a, b, preferred_element_type=F32)


def _dot_nt(a, b):
    return lax.dot_general(a, b, (((1,), (1,)), ((), ())), preferred_element_type=F32)


def _dot_tn(a, b):
    return lax.dot_general(a, b, (((0,), (0,)), ((), ())), preferred_element_type=F32)


def _rms_fwd(x, g):
    r = lax.rsqrt(jnp.mean(x * x, axis=-1, keepdims=True) + EPS)
    return x * r * g


def _rms_bwd(x, g, dy):
    r = lax.rsqrt(jnp.mean(x * x, axis=-1, keepdims=True) + EPS)
    xr = x * r
    gy = dy * g
    dx = r * (gy - xr * jnp.mean(gy * xr, axis=-1, keepdims=True))
    return dx, dy * xr


def _sigmoid(x):
    return 1.0 / (1.0 + jnp.exp(-x))


def _accum_rows(ref, rows):
    ref[...] += jnp.sum(rows, axis=0, keepdims=True)


def _band(band, x):
    h1 = x.astype(BF16)
    r1 = x - h1.astype(F32)
    h2 = r1.astype(BF16)
    h3 = (r1 - h2.astype(F32)).astype(BF16)
    return _dot(band, h1) + _dot(band, h2) + _dot(band, h3)


def _rope(x, c, s1, s2, sign):
    return x * c + sign * (pltpu.roll(x, 96, 1) * s1 + pltpu.roll(x, 32, 1) * s2)


def _full(shape):
    n = len(shape)
    return pl.BlockSpec(shape, lambda *_: (0,) * n)


def matmul_tn(a, b, *, tn, tk, name, tm=1024):
    T, M = a.shape
    N = b.shape[1]
    tm, tn, tk = min(tm, M), min(tn, N), min(tk, T)
    nk = T // tk

    def body(a_ref, b_ref, o_ref, acc_ref):
        k = pl.program_id(2)

        @pl.when(k == 0)
        def _():
            acc_ref[...] = jnp.zeros_like(acc_ref)

        acc_ref[...] += _dot_tn(a_ref[...].astype(BF16), b_ref[...].astype(BF16))

        @pl.when(k == nk - 1)
        def _():
            o_ref[...] = acc_ref[...].astype(BF16)

    return pl.pallas_call(
        body, name=name, grid=(M // tm, N // tn, nk),
        in_specs=[pl.BlockSpec((tk, tm), lambda i, j, k: (k, i)),
                  pl.BlockSpec((tk, tn), lambda i, j, k: (k, j))],
        out_specs=pl.BlockSpec((tm, tn), lambda i, j, k: (i, j)),
        out_shape=jax.ShapeDtypeStruct((M, N), BF16),
        scratch_shapes=[pltpu.VMEM((tm, tn), F32)],
        compiler_params=_params("parallel", "parallel", "arbitrary"),
    )(a, b)


def matmul_tn_pieces(a, b, *, tk, name):
    a3 = a.ndim == 3
    T = a.shape[-2]
    m, n = a.shape[-1], b.shape[-1]
    tk = min(tk, T)
    nk = T // tk
    out = (N_DEV, FF_PAD, n) if a3 else (N_DEV, m, FF_PAD)

    def body(a_ref, b_ref, o_ref, acc_ref):
        k = pl.program_id(0)

        @pl.when(k == 0)
        def _():
            acc_ref[...] = jnp.zeros_like(acc_ref)

        whole = (b_ref if a3 else a_ref)[...].astype(BF16)
        for j in range(FF_PAIRS):
            if a3:
                acc_ref[j] += _dot_tn(a_ref[j].astype(BF16), whole)
            else:
                acc_ref[j] += _dot_tn(whole, b_ref[j].astype(BF16))

        @pl.when(k == nk - 1)
        def _():
            for j in range(FF_PAIRS):
                for half in range(2):
                    cut = slice(half * FF_PAD, (half + 1) * FF_PAD)
                    piece = acc_ref[j, cut, :] if a3 else acc_ref[j, :, cut]
                    o_ref[2 * j + half] = piece.astype(BF16)

    pairs = lambda w: pl.BlockSpec((FF_PAIRS, tk, w), lambda k: (0, k, 0))
    whole = lambda w: pl.BlockSpec((tk, w), lambda k: (k, 0))
    return pl.pallas_call(
        body, name=name, grid=(nk,),
        in_specs=[pairs(m) if a3 else whole(m), whole(n) if a3 else pairs(n)],
        out_specs=_full(out),
        out_shape=jax.ShapeDtypeStruct(out, BF16),
        scratch_shapes=[pltpu.VMEM((FF_PAIRS, m, n), F32)],
        compiler_params=_params("arbitrary"),
    )(a, b)


def dw_in_proj(h, segs, *, tk):
    T = h.shape[0]
    tk = min(tk, T)
    nk = T // tk
    widths = [s.shape[1] for s, _ in segs]
    offs = [o for _, o in segs]

    def body(*refs):
        h_ref, seg_refs, o_ref, acc_ref = refs[0], refs[1:-2], refs[-2], refs[-1]
        k = pl.program_id(0)

        @pl.when(k == 0)
        def _():
            acc_ref[...] = jnp.zeros_like(acc_ref)

        hv = h_ref[...]
        for s_ref, off, w in zip(seg_refs, offs, widths):
            acc_ref[:, off:off + w] += _dot_tn(hv, s_ref[...])

        @pl.when(k == nk - 1)
        def _():
            o_ref[...] = acc_ref[...].astype(BF16)

    return pl.pallas_call(
        body, name="dw_in", grid=(nk,),
        in_specs=[pl.BlockSpec((tk, D_MODEL), lambda k: (k, 0))]
        + [pl.BlockSpec((tk, w), lambda k: (k, 0)) for w in widths],
        out_specs=_full((D_MODEL, IN_PAD)),
        out_shape=jax.ShapeDtypeStruct((D_MODEL, IN_PAD), BF16),
        scratch_shapes=[pltpu.VMEM((D_MODEL, IN_PAD), F32)],
        compiler_params=_params("arbitrary"),
    )(h, *[s for s, _ in segs])


def _piece_row_map():
    src = np.full(IN_PAD, -1, np.int64)
    for orig, pad, width in ((0, OFF_U, 512), (512, OFF_CQ, 512), (1024, OFF_CKV, 256),
                             (1280, OFF_KR, 64), (1344, OFF_GA, 1024), (2368, OFF_GB, 1024)):
        src[pad:pad + width] = np.arange(orig, orig + width)
    rows = np.where(src >= 0, (src // W_IN_PIECE) * PIECE_PAD + src % W_IN_PIECE, -1)
    return rows.astype(np.int32)[None, :]


def _selector(j, map_ref):
    rid = j * PIECE_PAD + lax.broadcasted_iota(jnp.int32, (PIECE_PAD, IN_PAD), 0)
    return jnp.where(rid == map_ref[...], 1.0, 0.0).astype(BF16)


def assemble_w_in(pieces, row_map, *, tr):
    def body(p_ref, map_ref, o_ref, acc_ref):
        j = pl.program_id(1)

        @pl.when(j == 0)
        def _():
            acc_ref[...] = jnp.zeros_like(acc_ref)

        acc_ref[...] += _dot(p_ref[0], _selector(j, map_ref))

        @pl.when(j == N_DEV - 1)
        def _():
            o_ref[...] = acc_ref[...].astype(BF16)

    return pl.pallas_call(
        body, name="assemble_w_in", grid=(D_MODEL // tr, N_DEV),
        in_specs=[pl.BlockSpec((1, tr, PIECE_PAD), lambda i, j: (j, i, 0)), _full((1, IN_PAD))],
        out_specs=pl.BlockSpec((tr, IN_PAD), lambda i, j: (i, 0)),
        out_shape=jax.ShapeDtypeStruct((D_MODEL, IN_PAD), BF16),
        scratch_shapes=[pltpu.VMEM((tr, IN_PAD), F32)],
        compiler_params=_params("parallel", "arbitrary"),
    )(pieces, row_map)


def split_dw_in(dwp, row_map):
    def body(d_ref, map_ref, o_ref):
        o_ref[0] = _dot_nt(d_ref[...], _selector(pl.program_id(0), map_ref)).astype(BF16)

    return pl.pallas_call(
        body, name="split_dw_in", grid=(N_DEV,),
        in_specs=[_full((D_MODEL, IN_PAD)), _full((1, IN_PAD))],
        out_specs=pl.BlockSpec((1, D_MODEL, PIECE_PAD), lambda j: (j, 0, 0)),
        out_shape=jax.ShapeDtypeStruct((N_DEV, D_MODEL, PIECE_PAD), BF16),
        compiler_params=_params("parallel"),
    )(dwp, row_map)


def rms_matmul(x, g, w, *, tm, tn, name):
    T, D = x.shape
    N = w.shape[1]
    tm = min(tm, T)

    def body(x_ref, g_ref, w_ref, o_ref, h_ref):
        @pl.when(pl.program_id(1) == 0)
        def _():
            h_ref[...] = _rms_fwd(x_ref[...], g_ref[...]).astype(BF16)

        o_ref[...] = _dot(h_ref[...], w_ref[...])

    return pl.pallas_call(
        body, name=name, grid=(T // tm, N // tn),
        in_specs=[pl.BlockSpec((tm, D), lambda i, j: (i, 0)), _full((1, D)),
                  pl.BlockSpec((D, tn), lambda i, j: (0, j))],
        out_specs=pl.BlockSpec((tm, tn), lambda i, j: (i, j)),
        out_shape=jax.ShapeDtypeStruct((T, N), F32),
        scratch_shapes=[pltpu.VMEM((tm, D), BF16)],
        compiler_params=_params("parallel", "arbitrary"),
    )(x, g, w)


def _band_matrices():
    s = np.arange(CHUNK)[:, None]
    t = np.arange(CHUNK)[None, :]
    low = np.stack([((s - t >= 0) & (s - t < w)) for w in POOL_WINDOWS]).astype(np.float32)
    up = np.stack([(t > s + CHUNK - w) for w in POOL_WINDOWS]).astype(np.float32)
    return low, up


def _window_count(row0, g):
    t = row0 + lax.broadcasted_iota(jnp.int32, (CHUNK, 1), 0)
    return jnp.minimum(t + 1, POOL_WINDOWS[g]).astype(F32)


def pool_fwd(z, low, up, w_pool, pool_scale, w_a, *, tm):
    T = z.shape[0]
    tm = min(tm, T)
    nch = tm // CHUNK
    ublk = OFF_U // POOL_WIDTH

    def body(u_ref, halo_ref, low_ref, up_ref, wp_ref, sc_ref, wa_ref, pooled_ref, ms_ref, ya_ref):
        i = pl.program_id(0)
        for c in range(nch):
            rows = slice(c * CHUNK, (c + 1) * CHUNK)
            for g in range(4):
                cols = slice(g * POOL_GROUP, (g + 1) * POOL_GROUP)
                cur = u_ref[rows, cols]
                if c == 0:
                    prev = jnp.where(i > 0, halo_ref[:, cols], 0.0)
                else:
                    prev = u_ref[(c - 1) * CHUNK:c * CHUNK, cols]
                s = _band(low_ref[g], cur) + _band(up_ref[g], prev)
                pooled = (s / _window_count(i * tm + c * CHUNK, g) - cur).astype(BF16)
                pooled_ref[rows, cols] = pooled
                ms_ref[rows, cols] = (_dot(pooled, wp_ref[g]) * sc_ref[:, cols]).astype(BF16)
        ya_ref[...] = _dot(ms_ref[...], wa_ref[...])

    return pl.pallas_call(
        body, name="pool_fwd", grid=(T // tm,),
        in_specs=[pl.BlockSpec((tm, POOL_WIDTH), lambda i: (i, ublk)),
                  pl.BlockSpec((CHUNK, POOL_WIDTH), lambda i: (jnp.maximum(i * nch - 1, 0), ublk)),
                  _full((4, CHUNK, CHUNK)), _full((4, CHUNK, CHUNK)),
                  _full((4, POOL_GROUP, POOL_GROUP)), _full((1, POOL_WIDTH)),
                  _full((POOL_WIDTH, D_MODEL))],
        out_specs=[pl.BlockSpec((tm, POOL_WIDTH), lambda i: (i, 0)),
                   pl.BlockSpec((tm, POOL_WIDTH), lambda i: (i, 0)),
                   pl.BlockSpec((tm, D_MODEL), lambda i: (i, 0))],
        out_shape=[jax.ShapeDtypeStruct((T, POOL_WIDTH), BF16),
                   jax.ShapeDtypeStruct((T, POOL_WIDTH), BF16),
                   jax.ShapeDtypeStruct((T, D_MODEL), F32)],
        compiler_params=_params("parallel"),
    )(z, z, low, up, w_pool, pool_scale, w_a)


def pool_bwd_a(dya, pooled, w_a, w_pool, pool_scale, *, tm):
    T = dya.shape[0]
    tm = min(tm, T)
    nch = tm // CHUNK

    def body(dya_ref, pooled_ref, wa_ref, wp_ref, sc_ref, dpc_ref, dsc_ref, dwp_ref):
        i = pl.program_id(0)

        @pl.when(i == 0)
        def _():
            dsc_ref[...] = jnp.zeros_like(dsc_ref)
            dwp_ref[...] = jnp.zeros_like(dwp_ref)

        dms = _dot_nt(dya_ref[...], wa_ref[...])
        for g in range(4):
            cols = slice(g * POOL_GROUP, (g + 1) * POOL_GROUP)
            pg = pooled_ref[:, cols]
            dmg = dms[:, cols]
            mixed = _dot(pg, wp_ref[g])
            dsc_ref[:, cols] += jnp.sum(dmg * mixed, axis=0, keepdims=True)
            dmixed = (dmg * sc_ref[:, cols]).astype(BF16)
            dwp_ref[g] += _dot_tn(pg, dmixed)
            dpooled = _dot_nt(dmixed, wp_ref[g])
            for c in range(nch):
                rows = slice(c * CHUNK, (c + 1) * CHUNK)
                dpc_ref[rows, cols] = dpooled[rows] / _window_count(i * tm + c * CHUNK, g)

    return pl.pallas_call(
        body, name="pool_bwd_a", grid=(T // tm,),
        in_specs=[pl.BlockSpec((tm, D_MODEL), lambda i: (i, 0)),
                  pl.BlockSpec((tm, POOL_WIDTH), lambda i: (i, 0)),
                  _full((POOL_WIDTH, D_MODEL)), _full((4, POOL_GROUP, POOL_GROUP)),
                  _full((1, POOL_WIDTH))],
        out_specs=[pl.BlockSpec((tm, POOL_WIDTH), lambda i: (i, 0)), _full((1, POOL_WIDTH)),
                   _full((4, POOL_GROUP, POOL_GROUP))],
        out_shape=[jax.ShapeDtypeStruct((T, POOL_WIDTH), F32),
                   jax.ShapeDtypeStruct((1, POOL_WIDTH), F32),
                   jax.ShapeDtypeStruct((4, POOL_GROUP, POOL_GROUP), F32)],
        compiler_params=_params("arbitrary"),
    )(dya, pooled, w_a, w_pool, pool_scale)


def pool_bwd_b(dpc, low_t, up_t, *, tm):
    T = dpc.shape[0]
    tm = min(tm, T)
    nch = tm // CHUNK
    last_chunk = T // CHUNK - 1

    def body(d_ref, halo_ref, low_ref, up_ref, du_ref):
        i = pl.program_id(0)
        for c in range(nch):
            rows = slice(c * CHUNK, (c + 1) * CHUNK)
            for g in range(4):
                cols = slice(g * POOL_GROUP, (g + 1) * POOL_GROUP)
                cur = d_ref[rows, cols]
                if c == nch - 1:
                    nxt = jnp.where(i < pl.num_programs(0) - 1, halo_ref[:, cols], 0.0)
                else:
                    nxt = d_ref[(c + 1) * CHUNK:(c + 2) * CHUNK, cols]
                s = _band(low_ref[g], cur) + _band(up_ref[g], nxt)
                du_ref[rows, cols] = (s - cur * _window_count(i * tm + c * CHUNK, g)).astype(BF16)

    return pl.pallas_call(
        body, name="pool_bwd_b", grid=(T // tm,),
        in_specs=[pl.BlockSpec((tm, POOL_WIDTH), lambda i: (i, 0)),
                  pl.BlockSpec((CHUNK, POOL_WIDTH),
                               lambda i: (jnp.minimum((i + 1) * nch, last_chunk), 0)),
                  _full((4, CHUNK, CHUNK)), _full((4, CHUNK, CHUNK))],
        out_specs=pl.BlockSpec((tm, POOL_WIDTH), lambda i: (i, 0)),
        out_shape=jax.ShapeDtypeStruct((T, POOL_WIDTH), BF16),
        compiler_params=_params("parallel"),
    )(dpc, dpc, low_t, up_t)


def mla_prep(z, cos_t, s1_t, s2_t, q_norm, kv_norm, w_uq, w_ukv, *, tm):
    T = z.shape[0]
    tm = min(tm, T)

    def body(cq_ref, ckv_ref, kr_ref, c_ref, s1_ref, s2_ref, qg_ref, kvg_ref, wuq_ref, wukv_ref,
             q_ref, k_ref, v_ref):
        c, s1, s2 = c_ref[...], s1_ref[...], s2_ref[...]
        qn = _rms_fwd(cq_ref[...], qg_ref[...]).astype(BF16)
        q = _dot(qn, wuq_ref[...])
        kvn = _rms_fwd(ckv_ref[...], kvg_ref[...]).astype(BF16)
        kv = _dot(kvn, wukv_ref[...])
        kpe = _rope(kr_ref[...], c, s1, s2, 1.0).astype(BF16)
        for h in range(N_HEADS):
            o = h * HEAD_PAD
            q_ref[h, :, 0:QK_NOPE] = (q[:, o:o + QK_NOPE] * ATTN_SCALE).astype(BF16)
            q_ref[h, :, QK_NOPE:HEAD_PAD] = (
                _rope(q[:, o + QK_NOPE:o + HEAD_PAD], c, s1, s2, 1.0) * ATTN_SCALE).astype(BF16)
            k_ref[h, :, 0:QK_NOPE] = kv[:, o:o + QK_NOPE].astype(BF16)
            k_ref[h, :, QK_NOPE:HEAD_PAD] = kpe
            v_ref[h] = kv[:, o + QK_NOPE:o + HEAD_PAD].astype(BF16)

    tok = lambda w: pl.BlockSpec((tm, w), lambda i: (i, 0))
    return pl.pallas_call(
        body, name="mla_prep", grid=(T // tm,),
        in_specs=[pl.BlockSpec((tm, Q_LORA), lambda i: (i, OFF_CQ // Q_LORA)),
                  pl.BlockSpec((tm, KV_LORA), lambda i: (i, OFF_CKV // KV_LORA)),
                  pl.BlockSpec((tm, LANES), lambda i: (i, OFF_KR // LANES)),
                  tok(LANES), tok(LANES), tok(LANES),
                  _full((1, Q_LORA)), _full((1, KV_LORA)),
                  _full((Q_LORA, N_HEADS * HEAD_PAD)), _full((KV_LORA, N_HEADS * HEAD_PAD))],
        out_specs=[pl.BlockSpec((N_HEADS, tm, HEAD_PAD), lambda i: (0, i, 0)),
                   pl.BlockSpec((N_HEADS, tm, HEAD_PAD), lambda i: (0, i, 0)),
                   pl.BlockSpec((N_HEADS, tm, V_HEAD), lambda i: (0, i, 0))],
        out_shape=[jax.ShapeDtypeStruct((N_HEADS, T, HEAD_PAD), BF16),
                   jax.ShapeDtypeStruct((N_HEADS, T, HEAD_PAD), BF16),
                   jax.ShapeDtypeStruct((N_HEADS, T, V_HEAD), BF16)],
        compiler_params=_params("parallel"),
    )(z, z, z, cos_t, s1_t, s2_t, q_norm, kv_norm, w_uq, w_ukv)


def _row_vector(col, n):
    return jnp.transpose(jnp.broadcast_to(col, (n, LANES)))[0:1, :]


def flash_fwd(q, k, v, *, tq):
    H, T, _ = q.shape
    tq = min(tq, T)
    nq = T // tq
    neg = -1e30

    def body(q_ref, k_ref, v_ref, o_ref, lse_ref):
        qi = pl.program_id(1)
        qb = q_ref[0]

        def tile(j, carry, masked):
            m, l, acc = carry
            rows = pl.ds(pl.multiple_of(j * tq, tq), tq)
            s = _dot_nt(qb, k_ref[0, rows, :])
            if masked:
                r = lax.broadcasted_iota(jnp.int32, (tq, tq), 0)
                c = lax.broadcasted_iota(jnp.int32, (tq, tq), 1)
                s = jnp.where(c <= r, s, neg)
            m_new = jnp.maximum(m, jnp.max(s, axis=-1, keepdims=True))
            alpha = jnp.exp(m - m_new)
            p = jnp.exp(s - m_new)
            l = alpha * l + jnp.sum(p, axis=-1, keepdims=True)
            acc = alpha * acc + _dot(p.astype(BF16), v_ref[0, rows, :])
            return m_new, l, acc

        init = (jnp.full((tq, 1), neg, F32), jnp.zeros((tq, 1), F32), jnp.zeros((tq, V_HEAD), F32))
        carry = lax.fori_loop(0, qi, lambda j, cr: tile(j, cr, False), init)
        m, l, acc = tile(qi, carry, True)
        o_ref[...] = (acc / l).astype(BF16)
        lse_ref[0, 0] = _row_vector(m + jnp.log(l), tq)

    return pl.pallas_call(
        body, name="flash_fwd", grid=(H, nq),
        in_specs=[pl.BlockSpec((1, tq, HEAD_PAD), lambda h, i: (h, i, 0)),
                  pl.BlockSpec((1, T, HEAD_PAD), lambda h, i: (h, 0, 0)),
                  pl.BlockSpec((1, T, V_HEAD), lambda h, i: (h, 0, 0))],
        out_specs=[pl.BlockSpec((tq, V_HEAD), lambda h, i: (i, h)),
                   pl.BlockSpec((1, 1, 1, tq), lambda h, i: (h, i, 0, 0))],
        out_shape=[jax.ShapeDtypeStruct((T, H * V_HEAD), BF16),
                   jax.ShapeDtypeStruct((H, nq, 1, tq), F32)],
        compiler_params=_params("parallel", "arbitrary"),
    )(q, k, v)


def attn_delta(do, o, *, tq):
    T = do.shape[0]
    tq = min(tq, T)
    nq = T // tq
    group = max(g for g in (1, 2, 4) if nq % g == 0)

    def body(do_ref, o_ref, d_ref):
        for b in range(group):
            rows = slice(b * tq, (b + 1) * tq)
            prod = do_ref[rows, :].astype(F32) * o_ref[rows, :].astype(F32)
            d_ref[0, b] = jnp.sum(jnp.transpose(prod), axis=0, keepdims=True)

    return pl.pallas_call(
        body, name="attn_delta", grid=(N_HEADS, nq // group),
        in_specs=[pl.BlockSpec((group * tq, V_HEAD), lambda h, i: (i, h)),
                  pl.BlockSpec((group * tq, V_HEAD), lambda h, i: (i, h))],
        out_specs=pl.BlockSpec((1, group, 1, tq), lambda h, i: (h, i, 0, 0)),
        out_shape=jax.ShapeDtypeStruct((N_HEADS, nq, 1, tq), F32),
        compiler_params=_params("parallel", "parallel"),
    )(do, o)


def flash_bwd(q, k, v, do, lse, delta, *, tq, tk):
    H, T, _ = q.shape
    tq, tk = min(tq, T), min(tk, T)
    nq, nk = T // tq, T // tk
    neg = -1e30

    def body(q_ref, do_ref, lse_ref, dl_ref, k_ref, v_ref, dq_ref, dk_ref, dv_ref, dq_acc):
        j = pl.program_id(1)

        @pl.when(j == 0)
        def _():
            dq_acc[...] = jnp.zeros_like(dq_acc)

        kb = k_ref[0]
        vb = v_ref[0]

        def tile(i, carry, masked):
            dk, dv = carry
            rows = pl.ds(pl.multiple_of(i * tq, tq), tq)
            qb = q_ref[0, rows, :]
            dob = do_ref[rows, :]
            st = _dot_nt(kb, qb)
            if masked:
                key = j * tk + lax.broadcasted_iota(jnp.int32, (tk, tq), 0)
                query = i * tq + lax.broadcasted_iota(jnp.int32, (tk, tq), 1)
                st = jnp.where(key <= query, st, neg)
            pt = jnp.exp(st - lse_ref[0, i])
            dv = dv + _dot(pt.astype(BF16), dob)
            dpt = _dot_nt(vb, dob)
            dst = (pt * (dpt - dl_ref[0, i])).astype(BF16)
            dk = dk + _dot(dst, qb)
            dq_acc[rows, :] += _dot_tn(dst, kb)
            return dk, dv

        first = (j * tk) // tq
        carry = tile(first, (jnp.zeros((tk, HEAD_PAD), F32), jnp.zeros((tk, V_HEAD), F32)), True)
        dk, dv = lax.fori_loop(first + 1, nq, lambda i, cr: tile(i, cr, False), carry)
        dk_ref[0] = dk.astype(BF16)
        dv_ref[0] = dv.astype(BF16)

        @pl.when(j == nk - 1)
        def _():
            dq_ref[0] = dq_acc[...].astype(BF16)

    return pl.pallas_call(
        body, name="flash_bwd", grid=(H, nk),
        in_specs=[pl.BlockSpec((1, T, HEAD_PAD), lambda h, j: (h, 0, 0)),
                  pl.BlockSpec((T, V_HEAD), lambda h, j: (0, h)),
                  pl.BlockSpec((1, nq, 1, tq), lambda h, j: (h, 0, 0, 0)),
                  pl.BlockSpec((1, nq, 1, tq), lambda h, j: (h, 0, 0, 0)),
                  pl.BlockSpec((1, tk, HEAD_PAD), lambda h, j: (h, j, 0)),
                  pl.BlockSpec((1, tk, V_HEAD), lambda h, j: (h, j, 0))],
        out_specs=[pl.BlockSpec((1, T, HEAD_PAD), lambda h, j: (h, 0, 0)),
                   pl.BlockSpec((1, tk, HEAD_PAD), lambda h, j: (h, j, 0)),
                   pl.BlockSpec((1, tk, V_HEAD), lambda h, j: (h, j, 0))],
        out_shape=[jax.ShapeDtypeStruct((H, T, HEAD_PAD), BF16),
                   jax.ShapeDtypeStruct((H, T, HEAD_PAD), BF16),
                   jax.ShapeDtypeStruct((H, T, V_HEAD), BF16)],
        scratch_shapes=[pltpu.VMEM((T, HEAD_PAD), F32)],
        compiler_params=_params("parallel", "arbitrary"),
    )(q, do, lse, delta, k, v)


def mla_bwd(dq, dk, dv, z, cos_t, s1_t, s2_t, q_norm, kv_norm, w_uq, w_ukv, *, tm):
    T = z.shape[0]
    tm = min(tm, T)
    HW = N_HEADS * HEAD_PAD

    def body(dq_ref, dk_ref, dv_ref, cq_ref, ckv_ref, c_ref, s1_ref, s2_ref, qg_ref, kvg_ref,
             wuq_ref, wukv_ref, dcq_ref, dckv_ref, dkr_ref, dqf_ref, dkvf_ref, qn_ref, kvn_ref,
             dqg_ref, dkvg_ref):
        @pl.when(pl.program_id(0) == 0)
        def _():
            dqg_ref[...] = jnp.zeros_like(dqg_ref)
            dkvg_ref[...] = jnp.zeros_like(dkvg_ref)

        c, s1, s2 = c_ref[...], s1_ref[...], s2_ref[...]
        dkpe = jnp.zeros((tm, LANES), F32)
        for h in range(N_HEADS):
            o = h * HEAD_PAD
            dqf_ref[:, o:o + QK_NOPE] = (
                dq_ref[h, :, 0:QK_NOPE].astype(F32) * ATTN_SCALE).astype(BF16)
            dqf_ref[:, o + QK_NOPE:o + HEAD_PAD] = (
                _rope(dq_ref[h, :, QK_NOPE:HEAD_PAD].astype(F32), c, s1, s2, -1.0)
                * ATTN_SCALE).astype(BF16)
            dkvf_ref[:, o:o + QK_NOPE] = dk_ref[h, :, 0:QK_NOPE]
            dkvf_ref[:, o + QK_NOPE:o + HEAD_PAD] = dv_ref[h]
            dkpe = dkpe + dk_ref[h, :, QK_NOPE:HEAD_PAD].astype(F32)
        dkr_ref[...] = _rope(dkpe, c, s1, s2, -1.0).astype(BF16)

        cq = cq_ref[...]
        qn_ref[...] = _rms_fwd(cq, qg_ref[...]).astype(BF16)
        dcq, dgrows = _rms_bwd(cq, qg_ref[...], _dot_nt(dqf_ref[...], wuq_ref[...]))
        dcq_ref[...] = dcq.astype(BF16)
        _accum_rows(dqg_ref, dgrows)

        ckv = ckv_ref[...]
        kvn_ref[...] = _rms_fwd(ckv, kvg_ref[...]).astype(BF16)
        dckv, dgrows = _rms_bwd(ckv, kvg_ref[...], _dot_nt(dkvf_ref[...], wukv_ref[...]))
        dckv_ref[...] = dckv.astype(BF16)
        _accum_rows(dkvg_ref, dgrows)

    tok = lambda w: pl.BlockSpec((tm, w), lambda i: (i, 0))
    head = lambda w: pl.BlockSpec((N_HEADS, tm, w), lambda i: (0, i, 0))
    return pl.pallas_call(
        body, name="mla_bwd", grid=(T // tm,),
        in_specs=[head(HEAD_PAD), head(HEAD_PAD), head(V_HEAD),
                  pl.BlockSpec((tm, Q_LORA), lambda i: (i, OFF_CQ // Q_LORA)),
                  pl.BlockSpec((tm, KV_LORA), lambda i: (i, OFF_CKV // KV_LORA)),
                  tok(LANES), tok(LANES), tok(LANES),
                  _full((1, Q_LORA)), _full((1, KV_LORA)),
                  _full((Q_LORA, HW)), _full((KV_LORA, HW))],
        out_specs=[tok(Q_LORA), tok(KV_LORA), tok(LANES), tok(HW), tok(HW), tok(Q_LORA),
                   tok(KV_LORA), _full((1, Q_LORA)), _full((1, KV_LORA))],
        out_shape=[jax.ShapeDtypeStruct((T, Q_LORA), BF16),
                   jax.ShapeDtypeStruct((T, KV_LORA), BF16),
                   jax.ShapeDtypeStruct((T, LANES), BF16),
                   jax.ShapeDtypeStruct((T, HW), BF16),
                   jax.ShapeDtypeStruct((T, HW), BF16),
                   jax.ShapeDtypeStruct((T, Q_LORA), BF16),
                   jax.ShapeDtypeStruct((T, KV_LORA), BF16),
                   jax.ShapeDtypeStruct((1, Q_LORA), F32),
                   jax.ShapeDtypeStruct((1, KV_LORA), F32)],
        compiler_params=_params("arbitrary"),
    )(dq, dk, dv, z, z, cos_t, s1_t, s2_t, q_norm, kv_norm, w_uq, w_ukv)


def merge_fwd(o, ya, z, x, w_b, w_o, *, tm):
    T = x.shape[0]
    tm = min(tm, T)

    def body(o_ref, ya_ref, ga_ref, gb_ref, x_ref, wb_ref, wo_ref, yb_ref, mg_ref, x1_ref):
        yb = _dot(o_ref[...], wb_ref[...])
        yb_ref[...] = yb
        merged = (_sigmoid(ga_ref[...]) * ya_ref[...] + _sigmoid(gb_ref[...]) * yb).astype(BF16)
        mg_ref[...] = merged
        x1_ref[...] = x_ref[...] + _dot(merged, wo_ref[...])

    tok = pl.BlockSpec((tm, D_MODEL), lambda i: (i, 0))
    return pl.pallas_call(
        body, name="merge_fwd", grid=(T // tm,),
        in_specs=[tok, tok, pl.BlockSpec((tm, D_MODEL), lambda i: (i, OFF_GA // D_MODEL)),
                  pl.BlockSpec((tm, D_MODEL), lambda i: (i, OFF_GB // D_MODEL)), tok,
                  _full((D_MODEL, D_MODEL)), _full((D_MODEL, D_MODEL))],
        out_specs=[tok, tok, tok],
        out_shape=[jax.ShapeDtypeStruct((T, D_MODEL), F32),
                   jax.ShapeDtypeStruct((T, D_MODEL), BF16),
                   jax.ShapeDtypeStruct((T, D_MODEL), F32)],
        compiler_params=_params("parallel"),
    )(o, ya, z, z, x, w_b, w_o)


def merge_bwd(dx1, ya, yb, z, w_o, w_b, dep, *, tm):
    T = dx1.shape[0]
    tm = min(tm, T)

    def body(dx_ref, ya_ref, yb_ref, ga_ref, gb_ref, wo_ref, wb_ref, dep_ref, dya_ref, dyb_ref,
             do_ref, dga_ref, dgb_ref):
        dm = _dot_nt(dx_ref[...].astype(BF16), wo_ref[...])
        sa = _sigmoid(ga_ref[...])
        sb = _sigmoid(gb_ref[...])
        dya_ref[...] = (dm * sa).astype(BF16)
        dyb = (dm * sb).astype(BF16)
        dyb_ref[...] = dyb
        dga_ref[...] = (dm * ya_ref[...] * sa * (1.0 - sa)).astype(BF16)
        dgb_ref[...] = (dm * yb_ref[...] * sb * (1.0 - sb)).astype(BF16)
        do_ref[...] = _dot_nt(dyb, wb_ref[...]).astype(BF16)

    tok = pl.BlockSpec((tm, D_MODEL), lambda i: (i, 0))
    return pl.pallas_call(
        body, name="merge_bwd", grid=(T // tm,),
        in_specs=[tok, tok, tok, pl.BlockSpec((tm, D_MODEL), lambda i: (i, OFF_GA // D_MODEL)),
                  pl.BlockSpec((tm, D_MODEL), lambda i: (i, OFF_GB // D_MODEL)),
                  _full((D_MODEL, D_MODEL)), _full((D_MODEL, D_MODEL)),
                  pl.BlockSpec(memory_space=pl.ANY)],
        out_specs=[tok] * 5,
        out_shape=[jax.ShapeDtypeStruct((T, D_MODEL), BF16)] * 5,
        compiler_params=_params("parallel"),
    )(dx1, ya, yb, z, z, w_o, w_b, dep)


def in_bwd(dx1, x, g, segs, w_in, *, tm):
    T = x.shape[0]
    tm = min(tm, T)
    widths = [s.shape[1] for s, _ in segs]
    offs = [o for _, o in segs]
    n = len(segs)

    def body(*refs):
        dx1_ref, x_ref, g_ref = refs[:3]
        seg_refs = refs[3:3 + n]
        w_ref = refs[3 + n]
        dx_ref, h_ref, dg_ref = refs[4 + n:]

        @pl.when(pl.program_id(0) == 0)
        def _():
            dg_ref[...] = jnp.zeros_like(dg_ref)

        dh = jnp.zeros((tm, D_MODEL), F32)
        for s_ref, off, w in zip(seg_refs, offs, widths):
            dh = dh + _dot_nt(s_ref[...], w_ref[:, off:off + w])
        xv = x_ref[...]
        h_ref[...] = _rms_fwd(xv, g_ref[...]).astype(BF16)
        dxn, dgrows = _rms_bwd(xv, g_ref[...], dh)
        dx_ref[...] = dx1_ref[...] + dxn
        _accum_rows(dg_ref, dgrows)

    tok = pl.BlockSpec((tm, D_MODEL), lambda i: (i, 0))
    return pl.pallas_call(
        body, name="in_bwd", grid=(T // tm,),
        in_specs=[tok, tok, _full((1, D_MODEL))]
        + [pl.BlockSpec((tm, w), lambda i: (i, 0)) for w in widths] + [_full((D_MODEL, IN_PAD))],
        out_specs=[tok, tok, _full((1, D_MODEL))],
        out_shape=[jax.ShapeDtypeStruct((T, D_MODEL), F32),
                   jax.ShapeDtypeStruct((T, D_MODEL), BF16),
                   jax.ShapeDtypeStruct((1, D_MODEL), F32)],
        compiler_params=_params("arbitrary"),
    )(dx1, x, g, *[s for s, _ in segs], w_in)


def _pair_cols(w_ref):
    return jnp.concatenate([w_ref[0], w_ref[1]], axis=1)


def _pair_rows(w_ref):
    return jnp.concatenate([w_ref[0], w_ref[1]], axis=0)


def ffn_fwd(x1, g, w_gate, w_up, w_down, *, tm):
    T = x1.shape[0]
    tm = min(tm, T)

    def body(x_ref, g_ref, wg_ref, wu_ref, wd_ref, gp_ref, up_ref, x2_ref, h_ref, acc_ref):
        f = pl.program_id(1)

        @pl.when(f == 0)
        def _():
            h_ref[...] = _rms_fwd(x_ref[...], g_ref[...]).astype(BF16)
            acc_ref[...] = jnp.zeros_like(acc_ref)

        gp = _dot(h_ref[...], _pair_cols(wg_ref))
        up = _dot(h_ref[...], _pair_cols(wu_ref))
        gp_ref[0] = gp
        up_ref[0] = up
        act = (gp * _sigmoid(gp) * up).astype(BF16)
        acc_ref[...] += _dot(act, _pair_rows(wd_ref))

        @pl.when(f == FF_PAIRS - 1)
        def _():
            x2_ref[...] = x_ref[...] + acc_ref[...]

    tok = pl.BlockSpec((tm, D_MODEL), lambda i, f: (i, 0))
    ff = pl.BlockSpec((1, tm, FF_STEP), lambda i, f: (f, i, 0))
    w_col = pl.BlockSpec((2, D_MODEL, FF_PAD), lambda i, f: (f, 0, 0))
    w_row = pl.BlockSpec((2, FF_PAD, D_MODEL), lambda i, f: (f, 0, 0))
    return pl.pallas_call(
        body, name="ffn_fwd", grid=(T // tm, FF_PAIRS),
        in_specs=[tok, _full((1, D_MODEL)), w_col, w_col, w_row],
        out_specs=[ff, ff, tok],
        out_shape=[jax.ShapeDtypeStruct((FF_PAIRS, T, FF_STEP), F32),
                   jax.ShapeDtypeStruct((FF_PAIRS, T, FF_STEP), F32),
                   jax.ShapeDtypeStruct((T, D_MODEL), F32)],
        scratch_shapes=[pltpu.VMEM((tm, D_MODEL), BF16), pltpu.VMEM((tm, D_MODEL), F32)],
        compiler_params=_params("parallel", "arbitrary"),
    )(x1, g, w_gate, w_up, w_down)


def ffn_bwd(dx2, x1, g, gpre, upre, w_gate, w_up, w_down, *, tm):
    T = x1.shape[0]
    tm = min(tm, T)

    def body(dx_ref, x_ref, g_ref, gp_ref, up_ref, wg_ref, wu_ref, wd_ref,
             act_ref, dg_ref, du_ref, h_ref, dx1_ref, dgain_ref, dxb_ref, acc_ref):
        f = pl.program_id(1)

        @pl.when((pl.program_id(0) == 0) & (f == 0))
        def _():
            dgain_ref[...] = jnp.zeros_like(dgain_ref)

        @pl.when(f == 0)
        def _():
            dxb_ref[...] = dx_ref[...].astype(BF16)
            acc_ref[...] = jnp.zeros_like(acc_ref)

        gp = gp_ref[0]
        up = up_ref[0]
        dact = _dot_nt(dxb_ref[...], _pair_rows(wd_ref))
        sg = _sigmoid(gp)
        silu = gp * sg
        act_ref[0] = (silu * up).astype(BF16)
        dgp = (dact * up * (sg * (1.0 + gp * (1.0 - sg)))).astype(BF16)
        dup = (dact * silu).astype(BF16)
        dg_ref[0] = dgp
        du_ref[0] = dup
        acc_ref[...] += _dot_nt(dgp, _pair_cols(wg_ref)) + _dot_nt(dup, _pair_cols(wu_ref))

        @pl.when(f == FF_PAIRS - 1)
        def _():
            xv = x_ref[...]
            h_ref[...] = _rms_fwd(xv, g_ref[...]).astype(BF16)
            dxn, dgrows = _rms_bwd(xv, g_ref[...], acc_ref[...])
            dx1_ref[...] = dx_ref[...] + dxn
            _accum_rows(dgain_ref, dgrows)

    tok = pl.BlockSpec((tm, D_MODEL), lambda i, f: (i, 0))
    ff = pl.BlockSpec((1, tm, FF_STEP), lambda i, f: (f, i, 0))
    w_col = pl.BlockSpec((2, D_MODEL, FF_PAD), lambda i, f: (f, 0, 0))
    w_row = pl.BlockSpec((2, FF_PAD, D_MODEL), lambda i, f: (f, 0, 0))
    return pl.pallas_call(
        body, name="ffn_bwd", grid=(T // tm, FF_PAIRS),
        in_specs=[tok, tok, _full((1, D_MODEL)), ff, ff, w_col, w_col, w_row],
        out_specs=[ff, ff, ff, tok, tok, _full((1, D_MODEL))],
        out_shape=[jax.ShapeDtypeStruct((FF_PAIRS, T, FF_STEP), BF16)] * 3
        + [jax.ShapeDtypeStruct((T, D_MODEL), BF16), jax.ShapeDtypeStruct((T, D_MODEL), F32),
           jax.ShapeDtypeStruct((1, D_MODEL), F32)],
        scratch_shapes=[pltpu.VMEM((tm, D_MODEL), BF16), pltpu.VMEM((tm, D_MODEL), F32)],
        compiler_params=_params("arbitrary", "arbitrary"),
    )(dx2, x1, g, gpre, upre, w_gate, w_up, w_down)


def ple_fwd(x2, p, g, w_pg, w_ple, *, tm):
    T = x2.shape[0]
    tm = min(tm, T)

    def body(x_ref, p_ref, g_ref, wpg_ref, wple_ref, l_ref, pe_ref, x3_ref):
        xv = x_ref[...]
        logits = _dot(_rms_fwd(xv, g_ref[...]).astype(BF16), wpg_ref[...])
        pe = _dot(p_ref[...].astype(BF16), wple_ref[...])
        l_ref[...] = logits
        pe_ref[...] = pe
        x3_ref[...] = xv + _sigmoid(logits) * pe

    tok = pl.BlockSpec((tm, D_MODEL), lambda i: (i, 0))
    return pl.pallas_call(
        body, name="ple_fwd", grid=(T // tm,),
        in_specs=[tok, pl.BlockSpec((tm, PLE_DIM), lambda i: (i, 0)), _full((1, D_MODEL)),
                  _full((D_MODEL, D_MODEL)), _full((PLE_DIM, D_MODEL))],
        out_specs=[tok, tok, tok],
        out_shape=[jax.ShapeDtypeStruct((T, D_MODEL), F32)] * 3,
        compiler_params=_params("parallel"),
    )(x2, p, g, w_pg, w_ple)


def ple_bwd(dx3, logits, pe, x2, g, w_pg, *, tm):
    T = x2.shape[0]
    tm = min(tm, T)

    def body(dx_ref, l_ref, pe_ref, x_ref, g_ref, wpg_ref, dl_ref, dpe_ref, h_ref, dx2_ref,
             dg_ref):
        @pl.when(pl.program_id(0) == 0)
        def _():
            dg_ref[...] = jnp.zeros_like(dg_ref)

        dx = dx_ref[...]
        s = _sigmoid(l_ref[...])
        dpe_ref[...] = (dx * s).astype(BF16)
        dl = (dx * pe_ref[...] * s * (1.0 - s)).astype(BF16)
        dl_ref[...] = dl
        xv = x_ref[...]
        h_ref[...] = _rms_fwd(xv, g_ref[...]).astype(BF16)
        dxn, dgrows = _rms_bwd(xv, g_ref[...], _dot_nt(dl, wpg_ref[...]))
        dx2_ref[...] = dx + dxn
        _accum_rows(dg_ref, dgrows)

    tok = pl.BlockSpec((tm, D_MODEL), lambda i: (i, 0))
    return pl.pallas_call(
        body, name="ple_bwd", grid=(T // tm,),
        in_specs=[tok, tok, tok, tok, _full((1, D_MODEL)), _full((D_MODEL, D_MODEL))],
        out_specs=[tok, tok, tok, tok, _full((1, D_MODEL))],
        out_shape=[jax.ShapeDtypeStruct((T, D_MODEL), BF16)] * 3
        + [jax.ShapeDtypeStruct((T, D_MODEL), F32), jax.ShapeDtypeStruct((1, D_MODEL), F32)],
        compiler_params=_params("arbitrary"),
    )(dx3, logits, pe, x2, g, w_pg)


def loss_head(x, g, target, *, tm):
    T = x.shape[0]
    tm = min(tm, T)

    def body(x_ref, g_ref, t_ref, loss_ref, dx_ref, dg_ref):
        @pl.when(pl.program_id(0) == 0)
        def _():
            loss_ref[...] = jnp.zeros_like(loss_ref)
            dg_ref[...] = jnp.zeros_like(dg_ref)

        xv = x_ref[...]
        err = _rms_fwd(xv, g_ref[...]) - t_ref[...]
        part = 0.5 * jnp.sum(jnp.mean(err * err, axis=-1, keepdims=True), axis=0, keepdims=True)
        lane = lax.broadcasted_iota(jnp.int32, (1, LANES), 1)
        loss_ref[...] += jnp.where(lane == 0, part, 0.0)
        dxn, dgrows = _rms_bwd(xv, g_ref[...], err * (1.0 / D_MODEL))
        dx_ref[...] = dxn
        _accum_rows(dg_ref, dgrows)

    tok = pl.BlockSpec((tm, D_MODEL), lambda i: (i, 0))
    return pl.pallas_call(
        body, name="loss_head", grid=(T // tm,),
        in_specs=[tok, _full((1, D_MODEL)), tok],
        out_specs=[_full((1, LANES)), tok, _full((1, D_MODEL))],
        out_shape=[jax.ShapeDtypeStruct((1, LANES), F32), jax.ShapeDtypeStruct((T, D_MODEL), F32),
                   jax.ShapeDtypeStruct((1, D_MODEL), F32)],
        compiler_params=_params("arbitrary"),
    )(x, g, target)


def _peers():
    x, y, c = lax.axis_index("x"), lax.axis_index("y"), lax.axis_index("c")
    me = 4 * x + 2 * y + c
    out = []
    for d in range(1, N_DEV):
        px = 1 - x if d & 4 else x
        py = 1 - y if d & 2 else y
        pc = 1 - c if d & 1 else c
        out.append(((px, py, pc), 4 * px + 2 * py + pc))
    return me, out


_HBM = pl.BlockSpec(memory_space=pltpu.HBM)
_SEM = pl.BlockSpec(memory_space=pltpu.SEMAPHORE)
_ANY = pl.BlockSpec(memory_space=pl.ANY)
_DATAFLOW = pltpu.SideEffectType.DATAFLOW_SIDE_EFFECTING


def _exchange_copy(t, d, pos, idx, me, src_ref, scatter, land_ref, send_sems, recv_sems):
    k = t * (N_DEV - 1) + d
    return pltpu.make_async_remote_copy(
        src_ref=src_ref.at[idx] if scatter else src_ref, dst_ref=land_ref.at[me],
        send_sem=send_sems.at[k], recv_sem=recv_sems.at[k],
        device_id=pos, device_id_type=pl.DeviceIdType.MESH)


def exchange_start(gathers, scatters, dep, *, name):
    srcs = [pltpu.with_memory_space_constraint(a, pltpu.HBM) for a in list(gathers) + list(scatters)]
    kinds = [False] * len(gathers) + [True] * len(scatters)
    lands = [pltpu.with_memory_space_constraint(
        lax.empty((N_DEV,) + (a.shape[1:] if sc else a.shape), a.dtype), pltpu.HBM)
        for a, sc in zip(srcs, kinds)]
    n = len(srcs)

    def body(*refs):
        src_refs, land_refs = refs[:n], refs[n:2 * n]
        send_sems, recv_sems = refs[-2 * n - 4], refs[-2 * n - 3]
        token, local_sems = refs[-2], refs[-1]
        me, peers = _peers()
        locals_ = []
        for t in range(n):
            local = pltpu.make_async_copy(src_refs[t].at[me] if kinds[t] else src_refs[t],
                                          land_refs[t].at[me], local_sems.at[t])
            local.start()
            locals_.append(local)
        for t in range(n):
            for d, (pos, idx) in enumerate(peers):
                _exchange_copy(t, d, pos, idx, me, src_refs[t], kinds[t], land_refs[t],
                               send_sems, recv_sems).start()
        for local in locals_:
            local.wait()
        token[...] = jnp.zeros_like(token)

    sem = pltpu.SemaphoreType.DMA((n * (N_DEV - 1),))
    thru = [pltpu.HBM(a.shape, a.dtype) for a in srcs + lands]
    outs = pl.pallas_call(
        body, name=name,
        in_specs=[_HBM] * (2 * n) + ([_ANY] if dep is not None else []),
        out_specs=[_SEM, _SEM] + [_HBM] * (2 * n) + [pl.BlockSpec(memory_space=pltpu.VMEM)],
        out_shape=[sem, sem] + thru + [jax.ShapeDtypeStruct((8, LANES), F32)],
        input_output_aliases={i: 2 + i for i in range(2 * n)},
        scratch_shapes=[pltpu.SemaphoreType.DMA((n,))],
        compiler_params=pltpu.CompilerParams(has_side_effects=_DATAFLOW),
    )(*srcs, *lands, *([dep] if dep is not None else []))
    return dict(sems=outs[:2], srcs=outs[2:2 + n], lands=outs[2 + n:2 + 2 * n], token=outs[-1],
                kinds=kinds)


def exchange_wait(handle, after, *, name):
    kinds = handle["kinds"]
    n = len(kinds)

    def body(*refs):
        src_refs, land_refs = refs[:n], refs[n:2 * n]
        send_sems, recv_sems = refs[2 * n], refs[2 * n + 1]
        me, peers = _peers()
        for t in range(n):
            for d, (pos, idx) in enumerate(peers):
                cp = _exchange_copy(t, d, pos, idx, me, src_refs[t], kinds[t], land_refs[t],
                                    send_sems, recv_sems)
                cp.wait_send()
                cp.wait_recv()

    arrays = list(handle["srcs"]) + list(handle["lands"])
    outs = pl.pallas_call(
        body, name=name,
        in_specs=[_HBM] * (2 * n) + [_SEM, _SEM] + ([_ANY] if after is not None else []),
        out_specs=[_HBM] * (2 * n),
        out_shape=[pltpu.HBM(a.shape, a.dtype) for a in arrays],
        input_output_aliases={i: i for i in range(2 * n)},
        compiler_params=pltpu.CompilerParams(has_side_effects=_DATAFLOW),
    )(*arrays, *handle["sems"], *([after] if after is not None else []))
    return outs[n:]


def _row_block(rows, cols):
    best = None
    for rb in range(16, rows + 1, 16):
        if rows % rb == 0 and rb * cols <= 256 * 1024:
            best = rb
    return best or rows


def reduce_adamw(parts, w, m, v, *, name):
    nl = len(parts)
    C = w.shape[1]
    R, cp = parts[0].shape[1:]
    rb = _row_block(R, cp)
    nb = R // rb

    def body(*refs):
        p_refs = refs[:nl]
        w_ref, m_ref, v_ref, g_ref, d_ref, m2_ref, v2_ref = refs[nl:]

        def total(p_ref):
            g = p_ref[0, :, 0:C].astype(F32)
            for k in range(1, N_DEV):
                g = g + p_ref[k, :, 0:C].astype(F32)
            return g

        g = total(p_refs[0])
        for layer in range(1, nl):
            g = jnp.where(pl.program_id(0) == layer, total(p_refs[layer]), g)
        g_ref[...] = g
        m2 = ADAM_B1 * m_ref[...] + (1.0 - ADAM_B1) * g
        v2 = ADAM_B2 * v_ref[...] + (1.0 - ADAM_B2) * (g * g)
        m2_ref[...] = m2
        v2_ref[...] = v2
        m_hat = m2 / (1.0 - ADAM_B1 ** ADAM_STEP)
        v_hat = v2 / (1.0 - ADAM_B2 ** ADAM_STEP)
        d_ref[...] = -ADAM_LR * (m_hat / (jnp.sqrt(v_hat) + ADAM_EPS) + ADAM_WD * w_ref[...])

    def part_spec(layer):
        return pl.BlockSpec((N_DEV, rb, cp), lambda l, i: (0, jnp.where(l == layer, i, 0), 0))

    blk = pl.BlockSpec((rb, C), lambda l, i: (l * nb + i, 0))
    return pl.pallas_call(
        body, name=name, grid=(nl, nb),
        in_specs=[part_spec(layer) for layer in range(nl)] + [blk, blk, blk],
        out_specs=[blk] * 4,
        out_shape=[jax.ShapeDtypeStruct((nl * R, C), F32)] * 4,
        compiler_params=_params("arbitrary", "arbitrary"),
    )(*parts, w, m, v)


def _cols_to_pieces(full):
    r = full.shape[0]
    return jnp.transpose(full.reshape(r, N_DEV, -1), (1, 0, 2))


def _pieces_to_cols(pieces):
    r = pieces.shape[1]
    return jnp.transpose(pieces, (1, 0, 2)).reshape(r, -1)


def _rope_tables(positions):
    inv_freq = 1.0 / (ROPE_THETA ** (jnp.arange(0, QK_ROPE, 2, dtype=F32) / QK_ROPE))
    ang = positions.astype(F32)[:, None] * inv_freq
    cos, sin = jnp.cos(ang), jnp.sin(ang)
    zero = jnp.zeros_like(cos)
    return (jnp.concatenate([cos, cos, zero, zero], axis=1),
            jnp.concatenate([-sin, zero, zero, zero], axis=1),
            jnp.concatenate([zero, sin, zero, zero], axis=1))


def kernel(x, p, positions, norm_mix, w_in, w_pool, pool_scale, q_norm, kv_norm, w_uq, w_ukv, w_a, w_b, w_o, norm_ffn, w_gate, w_up, w_down, norm_ple, w_ple_gate, w_ple, final_norm, loss_target, m_norm_mix, m_w_in, m_w_pool, m_pool_scale, m_q_norm, m_kv_norm, m_w_uq, m_w_ukv, m_w_a, m_w_b, m_w_o, m_norm_ffn, m_w_gate, m_w_up, m_w_down, m_norm_ple, m_w_ple_gate, m_w_ple, m_final_norm, v_norm_mix, v_w_in, v_w_pool, v_pool_scale, v_q_norm, v_kv_norm, v_w_uq, v_w_ukv, v_w_a, v_w_b, v_w_o, v_norm_ffn, v_w_gate, v_w_up, v_w_down, v_norm_ple, v_w_ple_gate, v_w_ple, v_final_norm):
    given = dict(locals())
    shard = {n: given[n] for n in WEIGHTS}
    TM = 512
    TQ = 512

    def send_form(n, i):
        w = shard[n][i].astype(BF16)
        if n == "w_in":
            return jnp.pad(w, ((0, 0), (0, PIECE_PAD - W_IN_PIECE)))
        if n == "w_uq":
            w = jnp.pad(w, ((0, 0), (0, 0), (0, HEAD_PAD - QK_HEAD)))
        if n in ("w_uq", "w_ukv"):
            return w.reshape(-1, N_HEADS * HEAD_PAD)
        if n in ("w_gate", "w_up"):
            return jnp.pad(w, ((0, 0), (0, FF_PAD - FF_PIECE)))
        if n == "w_down":
            return jnp.pad(w, ((0, FF_PAD - FF_PIECE), (0, 0)))
        return w

    names = [n for n, _ in SHARDED]
    FIRST = ("w_in", "w_uq", "w_ukv", "w_a", "w_b", "w_o")
    LATER = ("w_gate", "w_up", "w_down", "w_ple_gate", "w_ple")
    start_a = exchange_start([send_form(n, 0) for n in FIRST], [], None, name="gather_start_a")
    got_a = exchange_wait(start_a, None, name="gather_wait_a")
    rest = [(n, 0) for n in LATER] + [(n, 1) for n in names]
    start_b = exchange_start([send_form(n, i) for n, i in rest], [], got_a[0],
                             name="gather_start_b")
    gathered = {(n, 0): a for n, a in zip(FIRST, got_a)}
    row_map = jnp.asarray(_piece_row_map())

    low_np, up_np = _band_matrices()
    low, up = jnp.asarray(low_np, BF16), jnp.asarray(up_np, BF16)
    low_t = jnp.asarray(low_np.transpose(0, 2, 1), BF16)
    up_t = jnp.asarray(up_np.transpose(0, 2, 1), BF16)
    cos_t, s1_t, s2_t = _rope_tables(positions[0])

    def mix_weights(i):
        g = lambda n: gathered[n, i]
        return dict(
            w_in=assemble_w_in(g("w_in"), row_map, tr=512),
            w_uq=g("w_uq").reshape(Q_LORA, N_HEADS * HEAD_PAD),
            w_ukv=g("w_ukv").reshape(KV_LORA, N_HEADS * HEAD_PAD),
            w_a=_pieces_to_cols(g("w_a")), w_b=g("w_b").reshape(D_MODEL, D_MODEL),
            w_o=g("w_o").reshape(D_MODEL, D_MODEL),
            w_pool=w_pool[i].astype(BF16),
            pool_scale=pool_scale[i][None], norm_mix=norm_mix[i][None], q_norm=q_norm[i][None],
            kv_norm=kv_norm[i][None], norm_ffn=norm_ffn[i][None], norm_ple=norm_ple[i][None])

    def channel_weights(i):
        g = lambda n: gathered[n, i]
        return dict(w_gate=g("w_gate"), w_up=g("w_up"), w_down=g("w_down"),
                    w_pg=g("w_ple_gate").reshape(D_MODEL, D_MODEL),
                    w_ple=_pieces_to_cols(g("w_ple")))

    def forward_mix(L, xs):
        z = rms_matmul(xs, L["norm_mix"], L["w_in"], tm=TM, tn=IN_PAD, name="in_proj")
        pooled, ms, ya = pool_fwd(z, low, up, L["w_pool"], L["pool_scale"], L["w_a"], tm=TM)
        q, k, v = mla_prep(z, cos_t, s1_t, s2_t, L["q_norm"], L["kv_norm"], L["w_uq"], L["w_ukv"],
                           tm=TM)
        o, lse = flash_fwd(q, k, v, tq=2 * TQ)
        return dict(x=xs, z=z, pooled=pooled, ms=ms, ya=ya, q=q, k=k, v=v, o=o, lse=lse)

    def forward_rest(L, S, i):
        yb, merged, x1 = merge_fwd(S["o"], S["ya"], S["z"], S["x"], L["w_b"], L["w_o"], tm=TM)
        gpre, upre, x2 = ffn_fwd(x1, L["norm_ffn"], L["w_gate"], L["w_up"], L["w_down"], tm=TM)
        logits, pe, x3 = ple_fwd(x2, p[i, 0], L["norm_ple"], L["w_pg"], L["w_ple"], tm=TM)
        S.update(yb=yb, merged=merged, x1=x1, gpre=gpre, upre=upre, x2=x2, logits=logits, pe=pe)
        return x3

    layers, saved = [], []
    L = mix_weights(0)
    first = dict(L, norm_mix=L["norm_mix"] + start_b["token"][0, 0])
    S = forward_mix(first, x[0])
    got_b = exchange_wait(start_b, S["o"], name="gather_wait_b")
    gathered.update(dict(zip(rest, got_b)))
    L.update(channel_weights(0))
    xs = forward_rest(L, S, 0)
    layers.append(L)
    saved.append(S)
    L = dict(mix_weights(1), **channel_weights(1))
    S = forward_mix(L, xs)
    xs = forward_rest(L, S, 1)
    layers.append(L)
    saved.append(S)

    loss_part, dx, d_final = loss_head(xs, final_norm[None], loss_target[0], tm=TM)

    grads = {n: [None] * DEPTH for n in REPLICATED if n != "final_norm"}
    pieces = {n: [None] * DEPTH for n in names}
    tn_mm = functools.partial(matmul_tn, tn=1024, tk=1024)
    row_pieces = lambda a: a.reshape(N_DEV, -1, a.shape[-1])
    start_1 = None
    for i in reversed(range(DEPTH)):
        L, S = layers[i], saved[i]
        g_ple = L["norm_ple"] if start_1 is None else L["norm_ple"] + start_1["token"][0, 0]
        dl, dpe, hn, dx2, d_norm_ple = ple_bwd(dx, S["logits"], S["pe"], S["x2"], g_ple,
                                               L["w_pg"], tm=TM)
        grads["norm_ple"][i] = d_norm_ple[0]
        pieces["w_ple_gate"][i] = row_pieces(tn_mm(hn, dl, name="dw_ple_gate"))
        pieces["w_ple"][i] = _cols_to_pieces(tn_mm(p[i, 0], dpe, name="dw_ple"))

        act, dgp, dup, h2, dx1, d_norm_ffn = ffn_bwd(dx2, S["x1"], L["norm_ffn"], S["gpre"],
                                                     S["upre"], L["w_gate"], L["w_up"],
                                                     L["w_down"], tm=TM)
        grads["norm_ffn"][i] = d_norm_ffn[0]
        pieces["w_down"][i] = matmul_tn_pieces(act, dx2, tk=512, name="dw_down")
        pieces["w_gate"][i] = matmul_tn_pieces(h2, dgp, tk=512, name="dw_gate")
        pieces["w_up"][i] = matmul_tn_pieces(h2, dup, tk=512, name="dw_up")
        if i == 0:
            start_0a = exchange_start([], [pieces[n][0] for n in LATER], None,
                                      name="scatter_start_0a")
        dep = start_0a["token"] if i == 0 else loss_part

        dya, dyb, do, dga, dgb = merge_bwd(dx1, S["ya"], S["yb"], S["z"], L["w_o"], L["w_b"],
                                           dep, tm=TM)
        pieces["w_o"][i] = row_pieces(tn_mm(S["merged"], dx1, name="dw_o"))
        pieces["w_b"][i] = row_pieces(tn_mm(S["o"], dyb, name="dw_b"))

        delta = attn_delta(do, S["o"], tq=TQ).reshape(S["lse"].shape)
        dq, dk, dv = flash_bwd(S["q"], S["k"], S["v"], do, S["lse"], delta, tq=2 * TQ, tk=TQ)
        dcq, dckv, dkr, dqf, dkvf, qn, kvn, d_q_norm, d_kv_norm = mla_bwd(
            dq, dk, dv, S["z"], cos_t, s1_t, s2_t, L["q_norm"], L["kv_norm"], L["w_uq"],
            L["w_ukv"], tm=TM)
        grads["q_norm"][i] = d_q_norm[0]
        grads["kv_norm"][i] = d_kv_norm[0]
        d_w_uq = tn_mm(qn, dqf, name="dw_uq")
        pieces["w_uq"][i] = d_w_uq.reshape(N_DEV, -1, N_HEADS, HEAD_PAD)[..., :QK_HEAD]
        pieces["w_ukv"][i] = tn_mm(kvn, dkvf, name="dw_ukv").reshape(
            N_DEV, -1, N_HEADS, QK_NOPE + V_HEAD)

        dpc, d_pool_scale, d_w_pool = pool_bwd_a(dya, S["pooled"], L["w_a"], L["w_pool"],
                                                 L["pool_scale"], tm=TM)
        grads["pool_scale"][i] = d_pool_scale[0]
        grads["w_pool"][i] = d_w_pool
        pieces["w_a"][i] = _cols_to_pieces(tn_mm(S["ms"], dya, name="dw_a"))
        du = pool_bwd_b(dpc, low_t, up_t, tm=TM)

        segs = [(dga, OFF_GA), (dgb, OFF_GB), (du, OFF_U), (dcq, OFF_CQ), (dckv, OFF_CKV),
                (dkr, OFF_KR)]
        dx, h, d_norm_mix = in_bwd(dx1, S["x"], L["norm_mix"], segs, L["w_in"], tm=TM)
        grads["norm_mix"][i] = d_norm_mix[0]
        pieces["w_in"][i] = split_dw_in(dw_in_proj(h, segs, tk=512), row_map)
        if i == 1:
            start_1 = exchange_start([], [pieces[n][1] for n in names], None,
                                     name="scatter_start_1")

    small = {n: jnp.stack(g) for n, g in grads.items()}
    small["final_norm"] = d_final[0]
    rep = jnp.concatenate([small[n].reshape(-1) for n in REPLICATED] + [loss_part.reshape(-1)])
    n_small = -(-rep.shape[0] // (8 * FLAT_COLS)) * (8 * FLAT_COLS)

    def small_flat(parts):
        flat = jnp.concatenate([a.reshape(-1) for a in parts])
        return jnp.pad(flat, (0, n_small - flat.shape[0])).reshape(-1, FLAT_COLS)

    start_0b = exchange_start([small_flat([rep])], [pieces[n][0] for n in FIRST], None,
                              name="scatter_start_0b")
    got_1 = exchange_wait(start_1, start_0b["token"], name="scatter_wait_1")
    got_0a = exchange_wait(start_0a, start_0b["token"], name="scatter_wait_0a")
    got_0b = exchange_wait(start_0b, None, name="scatter_wait_0b")
    parts = {(n, 1): a for n, a in zip(names, got_1)}
    parts.update({(n, 0): a for n, a in zip(LATER, got_0a)})
    parts.update({(n, 0): a for n, a in zip(FIRST, got_0b[1:])})
    for i in range(DEPTH):
        parts["w_down", i] = parts["w_down", i][:, :FF_PIECE]

    results = {}
    for n in names:
        w = shard[n]
        rows2d = lambda a: a.reshape(-1, w.shape[-1])
        w2 = rows2d(w)
        per_layer = [parts[n, i].reshape(N_DEV, w2.shape[0] // DEPTH, -1) for i in range(DEPTH)]
        out = reduce_adamw(per_layer, w2, rows2d(given["m_" + n]), rows2d(given["v_" + n]),
                           name="adamw_" + n)
        results[n] = [a.reshape(w.shape) for a in out]

    out = reduce_adamw([got_0b[0]], *[small_flat([given[pre + n] for n in REPLICATED])
                                      for pre in ("", "m_", "v_")], name="adamw_small")
    loss = None
    for n in REPLICATED:
        results[n] = []
    for a in out:
        a = a.reshape(-1)
        off = 0
        for n in REPLICATED:
            size = int(np.prod(shard[n].shape))
            results[n].append(a[off:off + size].reshape(shard[n].shape))
            off += size
        if loss is None:
            loss = a[off]
    return (loss, dx[None], *[results[n][0] for n in WEIGHTS], *[results[n][1] for n in WEIGHTS],
            *[results[n][2] for n in WEIGHTS], *[results[n][3] for n in WEIGHTS])
```

```python
import functools

import numpy as np
import jax
import jax.numpy as jnp
from jax import lax
from jax.experimental import pallas as pl
from jax.experimental.pallas import tpu as pltpu

F32 = jnp.float32
BF16 = jnp.bfloat16

D_MODEL = 1024
DEPTH = 2
PLE_DIM = 256
POOL_WINDOWS = (2, 4, 8, 16)
POOL_GROUP = 128
POOL_WIDTH = 512
N_HEADS = 8
Q_LORA = 512
KV_LORA = 256
QK_NOPE = 128
QK_ROPE = 64
QK_HEAD = 192
V_HEAD = 128
HEAD_PAD = 256
D_FF = 2816
ROPE_THETA = 10000.0
EPS = 1e-6
IN_WIDTH = 3392
ATTN_SCALE = QK_HEAD ** -0.5

OFF_GA, OFF_GB, OFF_U, OFF_CQ, OFF_CKV, OFF_KR = 0, 1024, 2048, 2560, 3072, 3328
IN_PAD = 3456
W_IN_PIECE = IN_WIDTH // 8
PIECE_PAD = 512
FF_PIECE = D_FF // 8
FF_PAD = 384
FF_PAIRS = 4
FF_STEP = 2 * FF_PAD

ADAM_LR = 0.001
ADAM_B1 = 0.9
ADAM_B2 = 0.999
ADAM_EPS = 1e-08
ADAM_WD = 0.01
ADAM_STEP = 10

N_DEV = 8
LANES = 128
CHUNK = 128
VMEM_LIMIT = 56 * 1024 * 1024

SHARDED = (("w_in", 2), ("w_uq", 1), ("w_ukv", 1), ("w_a", 2), ("w_b", 1), ("w_o", 1),
           ("w_gate", 2), ("w_up", 2), ("w_down", 1), ("w_ple_gate", 1), ("w_ple", 2))
REPLICATED = ("norm_mix", "w_pool", "pool_scale", "q_norm", "kv_norm", "norm_ffn", "norm_ple",
              "final_norm")
WEIGHTS = ("norm_mix", "w_in", "w_pool", "pool_scale", "q_norm", "kv_norm", "w_uq", "w_ukv",
           "w_a", "w_b", "w_o", "norm_ffn", "w_gate", "w_up", "w_down", "norm_ple",
           "w_ple_gate", "w_ple", "final_norm")
FLAT_COLS = 1024
FLAT_ROW_BLOCK = 192


def _params(*sem):
    return pltpu.CompilerParams(dimension_semantics=sem, vmem_limit_bytes=VMEM_LIMIT)


def _dot(a, b):
    return jnp.dot(a, b, preferred_element_type=F32)


def _dot_nt(a, b):
    return lax.dot_general(a, b, (((1,), (1,)), ((), ())), preferred_element_type=F32)


def _dot_tn(a, b):
    return lax.dot_general(a, b, (((0,), (0,)), ((), ())), preferred_element_type=F32)


def _rms_fwd(x, g):
    r = lax.rsqrt(jnp.mean(x * x, axis=-1, keepdims=True) + EPS)
    return x * r * g


def _rms_bwd(x, g, dy):
    r = lax.rsqrt(jnp.mean(x * x, axis=-1, keepdims=True) + EPS)
    xr = x * r
    gy = dy * g
    dx = r * (gy - xr * jnp.mean(gy * xr, axis=-1, keepdims=True))
    return dx, dy * xr


def _sigmoid(x):
    return 1.0 / (1.0 + jnp.exp(-x))


def _accum_rows(ref, rows):
    ref[...] += jnp.sum(rows, axis=0, keepdims=True)


def _band(band, x):
    h1 = x.astype(BF16)
    r1 = x - h1.astype(F32)
    h2 = r1.astype(BF16)
    h3 = (r1 - h2.astype(F32)).astype(BF16)
    return _dot(band, h1) + _dot(band, h2) + _dot(band, h3)


def _rope(x, c, s1, s2, sign):
    return x * c + sign * (pltpu.roll(x, 96, 1) * s1 + pltpu.roll(x, 32, 1) * s2)


def _full(shape):
    n = len(shape)
    return pl.BlockSpec(shape, lambda *_: (0,) * n)


def matmul_tn(a, b, *, tn, tk, name, tm=1024):
    T, M = a.shape
    N = b.shape[1]
    tm, tn, tk = min(tm, M), min(tn, N), min(tk, T)
    nk = T // tk

    def body(a_ref, b_ref, o_ref, acc_ref):
        k = pl.program_id(2)

        @pl.when(k == 0)
        def _():
            acc_ref[...] = jnp.zeros_like(acc_ref)

        acc_ref[...] += _dot_tn(a_ref[...].astype(BF16), b_ref[...].astype(BF16))

        @pl.when(k == nk - 1)
        def _():
            o_ref[...] = acc_ref[...].astype(BF16)

    return pl.pallas_call(
        body, name=name, grid=(M // tm, N // tn, nk),
        in_specs=[pl.BlockSpec((tk, tm), lambda i, j, k: (k, i)),
                  pl.BlockSpec((tk, tn), lambda i, j, k: (k, j))],
        out_specs=pl.BlockSpec((tm, tn), lambda i, j, k: (i, j)),
        out_shape=jax.ShapeDtypeStruct((M, N), BF16),
        scratch_shapes=[pltpu.VMEM((tm, tn), F32)],
        compiler_params=_params("parallel", "parallel", "arbitrary"),
    )(a, b)


def matmul_tn_pieces(a, b, *, tk, name):
    a3 = a.ndim == 3
    T = a.shape[-2]
    m, n = a.shape[-1], b.shape[-1]
    tk = min(tk, T)
    nk = T // tk
    out = (N_DEV, FF_PAD, n) if a3 else (N_DEV, m, FF_PAD)

    def body(a_ref, b_ref, o_ref, acc_ref):
        k = pl.program_id(0)

        @pl.when(k == 0)
        def _():
            acc_ref[...] = jnp.zeros_like(acc_ref)

        whole = (b_ref if a3 else a_ref)[...].astype(BF16)
        for j in range(FF_PAIRS):
            if a3:
                acc_ref[j] += _dot_tn(a_ref[j].astype(BF16), whole)
            else:
                acc_ref[j] += _dot_tn(whole, b_ref[j].astype(BF16))

        @pl.when(k == nk - 1)
        def _():
            for j in range(FF_PAIRS):
                for half in range(2):
                    cut = slice(half * FF_PAD, (half + 1) * FF_PAD)
                    piece = acc_ref[j, cut, :] if a3 else acc_ref[j, :, cut]
                    o_ref[2 * j + half] = piece.astype(BF16)

    pairs = lambda w: pl.BlockSpec((FF_PAIRS, tk, w), lambda k: (0, k, 0))
    whole = lambda w: pl.BlockSpec((tk, w), lambda k: (k, 0))
    return pl.pallas_call(
        body, name=name, grid=(nk,),
        in_specs=[pairs(m) if a3 else whole(m), whole(n) if a3 else pairs(n)],
        out_specs=_full(out),
        out_shape=jax.ShapeDtypeStruct(out, BF16),
        scratch_shapes=[pltpu.VMEM((FF_PAIRS, m, n), F32)],
        compiler_params=_params("arbitrary"),
    )(a, b)


def dw_in_proj(h, segs, *, tk):
    T = h.shape[0]
    tk = min(tk, T)
    nk = T // tk
    widths = [s.shape[1] for s, _ in segs]
    offs = [o for _, o in segs]

    def body(*refs):
        h_ref, seg_refs, o_ref, acc_ref = refs[0], refs[1:-2], refs[-2], refs[-1]
        k = pl.program_id(0)

        @pl.when(k == 0)
        def _():
            acc_ref[...] = jnp.zeros_like(acc_ref)

        hv = h_ref[...]
        for s_ref, off, w in zip(seg_refs, offs, widths):
            acc_ref[:, off:off + w] += _dot_tn(hv, s_ref[...])

        @pl.when(k == nk - 1)
        def _():
            o_ref[...] = acc_ref[...].astype(BF16)

    return pl.pallas_call(
        body, name="dw_in", grid=(nk,),
        in_specs=[pl.BlockSpec((tk, D_MODEL), lambda k: (k, 0))]
        + [pl.BlockSpec((tk, w), lambda k: (k, 0)) for w in widths],
        out_specs=_full((D_MODEL, IN_PAD)),
        out_shape=jax.ShapeDtypeStruct((D_MODEL, IN_PAD), BF16),
        scratch_shapes=[pltpu.VMEM((D_MODEL, IN_PAD), F32)],
        compiler_params=_params("arbitrary"),
    )(h, *[s for s, _ in segs])


def _piece_row_map():
    src = np.full(IN_PAD, -1, np.int64)
    for orig, pad, width in ((0, OFF_U, 512), (512, OFF_CQ, 512), (1024, OFF_CKV, 256),
                             (1280, OFF_KR, 64), (1344, OFF_GA, 1024), (2368, OFF_GB, 1024)):
        src[pad:pad + width] = np.arange(orig, orig + width)
    rows = np.where(src >= 0, (src // W_IN_PIECE) * PIECE_PAD + src % W_IN_PIECE, -1)
    return rows.astype(np.int32)[None, :]


def _selector(j, map_ref):
    rid = j * PIECE_PAD + lax.broadcasted_iota(jnp.int32, (PIECE_PAD, IN_PAD), 0)
    return jnp.where(rid == map_ref[...], 1.0, 0.0).astype(BF16)


def assemble_w_in(pieces, row_map, *, tr):
    def body(p_ref, map_ref, o_ref, acc_ref):
        j = pl.program_id(1)

        @pl.when(j == 0)
        def _():
            acc_ref[...] = jnp.zeros_like(acc_ref)

        acc_ref[...] += _dot(p_ref[0], _selector(j, map_ref))

        @pl.when(j == N_DEV - 1)
        def _():
            o_ref[...] = acc_ref[...].astype(BF16)

    return pl.pallas_call(
        body, name="assemble_w_in", grid=(D_MODEL // tr, N_DEV),
        in_specs=[pl.BlockSpec((1, tr, PIECE_PAD), lambda i, j: (j, i, 0)), _full((1, IN_PAD))],
        out_specs=pl.BlockSpec((tr, IN_PAD), lambda i, j: (i, 0)),
        out_shape=jax.ShapeDtypeStruct((D_MODEL, IN_PAD), BF16),
        scratch_shapes=[pltpu.VMEM((tr, IN_PAD), F32)],
        compiler_params=_params("parallel", "arbitrary"),
    )(pieces, row_map)


def split_dw_in(dwp, row_map):
    def body(d_ref, map_ref, o_ref):
        o_ref[0] = _dot_nt(d_ref[...], _selector(pl.program_id(0), map_ref)).astype(BF16)

    return pl.pallas_call(
        body, name="split_dw_in", grid=(N_DEV,),
        in_specs=[_full((D_MODEL, IN_PAD)), _full((1, IN_PAD))],
        out_specs=pl.BlockSpec((1, D_MODEL, PIECE_PAD), lambda j: (j, 0, 0)),
        out_shape=jax.ShapeDtypeStruct((N_DEV, D_MODEL, PIECE_PAD), BF16),
        compiler_params=_params("parallel"),
    )(dwp, row_map)


def rms_matmul(x, g, w, *, tm, tn, name):
    T, D = x.shape
    N = w.shape[1]
    tm = min(tm, T)

    def body(x_ref, g_ref, w_ref, o_ref, h_ref):
        @pl.when(pl.program_id(1) == 0)
        def _():
            h_ref[...] = _rms_fwd(x_ref[...], g_ref[...]).astype(BF16)

        o_ref[...] = _dot(h_ref[...], w_ref[...])

    return pl.pallas_call(
        body, name=name, grid=(T // tm, N // tn),
        in_specs=[pl.BlockSpec((tm, D), lambda i, j: (i, 0)), _full((1, D)),
                  pl.BlockSpec((D, tn), lambda i, j: (0, j))],
        out_specs=pl.BlockSpec((tm, tn), lambda i, j: (i, j)),
        out_shape=jax.ShapeDtypeStruct((T, N), F32),
        scratch_shapes=[pltpu.VMEM((tm, D), BF16)],
        compiler_params=_params("parallel", "arbitrary"),
    )(x, g, w)


def _band_matrices():
    s = np.arange(CHUNK)[:, None]
    t = np.arange(CHUNK)[None, :]
    low = np.stack([((s - t >= 0) & (s - t < w)) for w in POOL_WINDOWS]).astype(np.float32)
    up = np.stack([(t > s + CHUNK - w) for w in POOL_WINDOWS]).astype(np.float32)
    return low, up


def _window_count(row0, g):
    t = row0 + lax.broadcasted_iota(jnp.int32, (CHUNK, 1), 0)
    return jnp.minimum(t + 1, POOL_WINDOWS[g]).astype(F32)


def pool_fwd(z, low, up, w_pool, pool_scale, w_a, *, tm):
    T = z.shape[0]
    tm = min(tm, T)
    nch = tm // CHUNK
    ublk = OFF_U // POOL_WIDTH

    def body(u_ref, halo_ref, low_ref, up_ref, wp_ref, sc_ref, wa_ref, pooled_ref, ms_ref, ya_ref):
        i = pl.program_id(0)
        for c in range(nch):
            rows = slice(c * CHUNK, (c + 1) * CHUNK)
            for g in range(4):
                cols = slice(g * POOL_GROUP, (g + 1) * POOL_GROUP)
                cur = u_ref[rows, cols]
                if c == 0:
                    prev = jnp.where(i > 0, halo_ref[:, cols], 0.0)
                else:
                    prev = u_ref[(c - 1) * CHUNK:c * CHUNK, cols]
                s = _band(low_ref[g], cur) + _band(up_ref[g], prev)
                pooled = (s / _window_count(i * tm + c * CHUNK, g) - cur).astype(BF16)
                pooled_ref[rows, cols] = pooled
                ms_ref[rows, cols] = (_dot(pooled, wp_ref[g]) * sc_ref[:, cols]).astype(BF16)
        ya_ref[...] = _dot(ms_ref[...], wa_ref[...])

    return pl.pallas_call(
        body, name="pool_fwd", grid=(T // tm,),
        in_specs=[pl.BlockSpec((tm, POOL_WIDTH), lambda i: (i, ublk)),
                  pl.BlockSpec((CHUNK, POOL_WIDTH), lambda i: (jnp.maximum(i * nch - 1, 0), ublk)),
                  _full((4, CHUNK, CHUNK)), _full((4, CHUNK, CHUNK)),
                  _full((4, POOL_GROUP, POOL_GROUP)), _full((1, POOL_WIDTH)),
                  _full((POOL_WIDTH, D_MODEL))],
        out_specs=[pl.BlockSpec((tm, POOL_WIDTH), lambda i: (i, 0)),
                   pl.BlockSpec((tm, POOL_WIDTH), lambda i: (i, 0)),
                   pl.BlockSpec((tm, D_MODEL), lambda i: (i, 0))],
        out_shape=[jax.ShapeDtypeStruct((T, POOL_WIDTH), BF16),
                   jax.ShapeDtypeStruct((T, POOL_WIDTH), BF16),
                   jax.ShapeDtypeStruct((T, D_MODEL), F32)],
        compiler_params=_params("parallel"),
    )(z, z, low, up, w_pool, pool_scale, w_a)


def pool_bwd_a(dya, pooled, w_a, w_pool, pool_scale, *, tm):
    T = dya.shape[0]
    tm = min(tm, T)
    nch = tm // CHUNK

    def body(dya_ref, pooled_ref, wa_ref, wp_ref, sc_ref, dpc_ref, dsc_ref, dwp_ref):
        i = pl.program_id(0)

        @pl.when(i == 0)
        def _():
            dsc_ref[...] = jnp.zeros_like(dsc_ref)
            dwp_ref[...] = jnp.zeros_like(dwp_ref)

        dms = _dot_nt(dya_ref[...], wa_ref[...])
        for g in range(4):
            cols = slice(g * POOL_GROUP, (g + 1) * POOL_GROUP)
            pg = pooled_ref[:, cols]
            dmg = dms[:, cols]
            mixed = _dot(pg, wp_ref[g])
            dsc_ref[:, cols] += jnp.sum(dmg * mixed, axis=0, keepdims=True)
            dmixed = (dmg * sc_ref[:, cols]).astype(BF16)
            dwp_ref[g] += _dot_tn(pg, dmixed)
            dpooled = _dot_nt(dmixed, wp_ref[g])
            for c in range(nch):
                rows = slice(c * CHUNK, (c + 1) * CHUNK)
                dpc_ref[rows, cols] = dpooled[rows] / _window_count(i * tm + c * CHUNK, g)

    return pl.pallas_call(
        body, name="pool_bwd_a", grid=(T // tm,),
        in_specs=[pl.BlockSpec((tm, D_MODEL), lambda i: (i, 0)),
                  pl.BlockSpec((tm, POOL_WIDTH), lambda i: (i, 0)),
                  _full((POOL_WIDTH, D_MODEL)), _full((4, POOL_GROUP, POOL_GROUP)),
                  _full((1, POOL_WIDTH))],
        out_specs=[pl.BlockSpec((tm, POOL_WIDTH), lambda i: (i, 0)), _full((1, POOL_WIDTH)),
                   _full((4, POOL_GROUP, POOL_GROUP))],
        out_shape=[jax.ShapeDtypeStruct((T, POOL_WIDTH), F32),
                   jax.ShapeDtypeStruct((1, POOL_WIDTH), F32),
                   jax.ShapeDtypeStruct((4, POOL_GROUP, POOL_GROUP), F32)],
        compiler_params=_params("arbitrary"),
    )(dya, pooled, w_a, w_pool, pool_scale)


def pool_bwd_b(dpc, low_t, up_t, *, tm):
    T = dpc.shape[0]
    tm = min(tm, T)
    nch = tm // CHUNK
    last_chunk = T // CHUNK - 1

    def body(d_ref, halo_ref, low_ref, up_ref, du_ref):
        i = pl.program_id(0)
        for c in range(nch):
            rows = slice(c * CHUNK, (c + 1) * CHUNK)
            for g in range(4):
                cols = slice(g * POOL_GROUP, (g + 1) * POOL_GROUP)
                cur = d_ref[rows, cols]
                if c == nch - 1:
                    nxt = jnp.where(i < pl.num_programs(0) - 1, halo_ref[:, cols], 0.0)
                else:
                    nxt = d_ref[(c + 1) * CHUNK:(c + 2) * CHUNK, cols]
                s = _band(low_ref[g], cur) + _band(up_ref[g], nxt)
                du_ref[rows, cols] = (s - cur * _window_count(i * tm + c * CHUNK, g)).astype(BF16)

    return pl.pallas_call(
        body, name="pool_bwd_b", grid=(T // tm,),
        in_specs=[pl.BlockSpec((tm, POOL_WIDTH), lambda i: (i, 0)),
                  pl.BlockSpec((CHUNK, POOL_WIDTH),
                               lambda i: (jnp.minimum((i + 1) * nch, last_chunk), 0)),
                  _full((4, CHUNK, CHUNK)), _full((4, CHUNK, CHUNK))],
        out_specs=pl.BlockSpec((tm, POOL_WIDTH), lambda i: (i, 0)),
        out_shape=jax.ShapeDtypeStruct((T, POOL_WIDTH), BF16),
        compiler_params=_params("parallel"),
    )(dpc, dpc, low_t, up_t)


def mla_prep(z, cos_t, s1_t, s2_t, q_norm, kv_norm, w_uq, w_ukv, *, tm):
    T = z.shape[0]
    tm = min(tm, T)

    def body(cq_ref, ckv_ref, kr_ref, c_ref, s1_ref, s2_ref, qg_ref, kvg_ref, wuq_ref, wukv_ref,
             q_ref, k_ref, v_ref):
        c, s1, s2 = c_ref[...], s1_ref[...], s2_ref[...]
        qn = _rms_fwd(cq_ref[...], qg_ref[...]).astype(BF16)
        q = _dot(qn, wuq_ref[...])
        kvn = _rms_fwd(ckv_ref[...], kvg_ref[...]).astype(BF16)
        kv = _dot(kvn, wukv_ref[...])
        kpe = _rope(kr_ref[...], c, s1, s2, 1.0).astype(BF16)
        for h in range(N_HEADS):
            o = h * HEAD_PAD
            q_ref[h, :, 0:QK_NOPE] = (q[:, o:o + QK_NOPE] * ATTN_SCALE).astype(BF16)
            q_ref[h, :, QK_NOPE:HEAD_PAD] = (
                _rope(q[:, o + QK_NOPE:o + HEAD_PAD], c, s1, s2, 1.0) * ATTN_SCALE).astype(BF16)
            k_ref[h, :, 0:QK_NOPE] = kv[:, o:o + QK_NOPE].astype(BF16)
            k_ref[h, :, QK_NOPE:HEAD_PAD] = kpe
            v_ref[h] = kv[:, o + QK_NOPE:o + HEAD_PAD].astype(BF16)

    tok = lambda w: pl.BlockSpec((tm, w), lambda i: (i, 0))
    return pl.pallas_call(
        body, name="mla_prep", grid=(T // tm,),
        in_specs=[pl.BlockSpec((tm, Q_LORA), lambda i: (i, OFF_CQ // Q_LORA)),
                  pl.BlockSpec((tm, KV_LORA), lambda i: (i, OFF_CKV // KV_LORA)),
                  pl.BlockSpec((tm, LANES), lambda i: (i, OFF_KR // LANES)),
                  tok(LANES), tok(LANES), tok(LANES),
                  _full((1, Q_LORA)), _full((1, KV_LORA)),
                  _full((Q_LORA, N_HEADS * HEAD_PAD)), _full((KV_LORA, N_HEADS * HEAD_PAD))],
        out_specs=[pl.BlockSpec((N_HEADS, tm, HEAD_PAD), lambda i: (0, i, 0)),
                   pl.BlockSpec((N_HEADS, tm, HEAD_PAD), lambda i: (0, i, 0)),
                   pl.BlockSpec((N_HEADS, tm, V_HEAD), lambda i: (0, i, 0))],
        out_shape=[jax.ShapeDtypeStruct((N_HEADS, T, HEAD_PAD), BF16),
                   jax.ShapeDtypeStruct((N_HEADS, T, HEAD_PAD), BF16),
                   jax.ShapeDtypeStruct((N_HEADS, T, V_HEAD), BF16)],
        compiler_params=_params("parallel"),
    )(z, z, z, cos_t, s1_t, s2_t, q_norm, kv_norm, w_uq, w_ukv)


def _row_vector(col, n):
    return jnp.transpose(jnp.broadcast_to(col, (n, LANES)))[0:1, :]


def flash_fwd(q, k, v, *, tq):
    H, T, _ = q.shape
    tq = min(tq, T)
    nq = T // tq
    neg = -1e30

    def body(q_ref, k_ref, v_ref, o_ref, lse_ref):
        qi = pl.program_id(1)
        qb = q_ref[0]

        def tile(j, carry, masked):
            m, l, acc = carry
            rows = pl.ds(pl.multiple_of(j * tq, tq), tq)
            s = _dot_nt(qb, k_ref[0, rows, :])
            if masked:
                r = lax.broadcasted_iota(jnp.int32, (tq, tq), 0)
                c = lax.broadcasted_iota(jnp.int32, (tq, tq), 1)
                s = jnp.where(c <= r, s, neg)
            m_new = jnp.maximum(m, jnp.max(s, axis=-1, keepdims=True))
            alpha = jnp.exp(m - m_new)
            p = jnp.exp(s - m_new)
            l = alpha * l + jnp.sum(p, axis=-1, keepdims=True)
            acc = alpha * acc + _dot(p.astype(BF16), v_ref[0, rows, :])
            return m_new, l, acc

        init = (jnp.full((tq, 1), neg, F32), jnp.zeros((tq, 1), F32), jnp.zeros((tq, V_HEAD), F32))
        carry = lax.fori_loop(0, qi, lambda j, cr: tile(j, cr, False), init)
        m, l, acc = tile(qi, carry, True)
        o_ref[...] = (acc / l).astype(BF16)
        lse_ref[0, 0] = _row_vector(m + jnp.log(l), tq)

    return pl.pallas_call(
        body, name="flash_fwd", grid=(H, nq),
        in_specs=[pl.BlockSpec((1, tq, HEAD_PAD), lambda h, i: (h, i, 0)),
                  pl.BlockSpec((1, T, HEAD_PAD), lambda h, i: (h, 0, 0)),
                  pl.BlockSpec((1, T, V_HEAD), lambda h, i: (h, 0, 0))],
        out_specs=[pl.BlockSpec((tq, V_HEAD), lambda h, i: (i, h)),
                   pl.BlockSpec((1, 1, 1, tq), lambda h, i: (h, i, 0, 0))],
        out_shape=[jax.ShapeDtypeStruct((T, H * V_HEAD), BF16),
                   jax.ShapeDtypeStruct((H, nq, 1, tq), F32)],
        compiler_params=_params("parallel", "arbitrary"),
    )(q, k, v)


def attn_delta(do, o, *, tq):
    T = do.shape[0]
    tq = min(tq, T)
    nq = T // tq
    group = max(g for g in (1, 2, 4) if nq % g == 0)

    def body(do_ref, o_ref, d_ref):
        for b in range(group):
            rows = slice(b * tq, (b + 1) * tq)
            prod = do_ref[rows, :].astype(F32) * o_ref[rows, :].astype(F32)
            d_ref[0, b] = jnp.sum(jnp.transpose(prod), axis=0, keepdims=True)

    return pl.pallas_call(
        body, name="attn_delta", grid=(N_HEADS, nq // group),
        in_specs=[pl.BlockSpec((group * tq, V_HEAD), lambda h, i: (i, h)),
                  pl.BlockSpec((group * tq, V_HEAD), lambda h, i: (i, h))],
        out_specs=pl.BlockSpec((1, group, 1, tq), lambda h, i: (h, i, 0, 0)),
        out_shape=jax.ShapeDtypeStruct((N_HEADS, nq, 1, tq), F32),
        compiler_params=_params("parallel", "parallel"),
    )(do, o)


def flash_bwd(q, k, v, do, lse, delta, *, tq, tk):
    H, T, _ = q.shape
    tq, tk = min(tq, T), min(tk, T)
    nq, nk = T // tq, T // tk
    neg = -1e30

    def body(q_ref, do_ref, lse_ref, dl_ref, k_ref, v_ref, dq_ref, dk_ref, dv_ref, dq_acc):
        j = pl.program_id(1)

        @pl.when(j == 0)
        def _():
            dq_acc[...] = jnp.zeros_like(dq_acc)

        kb = k_ref[0]
        vb = v_ref[0]

        def tile(i, carry, masked):
            dk, dv = carry
            rows = pl.ds(pl.multiple_of(i * tq, tq), tq)
            qb = q_ref[0, rows, :]
            dob = do_ref[rows, :]
            st = _dot_nt(kb, qb)
            if masked:
                key = j * tk + lax.broadcasted_iota(jnp.int32, (tk, tq), 0)
                query = i * tq + lax.broadcasted_iota(jnp.int32, (tk, tq), 1)
                st = jnp.where(key <= query, st, neg)
            pt = jnp.exp(st - lse_ref[0, i])
            dv = dv + _dot(pt.astype(BF16), dob)
            dpt = _dot_nt(vb, dob)
            dst = (pt * (dpt - dl_ref[0, i])).astype(BF16)
            dk = dk + _dot(dst, qb)
            dq_acc[rows, :] += _dot_tn(dst, kb)
            return dk, dv

        first = (j * tk) // tq
        carry = tile(first, (jnp.zeros((tk, HEAD_PAD), F32), jnp.zeros((tk, V_HEAD), F32)), True)
        dk, dv = lax.fori_loop(first + 1, nq, lambda i, cr: tile(i, cr, False), carry)
        dk_ref[0] = dk.astype(BF16)
        dv_ref[0] = dv.astype(BF16)

        @pl.when(j == nk - 1)
        def _():
            dq_ref[0] = dq_acc[...].astype(BF16)

    return pl.pallas_call(
        body, name="flash_bwd", grid=(H, nk),
        in_specs=[pl.BlockSpec((1, T, HEAD_PAD), lambda h, j: (h, 0, 0)),
                  pl.BlockSpec((T, V_HEAD), lambda h, j: (0, h)),
                  pl.BlockSpec((1, nq, 1, tq), lambda h, j: (h, 0, 0, 0)),
                  pl.BlockSpec((1, nq, 1, tq), lambda h, j: (h, 0, 0, 0)),
                  pl.BlockSpec((1, tk, HEAD_PAD), lambda h, j: (h, j, 0)),
                  pl.BlockSpec((1, tk, V_HEAD), lambda h, j: (h, j, 0))],
        out_specs=[pl.BlockSpec((1, T, HEAD_PAD), lambda h, j: (h, 0, 0)),
                   pl.BlockSpec((1, tk, HEAD_PAD), lambda h, j: (h, j, 0)),
                   pl.BlockSpec((1, tk, V_HEAD), lambda h, j: (h, j, 0))],
        out_shape=[jax.ShapeDtypeStruct((H, T, HEAD_PAD), BF16),
                   jax.ShapeDtypeStruct((H, T, HEAD_PAD), BF16),
                   jax.ShapeDtypeStruct((H, T, V_HEAD), BF16)],
        scratch_shapes=[pltpu.VMEM((T, HEAD_PAD), F32)],
        compiler_params=_params("parallel", "arbitrary"),
    )(q, do, lse, delta, k, v)


def mla_bwd(dq, dk, dv, z, cos_t, s1_t, s2_t, q_norm, kv_norm, w_uq, w_ukv, *, tm):
    T = z.shape[0]
    tm = min(tm, T)
    HW = N_HEADS * HEAD_PAD

    def body(dq_ref, dk_ref, dv_ref, cq_ref, ckv_ref, c_ref, s1_ref, s2_ref, qg_ref, kvg_ref,
             wuq_ref, wukv_ref, dcq_ref, dckv_ref, dkr_ref, dqf_ref, dkvf_ref, qn_ref, kvn_ref,
             dqg_ref, dkvg_ref):
        @pl.when(pl.program_id(0) == 0)
        def _():
            dqg_ref[...] = jnp.zeros_like(dqg_ref)
            dkvg_ref[...] = jnp.zeros_like(dkvg_ref)

        c, s1, s2 = c_ref[...], s1_ref[...], s2_ref[...]
        dkpe = jnp.zeros((tm, LANES), F32)
        for h in range(N_HEADS):
            o = h * HEAD_PAD
            dqf_ref[:, o:o + QK_NOPE] = (
                dq_ref[h, :, 0:QK_NOPE].astype(F32) * ATTN_SCALE).astype(BF16)
            dqf_ref[:, o + QK_NOPE:o + HEAD_PAD] = (
                _rope(dq_ref[h, :, QK_NOPE:HEAD_PAD].astype(F32), c, s1, s2, -1.0)
                * ATTN_SCALE).astype(BF16)
            dkvf_ref[:, o:o + QK_NOPE] = dk_ref[h, :, 0:QK_NOPE]
            dkvf_ref[:, o + QK_NOPE:o + HEAD_PAD] = dv_ref[h]
            dkpe = dkpe + dk_ref[h, :, QK_NOPE:HEAD_PAD].astype(F32)
        dkr_ref[...] = _rope(dkpe, c, s1, s2, -1.0).astype(BF16)

        cq = cq_ref[...]
        qn_ref[...] = _rms_fwd(cq, qg_ref[...]).astype(BF16)
        dcq, dgrows = _rms_bwd(cq, qg_ref[...], _dot_nt(dqf_ref[...], wuq_ref[...]))
        dcq_ref[...] = dcq.astype(BF16)
        _accum_rows(dqg_ref, dgrows)

        ckv = ckv_ref[...]
        kvn_ref[...] = _rms_fwd(ckv, kvg_ref[...]).astype(BF16)
        dckv, dgrows = _rms_bwd(ckv, kvg_ref[...], _dot_nt(dkvf_ref[...], wukv_ref[...]))
        dckv_ref[...] = dckv.astype(BF16)
        _accum_rows(dkvg_ref, dgrows)

    tok = lambda w: pl.BlockSpec((tm, w), lambda i: (i, 0))
    head = lambda w: pl.BlockSpec((N_HEADS, tm, w), lambda i: (0, i, 0))
    return pl.pallas_call(
        body, name="mla_bwd", grid=(T // tm,),
        in_specs=[head(HEAD_PAD), head(HEAD_PAD), head(V_HEAD),
                  pl.BlockSpec((tm, Q_LORA), lambda i: (i, OFF_CQ // Q_LORA)),
                  pl.BlockSpec((tm, KV_LORA), lambda i: (i, OFF_CKV // KV_LORA)),
                  tok(LANES), tok(LANES), tok(LANES),
                  _full((1, Q_LORA)), _full((1, KV_LORA)),
                  _full((Q_LORA, HW)), _full((KV_LORA, HW))],
        out_specs=[tok(Q_LORA), tok(KV_LORA), tok(LANES), tok(HW), tok(HW), tok(Q_LORA),
                   tok(KV_LORA), _full((1, Q_LORA)), _full((1, KV_LORA))],
        out_shape=[jax.ShapeDtypeStruct((T, Q_LORA), BF16),
                   jax.ShapeDtypeStruct((T, KV_LORA), BF16),
                   jax.ShapeDtypeStruct((T, LANES), BF16),
                   jax.ShapeDtypeStruct((T, HW), BF16),
                   jax.ShapeDtypeStruct((T, HW), BF16),
                   jax.ShapeDtypeStruct((T, Q_LORA), BF16),
                   jax.ShapeDtypeStruct((T, KV_LORA), BF16),
                   jax.ShapeDtypeStruct((1, Q_LORA), F32),
                   jax.ShapeDtypeStruct((1, KV_LORA), F32)],
        compiler_params=_params("arbitrary"),
    )(dq, dk, dv, z, z, cos_t, s1_t, s2_t, q_norm, kv_norm, w_uq, w_ukv)


def merge_fwd(o, ya, z, x, w_b, w_o, *, tm):
    T = x.shape[0]
    tm = min(tm, T)

    def body(o_ref, ya_ref, ga_ref, gb_ref, x_ref, wb_ref, wo_ref, yb_ref, mg_ref, x1_ref):
        yb = _dot(o_ref[...], wb_ref[...])
        yb_ref[...] = yb
        merged = (_sigmoid(ga_ref[...]) * ya_ref[...] + _sigmoid(gb_ref[...]) * yb).astype(BF16)
        mg_ref[...] = merged
        x1_ref[...] = x_ref[...] + _dot(merged, wo_ref[...])

    tok = pl.BlockSpec((tm, D_MODEL), lambda i: (i, 0))
    return pl.pallas_call(
        body, name="merge_fwd", grid=(T // tm,),
        in_specs=[tok, tok, pl.BlockSpec((tm, D_MODEL), lambda i: (i, OFF_GA // D_MODEL)),
                  pl.BlockSpec((tm, D_MODEL), lambda i: (i, OFF_GB // D_MODEL)), tok,
                  _full((D_MODEL, D_MODEL)), _full((D_MODEL, D_MODEL))],
        out_specs=[tok, tok, tok],
        out_shape=[jax.ShapeDtypeStruct((T, D_MODEL), F32),
                   jax.ShapeDtypeStruct((T, D_MODEL), BF16),
                   jax.ShapeDtypeStruct((T, D_MODEL), F32)],
        compiler_params=_params("parallel"),
    )(o, ya, z, z, x, w_b, w_o)


def merge_bwd(dx1, ya, yb, z, w_o, w_b, dep, *, tm):
    T = dx1.shape[0]
    tm = min(tm, T)

    def body(dx_ref, ya_ref, yb_ref, ga_ref, gb_ref, wo_ref, wb_ref, dep_ref, dya_ref, dyb_ref,
             do_ref, dga_ref, dgb_ref):
        dm = _dot_nt(dx_ref[...].astype(BF16), wo_ref[...])
        sa = _sigmoid(ga_ref[...])
        sb = _sigmoid(gb_ref[...])
        dya_ref[...] = (dm * sa).astype(BF16)
        dyb = (dm * sb).astype(BF16)
        dyb_ref[...] = dyb
        dga_ref[...] = (dm * ya_ref[...] * sa * (1.0 - sa)).astype(BF16)
        dgb_ref[...] = (dm * yb_ref[...] * sb * (1.0 - sb)).astype(BF16)
        do_ref[...] = _dot_nt(dyb, wb_ref[...]).astype(BF16)

    tok = pl.BlockSpec((tm, D_MODEL), lambda i: (i, 0))
    return pl.pallas_call(
        body, name="merge_bwd", grid=(T // tm,),
        in_specs=[tok, tok, tok, pl.BlockSpec((tm, D_MODEL), lambda i: (i, OFF_GA // D_MODEL)),
                  pl.BlockSpec((tm, D_MODEL), lambda i: (i, OFF_GB // D_MODEL)),
                  _full((D_MODEL, D_MODEL)), _full((D_MODEL, D_MODEL)),
                  pl.BlockSpec(memory_space=pl.ANY)],
        out_specs=[tok] * 5,
        out_shape=[jax.ShapeDtypeStruct((T, D_MODEL), BF16)] * 5,
        compiler_params=_params("parallel"),
    )(dx1, ya, yb, z, z, w_o, w_b, dep)


def in_bwd(dx1, x, g, segs, w_in, *, tm):
    T = x.shape[0]
    tm = min(tm, T)
    widths = [s.shape[1] for s, _ in segs]
    offs = [o for _, o in segs]
    n = len(segs)

    def body(*refs):
        dx1_ref, x_ref, g_ref = refs[:3]
        seg_refs = refs[3:3 + n]
        w_ref = refs[3 + n]
        dx_ref, h_ref, dg_ref = refs[4 + n:]

        @pl.when(pl.program_id(0) == 0)
        def _():
            dg_ref[...] = jnp.zeros_like(dg_ref)

        dh = jnp.zeros((tm, D_MODEL), F32)
        for s_ref, off, w in zip(seg_refs, offs, widths):
            dh = dh + _dot_nt(s_ref[...], w_ref[:, off:off + w])
        xv = x_ref[...]
        h_ref[...] = _rms_fwd(xv, g_ref[...]).astype(BF16)
        dxn, dgrows = _rms_bwd(xv, g_ref[...], dh)
        dx_ref[...] = dx1_ref[...] + dxn
        _accum_rows(dg_ref, dgrows)

    tok = pl.BlockSpec((tm, D_MODEL), lambda i: (i, 0))
    return pl.pallas_call(
        body, name="in_bwd", grid=(T // tm,),
        in_specs=[tok, tok, _full((1, D_MODEL))]
        + [pl.BlockSpec((tm, w), lambda i: (i, 0)) for w in widths] + [_full((D_MODEL, IN_PAD))],
        out_specs=[tok, tok, _full((1, D_MODEL))],
        out_shape=[jax.ShapeDtypeStruct((T, D_MODEL), F32),
                   jax.ShapeDtypeStruct((T, D_MODEL), BF16),
                   jax.ShapeDtypeStruct((1, D_MODEL), F32)],
        compiler_params=_params("arbitrary"),
    )(dx1, x, g, *[s for s, _ in segs], w_in)


def _pair_cols(w_ref):
    return jnp.concatenate([w_ref[0], w_ref[1]], axis=1)


def _pair_rows(w_ref):
    return jnp.concatenate([w_ref[0], w_ref[1]], axis=0)


def ffn_fwd(x1, g, w_gate, w_up, w_down, *, tm):
    T = x1.shape[0]
    tm = min(tm, T)

    def body(x_ref, g_ref, wg_ref, wu_ref, wd_ref, gp_ref, up_ref, x2_ref, h_ref, acc_ref):
        f = pl.program_id(1)

        @pl.when(f == 0)
        def _():
            h_ref[...] = _rms_fwd(x_ref[...], g_ref[...]).astype(BF16)
            acc_ref[...] = jnp.zeros_like(acc_ref)

        gp = _dot(h_ref[...], _pair_cols(wg_ref))
        up = _dot(h_ref[...], _pair_cols(wu_ref))
        gp_ref[0] = gp
        up_ref[0] = up
        act = (gp * _sigmoid(gp) * up).astype(BF16)
        acc_ref[...] += _dot(act, _pair_rows(wd_ref))

        @pl.when(f == FF_PAIRS - 1)
        def _():
            x2_ref[...] = x_ref[...] + acc_ref[...]

    tok = pl.BlockSpec((tm, D_MODEL), lambda i, f: (i, 0))
    ff = pl.BlockSpec((1, tm, FF_STEP), lambda i, f: (f, i, 0))
    w_col = pl.BlockSpec((2, D_MODEL, FF_PAD), lambda i, f: (f, 0, 0))
    w_row = pl.BlockSpec((2, FF_PAD, D_MODEL), lambda i, f: (f, 0, 0))
    return pl.pallas_call(
        body, name="ffn_fwd", grid=(T // tm, FF_PAIRS),
        in_specs=[tok, _full((1, D_MODEL)), w_col, w_col, w_row],
        out_specs=[ff, ff, tok],
        out_shape=[jax.ShapeDtypeStruct((FF_PAIRS, T, FF_STEP), F32),
                   jax.ShapeDtypeStruct((FF_PAIRS, T, FF_STEP), F32),
                   jax.ShapeDtypeStruct((T, D_MODEL), F32)],
        scratch_shapes=[pltpu.VMEM((tm, D_MODEL), BF16), pltpu.VMEM((tm, D_MODEL), F32)],
        compiler_params=_params("parallel", "arbitrary"),
    )(x1, g, w_gate, w_up, w_down)


def ffn_bwd(dx2, x1, g, gpre, upre, w_gate, w_up, w_down, *, tm):
    T = x1.shape[0]
    tm = min(tm, T)

    def body(dx_ref, x_ref, g_ref, gp_ref, up_ref, wg_ref, wu_ref, wd_ref,
             act_ref, dg_ref, du_ref, h_ref, dx1_ref, dgain_ref, dxb_ref, acc_ref):
        f = pl.program_id(1)

        @pl.when((pl.program_id(0) == 0) & (f == 0))
        def _():
            dgain_ref[...] = jnp.zeros_like(dgain_ref)

        @pl.when(f == 0)
        def _():
            dxb_ref[...] = dx_ref[...].astype(BF16)
            acc_ref[...] = jnp.zeros_like(acc_ref)

        gp = gp_ref[0]
        up = up_ref[0]
        dact = _dot_nt(dxb_ref[...], _pair_rows(wd_ref))
        sg = _sigmoid(gp)
        silu = gp * sg
        act_ref[0] = (silu * up).astype(BF16)
        dgp = (dact * up * (sg * (1.0 + gp * (1.0 - sg)))).astype(BF16)
        dup = (dact * silu).astype(BF16)
        dg_ref[0] = dgp
        du_ref[0] = dup
        acc_ref[...] += _dot_nt(dgp, _pair_cols(wg_ref)) + _dot_nt(dup, _pair_cols(wu_ref))

        @pl.when(f == FF_PAIRS - 1)
        def _():
            xv = x_ref[...]
            h_ref[...] = _rms_fwd(xv, g_ref[...]).astype(BF16)
            dxn, dgrows = _rms_bwd(xv, g_ref[...], acc_ref[...])
            dx1_ref[...] = dx_ref[...] + dxn
            _accum_rows(dgain_ref, dgrows)

    tok = pl.BlockSpec((tm, D_MODEL), lambda i, f: (i, 0))
    ff = pl.BlockSpec((1, tm, FF_STEP), lambda i, f: (f, i, 0))
    w_col = pl.BlockSpec((2, D_MODEL, FF_PAD), lambda i, f: (f, 0, 0))
    w_row = pl.BlockSpec((2, FF_PAD, D_MODEL), lambda i, f: (f, 0, 0))
    return pl.pallas_call(
        body, name="ffn_bwd", grid=(T // tm, FF_PAIRS),
        in_specs=[tok, tok, _full((1, D_MODEL)), ff, ff, w_col, w_col, w_row],
        out_specs=[ff, ff, ff, tok, tok, _full((1, D_MODEL))],
        out_shape=[jax.ShapeDtypeStruct((FF_PAIRS, T, FF_STEP), BF16)] * 3
        + [jax.ShapeDtypeStruct((T, D_MODEL), BF16), jax.ShapeDtypeStruct((T, D_MODEL), F32),
           jax.ShapeDtypeStruct((1, D_MODEL), F32)],
        scratch_shapes=[pltpu.VMEM((tm, D_MODEL), BF16), pltpu.VMEM((tm, D_MODEL), F32)],
        compiler_params=_params("arbitrary", "arbitrary"),
    )(dx2, x1, g, gpre, upre, w_gate, w_up, w_down)


def ple_fwd(x2, p, g, w_pg, w_ple, *, tm):
    T = x2.shape[0]
    tm = min(tm, T)

    def body(x_ref, p_ref, g_ref, wpg_ref, wple_ref, l_ref, pe_ref, x3_ref):
        xv = x_ref[...]
        logits = _dot(_rms_fwd(xv, g_ref[...]).astype(BF16), wpg_ref[...])
        pe = _dot(p_ref[...].astype(BF16), wple_ref[...])
        l_ref[...] = logits
        pe_ref[...] = pe
        x3_ref[...] = xv + _sigmoid(logits) * pe

    tok = pl.BlockSpec((tm, D_MODEL), lambda i: (i, 0))
    return pl.pallas_call(
        body, name="ple_fwd", grid=(T // tm,),
        in_specs=[tok, pl.BlockSpec((tm, PLE_DIM), lambda i: (i, 0)), _full((1, D_MODEL)),
                  _full((D_MODEL, D_MODEL)), _full((PLE_DIM, D_MODEL))],
        out_specs=[tok, tok, tok],
        out_shape=[jax.ShapeDtypeStruct((T, D_MODEL), F32)] * 3,
        compiler_params=_params("parallel"),
    )(x2, p, g, w_pg, w_ple)


def ple_bwd(dx3, logits, pe, x2, g, w_pg, *, tm):
    T = x2.shape[0]
    tm = min(tm, T)

    def body(dx_ref, l_ref, pe_ref, x_ref, g_ref, wpg_ref, dl_ref, dpe_ref, h_ref, dx2_ref,
             dg_ref):
        @pl.when(pl.program_id(0) == 0)
        def _():
            dg_ref[...] = jnp.zeros_like(dg_ref)

        dx = dx_ref[...]
        s = _sigmoid(l_ref[...])
        dpe_ref[...] = (dx * s).astype(BF16)
        dl = (dx * pe_ref[...] * s * (1.0 - s)).astype(BF16)
        dl_ref[...] = dl
        xv = x_ref[...]
        h_ref[...] = _rms_fwd(xv, g_ref[...]).astype(BF16)
        dxn, dgrows = _rms_bwd(xv, g_ref[...], _dot_nt(dl, wpg_ref[...]))
        dx2_ref[...] = dx + dxn
        _accum_rows(dg_ref, dgrows)

    tok = pl.BlockSpec((tm, D_MODEL), lambda i: (i, 0))
    return pl.pallas_call(
        body, name="ple_bwd", grid=(T // tm,),
        in_specs=[tok, tok, tok, tok, _full((1, D_MODEL)), _full((D_MODEL, D_MODEL))],
        out_specs=[tok, tok, tok, tok, _full((1, D_MODEL))],
        out_shape=[jax.ShapeDtypeStruct((T, D_MODEL), BF16)] * 3
        + [jax.ShapeDtypeStruct((T, D_MODEL), F32), jax.ShapeDtypeStruct((1, D_MODEL), F32)],
        compiler_params=_params("arbitrary"),
    )(dx3, logits, pe, x2, g, w_pg)


def loss_head(x, g, target, *, tm):
    T = x.shape[0]
    tm = min(tm, T)

    def body(x_ref, g_ref, t_ref, loss_ref, dx_ref, dg_ref):
        @pl.when(pl.program_id(0) == 0)
        def _():
            loss_ref[...] = jnp.zeros_like(loss_ref)
            dg_ref[...] = jnp.zeros_like(dg_ref)

        xv = x_ref[...]
        err = _rms_fwd(xv, g_ref[...]) - t_ref[...]
        part = 0.5 * jnp.sum(jnp.mean(err * err, axis=-1, keepdims=True), axis=0, keepdims=True)
        lane = lax.broadcasted_iota(jnp.int32, (1, LANES), 1)
        loss_ref[...] += jnp.where(lane == 0, part, 0.0)
        dxn, dgrows = _rms_bwd(xv, g_ref[...], err * (1.0 / D_MODEL))
        dx_ref[...] = dxn
        _accum_rows(dg_ref, dgrows)

    tok = pl.BlockSpec((tm, D_MODEL), lambda i: (i, 0))
    return pl.pallas_call(
        body, name="loss_head", grid=(T // tm,),
        in_specs=[tok, _full((1, D_MODEL)), tok],
        out_specs=[_full((1, LANES)), tok, _full((1, D_MODEL))],
        out_shape=[jax.ShapeDtypeStruct((1, LANES), F32), jax.ShapeDtypeStruct((T, D_MODEL), F32),
                   jax.ShapeDtypeStruct((1, D_MODEL), F32)],
        compiler_params=_params("arbitrary"),
    )(x, g, target)


def _peers():
    x, y, c = lax.axis_index("x"), lax.axis_index("y"), lax.axis_index("c")
    me = 4 * x + 2 * y + c
    out = []
    for d in range(1, N_DEV):
        px = 1 - x if d & 4 else x
        py = 1 - y if d & 2 else y
        pc = 1 - c if d & 1 else c
        out.append(((px, py, pc), 4 * px + 2 * py + pc))
    return me, out


_HBM = pl.BlockSpec(memory_space=pltpu.HBM)
_SEM = pl.BlockSpec(memory_space=pltpu.SEMAPHORE)
_ANY = pl.BlockSpec(memory_space=pl.ANY)
_DATAFLOW = pltpu.SideEffectType.DATAFLOW_SIDE_EFFECTING


def _exchange_copy(t, d, pos, idx, me, src_ref, scatter, land_ref, send_sems, recv_sems):
    k = t * (N_DEV - 1) + d
    return pltpu.make_async_remote_copy(
        src_ref=src_ref.at[idx] if scatter else src_ref, dst_ref=land_ref.at[me],
        send_sem=send_sems.at[k], recv_sem=recv_sems.at[k],
        device_id=pos, device_id_type=pl.DeviceIdType.MESH)


def _own_slot_copy(t, me, src_ref, scatter, land_ref, local_sems):
    return pltpu.make_async_copy(src_ref.at[me] if scatter else src_ref, land_ref.at[me],
                                 local_sems.at[t])


def exchange_start(gathers, scatters, dep, *, name):
    srcs = [pltpu.with_memory_space_constraint(a, pltpu.HBM) for a in list(gathers) + list(scatters)]
    kinds = [False] * len(gathers) + [True] * len(scatters)
    lands = [pltpu.with_memory_space_constraint(
        lax.empty((N_DEV,) + (a.shape[1:] if sc else a.shape), a.dtype), pltpu.HBM)
        for a, sc in zip(srcs, kinds)]
    n = len(srcs)

    def body(*refs):
        src_refs, land_refs = refs[:n], refs[n:2 * n]
        send_sems, recv_sems, local_sems = refs[-2 * n - 4:-2 * n - 1]
        token = refs[-1]
        me, peers = _peers()
        for t in range(n):
            _own_slot_copy(t, me, src_refs[t], kinds[t], land_refs[t], local_sems).start()
        for t in range(n):
            for d, (pos, idx) in enumerate(peers):
                _exchange_copy(t, d, pos, idx, me, src_refs[t], kinds[t], land_refs[t],
                               send_sems, recv_sems).start()
        token[...] = jnp.zeros_like(token)

    sem = pltpu.SemaphoreType.DMA((n * (N_DEV - 1),))
    thru = [pltpu.HBM(a.shape, a.dtype) for a in srcs + lands]
    outs = pl.pallas_call(
        body, name=name,
        in_specs=[_HBM] * (2 * n) + ([_ANY] if dep is not None else []),
        out_specs=[_SEM] * 3 + [_HBM] * (2 * n) + [pl.BlockSpec(memory_space=pltpu.VMEM)],
        out_shape=[sem, sem, pltpu.SemaphoreType.DMA((n,))] + thru
        + [jax.ShapeDtypeStruct((8, LANES), F32)],
        input_output_aliases={i: 3 + i for i in range(2 * n)},
        compiler_params=pltpu.CompilerParams(has_side_effects=_DATAFLOW),
    )(*srcs, *lands, *([dep] if dep is not None else []))
    return dict(sems=outs[:3], srcs=outs[3:3 + n], lands=outs[3 + n:3 + 2 * n], token=outs[-1],
                kinds=kinds)


def exchange_wait(handle, after, *, name):
    kinds = handle["kinds"]
    n = len(kinds)

    def body(*refs):
        src_refs, land_refs = refs[:n], refs[n:2 * n]
        send_sems, recv_sems, local_sems = refs[2 * n:2 * n + 3]
        me, peers = _peers()
        for t in range(n):
            _own_slot_copy(t, me, src_refs[t], kinds[t], land_refs[t], local_sems).wait()
            for d, (pos, idx) in enumerate(peers):
                cp = _exchange_copy(t, d, pos, idx, me, src_refs[t], kinds[t], land_refs[t],
                                    send_sems, recv_sems)
                cp.wait_send()
                cp.wait_recv()

    arrays = list(handle["srcs"]) + list(handle["lands"])
    outs = pl.pallas_call(
        body, name=name,
        in_specs=[_HBM] * (2 * n) + [_SEM] * 3 + ([_ANY] if after is not None else []),
        out_specs=[_HBM] * (2 * n),
        out_shape=[pltpu.HBM(a.shape, a.dtype) for a in arrays],
        input_output_aliases={i: i for i in range(2 * n)},
        compiler_params=pltpu.CompilerParams(has_side_effects=_DATAFLOW),
    )(*arrays, *handle["sems"], *([after] if after is not None else []))
    return outs[n:]


def _row_block(rows, cols):
    best = None
    for rb in range(16, rows + 1, 16):
        if rows % rb == 0 and rb * cols <= 256 * 1024:
            best = rb
    return best or rows


def reduce_adamw(parts, w, m, v, *, name):
    nl = len(parts)
    C = w.shape[1]
    R, cp = parts[0].shape[1:]
    rb = _row_block(R, cp)
    nb = R // rb

    def body(*refs):
        p_refs = refs[:nl]
        w_ref, m_ref, v_ref, g_ref, d_ref, m2_ref, v2_ref = refs[nl:]

        def total(p_ref):
            g = p_ref[0, :, 0:C].astype(F32)
            for k in range(1, N_DEV):
                g = g + p_ref[k, :, 0:C].astype(F32)
            return g

        g = total(p_refs[0])
        for layer in range(1, nl):
            g = jnp.where(pl.program_id(0) == layer, total(p_refs[layer]), g)
        g_ref[...] = g
        m2 = ADAM_B1 * m_ref[...] + (1.0 - ADAM_B1) * g
        v2 = ADAM_B2 * v_ref[...] + (1.0 - ADAM_B2) * (g * g)
        m2_ref[...] = m2
        v2_ref[...] = v2
        m_hat = m2 / (1.0 - ADAM_B1 ** ADAM_STEP)
        v_hat = v2 / (1.0 - ADAM_B2 ** ADAM_STEP)
        d_ref[...] = -ADAM_LR * (m_hat / (jnp.sqrt(v_hat) + ADAM_EPS) + ADAM_WD * w_ref[...])

    def part_spec(layer):
        return pl.BlockSpec((N_DEV, rb, cp), lambda l, i: (0, jnp.where(l == layer, i, 0), 0))

    blk = pl.BlockSpec((rb, C), lambda l, i: (l * nb + i, 0))
    return pl.pallas_call(
        body, name=name, grid=(nl, nb),
        in_specs=[part_spec(layer) for layer in range(nl)] + [blk, blk, blk],
        out_specs=[blk] * 4,
        out_shape=[jax.ShapeDtypeStruct((nl * R, C), F32)] * 4,
        compiler_params=_params("arbitrary", "arbitrary"),
    )(*parts, w, m, v)


def _cols_to_pieces(full):
    r = full.shape[0]
    return jnp.transpose(full.reshape(r, N_DEV, -1), (1, 0, 2))


def _pieces_to_cols(pieces):
    r = pieces.shape[1]
    return jnp.transpose(pieces, (1, 0, 2)).reshape(r, -1)


def _rope_tables(positions):
    inv_freq = 1.0 / (ROPE_THETA ** (jnp.arange(0, QK_ROPE, 2, dtype=F32) / QK_ROPE))
    ang = positions.astype(F32)[:, None] * inv_freq
    cos, sin = jnp.cos(ang), jnp.sin(ang)
    zero = jnp.zeros_like(cos)
    return (jnp.concatenate([cos, cos, zero, zero], axis=1),
            jnp.concatenate([-sin, zero, zero, zero], axis=1),
            jnp.concatenate([zero, sin, zero, zero], axis=1))


def kernel(x, p, positions, norm_mix, w_in, w_pool, pool_scale, q_norm, kv_norm, w_uq, w_ukv, w_a, w_b, w_o, norm_ffn, w_gate, w_up, w_down, norm_ple, w_ple_gate, w_ple, final_norm, loss_target, m_norm_mix, m_w_in, m_w_pool, m_pool_scale, m_q_norm, m_kv_norm, m_w_uq, m_w_ukv, m_w_a, m_w_b, m_w_o, m_norm_ffn, m_w_gate, m_w_up, m_w_down, m_norm_ple, m_w_ple_gate, m_w_ple, m_final_norm, v_norm_mix, v_w_in, v_w_pool, v_pool_scale, v_q_norm, v_kv_norm, v_w_uq, v_w_ukv, v_w_a, v_w_b, v_w_o, v_norm_ffn, v_w_gate, v_w_up, v_w_down, v_norm_ple, v_w_ple_gate, v_w_ple, v_final_norm):
    given = dict(locals())
    shard = {n: given[n] for n in WEIGHTS}
    TM = 512
    TQ = 512

    def send_form(n, i):
        w = shard[n][i].astype(BF16)
        if n == "w_in":
            return jnp.pad(w, ((0, 0), (0, PIECE_PAD - W_IN_PIECE)))
        if n == "w_uq":
            w = jnp.pad(w, ((0, 0), (0, 0), (0, HEAD_PAD - QK_HEAD)))
        if n in ("w_uq", "w_ukv"):
            return w.reshape(-1, N_HEADS * HEAD_PAD)
        if n in ("w_gate", "w_up"):
            return jnp.pad(w, ((0, 0), (0, FF_PAD - FF_PIECE)))
        if n == "w_down":
            return jnp.pad(w, ((0, FF_PAD - FF_PIECE), (0, 0)))
        return w

    names = [n for n, _ in SHARDED]
    FIRST = ("w_in", "w_uq", "w_ukv", "w_a", "w_b", "w_o")
    LATER = ("w_gate", "w_up", "w_down", "w_ple_gate", "w_ple")
    start_a = exchange_start([send_form(n, 0) for n in FIRST], [], None, name="gather_start_a")
    got_a = exchange_wait(start_a, None, name="gather_wait_a")
    start_b1 = exchange_start([send_form(n, 0) for n in LATER], [], got_a[0],
                              name="gather_start_b1")
    gathered = {(n, 0): a for n, a in zip(FIRST, got_a)}
    row_map = jnp.asarray(_piece_row_map())

    low_np, up_np = _band_matrices()
    low, up = jnp.asarray(low_np, BF16), jnp.asarray(up_np, BF16)
    low_t = jnp.asarray(low_np.transpose(0, 2, 1), BF16)
    up_t = jnp.asarray(up_np.transpose(0, 2, 1), BF16)
    cos_t, s1_t, s2_t = _rope_tables(positions[0])

    def mix_weights(i):
        g = lambda n: gathered[n, i]
        return dict(
            w_in=assemble_w_in(g("w_in"), row_map, tr=512),
            w_uq=g("w_uq").reshape(Q_LORA, N_HEADS * HEAD_PAD),
            w_ukv=g("w_ukv").reshape(KV_LORA, N_HEADS * HEAD_PAD),
            w_a=_pieces_to_cols(g("w_a")), w_b=g("w_b").reshape(D_MODEL, D_MODEL),
            w_o=g("w_o").reshape(D_MODEL, D_MODEL),
            w_pool=w_pool[i].astype(BF16),
            pool_scale=pool_scale[i][None], norm_mix=norm_mix[i][None], q_norm=q_norm[i][None],
            kv_norm=kv_norm[i][None], norm_ffn=norm_ffn[i][None], norm_ple=norm_ple[i][None])

    def channel_weights(i):
        g = lambda n: gathered[n, i]
        return dict(w_gate=g("w_gate"), w_up=g("w_up"), w_down=g("w_down"),
                    w_pg=g("w_ple_gate").reshape(D_MODEL, D_MODEL),
                    w_ple=_pieces_to_cols(g("w_ple")))

    def behind(gain, start):
        return gain + start["token"][0, 0]

    layers, saved = [], []
    starts = {}
    for i in range(DEPTH):
        if i == 1:
            gathered.update({(n, 1): a for n, a in zip(
                FIRST, exchange_wait(starts["b2"], saved[0]["logits"], name="gather_wait_b2"))})
        L = mix_weights(i)
        g_mix = behind(L["norm_mix"], start_b1) if i == 0 else L["norm_mix"]
        z = rms_matmul(x[0] if i == 0 else xs, g_mix, L["w_in"], tm=TM, tn=IN_PAD, name="in_proj")
        pooled, ms, ya = pool_fwd(z, low, up, L["w_pool"], L["pool_scale"], L["w_a"], tm=TM)
        g_q = L["q_norm"]
        if i == 0:
            starts["b2"] = exchange_start([send_form(n, 1) for n in FIRST], [], ya,
                                          name="gather_start_b2")
            g_q = behind(g_q, starts["b2"])
        q, k, v = mla_prep(z, cos_t, s1_t, s2_t, g_q, L["kv_norm"], L["w_uq"], L["w_ukv"], tm=TM)
        o, lse = flash_fwd(q, k, v, tq=2 * TQ)
        yb, merged, x1 = merge_fwd(o, ya, z, x[0] if i == 0 else xs, L["w_b"], L["w_o"], tm=TM)
        g_ffn = L["norm_ffn"]
        if i == 0:
            got = exchange_wait(start_b1, o, name="gather_wait_b1")
            starts["b3"] = exchange_start([send_form(n, 1) for n in LATER], [], x1,
                                          name="gather_start_b3")
            g_ffn = behind(g_ffn, starts["b3"])
        else:
            got = exchange_wait(starts["b3"], o, name="gather_wait_b3")
        gathered.update({(n, i): a for n, a in zip(LATER, got)})
        L.update(channel_weights(i))
        gpre, upre, x2 = ffn_fwd(x1, g_ffn, L["w_gate"], L["w_up"], L["w_down"], tm=TM)
        logits, pe, x3 = ple_fwd(x2, p[i, 0], L["norm_ple"], L["w_pg"], L["w_ple"], tm=TM)
        layers.append(L)
        saved.append(dict(x=x[0] if i == 0 else xs, z=z, pooled=pooled, ms=ms, ya=ya, q=q, k=k,
                          v=v, o=o, lse=lse, yb=yb, merged=merged, x1=x1, gpre=gpre, upre=upre,
                          x2=x2, logits=logits, pe=pe))
        xs = x3

    loss_part, dx, d_final = loss_head(xs, final_norm[None], loss_target[0], tm=TM)

    grads = {n: [None] * DEPTH for n in REPLICATED if n != "final_norm"}
    pieces = {n: [None] * DEPTH for n in names}
    tn_mm = functools.partial(matmul_tn, tn=1024, tk=1024)
    row_pieces = lambda a: a.reshape(N_DEV, -1, a.shape[-1])
    MIDDLE = ("w_o", "w_b", "w_uq", "w_ukv")
    LAST = ("w_a", "w_in")
    scatters = []
    for i in reversed(range(DEPTH)):
        L, S = layers[i], saved[i]
        g_ple = L["norm_ple"] if not scatters else behind(L["norm_ple"], scatters[-1][0])
        dl, dpe, hn, dx2, d_norm_ple = ple_bwd(dx, S["logits"], S["pe"], S["x2"], g_ple,
                                               L["w_pg"], tm=TM)
        grads["norm_ple"][i] = d_norm_ple[0]
        pieces["w_ple_gate"][i] = row_pieces(tn_mm(hn, dl, name="dw_ple_gate"))
        pieces["w_ple"][i] = _cols_to_pieces(tn_mm(p[i, 0], dpe, name="dw_ple"))

        act, dgp, dup, h2, dx1, d_norm_ffn = ffn_bwd(dx2, S["x1"], L["norm_ffn"], S["gpre"],
                                                     S["upre"], L["w_gate"], L["w_up"],
                                                     L["w_down"], tm=TM)
        grads["norm_ffn"][i] = d_norm_ffn[0]
        pieces["w_down"][i] = matmul_tn_pieces(act, dx2, tk=512, name="dw_down")
        pieces["w_gate"][i] = matmul_tn_pieces(h2, dgp, tk=512, name="dw_gate")
        pieces["w_up"][i] = matmul_tn_pieces(h2, dup, tk=512, name="dw_up")
        scatters.append((exchange_start([], [pieces[n][i] for n in LATER], None,
                                        name=f"scatter_start_{i}a"), LATER, i))

        dya, dyb, do, dga, dgb = merge_bwd(dx1, S["ya"], S["yb"], S["z"], L["w_o"], L["w_b"],
                                           scatters[-1][0]["token"], tm=TM)
        pieces["w_o"][i] = row_pieces(tn_mm(S["merged"], dx1, name="dw_o"))
        pieces["w_b"][i] = row_pieces(tn_mm(S["o"], dyb, name="dw_b"))

        delta = attn_delta(do, S["o"], tq=TQ).reshape(S["lse"].shape)
        dq, dk, dv = flash_bwd(S["q"], S["k"], S["v"], do, S["lse"], delta, tq=2 * TQ, tk=TQ)
        dcq, dckv, dkr, dqf, dkvf, qn, kvn, d_q_norm, d_kv_norm = mla_bwd(
            dq, dk, dv, S["z"], cos_t, s1_t, s2_t, L["q_norm"], L["kv_norm"], L["w_uq"],
            L["w_ukv"], tm=TM)
        grads["q_norm"][i] = d_q_norm[0]
        grads["kv_norm"][i] = d_kv_norm[0]
        d_w_uq = tn_mm(qn, dqf, name="dw_uq")
        pieces["w_uq"][i] = d_w_uq.reshape(N_DEV, -1, N_HEADS, HEAD_PAD)[..., :QK_HEAD]
        pieces["w_ukv"][i] = tn_mm(kvn, dkvf, name="dw_ukv").reshape(
            N_DEV, -1, N_HEADS, QK_NOPE + V_HEAD)
        scatters.append((exchange_start([], [pieces[n][i] for n in MIDDLE], None,
                                        name=f"scatter_start_{i}b"), MIDDLE, i))

        dpc, d_pool_scale, d_w_pool = pool_bwd_a(dya, S["pooled"], L["w_a"], L["w_pool"],
                                                 behind(L["pool_scale"], scatters[-1][0]), tm=TM)
        grads["pool_scale"][i] = d_pool_scale[0]
        grads["w_pool"][i] = d_w_pool
        pieces["w_a"][i] = _cols_to_pieces(tn_mm(S["ms"], dya, name="dw_a"))
        du = pool_bwd_b(dpc, low_t, up_t, tm=TM)

        segs = [(dga, OFF_GA), (dgb, OFF_GB), (du, OFF_U), (dcq, OFF_CQ), (dckv, OFF_CKV),
                (dkr, OFF_KR)]
        dx, h, d_norm_mix = in_bwd(dx1, S["x"], L["norm_mix"], segs, L["w_in"], tm=TM)
        grads["norm_mix"][i] = d_norm_mix[0]
        pieces["w_in"][i] = split_dw_in(dw_in_proj(h, segs, tk=512), row_map)
        if i == 1:
            scatters.append((exchange_start([], [pieces[n][i] for n in LAST], None,
                                            name="scatter_start_1c"), LAST, i))

    small = {n: jnp.stack(g) for n, g in grads.items()}
    small["final_norm"] = d_final[0]
    rep = jnp.concatenate([small[n].reshape(-1) for n in REPLICATED] + [loss_part.reshape(-1)])
    n_small = -(-rep.shape[0] // (8 * FLAT_COLS)) * (8 * FLAT_COLS)

    def small_flat(parts):
        flat = jnp.concatenate([a.reshape(-1) for a in parts])
        return jnp.pad(flat, (0, n_small - flat.shape[0])).reshape(-1, FLAT_COLS)

    start_end = exchange_start([small_flat([rep])], [pieces[n][0] for n in LAST], None,
                               name="scatter_start_0c")
    parts = {}
    for start, group, i in scatters:
        tag = {LATER: "a", MIDDLE: "b", LAST: "c"}[group]
        got = exchange_wait(start, start_end["token"], name=f"scatter_wait_{i}{tag}")
        parts.update({(n, i): a for n, a in zip(group, got)})
    got_end = exchange_wait(start_end, None, name="scatter_wait_0c")
    parts.update({(n, 0): a for n, a in zip(LAST, got_end[1:])})
    for i in range(DEPTH):
        parts["w_down", i] = parts["w_down", i][:, :FF_PIECE]

    results = {}
    for n in names:
        w = shard[n]
        rows2d = lambda a: a.reshape(-1, w.shape[-1])
        w2 = rows2d(w)
        per_layer = [parts[n, i].reshape(N_DEV, w2.shape[0] // DEPTH, -1) for i in range(DEPTH)]
        out = reduce_adamw(per_layer, w2, rows2d(given["m_" + n]), rows2d(given["v_" + n]),
                           name="adamw_" + n)
        results[n] = [a.reshape(w.shape) for a in out]

    out = reduce_adamw([got_end[0]], *[small_flat([given[pre + n] for n in REPLICATED])
                                      for pre in ("", "m_", "v_")], name="adamw_small")
    loss = None
    for n in REPLICATED:
        results[n] = []
    for a in out:
        a = a.reshape(-1)
        off = 0
        for n in REPLICATED:
            size = int(np.prod(shard[n].shape))
            results[n].append(a[off:off + size].reshape(shard[n].shape))
            off += size
        if loss is None:
            loss = a[off]
    return (loss, dx[None], *[results[n][0] for n in WEIGHTS], *[results[n][1] for n in WEIGHTS],
            *[results[n][2] for n in WEIGHTS], *[results[n][3] for n in WEIGHTS])
```

```python
import functools

import numpy as np
import jax
import jax.numpy as jnp
from jax import lax
from jax.experimental import pallas as pl
from jax.experimental.pallas import tpu as pltpu

F32 = jnp.float32
BF16 = jnp.bfloat16

D_MODEL = 1024
DEPTH = 2
PLE_DIM = 256
POOL_WINDOWS = (2, 4, 8, 16)
POOL_GROUP = 128
POOL_WIDTH = 512
N_HEADS = 8
Q_LORA = 512
KV_LORA = 256
QK_NOPE = 128
QK_ROPE = 64
QK_HEAD = 192
V_HEAD = 128
HEAD_PAD = 256
D_FF = 2816
ROPE_THETA = 10000.0
EPS = 1e-6
IN_WIDTH = 3392
ATTN_SCALE = QK_HEAD ** -0.5

OFF_GA, OFF_GB, OFF_U, OFF_CQ, OFF_CKV, OFF_KR = 0, 1024, 2048, 2560, 3072, 3328
IN_PAD = 3456
W_IN_PIECE = IN_WIDTH // 8
PIECE_PAD = 512
FF_PIECE = D_FF // 8
FF_PAD = 384
FF_PAIRS = 4
FF_STEP = 2 * FF_PAD

ADAM_LR = 0.001
ADAM_B1 = 0.9
ADAM_B2 = 0.999
ADAM_EPS = 1e-08
ADAM_WD = 0.01
ADAM_STEP = 10

N_DEV = 8
LANES = 128
CHUNK = 128
VMEM_LIMIT = 56 * 1024 * 1024

SHARDED = (("w_in", 2), ("w_uq", 1), ("w_ukv", 1), ("w_a", 2), ("w_b", 1), ("w_o", 1),
           ("w_gate", 2), ("w_up", 2), ("w_down", 1), ("w_ple_gate", 1), ("w_ple", 2))
REPLICATED = ("norm_mix", "w_pool", "pool_scale", "q_norm", "kv_norm", "norm_ffn", "norm_ple",
              "final_norm")
WEIGHTS = ("norm_mix", "w_in", "w_pool", "pool_scale", "q_norm", "kv_norm", "w_uq", "w_ukv",
           "w_a", "w_b", "w_o", "norm_ffn", "w_gate", "w_up", "w_down", "norm_ple",
           "w_ple_gate", "w_ple", "final_norm")
FLAT_COLS = 1024
FLAT_ROW_BLOCK = 192


def _params(*sem):
    return pltpu.CompilerParams(dimension_semantics=sem, vmem_limit_bytes=VMEM_LIMIT)


def _dot(a, b):
    return jnp.dot(a, b, preferred_element_type=F32)


def _dot_nt(a, b):
    return lax.dot_general(a, b, (((1,), (1,)), ((), ())), preferred_element_type=F32)


def _dot_tn(a, b):
    return lax.dot_general(a, b, (((0,), (0,)), ((), ())), preferred_element_type=F32)


def _rms_fwd(x, g):
    r = lax.rsqrt(jnp.mean(x * x, axis=-1, keepdims=True) + EPS)
    return x * r * g


def _rms_bwd(x, g, dy):
    r = lax.rsqrt(jnp.mean(x * x, axis=-1, keepdims=True) + EPS)
    xr = x * r
    gy = dy * g
    dx = r * (gy - xr * jnp.mean(gy * xr, axis=-1, keepdims=True))
    return dx, dy * xr


def _sigmoid(x):
    return 1.0 / (1.0 + jnp.exp(-x))


def _accum_rows(ref, rows):
    ref[...] += jnp.sum(rows, axis=0, keepdims=True)


def _band(band, x):
    h1 = x.astype(BF16)
    r1 = x - h1.astype(F32)
    h2 = r1.astype(BF16)
    h3 = (r1 - h2.astype(F32)).astype(BF16)
    return _dot(band, h1) + _dot(band, h2) + _dot(band, h3)


def _rope(x, c, s1, s2, sign):
    return x * c + sign * (pltpu.roll(x, 96, 1) * s1 + pltpu.roll(x, 32, 1) * s2)


def _full(shape):
    n = len(shape)
    return pl.BlockSpec(shape, lambda *_: (0,) * n)


def matmul_tn(a, b, *, tn, tk, name, tm=1024):
    T, M = a.shape
    N = b.shape[1]
    tm, tn, tk = min(tm, M), min(tn, N), min(tk, T)
    nk = T // tk

    def body(a_ref, b_ref, o_ref, acc_ref):
        k = pl.program_id(2)

        @pl.when(k == 0)
        def _():
            acc_ref[...] = jnp.zeros_like(acc_ref)

        acc_ref[...] += _dot_tn(a_ref[...].astype(BF16), b_ref[...].astype(BF16))

        @pl.when(k == nk - 1)
        def _():
            o_ref[...] = acc_ref[...].astype(BF16)

    return pl.pallas_call(
        body, name=name, grid=(M // tm, N // tn, nk),
        in_specs=[pl.BlockSpec((tk, tm), lambda i, j, k: (k, i)),
                  pl.BlockSpec((tk, tn), lambda i, j, k: (k, j))],
        out_specs=pl.BlockSpec((tm, tn), lambda i, j, k: (i, j)),
        out_shape=jax.ShapeDtypeStruct((M, N), BF16),
        scratch_shapes=[pltpu.VMEM((tm, tn), F32)],
        compiler_params=_params("parallel", "parallel", "arbitrary"),
    )(a, b)


def matmul_tn_pieces(a, b, *, tk, name):
    a3 = a.ndim == 3
    T = a.shape[-2]
    m, n = a.shape[-1], b.shape[-1]
    tk = min(tk, T)
    nk = T // tk
    out = (N_DEV, FF_PAD, n) if a3 else (N_DEV, m, FF_PAD)

    def body(a_ref, b_ref, o_ref, acc_ref):
        k = pl.program_id(0)

        @pl.when(k == 0)
        def _():
            acc_ref[...] = jnp.zeros_like(acc_ref)

        whole = (b_ref if a3 else a_ref)[...].astype(BF16)
        for j in range(FF_PAIRS):
            if a3:
                acc_ref[j] += _dot_tn(a_ref[j].astype(BF16), whole)
            else:
                acc_ref[j] += _dot_tn(whole, b_ref[j].astype(BF16))

        @pl.when(k == nk - 1)
        def _():
            for j in range(FF_PAIRS):
                for half in range(2):
                    cut = slice(half * FF_PAD, (half + 1) * FF_PAD)
                    piece = acc_ref[j, cut, :] if a3 else acc_ref[j, :, cut]
                    o_ref[2 * j + half] = piece.astype(BF16)

    pairs = lambda w: pl.BlockSpec((FF_PAIRS, tk, w), lambda k: (0, k, 0))
    whole = lambda w: pl.BlockSpec((tk, w), lambda k: (k, 0))
    return pl.pallas_call(
        body, name=name, grid=(nk,),
        in_specs=[pairs(m) if a3 else whole(m), whole(n) if a3 else pairs(n)],
        out_specs=_full(out),
        out_shape=jax.ShapeDtypeStruct(out, BF16),
        scratch_shapes=[pltpu.VMEM((FF_PAIRS, m, n), F32)],
        compiler_params=_params("arbitrary"),
    )(a, b)


def dw_in_proj(h, segs, *, tk):
    T = h.shape[0]
    tk = min(tk, T)
    nk = T // tk
    widths = [s.shape[1] for s, _ in segs]
    offs = [o for _, o in segs]

    def body(*refs):
        h_ref, seg_refs, o_ref, acc_ref = refs[0], refs[1:-2], refs[-2], refs[-1]
        k = pl.program_id(0)

        @pl.when(k == 0)
        def _():
            acc_ref[...] = jnp.zeros_like(acc_ref)

        hv = h_ref[...]
        for s_ref, off, w in zip(seg_refs, offs, widths):
            acc_ref[:, off:off + w] += _dot_tn(hv, s_ref[...])

        @pl.when(k == nk - 1)
        def _():
            o_ref[...] = acc_ref[...].astype(BF16)

    return pl.pallas_call(
        body, name="dw_in", grid=(nk,),
        in_specs=[pl.BlockSpec((tk, D_MODEL), lambda k: (k, 0))]
        + [pl.BlockSpec((tk, w), lambda k: (k, 0)) for w in widths],
        out_specs=_full((D_MODEL, IN_PAD)),
        out_shape=jax.ShapeDtypeStruct((D_MODEL, IN_PAD), BF16),
        scratch_shapes=[pltpu.VMEM((D_MODEL, IN_PAD), F32)],
        compiler_params=_params("arbitrary"),
    )(h, *[s for s, _ in segs])


def _piece_row_map():
    src = np.full(IN_PAD, -1, np.int64)
    for orig, pad, width in ((0, OFF_U, 512), (512, OFF_CQ, 512), (1024, OFF_CKV, 256),
                             (1280, OFF_KR, 64), (1344, OFF_GA, 1024), (2368, OFF_GB, 1024)):
        src[pad:pad + width] = np.arange(orig, orig + width)
    rows = np.where(src >= 0, (src // W_IN_PIECE) * PIECE_PAD + src % W_IN_PIECE, -1)
    return rows.astype(np.int32)[None, :]


def _selector(j, map_ref):
    rid = j * PIECE_PAD + lax.broadcasted_iota(jnp.int32, (PIECE_PAD, IN_PAD), 0)
    return jnp.where(rid == map_ref[...], 1.0, 0.0).astype(BF16)


def assemble_w_in(pieces, row_map, *, tr):
    def body(p_ref, map_ref, o_ref, acc_ref):
        j = pl.program_id(1)

        @pl.when(j == 0)
        def _():
            acc_ref[...] = jnp.zeros_like(acc_ref)

        acc_ref[...] += _dot(p_ref[0], _selector(j, map_ref))

        @pl.when(j == N_DEV - 1)
        def _():
            o_ref[...] = acc_ref[...].astype(BF16)

    return pl.pallas_call(
        body, name="assemble_w_in", grid=(D_MODEL // tr, N_DEV),
        in_specs=[pl.BlockSpec((1, tr, PIECE_PAD), lambda i, j: (j, i, 0)), _full((1, IN_PAD))],
        out_specs=pl.BlockSpec((tr, IN_PAD), lambda i, j: (i, 0)),
        out_shape=jax.ShapeDtypeStruct((D_MODEL, IN_PAD), BF16),
        scratch_shapes=[pltpu.VMEM((tr, IN_PAD), F32)],
        compiler_params=_params("parallel", "arbitrary"),
    )(pieces, row_map)


def split_dw_in(dwp, row_map):
    def body(d_ref, map_ref, o_ref):
        o_ref[0] = _dot_nt(d_ref[...], _selector(pl.program_id(0), map_ref)).astype(BF16)

    return pl.pallas_call(
        body, name="split_dw_in", grid=(N_DEV,),
        in_specs=[_full((D_MODEL, IN_PAD)), _full((1, IN_PAD))],
        out_specs=pl.BlockSpec((1, D_MODEL, PIECE_PAD), lambda j: (j, 0, 0)),
        out_shape=jax.ShapeDtypeStruct((N_DEV, D_MODEL, PIECE_PAD), BF16),
        compiler_params=_params("parallel"),
    )(dwp, row_map)


def rms_matmul(x, g, w, *, tm, tn, name):
    T, D = x.shape
    N = w.shape[1]
    tm = min(tm, T)

    def body(x_ref, g_ref, w_ref, o_ref, h_ref):
        @pl.when(pl.program_id(1) == 0)
        def _():
            h_ref[...] = _rms_fwd(x_ref[...], g_ref[...]).astype(BF16)

        o_ref[...] = _dot(h_ref[...], w_ref[...])

    return pl.pallas_call(
        body, name=name, grid=(T // tm, N // tn),
        in_specs=[pl.BlockSpec((tm, D), lambda i, j: (i, 0)), _full((1, D)),
                  pl.BlockSpec((D, tn), lambda i, j: (0, j))],
        out_specs=pl.BlockSpec((tm, tn), lambda i, j: (i, j)),
        out_shape=jax.ShapeDtypeStruct((T, N), F32),
        scratch_shapes=[pltpu.VMEM((tm, D), BF16)],
        compiler_params=_params("parallel", "arbitrary"),
    )(x, g, w)


def _band_matrices():
    s = np.arange(CHUNK)[:, None]
    t = np.arange(CHUNK)[None, :]
    low = np.stack([((s - t >= 0) & (s - t < w)) for w in POOL_WINDOWS]).astype(np.float32)
    up = np.stack([(t > s + CHUNK - w) for w in POOL_WINDOWS]).astype(np.float32)
    return low, up


def _window_count(row0, g):
    t = row0 + lax.broadcasted_iota(jnp.int32, (CHUNK, 1), 0)
    return jnp.minimum(t + 1, POOL_WINDOWS[g]).astype(F32)


def pool_fwd(z, low, up, w_pool, pool_scale, w_a, *, tm):
    T = z.shape[0]
    tm = min(tm, T)
    nch = tm // CHUNK
    ublk = OFF_U // POOL_WIDTH

    def body(u_ref, halo_ref, low_ref, up_ref, wp_ref, sc_ref, wa_ref, pooled_ref, ms_ref, ya_ref):
        i = pl.program_id(0)
        for c in range(nch):
            rows = slice(c * CHUNK, (c + 1) * CHUNK)
            for g in range(4):
                cols = slice(g * POOL_GROUP, (g + 1) * POOL_GROUP)
                cur = u_ref[rows, cols]
                if c == 0:
                    prev = jnp.where(i > 0, halo_ref[:, cols], 0.0)
                else:
                    prev = u_ref[(c - 1) * CHUNK:c * CHUNK, cols]
                s = _band(low_ref[g], cur) + _band(up_ref[g], prev)
                pooled = (s / _window_count(i * tm + c * CHUNK, g) - cur).astype(BF16)
                pooled_ref[rows, cols] = pooled
                ms_ref[rows, cols] = (_dot(pooled, wp_ref[g]) * sc_ref[:, cols]).astype(BF16)
        ya_ref[...] = _dot(ms_ref[...], wa_ref[...])

    return pl.pallas_call(
        body, name="pool_fwd", grid=(T // tm,),
        in_specs=[pl.BlockSpec((tm, POOL_WIDTH), lambda i: (i, ublk)),
                  pl.BlockSpec((CHUNK, POOL_WIDTH), lambda i: (jnp.maximum(i * nch - 1, 0), ublk)),
                  _full((4, CHUNK, CHUNK)), _full((4, CHUNK, CHUNK)),
                  _full((4, POOL_GROUP, POOL_GROUP)), _full((1, POOL_WIDTH)),
                  _full((POOL_WIDTH, D_MODEL))],
        out_specs=[pl.BlockSpec((tm, POOL_WIDTH), lambda i: (i, 0)),
                   pl.BlockSpec((tm, POOL_WIDTH), lambda i: (i, 0)),
                   pl.BlockSpec((tm, D_MODEL), lambda i: (i, 0))],
        out_shape=[jax.ShapeDtypeStruct((T, POOL_WIDTH), BF16),
                   jax.ShapeDtypeStruct((T, POOL_WIDTH), BF16),
                   jax.ShapeDtypeStruct((T, D_MODEL), F32)],
        compiler_params=_params("parallel"),
    )(z, z, low, up, w_pool, pool_scale, w_a)


def pool_bwd_a(dya, pooled, w_a, w_pool, pool_scale, *, tm):
    T = dya.shape[0]
    tm = min(tm, T)
    nch = tm // CHUNK

    def body(dya_ref, pooled_ref, wa_ref, wp_ref, sc_ref, dpc_ref, dsc_ref, dwp_ref):
        i = pl.program_id(0)

        @pl.when(i == 0)
        def _():
            dsc_ref[...] = jnp.zeros_like(dsc_ref)
            dwp_ref[...] = jnp.zeros_like(dwp_ref)

        dms = _dot_nt(dya_ref[...], wa_ref[...])
        for g in range(4):
            cols = slice(g * POOL_GROUP, (g + 1) * POOL_GROUP)
            pg = pooled_ref[:, cols]
            dmg = dms[:, cols]
            mixed = _dot(pg, wp_ref[g])
            dsc_ref[:, cols] += jnp.sum(dmg * mixed, axis=0, keepdims=True)
            dmixed = (dmg * sc_ref[:, cols]).astype(BF16)
            dwp_ref[g] += _dot_tn(pg, dmixed)
            dpooled = _dot_nt(dmixed, wp_ref[g])
            for c in range(nch):
                rows = slice(c * CHUNK, (c + 1) * CHUNK)
                dpc_ref[rows, cols] = dpooled[rows] / _window_count(i * tm + c * CHUNK, g)

    return pl.pallas_call(
        body, name="pool_bwd_a", grid=(T // tm,),
        in_specs=[pl.BlockSpec((tm, D_MODEL), lambda i: (i, 0)),
                  pl.BlockSpec((tm, POOL_WIDTH), lambda i: (i, 0)),
                  _full((POOL_WIDTH, D_MODEL)), _full((4, POOL_GROUP, POOL_GROUP)),
                  _full((1, POOL_WIDTH))],
        out_specs=[pl.BlockSpec((tm, POOL_WIDTH), lambda i: (i, 0)), _full((1, POOL_WIDTH)),
                   _full((4, POOL_GROUP, POOL_GROUP))],
        out_shape=[jax.ShapeDtypeStruct((T, POOL_WIDTH), F32),
                   jax.ShapeDtypeStruct((1, POOL_WIDTH), F32),
                   jax.ShapeDtypeStruct((4, POOL_GROUP, POOL_GROUP), F32)],
        compiler_params=_params("arbitrary"),
    )(dya, pooled, w_a, w_pool, pool_scale)


def pool_bwd_b(dpc, low_t, up_t, *, tm):
    T = dpc.shape[0]
    tm = min(tm, T)
    nch = tm // CHUNK
    last_chunk = T // CHUNK - 1

    def body(d_ref, halo_ref, low_ref, up_ref, du_ref):
        i = pl.program_id(0)
        for c in range(nch):
            rows = slice(c * CHUNK, (c + 1) * CHUNK)
            for g in range(4):
                cols = slice(g * POOL_GROUP, (g + 1) * POOL_GROUP)
                cur = d_ref[rows, cols]
                if c == nch - 1:
                    nxt = jnp.where(i < pl.num_programs(0) - 1, halo_ref[:, cols], 0.0)
                else:
                    nxt = d_ref[(c + 1) * CHUNK:(c + 2) * CHUNK, cols]
                s = _band(low_ref[g], cur) + _band(up_ref[g], nxt)
                du_ref[rows, cols] = (s - cur * _window_count(i * tm + c * CHUNK, g)).astype(BF16)

    return pl.pallas_call(
        body, name="pool_bwd_b", grid=(T // tm,),
        in_specs=[pl.BlockSpec((tm, POOL_WIDTH), lambda i: (i, 0)),
                  pl.BlockSpec((CHUNK, POOL_WIDTH),
                               lambda i: (jnp.minimum((i + 1) * nch, last_chunk), 0)),
                  _full((4, CHUNK, CHUNK)), _full((4, CHUNK, CHUNK))],
        out_specs=pl.BlockSpec((tm, POOL_WIDTH), lambda i: (i, 0)),
        out_shape=jax.ShapeDtypeStruct((T, POOL_WIDTH), BF16),
        compiler_params=_params("parallel"),
    )(dpc, dpc, low_t, up_t)


def mla_prep(z, cos_t, s1_t, s2_t, q_norm, kv_norm, w_uq, w_ukv, *, tm):
    T = z.shape[0]
    tm = min(tm, T)

    def body(cq_ref, ckv_ref, kr_ref, c_ref, s1_ref, s2_ref, qg_ref, kvg_ref, wuq_ref, wukv_ref,
             q_ref, k_ref, v_ref):
        c, s1, s2 = c_ref[...], s1_ref[...], s2_ref[...]
        qn = _rms_fwd(cq_ref[...], qg_ref[...]).astype(BF16)
        q = _dot(qn, wuq_ref[...])
        kvn = _rms_fwd(ckv_ref[...], kvg_ref[...]).astype(BF16)
        kv = _dot(kvn, wukv_ref[...])
        kpe = _rope(kr_ref[...], c, s1, s2, 1.0).astype(BF16)
        for h in range(N_HEADS):
            o = h * HEAD_PAD
            q_ref[h, :, 0:QK_NOPE] = (q[:, o:o + QK_NOPE] * ATTN_SCALE).astype(BF16)
            q_ref[h, :, QK_NOPE:HEAD_PAD] = (
                _rope(q[:, o + QK_NOPE:o + HEAD_PAD], c, s1, s2, 1.0) * ATTN_SCALE).astype(BF16)
            k_ref[h, :, 0:QK_NOPE] = kv[:, o:o + QK_NOPE].astype(BF16)
            k_ref[h, :, QK_NOPE:HEAD_PAD] = kpe
            v_ref[h] = kv[:, o + QK_NOPE:o + HEAD_PAD].astype(BF16)

    tok = lambda w: pl.BlockSpec((tm, w), lambda i: (i, 0))
    return pl.pallas_call(
        body, name="mla_prep", grid=(T // tm,),
        in_specs=[pl.BlockSpec((tm, Q_LORA), lambda i: (i, OFF_CQ // Q_LORA)),
                  pl.BlockSpec((tm, KV_LORA), lambda i: (i, OFF_CKV // KV_LORA)),
                  pl.BlockSpec((tm, LANES), lambda i: (i, OFF_KR // LANES)),
                  tok(LANES), tok(LANES), tok(LANES),
                  _full((1, Q_LORA)), _full((1, KV_LORA)),
                  _full((Q_LORA, N_HEADS * HEAD_PAD)), _full((KV_LORA, N_HEADS * HEAD_PAD))],
        out_specs=[pl.BlockSpec((N_HEADS, tm, HEAD_PAD), lambda i: (0, i, 0)),
                   pl.BlockSpec((N_HEADS, tm, HEAD_PAD), lambda i: (0, i, 0)),
                   pl.BlockSpec((N_HEADS, tm, V_HEAD), lambda i: (0, i, 0))],
        out_shape=[jax.ShapeDtypeStruct((N_HEADS, T, HEAD_PAD), BF16),
                   jax.ShapeDtypeStruct((N_HEADS, T, HEAD_PAD), BF16),
                   jax.ShapeDtypeStruct((N_HEADS, T, V_HEAD), BF16)],
        compiler_params=_params("parallel"),
    )(z, z, z, cos_t, s1_t, s2_t, q_norm, kv_norm, w_uq, w_ukv)


def _row_vector(col, n):
    return jnp.transpose(jnp.broadcast_to(col, (n, LANES)))[0:1, :]


def flash_fwd(q, k, v, *, tq):
    H, T, _ = q.shape
    tq = min(tq, T)
    nq = T // tq
    neg = -1e30

    def body(q_ref, k_ref, v_ref, o_ref, lse_ref):
        qi = pl.program_id(1)
        qb = q_ref[0]

        def tile(j, carry, masked):
            m, l, acc = carry
            rows = pl.ds(pl.multiple_of(j * tq, tq), tq)
            s = _dot_nt(qb, k_ref[0, rows, :])
            if masked:
                r = lax.broadcasted_iota(jnp.int32, (tq, tq), 0)
                c = lax.broadcasted_iota(jnp.int32, (tq, tq), 1)
                s = jnp.where(c <= r, s, neg)
            m_new = jnp.maximum(m, jnp.max(s, axis=-1, keepdims=True))
            alpha = jnp.exp(m - m_new)
            p = jnp.exp(s - m_new)
            l = alpha * l + jnp.sum(p, axis=-1, keepdims=True)
            acc = alpha * acc + _dot(p.astype(BF16), v_ref[0, rows, :])
            return m_new, l, acc

        init = (jnp.full((tq, 1), neg, F32), jnp.zeros((tq, 1), F32), jnp.zeros((tq, V_HEAD), F32))
        carry = lax.fori_loop(0, qi, lambda j, cr: tile(j, cr, False), init)
        m, l, acc = tile(qi, carry, True)
        o_ref[...] = (acc / l).astype(BF16)
        lse_ref[0, 0] = _row_vector(m + jnp.log(l), tq)

    return pl.pallas_call(
        body, name="flash_fwd", grid=(H, nq),
        in_specs=[pl.BlockSpec((1, tq, HEAD_PAD), lambda h, i: (h, i, 0)),
                  pl.BlockSpec((1, T, HEAD_PAD), lambda h, i: (h, 0, 0)),
                  pl.BlockSpec((1, T, V_HEAD), lambda h, i: (h, 0, 0))],
        out_specs=[pl.BlockSpec((tq, V_HEAD), lambda h, i: (i, h)),
                   pl.BlockSpec((1, 1, 1, tq), lambda h, i: (h, i, 0, 0))],
        out_shape=[jax.ShapeDtypeStruct((T, H * V_HEAD), BF16),
                   jax.ShapeDtypeStruct((H, nq, 1, tq), F32)],
        compiler_params=_params("parallel", "arbitrary"),
    )(q, k, v)


def attn_delta(do, o, *, tq):
    T = do.shape[0]
    tq = min(tq, T)
    nq = T // tq
    group = max(g for g in (1, 2, 4) if nq % g == 0)

    def body(do_ref, o_ref, d_ref):
        for b in range(group):
            rows = slice(b * tq, (b + 1) * tq)
            prod = do_ref[rows, :].astype(F32) * o_ref[rows, :].astype(F32)
            d_ref[0, b] = jnp.sum(jnp.transpose(prod), axis=0, keepdims=True)

    return pl.pallas_call(
        body, name="attn_delta", grid=(N_HEADS, nq // group),
        in_specs=[pl.BlockSpec((group * tq, V_HEAD), lambda h, i: (i, h)),
                  pl.BlockSpec((group * tq, V_HEAD), lambda h, i: (i, h))],
        out_specs=pl.BlockSpec((1, group, 1, tq), lambda h, i: (h, i, 0, 0)),
        out_shape=jax.ShapeDtypeStruct((N_HEADS, nq, 1, tq), F32),
        compiler_params=_params("parallel", "parallel"),
    )(do, o)


def flash_bwd(q, k, v, do, lse, delta, *, tq, tk):
    H, T, _ = q.shape
    tq, tk = min(tq, T), min(tk, T)
    nq, nk = T // tq, T // tk
    neg = -1e30

    def body(q_ref, do_ref, lse_ref, dl_ref, k_ref, v_ref, dq_ref, dk_ref, dv_ref, dq_acc):
        j = pl.program_id(1)

        @pl.when(j == 0)
        def _():
            dq_acc[...] = jnp.zeros_like(dq_acc)

        kb = k_ref[0]
        vb = v_ref[0]

        def tile(i, carry, masked):
            dk, dv = carry
            rows = pl.ds(pl.multiple_of(i * tq, tq), tq)
            qb = q_ref[0, rows, :]
            dob = do_ref[rows, :]
            st = _dot_nt(kb, qb)
            if masked:
                key = j * tk + lax.broadcasted_iota(jnp.int32, (tk, tq), 0)
                query = i * tq + lax.broadcasted_iota(jnp.int32, (tk, tq), 1)
                st = jnp.where(key <= query, st, neg)
            pt = jnp.exp(st - lse_ref[0, i])
            dv = dv + _dot(pt.astype(BF16), dob)
            dpt = _dot_nt(vb, dob)
            dst = (pt * (dpt - dl_ref[0, i])).astype(BF16)
            dk = dk + _dot(dst, qb)
            dq_acc[rows, :] += _dot_tn(dst, kb)
            return dk, dv

        first = (j * tk) // tq
        carry = tile(first, (jnp.zeros((tk, HEAD_PAD), F32), jnp.zeros((tk, V_HEAD), F32)), True)
        dk, dv = lax.fori_loop(first + 1, nq, lambda i, cr: tile(i, cr, False), carry)
        dk_ref[0] = dk.astype(BF16)
        dv_ref[0] = dv.astype(BF16)

        @pl.when(j == nk - 1)
        def _():
            dq_ref[0] = dq_acc[...].astype(BF16)

    return pl.pallas_call(
        body, name="flash_bwd", grid=(H, nk),
        in_specs=[pl.BlockSpec((1, T, HEAD_PAD), lambda h, j: (h, 0, 0)),
                  pl.BlockSpec((T, V_HEAD), lambda h, j: (0, h)),
                  pl.BlockSpec((1, nq, 1, tq), lambda h, j: (h, 0, 0, 0)),
                  pl.BlockSpec((1, nq, 1, tq), lambda h, j: (h, 0, 0, 0)),
                  pl.BlockSpec((1, tk, HEAD_PAD), lambda h, j: (h, j, 0)),
                  pl.BlockSpec((1, tk, V_HEAD), lambda h, j: (h, j, 0))],
        out_specs=[pl.BlockSpec((1, T, HEAD_PAD), lambda h, j: (h, 0, 0)),
                   pl.BlockSpec((1, tk, HEAD_PAD), lambda h, j: (h, j, 0)),
                   pl.BlockSpec((1, tk, V_HEAD), lambda h, j: (h, j, 0))],
        out_shape=[jax.ShapeDtypeStruct((H, T, HEAD_PAD), BF16),
                   jax.ShapeDtypeStruct((H, T, HEAD_PAD), BF16),
                   jax.ShapeDtypeStruct((H, T, V_HEAD), BF16)],
        scratch_shapes=[pltpu.VMEM((T, HEAD_PAD), F32)],
        compiler_params=_params("parallel", "arbitrary"),
    )(q, do, lse, delta, k, v)


def mla_bwd(dq, dk, dv, z, cos_t, s1_t, s2_t, q_norm, kv_norm, w_uq, w_ukv, *, tm):
    T = z.shape[0]
    tm = min(tm, T)
    HW = N_HEADS * HEAD_PAD

    def body(dq_ref, dk_ref, dv_ref, cq_ref, ckv_ref, c_ref, s1_ref, s2_ref, qg_ref, kvg_ref,
             wuq_ref, wukv_ref, dcq_ref, dckv_ref, dkr_ref, dqf_ref, dkvf_ref, qn_ref, kvn_ref,
             dqg_ref, dkvg_ref):
        @pl.when(pl.program_id(0) == 0)
        def _():
            dqg_ref[...] = jnp.zeros_like(dqg_ref)
            dkvg_ref[...] = jnp.zeros_like(dkvg_ref)

        c, s1, s2 = c_ref[...], s1_ref[...], s2_ref[...]
        dkpe = jnp.zeros((tm, LANES), F32)
        for h in range(N_HEADS):
            o = h * HEAD_PAD
            dqf_ref[:, o:o + QK_NOPE] = (
                dq_ref[h, :, 0:QK_NOPE].astype(F32) * ATTN_SCALE).astype(BF16)
            dqf_ref[:, o + QK_NOPE:o + HEAD_PAD] = (
                _rope(dq_ref[h, :, QK_NOPE:HEAD_PAD].astype(F32), c, s1, s2, -1.0)
                * ATTN_SCALE).astype(BF16)
            dkvf_ref[:, o:o + QK_NOPE] = dk_ref[h, :, 0:QK_NOPE]
            dkvf_ref[:, o + QK_NOPE:o + HEAD_PAD] = dv_ref[h]
            dkpe = dkpe + dk_ref[h, :, QK_NOPE:HEAD_PAD].astype(F32)
        dkr_ref[...] = _rope(dkpe, c, s1, s2, -1.0).astype(BF16)

        cq = cq_ref[...]
        qn_ref[...] = _rms_fwd(cq, qg_ref[...]).astype(BF16)
        dcq, dgrows = _rms_bwd(cq, qg_ref[...], _dot_nt(dqf_ref[...], wuq_ref[...]))
        dcq_ref[...] = dcq.astype(BF16)
        _accum_rows(dqg_ref, dgrows)

        ckv = ckv_ref[...]
        kvn_ref[...] = _rms_fwd(ckv, kvg_ref[...]).astype(BF16)
        dckv, dgrows = _rms_bwd(ckv, kvg_ref[...], _dot_nt(dkvf_ref[...], wukv_ref[...]))
        dckv_ref[...] = dckv.astype(BF16)
        _accum_rows(dkvg_ref, dgrows)

    tok = lambda w: pl.BlockSpec((tm, w), lambda i: (i, 0))
    head = lambda w: pl.BlockSpec((N_HEADS, tm, w), lambda i: (0, i, 0))
    return pl.pallas_call(
        body, name="mla_bwd", grid=(T // tm,),
        in_specs=[head(HEAD_PAD), head(HEAD_PAD), head(V_HEAD),
                  pl.BlockSpec((tm, Q_LORA), lambda i: (i, OFF_CQ // Q_LORA)),
                  pl.BlockSpec((tm, KV_LORA), lambda i: (i, OFF_CKV // KV_LORA)),
                  tok(LANES), tok(LANES), tok(LANES),
                  _full((1, Q_LORA)), _full((1, KV_LORA)),
                  _full((Q_LORA, HW)), _full((KV_LORA, HW))],
        out_specs=[tok(Q_LORA), tok(KV_LORA), tok(LANES), tok(HW), tok(HW), tok(Q_LORA),
                   tok(KV_LORA), _full((1, Q_LORA)), _full((1, KV_LORA))],
        out_shape=[jax.ShapeDtypeStruct((T, Q_LORA), BF16),
                   jax.ShapeDtypeStruct((T, KV_LORA), BF16),
                   jax.ShapeDtypeStruct((T, LANES), BF16),
                   jax.ShapeDtypeStruct((T, HW), BF16),
                   jax.ShapeDtypeStruct((T, HW), BF16),
                   jax.ShapeDtypeStruct((T, Q_LORA), BF16),
                   jax.ShapeDtypeStruct((T, KV_LORA), BF16),
                   jax.ShapeDtypeStruct((1, Q_LORA), F32),
                   jax.ShapeDtypeStruct((1, KV_LORA), F32)],
        compiler_params=_params("arbitrary"),
    )(dq, dk, dv, z, z, cos_t, s1_t, s2_t, q_norm, kv_norm, w_uq, w_ukv)


def merge_fwd(o, ya, z, x, w_b, w_o, *, tm):
    T = x.shape[0]
    tm = min(tm, T)

    def body(o_ref, ya_ref, ga_ref, gb_ref, x_ref, wb_ref, wo_ref, yb_ref, mg_ref, x1_ref):
        yb = _dot(o_ref[...], wb_ref[...])
        yb_ref[...] = yb
        merged = (_sigmoid(ga_ref[...]) * ya_ref[...] + _sigmoid(gb_ref[...]) * yb).astype(BF16)
        mg_ref[...] = merged
        x1_ref[...] = x_ref[...] + _dot(merged, wo_ref[...])

    tok = pl.BlockSpec((tm, D_MODEL), lambda i: (i, 0))
    return pl.pallas_call(
        body, name="merge_fwd", grid=(T // tm,),
        in_specs=[tok, tok, pl.BlockSpec((tm, D_MODEL), lambda i: (i, OFF_GA // D_MODEL)),
                  pl.BlockSpec((tm, D_MODEL), lambda i: (i, OFF_GB // D_MODEL)), tok,
                  _full((D_MODEL, D_MODEL)), _full((D_MODEL, D_MODEL))],
        out_specs=[tok, tok, tok],
        out_shape=[jax.ShapeDtypeStruct((T, D_MODEL), F32),
                   jax.ShapeDtypeStruct((T, D_MODEL), BF16),
                   jax.ShapeDtypeStruct((T, D_MODEL), F32)],
        compiler_params=_params("parallel"),
    )(o, ya, z, z, x, w_b, w_o)


def merge_bwd(dx1, ya, yb, z, w_o, w_b, dep, *, tm):
    T = dx1.shape[0]
    tm = min(tm, T)

    def body(dx_ref, ya_ref, yb_ref, ga_ref, gb_ref, wo_ref, wb_ref, dep_ref, dya_ref, dyb_ref,
             do_ref, dga_ref, dgb_ref):
        dm = _dot_nt(dx_ref[...].astype(BF16), wo_ref[...])
        sa = _sigmoid(ga_ref[...])
        sb = _sigmoid(gb_ref[...])
        dya_ref[...] = (dm * sa).astype(BF16)
        dyb = (dm * sb).astype(BF16)
        dyb_ref[...] = dyb
        dga_ref[...] = (dm * ya_ref[...] * sa * (1.0 - sa)).astype(BF16)
        dgb_ref[...] = (dm * yb_ref[...] * sb * (1.0 - sb)).astype(BF16)
        do_ref[...] = _dot_nt(dyb, wb_ref[...]).astype(BF16)

    tok = pl.BlockSpec((tm, D_MODEL), lambda i: (i, 0))
    return pl.pallas_call(
        body, name="merge_bwd", grid=(T // tm,),
        in_specs=[tok, tok, tok, pl.BlockSpec((tm, D_MODEL), lambda i: (i, OFF_GA // D_MODEL)),
                  pl.BlockSpec((tm, D_MODEL), lambda i: (i, OFF_GB // D_MODEL)),
                  _full((D_MODEL, D_MODEL)), _full((D_MODEL, D_MODEL)),
                  pl.BlockSpec(memory_space=pl.ANY)],
        out_specs=[tok] * 5,
        out_shape=[jax.ShapeDtypeStruct((T, D_MODEL), BF16)] * 5,
        compiler_params=_params("parallel"),
    )(dx1, ya, yb, z, z, w_o, w_b, dep)


def in_bwd(dx1, x, g, segs, w_in, *, tm):
    T = x.shape[0]
    tm = min(tm, T)
    widths = [s.shape[1] for s, _ in segs]
    offs = [o for _, o in segs]
    n = len(segs)

    def body(*refs):
        dx1_ref, x_ref, g_ref = refs[:3]
        seg_refs = refs[3:3 + n]
        w_ref = refs[3 + n]
        dx_ref, h_ref, dg_ref = refs[4 + n:]

        @pl.when(pl.program_id(0) == 0)
        def _():
            dg_ref[...] = jnp.zeros_like(dg_ref)

        dh = jnp.zeros((tm, D_MODEL), F32)
        for s_ref, off, w in zip(seg_refs, offs, widths):
            dh = dh + _dot_nt(s_ref[...], w_ref[:, off:off + w])
        xv = x_ref[...]
        h_ref[...] = _rms_fwd(xv, g_ref[...]).astype(BF16)
        dxn, dgrows = _rms_bwd(xv, g_ref[...], dh)
        dx_ref[...] = dx1_ref[...] + dxn
        _accum_rows(dg_ref, dgrows)

    tok = pl.BlockSpec((tm, D_MODEL), lambda i: (i, 0))
    return pl.pallas_call(
        body, name="in_bwd", grid=(T // tm,),
        in_specs=[tok, tok, _full((1, D_MODEL))]
        + [pl.BlockSpec((tm, w), lambda i: (i, 0)) for w in widths] + [_full((D_MODEL, IN_PAD))],
        out_specs=[tok, tok, _full((1, D_MODEL))],
        out_shape=[jax.ShapeDtypeStruct((T, D_MODEL), F32),
                   jax.ShapeDtypeStruct((T, D_MODEL), BF16),
                   jax.ShapeDtypeStruct((1, D_MODEL), F32)],
        compiler_params=_params("arbitrary"),
    )(dx1, x, g, *[s for s, _ in segs], w_in)


def _pair_cols(w_ref):
    return jnp.concatenate([w_ref[0], w_ref[1]], axis=1)


def _pair_rows(w_ref):
    return jnp.concatenate([w_ref[0], w_ref[1]], axis=0)


def ffn_fwd(x1, g, w_gate, w_up, w_down, *, tm):
    T = x1.shape[0]
    tm = min(tm, T)

    def body(x_ref, g_ref, wg_ref, wu_ref, wd_ref, gp_ref, up_ref, x2_ref, h_ref, acc_ref):
        f = pl.program_id(1)

        @pl.when(f == 0)
        def _():
            h_ref[...] = _rms_fwd(x_ref[...], g_ref[...]).astype(BF16)
            acc_ref[...] = jnp.zeros_like(acc_ref)

        gp = _dot(h_ref[...], _pair_cols(wg_ref))
        up = _dot(h_ref[...], _pair_cols(wu_ref))
        gp_ref[0] = gp
        up_ref[0] = up
        act = (gp * _sigmoid(gp) * up).astype(BF16)
        acc_ref[...] += _dot(act, _pair_rows(wd_ref))

        @pl.when(f == FF_PAIRS - 1)
        def _():
            x2_ref[...] = x_ref[...] + acc_ref[...]

    tok = pl.BlockSpec((tm, D_MODEL), lambda i, f: (i, 0))
    ff = pl.BlockSpec((1, tm, FF_STEP), lambda i, f: (f, i, 0))
    w_col = pl.BlockSpec((2, D_MODEL, FF_PAD), lambda i, f: (f, 0, 0))
    w_row = pl.BlockSpec((2, FF_PAD, D_MODEL), lambda i, f: (f, 0, 0))
    return pl.pallas_call(
        body, name="ffn_fwd", grid=(T // tm, FF_PAIRS),
        in_specs=[tok, _full((1, D_MODEL)), w_col, w_col, w_row],
        out_specs=[ff, ff, tok],
        out_shape=[jax.ShapeDtypeStruct((FF_PAIRS, T, FF_STEP), F32),
                   jax.ShapeDtypeStruct((FF_PAIRS, T, FF_STEP), F32),
                   jax.ShapeDtypeStruct((T, D_MODEL), F32)],
        scratch_shapes=[pltpu.VMEM((tm, D_MODEL), BF16), pltpu.VMEM((tm, D_MODEL), F32)],
        compiler_params=_params("parallel", "arbitrary"),
    )(x1, g, w_gate, w_up, w_down)


def ffn_bwd(dx2, x1, g, gpre, upre, w_gate, w_up, w_down, *, tm):
    T = x1.shape[0]
    tm = min(tm, T)

    def body(dx_ref, x_ref, g_ref, gp_ref, up_ref, wg_ref, wu_ref, wd_ref,
             act_ref, dg_ref, du_ref, h_ref, dx1_ref, dgain_ref, dxb_ref, acc_ref):
        f = pl.program_id(1)

        @pl.when((pl.program_id(0) == 0) & (f == 0))
        def _():
            dgain_ref[...] = jnp.zeros_like(dgain_ref)

        @pl.when(f == 0)
        def _():
            dxb_ref[...] = dx_ref[...].astype(BF16)
            acc_ref[...] = jnp.zeros_like(acc_ref)

        gp = gp_ref[0]
        up = up_ref[0]
        dact = _dot_nt(dxb_ref[...], _pair_rows(wd_ref))
        sg = _sigmoid(gp)
        silu = gp * sg
        act_ref[0] = (silu * up).astype(BF16)
        dgp = (dact * up * (sg * (1.0 + gp * (1.0 - sg)))).astype(BF16)
        dup = (dact * silu).astype(BF16)
        dg_ref[0] = dgp
        du_ref[0] = dup
        acc_ref[...] += _dot_nt(dgp, _pair_cols(wg_ref)) + _dot_nt(dup, _pair_cols(wu_ref))

        @pl.when(f == FF_PAIRS - 1)
        def _():
            xv = x_ref[...]
            h_ref[...] = _rms_fwd(xv, g_ref[...]).astype(BF16)
            dxn, dgrows = _rms_bwd(xv, g_ref[...], acc_ref[...])
            dx1_ref[...] = dx_ref[...] + dxn
            _accum_rows(dgain_ref, dgrows)

    tok = pl.BlockSpec((tm, D_MODEL), lambda i, f: (i, 0))
    ff = pl.BlockSpec((1, tm, FF_STEP), lambda i, f: (f, i, 0))
    w_col = pl.BlockSpec((2, D_MODEL, FF_PAD), lambda i, f: (f, 0, 0))
    w_row = pl.BlockSpec((2, FF_PAD, D_MODEL), lambda i, f: (f, 0, 0))
    return pl.pallas_call(
        body, name="ffn_bwd", grid=(T // tm, FF_PAIRS),
        in_specs=[tok, tok, _full((1, D_MODEL)), ff, ff, w_col, w_col, w_row],
        out_specs=[ff, ff, ff, tok, tok, _full((1, D_MODEL))],
        out_shape=[jax.ShapeDtypeStruct((FF_PAIRS, T, FF_STEP), BF16)] * 3
        + [jax.ShapeDtypeStruct((T, D_MODEL), BF16), jax.ShapeDtypeStruct((T, D_MODEL), F32),
           jax.ShapeDtypeStruct((1, D_MODEL), F32)],
        scratch_shapes=[pltpu.VMEM((tm, D_MODEL), BF16), pltpu.VMEM((tm, D_MODEL), F32)],
        compiler_params=_params("arbitrary", "arbitrary"),
    )(dx2, x1, g, gpre, upre, w_gate, w_up, w_down)


def ple_fwd(x2, p, g, w_pg, w_ple, *, tm):
    T = x2.shape[0]
    tm = min(tm, T)

    def body(x_ref, p_ref, g_ref, wpg_ref, wple_ref, l_ref, pe_ref, x3_ref):
        xv = x_ref[...]
        logits = _dot(_rms_fwd(xv, g_ref[...]).astype(BF16), wpg_ref[...])
        pe = _dot(p_ref[...].astype(BF16), wple_ref[...])
        l_ref[...] = logits
        pe_ref[...] = pe
        x3_ref[...] = xv + _sigmoid(logits) * pe

    tok = pl.BlockSpec((tm, D_MODEL), lambda i: (i, 0))
    return pl.pallas_call(
        body, name="ple_fwd", grid=(T // tm,),
        in_specs=[tok, pl.BlockSpec((tm, PLE_DIM), lambda i: (i, 0)), _full((1, D_MODEL)),
                  _full((D_MODEL, D_MODEL)), _full((PLE_DIM, D_MODEL))],
        out_specs=[tok, tok, tok],
        out_shape=[jax.ShapeDtypeStruct((T, D_MODEL), F32)] * 3,
        compiler_params=_params("parallel"),
    )(x2, p, g, w_pg, w_ple)


def ple_bwd(dx3, logits, pe, x2, g, w_pg, *, tm):
    T = x2.shape[0]
    tm = min(tm, T)

    def body(dx_ref, l_ref, pe_ref, x_ref, g_ref, wpg_ref, dl_ref, dpe_ref, h_ref, dx2_ref,
             dg_ref):
        @pl.when(pl.program_id(0) == 0)
        def _():
            dg_ref[...] = jnp.zeros_like(dg_ref)

        dx = dx_ref[...]
        s = _sigmoid(l_ref[...])
        dpe_ref[...] = (dx * s).astype(BF16)
        dl = (dx * pe_ref[...] * s * (1.0 - s)).astype(BF16)
        dl_ref[...] = dl
        xv = x_ref[...]
        h_ref[...] = _rms_fwd(xv, g_ref[...]).astype(BF16)
        dxn, dgrows = _rms_bwd(xv, g_ref[...], _dot_nt(dl, wpg_ref[...]))
        dx2_ref[...] = dx + dxn
        _accum_rows(dg_ref, dgrows)

    tok = pl.BlockSpec((tm, D_MODEL), lambda i: (i, 0))
    return pl.pallas_call(
        body, name="ple_bwd", grid=(T // tm,),
        in_specs=[tok, tok, tok, tok, _full((1, D_MODEL)), _full((D_MODEL, D_MODEL))],
        out_specs=[tok, tok, tok, tok, _full((1, D_MODEL))],
        out_shape=[jax.ShapeDtypeStruct((T, D_MODEL), BF16)] * 3
        + [jax.ShapeDtypeStruct((T, D_MODEL), F32), jax.ShapeDtypeStruct((1, D_MODEL), F32)],
        compiler_params=_params("arbitrary"),
    )(dx3, logits, pe, x2, g, w_pg)


def loss_head(x, g, target, *, tm):
    T = x.shape[0]
    tm = min(tm, T)

    def body(x_ref, g_ref, t_ref, loss_ref, dx_ref, dg_ref):
        @pl.when(pl.program_id(0) == 0)
        def _():
            loss_ref[...] = jnp.zeros_like(loss_ref)
            dg_ref[...] = jnp.zeros_like(dg_ref)

        xv = x_ref[...]
        err = _rms_fwd(xv, g_ref[...]) - t_ref[...]
        part = 0.5 * jnp.sum(jnp.mean(err * err, axis=-1, keepdims=True), axis=0, keepdims=True)
        lane = lax.broadcasted_iota(jnp.int32, (1, LANES), 1)
        loss_ref[...] += jnp.where(lane == 0, part, 0.0)
        dxn, dgrows = _rms_bwd(xv, g_ref[...], err * (1.0 / D_MODEL))
        dx_ref[...] = dxn
        _accum_rows(dg_ref, dgrows)

    tok = pl.BlockSpec((tm, D_MODEL), lambda i: (i, 0))
    return pl.pallas_call(
        body, name="loss_head", grid=(T // tm,),
        in_specs=[tok, _full((1, D_MODEL)), tok],
        out_specs=[_full((1, LANES)), tok, _full((1, D_MODEL))],
        out_shape=[jax.ShapeDtypeStruct((1, LANES), F32), jax.ShapeDtypeStruct((T, D_MODEL), F32),
                   jax.ShapeDtypeStruct((1, D_MODEL), F32)],
        compiler_params=_params("arbitrary"),
    )(x, g, target)


def _peers():
    x, y, c = lax.axis_index("x"), lax.axis_index("y"), lax.axis_index("c")
    me = 4 * x + 2 * y + c
    out = []
    for d in range(1, N_DEV):
        px = 1 - x if d & 4 else x
        py = 1 - y if d & 2 else y
        pc = 1 - c if d & 1 else c
        out.append(((px, py, pc), 4 * px + 2 * py + pc))
    return me, out


_HBM = pl.BlockSpec(memory_space=pltpu.HBM)
_SEM = pl.BlockSpec(memory_space=pltpu.SEMAPHORE)
_ANY = pl.BlockSpec(memory_space=pl.ANY)
_DATAFLOW = pltpu.SideEffectType.DATAFLOW_SIDE_EFFECTING


def _exchange_copy(t, d, pos, idx, me, src_ref, scatter, land_ref, send_sems, recv_sems):
    k = t * (N_DEV - 1) + d
    return pltpu.make_async_remote_copy(
        src_ref=src_ref.at[idx] if scatter else src_ref, dst_ref=land_ref.at[me],
        send_sem=send_sems.at[k], recv_sem=recv_sems.at[k],
        device_id=pos, device_id_type=pl.DeviceIdType.MESH)


def _own_slot_copy(t, me, src_ref, scatter, land_ref, local_sems):
    return pltpu.make_async_copy(src_ref.at[me] if scatter else src_ref, land_ref.at[me],
                                 local_sems.at[t])


def exchange_start(gathers, scatters, dep, *, name):
    srcs = [pltpu.with_memory_space_constraint(a, pltpu.HBM) for a in list(gathers) + list(scatters)]
    kinds = [False] * len(gathers) + [True] * len(scatters)
    lands = [pltpu.with_memory_space_constraint(
        lax.empty((N_DEV,) + (a.shape[1:] if sc else a.shape), a.dtype), pltpu.HBM)
        for a, sc in zip(srcs, kinds)]
    n = len(srcs)

    def body(*refs):
        src_refs, land_refs = refs[:n], refs[n:2 * n]
        send_sems, recv_sems, local_sems = refs[-2 * n - 4:-2 * n - 1]
        token = refs[-1]
        me, peers = _peers()
        for t in range(n):
            _own_slot_copy(t, me, src_refs[t], kinds[t], land_refs[t], local_sems).start()
        for t in range(n):
            for d, (pos, idx) in enumerate(peers):
                _exchange_copy(t, d, pos, idx, me, src_refs[t], kinds[t], land_refs[t],
                               send_sems, recv_sems).start()
        token[...] = jnp.zeros_like(token)

    sem = pltpu.SemaphoreType.DMA((n * (N_DEV - 1),))
    thru = [pltpu.HBM(a.shape, a.dtype) for a in srcs + lands]
    outs = pl.pallas_call(
        body, name=name,
        in_specs=[_HBM] * (2 * n) + ([_ANY] if dep is not None else []),
        out_specs=[_SEM] * 3 + [_HBM] * (2 * n) + [pl.BlockSpec(memory_space=pltpu.VMEM)],
        out_shape=[sem, sem, pltpu.SemaphoreType.DMA((n,))] + thru
        + [jax.ShapeDtypeStruct((8, LANES), F32)],
        input_output_aliases={i: 3 + i for i in range(2 * n)},
        compiler_params=pltpu.CompilerParams(has_side_effects=_DATAFLOW),
    )(*srcs, *lands, *([dep] if dep is not None else []))
    return dict(sems=outs[:3], srcs=outs[3:3 + n], lands=outs[3 + n:3 + 2 * n], token=outs[-1],
                kinds=kinds)


def exchange_wait(handle, after, *, name):
    kinds = handle["kinds"]
    n = len(kinds)

    def body(*refs):
        src_refs, land_refs = refs[:n], refs[n:2 * n]
        send_sems, recv_sems, local_sems = refs[2 * n:2 * n + 3]
        me, peers = _peers()
        for t in range(n):
            _own_slot_copy(t, me, src_refs[t], kinds[t], land_refs[t], local_sems).wait()
            for d, (pos, idx) in enumerate(peers):
                cp = _exchange_copy(t, d, pos, idx, me, src_refs[t], kinds[t], land_refs[t],
                                    send_sems, recv_sems)
                cp.wait_send()
                cp.wait_recv()

    arrays = list(handle["srcs"]) + list(handle["lands"])
    outs = pl.pallas_call(
        body, name=name,
        in_specs=[_HBM] * (2 * n) + [_SEM] * 3 + ([_ANY] if after is not None else []),
        out_specs=[_HBM] * (2 * n),
        out_shape=[pltpu.HBM(a.shape, a.dtype) for a in arrays],
        input_output_aliases={i: i for i in range(2 * n)},
        compiler_params=pltpu.CompilerParams(has_side_effects=_DATAFLOW),
    )(*arrays, *handle["sems"], *([after] if after is not None else []))
    return outs[n:]


def _row_block(rows, cols):
    best = None
    for rb in range(16, rows + 1, 16):
        if rows % rb == 0 and rb * cols <= 256 * 1024:
            best = rb
    return best or rows


def reduce_adamw(parts, w, m, v, *, name):
    nl = len(parts)
    C = w.shape[1]
    R, cp = parts[0].shape[1:]
    rb = _row_block(R, cp)
    nb = R // rb

    def body(*refs):
        p_refs = refs[:nl]
        w_ref, m_ref, v_ref, g_ref, d_ref, m2_ref, v2_ref = refs[nl:]

        def total(p_ref):
            g = p_ref[0, :, 0:C].astype(F32)
            for k in range(1, N_DEV):
                g = g + p_ref[k, :, 0:C].astype(F32)
            return g

        g = total(p_refs[0])
        for layer in range(1, nl):
            g = jnp.where(pl.program_id(0) == layer, total(p_refs[layer]), g)
        g_ref[...] = g
        m2 = ADAM_B1 * m_ref[...] + (1.0 - ADAM_B1) * g
        v2 = ADAM_B2 * v_ref[...] + (1.0 - ADAM_B2) * (g * g)
        m2_ref[...] = m2
        v2_ref[...] = v2
        m_hat = m2 / (1.0 - ADAM_B1 ** ADAM_STEP)
        v_hat = v2 / (1.0 - ADAM_B2 ** ADAM_STEP)
        d_ref[...] = -ADAM_LR * (m_hat / (jnp.sqrt(v_hat) + ADAM_EPS) + ADAM_WD * w_ref[...])

    def part_spec(layer):
        return pl.BlockSpec((N_DEV, rb, cp), lambda l, i: (0, jnp.where(l == layer, i, 0), 0))

    blk = pl.BlockSpec((rb, C), lambda l, i: (l * nb + i, 0))
    return pl.pallas_call(
        body, name=name, grid=(nl, nb),
        in_specs=[part_spec(layer) for layer in range(nl)] + [blk, blk, blk],
        out_specs=[blk] * 4,
        out_shape=[jax.ShapeDtypeStruct((nl * R, C), F32)] * 4,
        compiler_params=_params("arbitrary", "arbitrary"),
    )(*parts, w, m, v)


def _cols_to_pieces(full):
    r = full.shape[0]
    return jnp.transpose(full.reshape(r, N_DEV, -1), (1, 0, 2))


def _pieces_to_cols(pieces):
    r = pieces.shape[1]
    return jnp.transpose(pieces, (1, 0, 2)).reshape(r, -1)


def _rope_tables(positions):
    inv_freq = 1.0 / (ROPE_THETA ** (jnp.arange(0, QK_ROPE, 2, dtype=F32) / QK_ROPE))
    ang = positions.astype(F32)[:, None] * inv_freq
    cos, sin = jnp.cos(ang), jnp.sin(ang)
    zero = jnp.zeros_like(cos)
    return (jnp.concatenate([cos, cos, zero, zero], axis=1),
            jnp.concatenate([-sin, zero, zero, zero], axis=1),
            jnp.concatenate([zero, sin, zero, zero], axis=1))


def kernel(x, p, positions, norm_mix, w_in, w_pool, pool_scale, q_norm, kv_norm, w_uq, w_ukv, w_a, w_b, w_o, norm_ffn, w_gate, w_up, w_down, norm_ple, w_ple_gate, w_ple, final_norm, loss_target, m_norm_mix, m_w_in, m_w_pool, m_pool_scale, m_q_norm, m_kv_norm, m_w_uq, m_w_ukv, m_w_a, m_w_b, m_w_o, m_norm_ffn, m_w_gate, m_w_up, m_w_down, m_norm_ple, m_w_ple_gate, m_w_ple, m_final_norm, v_norm_mix, v_w_in, v_w_pool, v_pool_scale, v_q_norm, v_kv_norm, v_w_uq, v_w_ukv, v_w_a, v_w_b, v_w_o, v_norm_ffn, v_w_gate, v_w_up, v_w_down, v_norm_ple, v_w_ple_gate, v_w_ple, v_final_norm):
    given = dict(locals())
    shard = {n: given[n] for n in WEIGHTS}
    TM = 512
    TQ = 512

    def send_form(n, i):
        w = shard[n][i].astype(BF16)
        if n == "w_in":
            return jnp.pad(w, ((0, 0), (0, PIECE_PAD - W_IN_PIECE)))
        if n == "w_uq":
            w = jnp.pad(w, ((0, 0), (0, 0), (0, HEAD_PAD - QK_HEAD)))
        if n in ("w_uq", "w_ukv"):
            return w.reshape(-1, N_HEADS * HEAD_PAD)
        if n in ("w_gate", "w_up"):
            return jnp.pad(w, ((0, 0), (0, FF_PAD - FF_PIECE)))
        if n == "w_down":
            return jnp.pad(w, ((0, FF_PAD - FF_PIECE), (0, 0)))
        return w

    names = [n for n, _ in SHARDED]
    FIRST = ("w_in", "w_uq", "w_ukv", "w_a", "w_b", "w_o")
    LATER = ("w_gate", "w_up", "w_down", "w_ple_gate", "w_ple")
    start_a0 = exchange_start([send_form("w_in", 0)], [], None, name="gather_start_a0")
    got_a0 = exchange_wait(start_a0, None, name="gather_wait_a0")
    start_a1 = exchange_start([send_form(n, 0) for n in FIRST[1:]], [], got_a0[0],
                              name="gather_start_a1")
    gathered = {("w_in", 0): got_a0[0]}
    row_map = jnp.asarray(_piece_row_map())

    low_np, up_np = _band_matrices()
    low, up = jnp.asarray(low_np, BF16), jnp.asarray(up_np, BF16)
    low_t = jnp.asarray(low_np.transpose(0, 2, 1), BF16)
    up_t = jnp.asarray(up_np.transpose(0, 2, 1), BF16)
    cos_t, s1_t, s2_t = _rope_tables(positions[0])

    def mix_weights(i):
        g = lambda n: gathered[n, i]
        return dict(
            w_uq=g("w_uq").reshape(Q_LORA, N_HEADS * HEAD_PAD),
            w_ukv=g("w_ukv").reshape(KV_LORA, N_HEADS * HEAD_PAD),
            w_a=_pieces_to_cols(g("w_a")), w_b=g("w_b").reshape(D_MODEL, D_MODEL),
            w_o=g("w_o").reshape(D_MODEL, D_MODEL),
            w_pool=w_pool[i].astype(BF16),
            pool_scale=pool_scale[i][None], norm_mix=norm_mix[i][None], q_norm=q_norm[i][None],
            kv_norm=kv_norm[i][None], norm_ffn=norm_ffn[i][None], norm_ple=norm_ple[i][None])

    def channel_weights(i):
        g = lambda n: gathered[n, i]
        return dict(w_gate=g("w_gate"), w_up=g("w_up"), w_down=g("w_down"),
                    w_pg=g("w_ple_gate").reshape(D_MODEL, D_MODEL),
                    w_ple=_pieces_to_cols(g("w_ple")))

    def behind(gain, start):
        return gain + start["token"][0, 0]

    layers, saved = [], []
    starts = {}
    for i in range(DEPTH):
        if i == 1:
            gathered.update({(n, 1): a for n, a in zip(
                FIRST, exchange_wait(starts["b2"], saved[0]["logits"], name="gather_wait_b2"))})
        w_in_i = assemble_w_in(gathered["w_in", i], row_map, tr=512)
        g_mix = norm_mix[i][None]
        z = rms_matmul(x[0] if i == 0 else xs, behind(g_mix, start_a1) if i == 0 else g_mix,
                       w_in_i, tm=TM, tn=IN_PAD, name="in_proj")
        if i == 0:
            got_a1 = exchange_wait(start_a1, z, name="gather_wait_a1")
            gathered.update({(n, 0): a for n, a in zip(FIRST[1:], got_a1)})
            start_b1 = exchange_start([send_form(n, 0) for n in LATER], [], got_a1[0],
                                      name="gather_start_b1")
        L = dict(mix_weights(i), w_in=w_in_i)
        g_pool = behind(L["pool_scale"], start_b1) if i == 0 else L["pool_scale"]
        pooled, ms, ya = pool_fwd(z, low, up, L["w_pool"], g_pool, L["w_a"], tm=TM)
        g_q = L["q_norm"]
        if i == 0:
            starts["b2"] = exchange_start([send_form(n, 1) for n in FIRST], [], ya,
                                          name="gather_start_b2")
            g_q = behind(g_q, starts["b2"])
        q, k, v = mla_prep(z, cos_t, s1_t, s2_t, g_q, L["kv_norm"], L["w_uq"], L["w_ukv"], tm=TM)
        o, lse = flash_fwd(q, k, v, tq=2 * TQ)
        yb, merged, x1 = merge_fwd(o, ya, z, x[0] if i == 0 else xs, L["w_b"], L["w_o"], tm=TM)
        g_ffn = L["norm_ffn"]
        if i == 0:
            got = exchange_wait(start_b1, o, name="gather_wait_b1")
            starts["b3"] = exchange_start([send_form(n, 1) for n in LATER], [], x1,
                                          name="gather_start_b3")
            g_ffn = behind(g_ffn, starts["b3"])
        else:
            got = exchange_wait(starts["b3"], o, name="gather_wait_b3")
        gathered.update({(n, i): a for n, a in zip(LATER, got)})
        L.update(channel_weights(i))
        gpre, upre, x2 = ffn_fwd(x1, g_ffn, L["w_gate"], L["w_up"], L["w_down"], tm=TM)
        logits, pe, x3 = ple_fwd(x2, p[i, 0], L["norm_ple"], L["w_pg"], L["w_ple"], tm=TM)
        layers.append(L)
        saved.append(dict(x=x[0] if i == 0 else xs, z=z, pooled=pooled, ms=ms, ya=ya, q=q, k=k,
                          v=v, o=o, lse=lse, yb=yb, merged=merged, x1=x1, gpre=gpre, upre=upre,
                          x2=x2, logits=logits, pe=pe))
        xs = x3

    loss_part, dx, d_final = loss_head(xs, final_norm[None], loss_target[0], tm=TM)

    grads = {n: [None] * DEPTH for n in REPLICATED if n != "final_norm"}
    pieces = {n: [None] * DEPTH for n in names}
    tn_mm = functools.partial(matmul_tn, tn=1024, tk=1024)
    row_pieces = lambda a: a.reshape(N_DEV, -1, a.shape[-1])
    MIDDLE = ("w_o", "w_b", "w_uq", "w_ukv")
    LAST = ("w_a", "w_in")
    scatters = []
    for i in reversed(range(DEPTH)):
        L, S = layers[i], saved[i]
        g_ple = L["norm_ple"] if not scatters else behind(L["norm_ple"], scatters[-1][0])
        dl, dpe, hn, dx2, d_norm_ple = ple_bwd(dx, S["logits"], S["pe"], S["x2"], g_ple,
                                               L["w_pg"], tm=TM)
        grads["norm_ple"][i] = d_norm_ple[0]
        pieces["w_ple_gate"][i] = row_pieces(tn_mm(hn, dl, name="dw_ple_gate"))
        pieces["w_ple"][i] = _cols_to_pieces(tn_mm(p[i, 0], dpe, name="dw_ple"))

        act, dgp, dup, h2, dx1, d_norm_ffn = ffn_bwd(dx2, S["x1"], L["norm_ffn"], S["gpre"],
                                                     S["upre"], L["w_gate"], L["w_up"],
                                                     L["w_down"], tm=TM)
        grads["norm_ffn"][i] = d_norm_ffn[0]
        pieces["w_down"][i] = matmul_tn_pieces(act, dx2, tk=512, name="dw_down")
        pieces["w_gate"][i] = matmul_tn_pieces(h2, dgp, tk=512, name="dw_gate")
        pieces["w_up"][i] = matmul_tn_pieces(h2, dup, tk=512, name="dw_up")
        scatters.append((exchange_start([], [pieces[n][i] for n in LATER], None,
                                        name=f"scatter_start_{i}a"), LATER, i))

        dya, dyb, do, dga, dgb = merge_bwd(dx1, S["ya"], S["yb"], S["z"], L["w_o"], L["w_b"],
                                           scatters[-1][0]["token"], tm=TM)
        pieces["w_o"][i] = row_pieces(tn_mm(S["merged"], dx1, name="dw_o"))
        pieces["w_b"][i] = row_pieces(tn_mm(S["o"], dyb, name="dw_b"))

        TQB = min(2 * TQ, do.shape[0])
        per_tile = lambda a: a.reshape(N_HEADS, -1, 1, TQB)
        delta = attn_delta(do, S["o"], tq=TQ)
        dq, dk, dv = flash_bwd(S["q"], S["k"], S["v"], do, per_tile(S["lse"]), per_tile(delta),
                               tq=TQB, tk=2 * TQ)
        dcq, dckv, dkr, dqf, dkvf, qn, kvn, d_q_norm, d_kv_norm = mla_bwd(
            dq, dk, dv, S["z"], cos_t, s1_t, s2_t, L["q_norm"], L["kv_norm"], L["w_uq"],
            L["w_ukv"], tm=TM)
        grads["q_norm"][i] = d_q_norm[0]
        grads["kv_norm"][i] = d_kv_norm[0]
        d_w_uq = tn_mm(qn, dqf, name="dw_uq")
        pieces["w_uq"][i] = d_w_uq.reshape(N_DEV, -1, N_HEADS, HEAD_PAD)[..., :QK_HEAD]
        pieces["w_ukv"][i] = tn_mm(kvn, dkvf, name="dw_ukv").reshape(
            N_DEV, -1, N_HEADS, QK_NOPE + V_HEAD)
        scatters.append((exchange_start([], [pieces[n][i] for n in MIDDLE], None,
                                        name=f"scatter_start_{i}b"), MIDDLE, i))

        dpc, d_pool_scale, d_w_pool = pool_bwd_a(dya, S["pooled"], L["w_a"], L["w_pool"],
                                                 behind(L["pool_scale"], scatters[-1][0]), tm=TM)
        grads["pool_scale"][i] = d_pool_scale[0]
        grads["w_pool"][i] = d_w_pool
        pieces["w_a"][i] = _cols_to_pieces(tn_mm(S["ms"], dya, name="dw_a"))
        du = pool_bwd_b(dpc, low_t, up_t, tm=TM)

        segs = [(dga, OFF_GA), (dgb, OFF_GB), (du, OFF_U), (dcq, OFF_CQ), (dckv, OFF_CKV),
                (dkr, OFF_KR)]
        dx, h, d_norm_mix = in_bwd(dx1, S["x"], L["norm_mix"], segs, L["w_in"], tm=TM)
        grads["norm_mix"][i] = d_norm_mix[0]
        pieces["w_in"][i] = split_dw_in(dw_in_proj(h, segs, tk=512), row_map)
        if i == 1:
            scatters.append((exchange_start([], [pieces[n][i] for n in LAST], None,
                                            name="scatter_start_1c"), LAST, i))

    small = {n: jnp.stack(g) for n, g in grads.items()}
    small["final_norm"] = d_final[0]
    rep = jnp.concatenate([small[n].reshape(-1) for n in REPLICATED] + [loss_part.reshape(-1)])
    n_small = -(-rep.shape[0] // (8 * FLAT_COLS)) * (8 * FLAT_COLS)

    def small_flat(parts):
        flat = jnp.concatenate([a.reshape(-1) for a in parts])
        return jnp.pad(flat, (0, n_small - flat.shape[0])).reshape(-1, FLAT_COLS)

    start_end = exchange_start([small_flat([rep])], [pieces[n][0] for n in LAST], None,
                               name="scatter_start_0c")
    parts = {}
    for start, group, i in scatters:
        tag = {LATER: "a", MIDDLE: "b", LAST: "c"}[group]
        got = exchange_wait(start, start_end["token"], name=f"scatter_wait_{i}{tag}")
        parts.update({(n, i): a for n, a in zip(group, got)})
    got_end = exchange_wait(start_end, None, name="scatter_wait_0c")
    parts.update({(n, 0): a for n, a in zip(LAST, got_end[1:])})
    for i in range(DEPTH):
        parts["w_down", i] = parts["w_down", i][:, :FF_PIECE]

    results = {}
    for n in names:
        w = shard[n]
        rows2d = lambda a: a.reshape(-1, w.shape[-1])
        w2 = rows2d(w)
        per_layer = [parts[n, i].reshape(N_DEV, w2.shape[0] // DEPTH, -1) for i in range(DEPTH)]
        out = reduce_adamw(per_layer, w2, rows2d(given["m_" + n]), rows2d(given["v_" + n]),
                           name="adamw_" + n)
        results[n] = [a.reshape(w.shape) for a in out]

    out = reduce_adamw([got_end[0]], *[small_flat([given[pre + n] for n in REPLICATED])
                                      for pre in ("", "m_", "v_")], name="adamw_small")
    loss = None
    for n in REPLICATED:
        results[n] = []
    for a in out:
        a = a.reshape(-1)
        off = 0
        for n in REPLICATED:
            size = int(np.prod(shard[n].shape))
            results[n].append(a[off:off + size].reshape(shard[n].shape))
            off += size
        if loss is None:
            loss = a[off]
    return (loss, dx[None], *[results[n][0] for n in WEIGHTS], *[results[n][1] for n in WEIGHTS],
            *[results[n][2] for n in WEIGHTS], *[results[n][3] for n in WEIGHTS])
```

```python
import functools

import numpy as np
import jax
import jax.numpy as jnp
from jax import lax
from jax.experimental import pallas as pl
from jax.experimental.pallas import tpu as pltpu

F32 = jnp.float32
BF16 = jnp.bfloat16

D_MODEL = 1024
DEPTH = 2
PLE_DIM = 256
POOL_WINDOWS = (2, 4, 8, 16)
POOL_GROUP = 128
POOL_WIDTH = 512
N_HEADS = 8
Q_LORA = 512
KV_LORA = 256
QK_NOPE = 128
QK_ROPE = 64
QK_HEAD = 192
V_HEAD = 128
HEAD_PAD = 256
D_FF = 2816
ROPE_THETA = 10000.0
EPS = 1e-6
IN_WIDTH = 3392
ATTN_SCALE = QK_HEAD ** -0.5

OFF_GA, OFF_GB, OFF_U, OFF_CQ, OFF_CKV, OFF_KR = 0, 1024, 2048, 2560, 3072, 3328
IN_PAD = 3456
W_IN_PIECE = IN_WIDTH // 8
PIECE_PAD = 512
FF_PIECE = D_FF // 8
FF_PAD = 384
FF_PAIRS = 4
FF_STEP = 2 * FF_PAD

ADAM_LR = 0.001
ADAM_B1 = 0.9
ADAM_B2 = 0.999
ADAM_EPS = 1e-08
ADAM_WD = 0.01
ADAM_STEP = 10

N_DEV = 8
LANES = 128
CHUNK = 128
VMEM_LIMIT = 56 * 1024 * 1024

SHARDED = (("w_in", 2), ("w_uq", 1), ("w_ukv", 1), ("w_a", 2), ("w_b", 1), ("w_o", 1),
           ("w_gate", 2), ("w_up", 2), ("w_down", 1), ("w_ple_gate", 1), ("w_ple", 2))
REPLICATED = ("norm_mix", "w_pool", "pool_scale", "q_norm", "kv_norm", "norm_ffn", "norm_ple",
              "final_norm")
WEIGHTS = ("norm_mix", "w_in", "w_pool", "pool_scale", "q_norm", "kv_norm", "w_uq", "w_ukv",
           "w_a", "w_b", "w_o", "norm_ffn", "w_gate", "w_up", "w_down", "norm_ple",
           "w_ple_gate", "w_ple", "final_norm")
FLAT_COLS = 1024
FLAT_ROW_BLOCK = 192


def _params(*sem):
    return pltpu.CompilerParams(dimension_semantics=sem, vmem_limit_bytes=VMEM_LIMIT)


def _dot(a, b):
    return jnp.dot(a, b, preferred_element_type=F32)


def _dot_nt(a, b):
    return lax.dot_general(a, b, (((1,), (1,)), ((), ())), preferred_element_type=F32)


def _dot_tn(a, b):
    return lax.dot_general(a, b, (((0,), (0,)), ((), ())), preferred_element_type=F32)


def _rms_fwd(x, g):
    r = lax.rsqrt(jnp.mean(x * x, axis=-1, keepdims=True) + EPS)
    return x * r * g


def _rms_bwd(x, g, dy):
    r = lax.rsqrt(jnp.mean(x * x, axis=-1, keepdims=True) + EPS)
    xr = x * r
    gy = dy * g
    dx = r * (gy - xr * jnp.mean(gy * xr, axis=-1, keepdims=True))
    return dx, dy * xr


def _sigmoid(x):
    return 1.0 / (1.0 + jnp.exp(-x))


def _accum_rows(ref, rows):
    ref[...] += jnp.sum(rows, axis=0, keepdims=True)


def _band(band, x):
    h1 = x.astype(BF16)
    r1 = x - h1.astype(F32)
    h2 = r1.astype(BF16)
    h3 = (r1 - h2.astype(F32)).astype(BF16)
    return _dot(band, h1) + _dot(band, h2) + _dot(band, h3)


def _rope(x, c, s1, s2, sign):
    return x * c + sign * (pltpu.roll(x, 96, 1) * s1 + pltpu.roll(x, 32, 1) * s2)


def _full(shape):
    n = len(shape)
    return pl.BlockSpec(shape, lambda *_: (0,) * n)


def matmul_tn(a, b, *, tn, tk, name, tm=1024):
    T, M = a.shape
    N = b.shape[1]
    tm, tn, tk = min(tm, M), min(tn, N), min(tk, T)
    nk = T // tk

    def body(a_ref, b_ref, o_ref, acc_ref):
        k = pl.program_id(2)

        @pl.when(k == 0)
        def _():
            acc_ref[...] = jnp.zeros_like(acc_ref)

        acc_ref[...] += _dot_tn(a_ref[...].astype(BF16), b_ref[...].astype(BF16))

        @pl.when(k == nk - 1)
        def _():
            o_ref[...] = acc_ref[...].astype(BF16)

    return pl.pallas_call(
        body, name=name, grid=(M // tm, N // tn, nk),
        in_specs=[pl.BlockSpec((tk, tm), lambda i, j, k: (k, i)),
                  pl.BlockSpec((tk, tn), lambda i, j, k: (k, j))],
        out_specs=pl.BlockSpec((tm, tn), lambda i, j, k: (i, j)),
        out_shape=jax.ShapeDtypeStruct((M, N), BF16),
        scratch_shapes=[pltpu.VMEM((tm, tn), F32)],
        compiler_params=_params("parallel", "parallel", "arbitrary"),
    )(a, b)


def matmul_tn_pieces(a, b, *, tk, name):
    a3 = a.ndim == 3
    T = a.shape[-2]
    m, n = a.shape[-1], b.shape[-1]
    tk = min(tk, T)
    nk = T // tk
    out = (N_DEV, FF_PAD, n) if a3 else (N_DEV, m, FF_PAD)

    def body(a_ref, b_ref, o_ref, acc_ref):
        k = pl.program_id(0)

        @pl.when(k == 0)
        def _():
            acc_ref[...] = jnp.zeros_like(acc_ref)

        whole = (b_ref if a3 else a_ref)[...].astype(BF16)
        for j in range(FF_PAIRS):
            if a3:
                acc_ref[j] += _dot_tn(a_ref[j].astype(BF16), whole)
            else:
                acc_ref[j] += _dot_tn(whole, b_ref[j].astype(BF16))

        @pl.when(k == nk - 1)
        def _():
            for j in range(FF_PAIRS):
                for half in range(2):
                    cut = slice(half * FF_PAD, (half + 1) * FF_PAD)
                    piece = acc_ref[j, cut, :] if a3 else acc_ref[j, :, cut]
                    o_ref[2 * j + half] = piece.astype(BF16)

    pairs = lambda w: pl.BlockSpec((FF_PAIRS, tk, w), lambda k: (0, k, 0))
    whole = lambda w: pl.BlockSpec((tk, w), lambda k: (k, 0))
    return pl.pallas_call(
        body, name=name, grid=(nk,),
        in_specs=[pairs(m) if a3 else whole(m), whole(n) if a3 else pairs(n)],
        out_specs=_full(out),
        out_shape=jax.ShapeDtypeStruct(out, BF16),
        scratch_shapes=[pltpu.VMEM((FF_PAIRS, m, n), F32)],
        compiler_params=_params("arbitrary"),
    )(a, b)


def dw_in_proj(h, segs, *, tk):
    T = h.shape[0]
    tk = min(tk, T)
    nk = T // tk
    widths = [s.shape[1] for s, _ in segs]
    offs = [o for _, o in segs]

    def body(*refs):
        h_ref, seg_refs, o_ref, acc_ref = refs[0], refs[1:-2], refs[-2], refs[-1]
        k = pl.program_id(0)

        @pl.when(k == 0)
        def _():
            acc_ref[...] = jnp.zeros_like(acc_ref)

        hv = h_ref[...]
        for s_ref, off, w in zip(seg_refs, offs, widths):
            acc_ref[:, off:off + w] += _dot_tn(hv, s_ref[...])

        @pl.when(k == nk - 1)
        def _():
            o_ref[...] = acc_ref[...].astype(BF16)

    return pl.pallas_call(
        body, name="dw_in", grid=(nk,),
        in_specs=[pl.BlockSpec((tk, D_MODEL), lambda k: (k, 0))]
        + [pl.BlockSpec((tk, w), lambda k: (k, 0)) for w in widths],
        out_specs=_full((D_MODEL, IN_PAD)),
        out_shape=jax.ShapeDtypeStruct((D_MODEL, IN_PAD), BF16),
        scratch_shapes=[pltpu.VMEM((D_MODEL, IN_PAD), F32)],
        compiler_params=_params("arbitrary"),
    )(h, *[s for s, _ in segs])


def _piece_row_map():
    src = np.full(IN_PAD, -1, np.int64)
    for orig, pad, width in ((0, OFF_U, 512), (512, OFF_CQ, 512), (1024, OFF_CKV, 256),
                             (1280, OFF_KR, 64), (1344, OFF_GA, 1024), (2368, OFF_GB, 1024)):
        src[pad:pad + width] = np.arange(orig, orig + width)
    rows = np.where(src >= 0, (src // W_IN_PIECE) * PIECE_PAD + src % W_IN_PIECE, -1)
    return rows.astype(np.int32)[None, :]


def _selector(j, map_ref, b):
    rid = j * PIECE_PAD + lax.broadcasted_iota(jnp.int32, (PIECE_PAD, LANES), 0)
    return jnp.where(rid == map_ref[:, b * LANES:(b + 1) * LANES], 1.0, 0.0).astype(BF16)


def _block_pieces():
    rows = _piece_row_map()[0].reshape(-1, LANES)
    return [sorted({int(r) // PIECE_PAD for r in blk if r >= 0}) for blk in rows]


def assemble_w_in(pieces, row_map, *, tr):
    feeds = _block_pieces()

    def body(p_ref, map_ref, o_ref):
        for b, js in enumerate(feeds):
            acc = jnp.zeros((tr, LANES), F32)
            for j in js:
                acc = acc + _dot(p_ref[j], _selector(j, map_ref, b))
            o_ref[:, b * LANES:(b + 1) * LANES] = acc.astype(BF16)

    return pl.pallas_call(
        body, name="assemble_w_in", grid=(D_MODEL // tr,),
        in_specs=[pl.BlockSpec((N_DEV, tr, PIECE_PAD), lambda i: (0, i, 0)), _full((1, IN_PAD))],
        out_specs=pl.BlockSpec((tr, IN_PAD), lambda i: (i, 0)),
        out_shape=jax.ShapeDtypeStruct((D_MODEL, IN_PAD), BF16),
        compiler_params=_params("parallel"),
    )(pieces, row_map)


def split_dw_in(dwp, row_map, *, tr):
    feeds = _block_pieces()

    def body(d_ref, map_ref, o_ref):
        for j in range(N_DEV):
            acc = jnp.zeros((tr, PIECE_PAD), F32)
            for b, js in enumerate(feeds):
                if j in js:
                    acc = acc + _dot_nt(d_ref[:, b * LANES:(b + 1) * LANES],
                                        _selector(j, map_ref, b))
            o_ref[j] = acc.astype(BF16)

    return pl.pallas_call(
        body, name="split_dw_in", grid=(D_MODEL // tr,),
        in_specs=[pl.BlockSpec((tr, IN_PAD), lambda i: (i, 0)), _full((1, IN_PAD))],
        out_specs=pl.BlockSpec((N_DEV, tr, PIECE_PAD), lambda i: (0, i, 0)),
        out_shape=jax.ShapeDtypeStruct((N_DEV, D_MODEL, PIECE_PAD), BF16),
        compiler_params=_params("parallel"),
    )(dwp, row_map)


def rms_matmul(x, g, w, *, tm, tn, name):
    T, D = x.shape
    N = w.shape[1]
    tm = min(tm, T)

    def body(x_ref, g_ref, w_ref, o_ref, h_ref):
        @pl.when(pl.program_id(1) == 0)
        def _():
            h_ref[...] = _rms_fwd(x_ref[...], g_ref[...]).astype(BF16)

        o_ref[...] = _dot(h_ref[...], w_ref[...])

    return pl.pallas_call(
        body, name=name, grid=(T // tm, N // tn),
        in_specs=[pl.BlockSpec((tm, D), lambda i, j: (i, 0)), _full((1, D)),
                  pl.BlockSpec((D, tn), lambda i, j: (0, j))],
        out_specs=pl.BlockSpec((tm, tn), lambda i, j: (i, j)),
        out_shape=jax.ShapeDtypeStruct((T, N), F32),
        scratch_shapes=[pltpu.VMEM((tm, D), BF16)],
        compiler_params=_params("parallel", "arbitrary"),
    )(x, g, w)


def _band_matrices():
    s = np.arange(CHUNK)[:, None]
    t = np.arange(CHUNK)[None, :]
    low = np.stack([((s - t >= 0) & (s - t < w)) for w in POOL_WINDOWS]).astype(np.float32)
    up = np.stack([(t > s + CHUNK - w) for w in POOL_WINDOWS]).astype(np.float32)
    return low, up


def _window_count(row0, g):
    t = row0 + lax.broadcasted_iota(jnp.int32, (CHUNK, 1), 0)
    return jnp.minimum(t + 1, POOL_WINDOWS[g]).astype(F32)


def pool_fwd(z, low, up, w_pool, pool_scale, w_a, *, tm):
    T = z.shape[0]
    tm = min(tm, T)
    nch = tm // CHUNK
    ublk = OFF_U // POOL_WIDTH

    def body(u_ref, halo_ref, low_ref, up_ref, wp_ref, sc_ref, wa_ref, pooled_ref, ms_ref, ya_ref):
        i = pl.program_id(0)
        for c in range(nch):
            rows = slice(c * CHUNK, (c + 1) * CHUNK)
            for g in range(4):
                cols = slice(g * POOL_GROUP, (g + 1) * POOL_GROUP)
                cur = u_ref[rows, cols]
                if c == 0:
                    prev = jnp.where(i > 0, halo_ref[:, cols], 0.0)
                else:
                    prev = u_ref[(c - 1) * CHUNK:c * CHUNK, cols]
                s = _band(low_ref[g], cur) + _band(up_ref[g], prev)
                pooled = (s / _window_count(i * tm + c * CHUNK, g) - cur).astype(BF16)
                pooled_ref[rows, cols] = pooled
                ms_ref[rows, cols] = (_dot(pooled, wp_ref[g]) * sc_ref[:, cols]).astype(BF16)
        ya_ref[...] = _dot(ms_ref[...], wa_ref[...])

    return pl.pallas_call(
        body, name="pool_fwd", grid=(T // tm,),
        in_specs=[pl.BlockSpec((tm, POOL_WIDTH), lambda i: (i, ublk)),
                  pl.BlockSpec((CHUNK, POOL_WIDTH), lambda i: (jnp.maximum(i * nch - 1, 0), ublk)),
                  _full((4, CHUNK, CHUNK)), _full((4, CHUNK, CHUNK)),
                  _full((4, POOL_GROUP, POOL_GROUP)), _full((1, POOL_WIDTH)),
                  _full((POOL_WIDTH, D_MODEL))],
        out_specs=[pl.BlockSpec((tm, POOL_WIDTH), lambda i: (i, 0)),
                   pl.BlockSpec((tm, POOL_WIDTH), lambda i: (i, 0)),
                   pl.BlockSpec((tm, D_MODEL), lambda i: (i, 0))],
        out_shape=[jax.ShapeDtypeStruct((T, POOL_WIDTH), BF16),
                   jax.ShapeDtypeStruct((T, POOL_WIDTH), BF16),
                   jax.ShapeDtypeStruct((T, D_MODEL), F32)],
        compiler_params=_params("parallel"),
    )(z, z, low, up, w_pool, pool_scale, w_a)


def pool_bwd_a(dya, pooled, w_a, w_pool, pool_scale, *, tm):
    T = dya.shape[0]
    tm = min(tm, T)
    nch = tm // CHUNK

    def body(dya_ref, pooled_ref, wa_ref, wp_ref, sc_ref, dpc_ref, dsc_ref, dwp_ref):
        i = pl.program_id(0)

        @pl.when(i == 0)
        def _():
            dsc_ref[...] = jnp.zeros_like(dsc_ref)
            dwp_ref[...] = jnp.zeros_like(dwp_ref)

        dms = _dot_nt(dya_ref[...], wa_ref[...])
        for g in range(4):
            cols = slice(g * POOL_GROUP, (g + 1) * POOL_GROUP)
            pg = pooled_ref[:, cols]
            dmg = dms[:, cols]
            mixed = _dot(pg, wp_ref[g])
            dsc_ref[:, cols] += jnp.sum(dmg * mixed, axis=0, keepdims=True)
            dmixed = (dmg * sc_ref[:, cols]).astype(BF16)
            dwp_ref[g] += _dot_tn(pg, dmixed)
            dpooled = _dot_nt(dmixed, wp_ref[g])
            for c in range(nch):
                rows = slice(c * CHUNK, (c + 1) * CHUNK)
                dpc_ref[rows, cols] = dpooled[rows] / _window_count(i * tm + c * CHUNK, g)

    return pl.pallas_call(
        body, name="pool_bwd_a", grid=(T // tm,),
        in_specs=[pl.BlockSpec((tm, D_MODEL), lambda i: (i, 0)),
                  pl.BlockSpec((tm, POOL_WIDTH), lambda i: (i, 0)),
                  _full((POOL_WIDTH, D_MODEL)), _full((4, POOL_GROUP, POOL_GROUP)),
                  _full((1, POOL_WIDTH))],
        out_specs=[pl.BlockSpec((tm, POOL_WIDTH), lambda i: (i, 0)), _full((1, POOL_WIDTH)),
                   _full((4, POOL_GROUP, POOL_GROUP))],
        out_shape=[jax.ShapeDtypeStruct((T, POOL_WIDTH), F32),
                   jax.ShapeDtypeStruct((1, POOL_WIDTH), F32),
                   jax.ShapeDtypeStruct((4, POOL_GROUP, POOL_GROUP), F32)],
        compiler_params=_params("arbitrary"),
    )(dya, pooled, w_a, w_pool, pool_scale)


def pool_bwd_b(dpc, low_t, up_t, *, tm):
    T = dpc.shape[0]
    tm = min(tm, T)
    nch = tm // CHUNK
    last_chunk = T // CHUNK - 1

    def body(d_ref, halo_ref, low_ref, up_ref, du_ref):
        i = pl.program_id(0)
        for c in range(nch):
            rows = slice(c * CHUNK, (c + 1) * CHUNK)
            for g in range(4):
                cols = slice(g * POOL_GROUP, (g + 1) * POOL_GROUP)
                cur = d_ref[rows, cols]
                if c == nch - 1:
                    nxt = jnp.where(i < pl.num_programs(0) - 1, halo_ref[:, cols], 0.0)
                else:
                    nxt = d_ref[(c + 1) * CHUNK:(c + 2) * CHUNK, cols]
                s = _band(low_ref[g], cur) + _band(up_ref[g], nxt)
                du_ref[rows, cols] = (s - cur * _window_count(i * tm + c * CHUNK, g)).astype(BF16)

    return pl.pallas_call(
        body, name="pool_bwd_b", grid=(T // tm,),
        in_specs=[pl.BlockSpec((tm, POOL_WIDTH), lambda i: (i, 0)),
                  pl.BlockSpec((CHUNK, POOL_WIDTH),
                               lambda i: (jnp.minimum((i + 1) * nch, last_chunk), 0)),
                  _full((4, CHUNK, CHUNK)), _full((4, CHUNK, CHUNK))],
        out_specs=pl.BlockSpec((tm, POOL_WIDTH), lambda i: (i, 0)),
        out_shape=jax.ShapeDtypeStruct((T, POOL_WIDTH), BF16),
        compiler_params=_params("parallel"),
    )(dpc, dpc, low_t, up_t)


def mla_prep(z, cos_t, s1_t, s2_t, q_norm, kv_norm, w_uq, w_ukv, *, tm):
    T = z.shape[0]
    tm = min(tm, T)

    def body(cq_ref, ckv_ref, kr_ref, c_ref, s1_ref, s2_ref, qg_ref, kvg_ref, wuq_ref, wukv_ref,
             q_ref, k_ref, v_ref):
        c, s1, s2 = c_ref[...], s1_ref[...], s2_ref[...]
        qn = _rms_fwd(cq_ref[...], qg_ref[...]).astype(BF16)
        q = _dot(qn, wuq_ref[...])
        kvn = _rms_fwd(ckv_ref[...], kvg_ref[...]).astype(BF16)
        kv = _dot(kvn, wukv_ref[...])
        kpe = _rope(kr_ref[...], c, s1, s2, 1.0).astype(BF16)
        for h in range(N_HEADS):
            o = h * HEAD_PAD
            q_ref[h, :, 0:QK_NOPE] = (q[:, o:o + QK_NOPE] * ATTN_SCALE).astype(BF16)
            q_ref[h, :, QK_NOPE:HEAD_PAD] = (
                _rope(q[:, o + QK_NOPE:o + HEAD_PAD], c, s1, s2, 1.0) * ATTN_SCALE).astype(BF16)
            k_ref[h, :, 0:QK_NOPE] = kv[:, o:o + QK_NOPE].astype(BF16)
            k_ref[h, :, QK_NOPE:HEAD_PAD] = kpe
            v_ref[h] = kv[:, o + QK_NOPE:o + HEAD_PAD].astype(BF16)

    tok = lambda w: pl.BlockSpec((tm, w), lambda i: (i, 0))
    return pl.pallas_call(
        body, name="mla_prep", grid=(T // tm,),
        in_specs=[pl.BlockSpec((tm, Q_LORA), lambda i: (i, OFF_CQ // Q_LORA)),
                  pl.BlockSpec((tm, KV_LORA), lambda i: (i, OFF_CKV // KV_LORA)),
                  pl.BlockSpec((tm, LANES), lambda i: (i, OFF_KR // LANES)),
                  tok(LANES), tok(LANES), tok(LANES),
                  _full((1, Q_LORA)), _full((1, KV_LORA)),
                  _full((Q_LORA, N_HEADS * HEAD_PAD)), _full((KV_LORA, N_HEADS * HEAD_PAD))],
        out_specs=[pl.BlockSpec((N_HEADS, tm, HEAD_PAD), lambda i: (0, i, 0)),
                   pl.BlockSpec((N_HEADS, tm, HEAD_PAD), lambda i: (0, i, 0)),
                   pl.BlockSpec((N_HEADS, tm, V_HEAD), lambda i: (0, i, 0))],
        out_shape=[jax.ShapeDtypeStruct((N_HEADS, T, HEAD_PAD), BF16),
                   jax.ShapeDtypeStruct((N_HEADS, T, HEAD_PAD), BF16),
                   jax.ShapeDtypeStruct((N_HEADS, T, V_HEAD), BF16)],
        compiler_params=_params("parallel"),
    )(z, z, z, cos_t, s1_t, s2_t, q_norm, kv_norm, w_uq, w_ukv)


def _row_vector(col, n):
    return jnp.transpose(jnp.broadcast_to(col, (n, LANES)))[0:1, :]


def flash_fwd(q, k, v, *, tq):
    H, T, _ = q.shape
    tq = min(tq, T)
    nq = T // tq
    neg = -1e30

    def body(q_ref, k_ref, v_ref, o_ref, lse_ref):
        qi = pl.program_id(1)
        qb = q_ref[0]

        def tile(j, carry, masked):
            m, l, acc = carry
            rows = pl.ds(pl.multiple_of(j * tq, tq), tq)
            s = _dot_nt(qb, k_ref[0, rows, :])
            if masked:
                r = lax.broadcasted_iota(jnp.int32, (tq, tq), 0)
                c = lax.broadcasted_iota(jnp.int32, (tq, tq), 1)
                s = jnp.where(c <= r, s, neg)
            m_new = jnp.maximum(m, jnp.max(s, axis=-1, keepdims=True))
            alpha = jnp.exp(m - m_new)
            p = jnp.exp(s - m_new)
            l = alpha * l + jnp.sum(p, axis=-1, keepdims=True)
            acc = alpha * acc + _dot(p.astype(BF16), v_ref[0, rows, :])
            return m_new, l, acc

        init = (jnp.full((tq, 1), neg, F32), jnp.zeros((tq, 1), F32), jnp.zeros((tq, V_HEAD), F32))
        carry = lax.fori_loop(0, qi, lambda j, cr: tile(j, cr, False), init)
        m, l, acc = tile(qi, carry, True)
        o_ref[...] = (acc / l).astype(BF16)
        lse_ref[0, 0] = _row_vector(m + jnp.log(l), tq)

    return pl.pallas_call(
        body, name="flash_fwd", grid=(H, nq),
        in_specs=[pl.BlockSpec((1, tq, HEAD_PAD), lambda h, i: (h, i, 0)),
                  pl.BlockSpec((1, T, HEAD_PAD), lambda h, i: (h, 0, 0)),
                  pl.BlockSpec((1, T, V_HEAD), lambda h, i: (h, 0, 0))],
        out_specs=[pl.BlockSpec((tq, V_HEAD), lambda h, i: (i, h)),
                   pl.BlockSpec((1, 1, 1, tq), lambda h, i: (h, i, 0, 0))],
        out_shape=[jax.ShapeDtypeStruct((T, H * V_HEAD), BF16),
                   jax.ShapeDtypeStruct((H, nq, 1, tq), F32)],
        compiler_params=_params("parallel", "arbitrary"),
    )(q, k, v)


def attn_delta(do, o, *, tq):
    T = do.shape[0]
    tq = min(tq, T)
    nq = T // tq
    group = max(g for g in (1, 2, 4) if nq % g == 0)

    def body(do_ref, o_ref, d_ref):
        for b in range(group):
            rows = slice(b * tq, (b + 1) * tq)
            prod = do_ref[rows, :].astype(F32) * o_ref[rows, :].astype(F32)
            d_ref[0, b] = jnp.sum(jnp.transpose(prod), axis=0, keepdims=True)

    return pl.pallas_call(
        body, name="attn_delta", grid=(N_HEADS, nq // group),
        in_specs=[pl.BlockSpec((group * tq, V_HEAD), lambda h, i: (i, h)),
                  pl.BlockSpec((group * tq, V_HEAD), lambda h, i: (i, h))],
        out_specs=pl.BlockSpec((1, group, 1, tq), lambda h, i: (h, i, 0, 0)),
        out_shape=jax.ShapeDtypeStruct((N_HEADS, nq, 1, tq), F32),
        compiler_params=_params("parallel", "parallel"),
    )(do, o)


def flash_bwd(q, k, v, do, lse, delta, *, tq, tk):
    H, T, _ = q.shape
    tq, tk = min(tq, T), min(tk, T)
    nq, nk = T // tq, T // tk
    neg = -1e30

    def body(q_ref, do_ref, lse_ref, dl_ref, k_ref, v_ref, dq_ref, dk_ref, dv_ref, dq_acc):
        j = pl.program_id(1)

        @pl.when(j == 0)
        def _():
            dq_acc[...] = jnp.zeros_like(dq_acc)

        kb = k_ref[0]
        vb = v_ref[0]

        def tile(i, carry, masked):
            dk, dv = carry
            rows = pl.ds(pl.multiple_of(i * tq, tq), tq)
            qb = q_ref[0, rows, :]
            dob = do_ref[rows, :]
            st = _dot_nt(kb, qb)
            if masked:
                key = j * tk + lax.broadcasted_iota(jnp.int32, (tk, tq), 0)
                query = i * tq + lax.broadcasted_iota(jnp.int32, (tk, tq), 1)
                st = jnp.where(key <= query, st, neg)
            pt = jnp.exp(st - lse_ref[0, i])
            dv = dv + _dot(pt.astype(BF16), dob)
            dpt = _dot_nt(vb, dob)
            dst = (pt * (dpt - dl_ref[0, i])).astype(BF16)
            dk = dk + _dot(dst, qb)
            dq_acc[rows, :] += _dot_tn(dst, kb)
            return dk, dv

        first = (j * tk) // tq
        carry = tile(first, (jnp.zeros((tk, HEAD_PAD), F32), jnp.zeros((tk, V_HEAD), F32)), True)
        dk, dv = lax.fori_loop(first + 1, nq, lambda i, cr: tile(i, cr, False), carry)
        dk_ref[0] = dk.astype(BF16)
        dv_ref[0] = dv.astype(BF16)

        @pl.when(j == nk - 1)
        def _():
            dq_ref[0] = dq_acc[...].astype(BF16)

    return pl.pallas_call(
        body, name="flash_bwd", grid=(H, nk),
        in_specs=[pl.BlockSpec((1, T, HEAD_PAD), lambda h, j: (h, 0, 0)),
                  pl.BlockSpec((T, V_HEAD), lambda h, j: (0, h)),
                  pl.BlockSpec((1, nq, 1, tq), lambda h, j: (h, 0, 0, 0)),
                  pl.BlockSpec((1, nq, 1, tq), lambda h, j: (h, 0, 0, 0)),
                  pl.BlockSpec((1, tk, HEAD_PAD), lambda h, j: (h, j, 0)),
                  pl.BlockSpec((1, tk, V_HEAD), lambda h, j: (h, j, 0))],
        out_specs=[pl.BlockSpec((1, T, HEAD_PAD), lambda h, j: (h, 0, 0)),
                   pl.BlockSpec((1, tk, HEAD_PAD), lambda h, j: (h, j, 0)),
                   pl.BlockSpec((1, tk, V_HEAD), lambda h, j: (h, j, 0))],
        out_shape=[jax.ShapeDtypeStruct((H, T, HEAD_PAD), BF16),
                   jax.ShapeDtypeStruct((H, T, HEAD_PAD), BF16),
                   jax.ShapeDtypeStruct((H, T, V_HEAD), BF16)],
        scratch_shapes=[pltpu.VMEM((T, HEAD_PAD), F32)],
        compiler_params=_params("parallel", "arbitrary"),
    )(q, do, lse, delta, k, v)


def mla_bwd(dq, dk, dv, z, cos_t, s1_t, s2_t, q_norm, kv_norm, w_uq, w_ukv, *, tm):
    T = z.shape[0]
    tm = min(tm, T)
    HW = N_HEADS * HEAD_PAD

    def body(dq_ref, dk_ref, dv_ref, cq_ref, ckv_ref, c_ref, s1_ref, s2_ref, qg_ref, kvg_ref,
             wuq_ref, wukv_ref, dcq_ref, dckv_ref, dkr_ref, dqf_ref, dkvf_ref, qn_ref, kvn_ref,
             dqg_ref, dkvg_ref):
        @pl.when(pl.program_id(0) == 0)
        def _():
            dqg_ref[...] = jnp.zeros_like(dqg_ref)
            dkvg_ref[...] = jnp.zeros_like(dkvg_ref)

        c, s1, s2 = c_ref[...], s1_ref[...], s2_ref[...]
        dkpe = jnp.zeros((tm, LANES), F32)
        for h in range(N_HEADS):
            o = h * HEAD_PAD
            dqf_ref[:, o:o + QK_NOPE] = (
                dq_ref[h, :, 0:QK_NOPE].astype(F32) * ATTN_SCALE).astype(BF16)
            dqf_ref[:, o + QK_NOPE:o + HEAD_PAD] = (
                _rope(dq_ref[h, :, QK_NOPE:HEAD_PAD].astype(F32), c, s1, s2, -1.0)
                * ATTN_SCALE).astype(BF16)
            dkvf_ref[:, o:o + QK_NOPE] = dk_ref[h, :, 0:QK_NOPE]
            dkvf_ref[:, o + QK_NOPE:o + HEAD_PAD] = dv_ref[h]
            dkpe = dkpe + dk_ref[h, :, QK_NOPE:HEAD_PAD].astype(F32)
        dkr_ref[...] = _rope(dkpe, c, s1, s2, -1.0).astype(BF16)

        cq = cq_ref[...]
        qn_ref[...] = _rms_fwd(cq, qg_ref[...]).astype(BF16)
        dcq, dgrows = _rms_bwd(cq, qg_ref[...], _dot_nt(dqf_ref[...], wuq_ref[...]))
        dcq_ref[...] = dcq.astype(BF16)
        _accum_rows(dqg_ref, dgrows)

        ckv = ckv_ref[...]
        kvn_ref[...] = _rms_fwd(ckv, kvg_ref[...]).astype(BF16)
        dckv, dgrows = _rms_bwd(ckv, kvg_ref[...], _dot_nt(dkvf_ref[...], wukv_ref[...]))
        dckv_ref[...] = dckv.astype(BF16)
        _accum_rows(dkvg_ref, dgrows)

    tok = lambda w: pl.BlockSpec((tm, w), lambda i: (i, 0))
    head = lambda w: pl.BlockSpec((N_HEADS, tm, w), lambda i: (0, i, 0))
    return pl.pallas_call(
        body, name="mla_bwd", grid=(T // tm,),
        in_specs=[head(HEAD_PAD), head(HEAD_PAD), head(V_HEAD),
                  pl.BlockSpec((tm, Q_LORA), lambda i: (i, OFF_CQ // Q_LORA)),
                  pl.BlockSpec((tm, KV_LORA), lambda i: (i, OFF_CKV // KV_LORA)),
                  tok(LANES), tok(LANES), tok(LANES),
                  _full((1, Q_LORA)), _full((1, KV_LORA)),
                  _full((Q_LORA, HW)), _full((KV_LORA, HW))],
        out_specs=[tok(Q_LORA), tok(KV_LORA), tok(LANES), tok(HW), tok(HW), tok(Q_LORA),
                   tok(KV_LORA), _full((1, Q_LORA)), _full((1, KV_LORA))],
        out_shape=[jax.ShapeDtypeStruct((T, Q_LORA), BF16),
                   jax.ShapeDtypeStruct((T, KV_LORA), BF16),
                   jax.ShapeDtypeStruct((T, LANES), BF16),
                   jax.ShapeDtypeStruct((T, HW), BF16),
                   jax.ShapeDtypeStruct((T, HW), BF16),
                   jax.ShapeDtypeStruct((T, Q_LORA), BF16),
                   jax.ShapeDtypeStruct((T, KV_LORA), BF16),
                   jax.ShapeDtypeStruct((1, Q_LORA), F32),
                   jax.ShapeDtypeStruct((1, KV_LORA), F32)],
        compiler_params=_params("arbitrary"),
    )(dq, dk, dv, z, z, cos_t, s1_t, s2_t, q_norm, kv_norm, w_uq, w_ukv)


def merge_fwd(o, ya, z, x, w_b, w_o, *, tm):
    T = x.shape[0]
    tm = min(tm, T)

    def body(o_ref, ya_ref, ga_ref, gb_ref, x_ref, wb_ref, wo_ref, yb_ref, mg_ref, x1_ref):
        yb = _dot(o_ref[...], wb_ref[...])
        yb_ref[...] = yb
        merged = (_sigmoid(ga_ref[...]) * ya_ref[...] + _sigmoid(gb_ref[...]) * yb).astype(BF16)
        mg_ref[...] = merged
        x1_ref[...] = x_ref[...] + _dot(merged, wo_ref[...])

    tok = pl.BlockSpec((tm, D_MODEL), lambda i: (i, 0))
    return pl.pallas_call(
        body, name="merge_fwd", grid=(T // tm,),
        in_specs=[tok, tok, pl.BlockSpec((tm, D_MODEL), lambda i: (i, OFF_GA // D_MODEL)),
                  pl.BlockSpec((tm, D_MODEL), lambda i: (i, OFF_GB // D_MODEL)), tok,
                  _full((D_MODEL, D_MODEL)), _full((D_MODEL, D_MODEL))],
        out_specs=[tok, tok, tok],
        out_shape=[jax.ShapeDtypeStruct((T, D_MODEL), F32),
                   jax.ShapeDtypeStruct((T, D_MODEL), BF16),
                   jax.ShapeDtypeStruct((T, D_MODEL), F32)],
        compiler_params=_params("parallel"),
    )(o, ya, z, z, x, w_b, w_o)


def merge_bwd(dx1, ya, yb, z, w_o, w_b, dep, *, tm):
    T = dx1.shape[0]
    tm = min(tm, T)

    def body(dx_ref, ya_ref, yb_ref, ga_ref, gb_ref, wo_ref, wb_ref, dep_ref, dya_ref, dyb_ref,
             do_ref, dga_ref, dgb_ref):
        dm = _dot_nt(dx_ref[...].astype(BF16), wo_ref[...])
        sa = _sigmoid(ga_ref[...])
        sb = _sigmoid(gb_ref[...])
        dya_ref[...] = (dm * sa).astype(BF16)
        dyb = (dm * sb).astype(BF16)
        dyb_ref[...] = dyb
        dga_ref[...] = (dm * ya_ref[...] * sa * (1.0 - sa)).astype(BF16)
        dgb_ref[...] = (dm * yb_ref[...] * sb * (1.0 - sb)).astype(BF16)
        do_ref[...] = _dot_nt(dyb, wb_ref[...]).astype(BF16)

    tok = pl.BlockSpec((tm, D_MODEL), lambda i: (i, 0))
    return pl.pallas_call(
        body, name="merge_bwd", grid=(T // tm,),
        in_specs=[tok, tok, tok, pl.BlockSpec((tm, D_MODEL), lambda i: (i, OFF_GA // D_MODEL)),
                  pl.BlockSpec((tm, D_MODEL), lambda i: (i, OFF_GB // D_MODEL)),
                  _full((D_MODEL, D_MODEL)), _full((D_MODEL, D_MODEL)),
                  pl.BlockSpec(memory_space=pl.ANY)],
        out_specs=[tok] * 5,
        out_shape=[jax.ShapeDtypeStruct((T, D_MODEL), BF16)] * 5,
        compiler_params=_params("parallel"),
    )(dx1, ya, yb, z, z, w_o, w_b, dep)


def in_bwd(dx1, x, g, segs, w_in, *, tm):
    T = x.shape[0]
    tm = min(tm, T)
    widths = [s.shape[1] for s, _ in segs]
    offs = [o for _, o in segs]
    n = len(segs)

    def body(*refs):
        dx1_ref, x_ref, g_ref = refs[:3]
        seg_refs = refs[3:3 + n]
        w_ref = refs[3 + n]
        dx_ref, h_ref, dg_ref = refs[4 + n:]

        @pl.when(pl.program_id(0) == 0)
        def _():
            dg_ref[...] = jnp.zeros_like(dg_ref)

        dh = jnp.zeros((tm, D_MODEL), F32)
        for s_ref, off, w in zip(seg_refs, offs, widths):
            dh = dh + _dot_nt(s_ref[...], w_ref[:, off:off + w])
        xv = x_ref[...]
        h_ref[...] = _rms_fwd(xv, g_ref[...]).astype(BF16)
        dxn, dgrows = _rms_bwd(xv, g_ref[...], dh)
        dx_ref[...] = dx1_ref[...] + dxn
        _accum_rows(dg_ref, dgrows)

    tok = pl.BlockSpec((tm, D_MODEL), lambda i: (i, 0))
    return pl.pallas_call(
        body, name="in_bwd", grid=(T // tm,),
        in_specs=[tok, tok, _full((1, D_MODEL))]
        + [pl.BlockSpec((tm, w), lambda i: (i, 0)) for w in widths] + [_full((D_MODEL, IN_PAD))],
        out_specs=[tok, tok, _full((1, D_MODEL))],
        out_shape=[jax.ShapeDtypeStruct((T, D_MODEL), F32),
                   jax.ShapeDtypeStruct((T, D_MODEL), BF16),
                   jax.ShapeDtypeStruct((1, D_MODEL), F32)],
        compiler_params=_params("arbitrary"),
    )(dx1, x, g, *[s for s, _ in segs], w_in)


def _pair_cols(w_ref):
    return jnp.concatenate([w_ref[0], w_ref[1]], axis=1)


def _pair_rows(w_ref):
    return jnp.concatenate([w_ref[0], w_ref[1]], axis=0)


def ffn_fwd(x1, g, w_gate, w_up, w_down, *, tm):
    T = x1.shape[0]
    tm = min(tm, T)

    def body(x_ref, g_ref, wg_ref, wu_ref, wd_ref, gp_ref, up_ref, x2_ref, h_ref, acc_ref):
        f = pl.program_id(1)

        @pl.when(f == 0)
        def _():
            h_ref[...] = _rms_fwd(x_ref[...], g_ref[...]).astype(BF16)
            acc_ref[...] = jnp.zeros_like(acc_ref)

        gp = _dot(h_ref[...], _pair_cols(wg_ref))
        up = _dot(h_ref[...], _pair_cols(wu_ref))
        gp_ref[0] = gp
        up_ref[0] = up
        act = (gp * _sigmoid(gp) * up).astype(BF16)
        acc_ref[...] += _dot(act, _pair_rows(wd_ref))

        @pl.when(f == FF_PAIRS - 1)
        def _():
            x2_ref[...] = x_ref[...] + acc_ref[...]

    tok = pl.BlockSpec((tm, D_MODEL), lambda i, f: (i, 0))
    ff = pl.BlockSpec((1, tm, FF_STEP), lambda i, f: (f, i, 0))
    w_col = pl.BlockSpec((2, D_MODEL, FF_PAD), lambda i, f: (f, 0, 0))
    w_row = pl.BlockSpec((2, FF_PAD, D_MODEL), lambda i, f: (f, 0, 0))
    return pl.pallas_call(
        body, name="ffn_fwd", grid=(T // tm, FF_PAIRS),
        in_specs=[tok, _full((1, D_MODEL)), w_col, w_col, w_row],
        out_specs=[ff, ff, tok],
        out_shape=[jax.ShapeDtypeStruct((FF_PAIRS, T, FF_STEP), F32),
                   jax.ShapeDtypeStruct((FF_PAIRS, T, FF_STEP), F32),
                   jax.ShapeDtypeStruct((T, D_MODEL), F32)],
        scratch_shapes=[pltpu.VMEM((tm, D_MODEL), BF16), pltpu.VMEM((tm, D_MODEL), F32)],
        compiler_params=_params("parallel", "arbitrary"),
    )(x1, g, w_gate, w_up, w_down)


def ffn_bwd(dx2, x1, g, gpre, upre, w_gate, w_up, w_down, *, tm):
    T = x1.shape[0]
    tm = min(tm, T)

    def body(dx_ref, x_ref, g_ref, gp_ref, up_ref, wg_ref, wu_ref, wd_ref,
             act_ref, dg_ref, du_ref, h_ref, dx1_ref, dgain_ref, dxb_ref, acc_ref):
        f = pl.program_id(1)

        @pl.when((pl.program_id(0) == 0) & (f == 0))
        def _():
            dgain_ref[...] = jnp.zeros_like(dgain_ref)

        @pl.when(f == 0)
        def _():
            dxb_ref[...] = dx_ref[...].astype(BF16)
            acc_ref[...] = jnp.zeros_like(acc_ref)

        gp = gp_ref[0]
        up = up_ref[0]
        dact = _dot_nt(dxb_ref[...], _pair_rows(wd_ref))
        sg = _sigmoid(gp)
        silu = gp * sg
        act_ref[0] = (silu * up).astype(BF16)
        dgp = (dact * up * (sg * (1.0 + gp * (1.0 - sg)))).astype(BF16)
        dup = (dact * silu).astype(BF16)
        dg_ref[0] = dgp
        du_ref[0] = dup
        acc_ref[...] += _dot_nt(dgp, _pair_cols(wg_ref)) + _dot_nt(dup, _pair_cols(wu_ref))

        @pl.when(f == FF_PAIRS - 1)
        def _():
            xv = x_ref[...]
            h_ref[...] = _rms_fwd(xv, g_ref[...]).astype(BF16)
            dxn, dgrows = _rms_bwd(xv, g_ref[...], acc_ref[...])
            dx1_ref[...] = dx_ref[...] + dxn
            _accum_rows(dgain_ref, dgrows)

    tok = pl.BlockSpec((tm, D_MODEL), lambda i, f: (i, 0))
    ff = pl.BlockSpec((1, tm, FF_STEP), lambda i, f: (f, i, 0))
    w_col = pl.BlockSpec((2, D_MODEL, FF_PAD), lambda i, f: (f, 0, 0))
    w_row = pl.BlockSpec((2, FF_PAD, D_MODEL), lambda i, f: (f, 0, 0))
    return pl.pallas_call(
        body, name="ffn_bwd", grid=(T // tm, FF_PAIRS),
        in_specs=[tok, tok, _full((1, D_MODEL)), ff, ff, w_col, w_col, w_row],
        out_specs=[ff, ff, ff, tok, tok, _full((1, D_MODEL))],
        out_shape=[jax.ShapeDtypeStruct((FF_PAIRS, T, FF_STEP), BF16)] * 3
        + [jax.ShapeDtypeStruct((T, D_MODEL), BF16), jax.ShapeDtypeStruct((T, D_MODEL), F32),
           jax.ShapeDtypeStruct((1, D_MODEL), F32)],
        scratch_shapes=[pltpu.VMEM((tm, D_MODEL), BF16), pltpu.VMEM((tm, D_MODEL), F32)],
        compiler_params=_params("arbitrary", "arbitrary"),
    )(dx2, x1, g, gpre, upre, w_gate, w_up, w_down)


def ple_fwd(x2, p, g, w_pg, w_ple, *, tm):
    T = x2.shape[0]
    tm = min(tm, T)

    def body(x_ref, p_ref, g_ref, wpg_ref, wple_ref, l_ref, pe_ref, x3_ref):
        xv = x_ref[...]
        logits = _dot(_rms_fwd(xv, g_ref[...]).astype(BF16), wpg_ref[...])
        pe = _dot(p_ref[...].astype(BF16), wple_ref[...])
        l_ref[...] = logits
        pe_ref[...] = pe
        x3_ref[...] = xv + _sigmoid(logits) * pe

    tok = pl.BlockSpec((tm, D_MODEL), lambda i: (i, 0))
    return pl.pallas_call(
        body, name="ple_fwd", grid=(T // tm,),
        in_specs=[tok, pl.BlockSpec((tm, PLE_DIM), lambda i: (i, 0)), _full((1, D_MODEL)),
                  _full((D_MODEL, D_MODEL)), _full((PLE_DIM, D_MODEL))],
        out_specs=[tok, tok, tok],
        out_shape=[jax.ShapeDtypeStruct((T, D_MODEL), F32)] * 3,
        compiler_params=_params("parallel"),
    )(x2, p, g, w_pg, w_ple)


def ple_bwd(dx3, logits, pe, x2, g, w_pg, *, tm):
    T = x2.shape[0]
    tm = min(tm, T)

    def body(dx_ref, l_ref, pe_ref, x_ref, g_ref, wpg_ref, dl_ref, dpe_ref, h_ref, dx2_ref,
             dg_ref):
        @pl.when(pl.program_id(0) == 0)
        def _():
            dg_ref[...] = jnp.zeros_like(dg_ref)

        dx = dx_ref[...]
        s = _sigmoid(l_ref[...])
        dpe_ref[...] = (dx * s).astype(BF16)
        dl = (dx * pe_ref[...] * s * (1.0 - s)).astype(BF16)
        dl_ref[...] = dl
        xv = x_ref[...]
        h_ref[...] = _rms_fwd(xv, g_ref[...]).astype(BF16)
        dxn, dgrows = _rms_bwd(xv, g_ref[...], _dot_nt(dl, wpg_ref[...]))
        dx2_ref[...] = dx + dxn
        _accum_rows(dg_ref, dgrows)

    tok = pl.BlockSpec((tm, D_MODEL), lambda i: (i, 0))
    return pl.pallas_call(
        body, name="ple_bwd", grid=(T // tm,),
        in_specs=[tok, tok, tok, tok, _full((1, D_MODEL)), _full((D_MODEL, D_MODEL))],
        out_specs=[tok, tok, tok, tok, _full((1, D_MODEL))],
        out_shape=[jax.ShapeDtypeStruct((T, D_MODEL), BF16)] * 3
        + [jax.ShapeDtypeStruct((T, D_MODEL), F32), jax.ShapeDtypeStruct((1, D_MODEL), F32)],
        compiler_params=_params("arbitrary"),
    )(dx3, logits, pe, x2, g, w_pg)


def loss_head(x, g, target, *, tm):
    T = x.shape[0]
    tm = min(tm, T)

    def body(x_ref, g_ref, t_ref, loss_ref, dx_ref, dg_ref):
        @pl.when(pl.program_id(0) == 0)
        def _():
            loss_ref[...] = jnp.zeros_like(loss_ref)
            dg_ref[...] = jnp.zeros_like(dg_ref)

        xv = x_ref[...]
        err = _rms_fwd(xv, g_ref[...]) - t_ref[...]
        part = 0.5 * jnp.sum(jnp.mean(err * err, axis=-1, keepdims=True), axis=0, keepdims=True)
        lane = lax.broadcasted_iota(jnp.int32, (1, LANES), 1)
        loss_ref[...] += jnp.where(lane == 0, part, 0.0)
        dxn, dgrows = _rms_bwd(xv, g_ref[...], err * (1.0 / D_MODEL))
        dx_ref[...] = dxn
        _accum_rows(dg_ref, dgrows)

    tok = pl.BlockSpec((tm, D_MODEL), lambda i: (i, 0))
    return pl.pallas_call(
        body, name="loss_head", grid=(T // tm,),
        in_specs=[tok, _full((1, D_MODEL)), tok],
        out_specs=[_full((1, LANES)), tok, _full((1, D_MODEL))],
        out_shape=[jax.ShapeDtypeStruct((1, LANES), F32), jax.ShapeDtypeStruct((T, D_MODEL), F32),
                   jax.ShapeDtypeStruct((1, D_MODEL), F32)],
        compiler_params=_params("arbitrary"),
    )(x, g, target)


def _peers():
    x, y, c = lax.axis_index("x"), lax.axis_index("y"), lax.axis_index("c")
    me = 4 * x + 2 * y + c
    out = []
    for d in range(1, N_DEV):
        px = 1 - x if d & 4 else x
        py = 1 - y if d & 2 else y
        pc = 1 - c if d & 1 else c
        out.append(((px, py, pc), 4 * px + 2 * py + pc))
    return me, out


_HBM = pl.BlockSpec(memory_space=pltpu.HBM)
_SEM = pl.BlockSpec(memory_space=pltpu.SEMAPHORE)
_ANY = pl.BlockSpec(memory_space=pl.ANY)
_DATAFLOW = pltpu.SideEffectType.DATAFLOW_SIDE_EFFECTING


def _exchange_copy(t, d, pos, idx, me, src_ref, scatter, land_ref, send_sems, recv_sems):
    k = t * (N_DEV - 1) + d
    return pltpu.make_async_remote_copy(
        src_ref=src_ref.at[idx] if scatter else src_ref, dst_ref=land_ref.at[me],
        send_sem=send_sems.at[k], recv_sem=recv_sems.at[k],
        device_id=pos, device_id_type=pl.DeviceIdType.MESH)


def _own_slot_copy(t, me, src_ref, scatter, land_ref, local_sems):
    return pltpu.make_async_copy(src_ref.at[me] if scatter else src_ref, land_ref.at[me],
                                 local_sems.at[t])


def exchange_start(gathers, scatters, dep, *, name):
    srcs = [pltpu.with_memory_space_constraint(a, pltpu.HBM) for a in list(gathers) + list(scatters)]
    kinds = [False] * len(gathers) + [True] * len(scatters)
    lands = [pltpu.with_memory_space_constraint(
        lax.empty((N_DEV,) + (a.shape[1:] if sc else a.shape), a.dtype), pltpu.HBM)
        for a, sc in zip(srcs, kinds)]
    n = len(srcs)

    def body(*refs):
        src_refs, land_refs = refs[:n], refs[n:2 * n]
        send_sems, recv_sems, local_sems = refs[-2 * n - 4:-2 * n - 1]
        token = refs[-1]
        me, peers = _peers()
        for t in range(n):
            _own_slot_copy(t, me, src_refs[t], kinds[t], land_refs[t], local_sems).start()
        for t in range(n):
            for d, (pos, idx) in enumerate(peers):
                _exchange_copy(t, d, pos, idx, me, src_refs[t], kinds[t], land_refs[t],
                               send_sems, recv_sems).start()
        token[...] = jnp.zeros_like(token)

    sem = pltpu.SemaphoreType.DMA((n * (N_DEV - 1),))
    thru = [pltpu.HBM(a.shape, a.dtype) for a in srcs + lands]
    outs = pl.pallas_call(
        body, name=name,
        in_specs=[_HBM] * (2 * n) + ([_ANY] if dep is not None else []),
        out_specs=[_SEM] * 3 + [_HBM] * (2 * n) + [pl.BlockSpec(memory_space=pltpu.VMEM)],
        out_shape=[sem, sem, pltpu.SemaphoreType.DMA((n,))] + thru
        + [jax.ShapeDtypeStruct((8, LANES), F32)],
        input_output_aliases={i: 3 + i for i in range(2 * n)},
        compiler_params=pltpu.CompilerParams(has_side_effects=_DATAFLOW),
    )(*srcs, *lands, *([dep] if dep is not None else []))
    return dict(sems=outs[:3], srcs=outs[3:3 + n], lands=outs[3 + n:3 + 2 * n], token=outs[-1],
                kinds=kinds)


def exchange_wait(handle, after, *, name):
    kinds = handle["kinds"]
    n = len(kinds)

    def body(*refs):
        src_refs, land_refs = refs[:n], refs[n:2 * n]
        send_sems, recv_sems, local_sems = refs[2 * n:2 * n + 3]
        me, peers = _peers()
        for t in range(n):
            _own_slot_copy(t, me, src_refs[t], kinds[t], land_refs[t], local_sems).wait()
            for d, (pos, idx) in enumerate(peers):
                cp = _exchange_copy(t, d, pos, idx, me, src_refs[t], kinds[t], land_refs[t],
                                    send_sems, recv_sems)
                cp.wait_send()
                cp.wait_recv()

    arrays = list(handle["srcs"]) + list(handle["lands"])
    outs = pl.pallas_call(
        body, name=name,
        in_specs=[_HBM] * (2 * n) + [_SEM] * 3 + ([_ANY] if after is not None else []),
        out_specs=[_HBM] * (2 * n),
        out_shape=[pltpu.HBM(a.shape, a.dtype) for a in arrays],
        input_output_aliases={i: i for i in range(2 * n)},
        compiler_params=pltpu.CompilerParams(has_side_effects=_DATAFLOW),
    )(*arrays, *handle["sems"], *([after] if after is not None else []))
    return outs[n:]


def _row_block(rows, cols):
    best = None
    for rb in range(16, rows + 1, 16):
        if rows % rb == 0 and rb * cols <= 256 * 1024:
            best = rb
    return best or rows


def reduce_adamw(parts, w, m, v, *, name):
    nl = len(parts)
    C = w.shape[1]
    R, cp = parts[0].shape[1:]
    rb = _row_block(R, cp)
    nb = R // rb

    def body(*refs):
        p_refs = refs[:nl]
        w_ref, m_ref, v_ref, g_ref, d_ref, m2_ref, v2_ref = refs[nl:]

        def total(p_ref):
            g = p_ref[0, :, 0:C].astype(F32)
            for k in range(1, N_DEV):
                g = g + p_ref[k, :, 0:C].astype(F32)
            return g

        g = total(p_refs[0])
        for layer in range(1, nl):
            g = jnp.where(pl.program_id(0) == layer, total(p_refs[layer]), g)
        g_ref[...] = g
        m2 = ADAM_B1 * m_ref[...] + (1.0 - ADAM_B1) * g
        v2 = ADAM_B2 * v_ref[...] + (1.0 - ADAM_B2) * (g * g)
        m2_ref[...] = m2
        v2_ref[...] = v2
        m_hat = m2 / (1.0 - ADAM_B1 ** ADAM_STEP)
        v_hat = v2 / (1.0 - ADAM_B2 ** ADAM_STEP)
        d_ref[...] = -ADAM_LR * (m_hat / (jnp.sqrt(v_hat) + ADAM_EPS) + ADAM_WD * w_ref[...])

    def part_spec(layer):
        return pl.BlockSpec((N_DEV, rb, cp), lambda l, i: (0, jnp.where(l == layer, i, 0), 0))

    blk = pl.BlockSpec((rb, C), lambda l, i: (l * nb + i, 0))
    return pl.pallas_call(
        body, name=name, grid=(nl, nb),
        in_specs=[part_spec(layer) for layer in range(nl)] + [blk, blk, blk],
        out_specs=[blk] * 4,
        out_shape=[jax.ShapeDtypeStruct((nl * R, C), F32)] * 4,
        compiler_params=_params("arbitrary", "arbitrary"),
    )(*parts, w, m, v)


def _cols_to_pieces(full):
    r = full.shape[0]
    return jnp.transpose(full.reshape(r, N_DEV, -1), (1, 0, 2))


def _pieces_to_cols(pieces):
    r = pieces.shape[1]
    return jnp.transpose(pieces, (1, 0, 2)).reshape(r, -1)


def _rope_tables(positions):
    inv_freq = 1.0 / (ROPE_THETA ** (jnp.arange(0, QK_ROPE, 2, dtype=F32) / QK_ROPE))
    ang = positions.astype(F32)[:, None] * inv_freq
    cos, sin = jnp.cos(ang), jnp.sin(ang)
    zero = jnp.zeros_like(cos)
    return (jnp.concatenate([cos, cos, zero, zero], axis=1),
            jnp.concatenate([-sin, zero, zero, zero], axis=1),
            jnp.concatenate([zero, sin, zero, zero], axis=1))


def kernel(x, p, positions, norm_mix, w_in, w_pool, pool_scale, q_norm, kv_norm, w_uq, w_ukv, w_a, w_b, w_o, norm_ffn, w_gate, w_up, w_down, norm_ple, w_ple_gate, w_ple, final_norm, loss_target, m_norm_mix, m_w_in, m_w_pool, m_pool_scale, m_q_norm, m_kv_norm, m_w_uq, m_w_ukv, m_w_a, m_w_b, m_w_o, m_norm_ffn, m_w_gate, m_w_up, m_w_down, m_norm_ple, m_w_ple_gate, m_w_ple, m_final_norm, v_norm_mix, v_w_in, v_w_pool, v_pool_scale, v_q_norm, v_kv_norm, v_w_uq, v_w_ukv, v_w_a, v_w_b, v_w_o, v_norm_ffn, v_w_gate, v_w_up, v_w_down, v_norm_ple, v_w_ple_gate, v_w_ple, v_final_norm):
    given = dict(locals())
    shard = {n: given[n] for n in WEIGHTS}
    TM = 512
    TQ = 512

    def send_form(n, i):
        w = shard[n][i].astype(BF16)
        if n == "w_in":
            return jnp.pad(w, ((0, 0), (0, PIECE_PAD - W_IN_PIECE)))
        if n == "w_uq":
            w = jnp.pad(w, ((0, 0), (0, 0), (0, HEAD_PAD - QK_HEAD)))
        if n in ("w_uq", "w_ukv"):
            return w.reshape(-1, N_HEADS * HEAD_PAD)
        if n in ("w_gate", "w_up"):
            return jnp.pad(w, ((0, 0), (0, FF_PAD - FF_PIECE)))
        if n == "w_down":
            return jnp.pad(w, ((0, FF_PAD - FF_PIECE), (0, 0)))
        return w

    names = [n for n, _ in SHARDED]
    FIRST = ("w_in", "w_uq", "w_ukv", "w_a", "w_b", "w_o")
    LATER = ("w_gate", "w_up", "w_down", "w_ple_gate", "w_ple")
    start_a0 = exchange_start([send_form("w_in", 0)], [], None, name="gather_start_a0")
    got_a0 = exchange_wait(start_a0, None, name="gather_wait_a0")
    start_a1 = exchange_start([send_form(n, 0) for n in FIRST[1:]], [], got_a0[0],
                              name="gather_start_a1")
    gathered = {("w_in", 0): got_a0[0]}
    row_map = jnp.asarray(_piece_row_map())

    low_np, up_np = _band_matrices()
    low, up = jnp.asarray(low_np, BF16), jnp.asarray(up_np, BF16)
    low_t = jnp.asarray(low_np.transpose(0, 2, 1), BF16)
    up_t = jnp.asarray(up_np.transpose(0, 2, 1), BF16)
    cos_t, s1_t, s2_t = _rope_tables(positions[0])

    def mix_weights(i):
        g = lambda n: gathered[n, i]
        return dict(
            w_uq=g("w_uq").reshape(Q_LORA, N_HEADS * HEAD_PAD),
            w_ukv=g("w_ukv").reshape(KV_LORA, N_HEADS * HEAD_PAD),
            w_a=_pieces_to_cols(g("w_a")), w_b=g("w_b").reshape(D_MODEL, D_MODEL),
            w_o=g("w_o").reshape(D_MODEL, D_MODEL),
            w_pool=w_pool[i].astype(BF16),
            pool_scale=pool_scale[i][None], norm_mix=norm_mix[i][None], q_norm=q_norm[i][None],
            kv_norm=kv_norm[i][None], norm_ffn=norm_ffn[i][None], norm_ple=norm_ple[i][None])

    def channel_weights(i):
        g = lambda n: gathered[n, i]
        return dict(w_gate=g("w_gate"), w_up=g("w_up"), w_down=g("w_down"),
                    w_pg=g("w_ple_gate").reshape(D_MODEL, D_MODEL),
                    w_ple=_pieces_to_cols(g("w_ple")))

    def behind(gain, start):
        return gain + start["token"][0, 0]

    layers, saved = [], []
    starts = {}
    for i in range(DEPTH):
        if i == 1:
            gathered.update({(n, 1): a for n, a in zip(
                FIRST, exchange_wait(starts["b2"], saved[0]["logits"], name="gather_wait_b2"))})
        w_in_i = assemble_w_in(gathered["w_in", i], row_map, tr=512)
        g_mix = norm_mix[i][None]
        z = rms_matmul(x[0] if i == 0 else xs, behind(g_mix, start_a1) if i == 0 else g_mix,
                       w_in_i, tm=TM, tn=IN_PAD, name="in_proj")
        if i == 0:
            got_a1 = exchange_wait(start_a1, z, name="gather_wait_a1")
            gathered.update({(n, 0): a for n, a in zip(FIRST[1:], got_a1)})
            start_b1 = exchange_start([send_form(n, 0) for n in LATER], [], got_a1[0],
                                      name="gather_start_b1")
        L = dict(mix_weights(i), w_in=w_in_i)
        g_pool = behind(L["pool_scale"], start_b1) if i == 0 else L["pool_scale"]
        pooled, ms, ya = pool_fwd(z, low, up, L["w_pool"], g_pool, L["w_a"], tm=TM)
        g_q = L["q_norm"]
        if i == 0:
            starts["b2"] = exchange_start([send_form(n, 1) for n in FIRST], [], ya,
                                          name="gather_start_b2")
            g_q = behind(g_q, starts["b2"])
        q, k, v = mla_prep(z, cos_t, s1_t, s2_t, g_q, L["kv_norm"], L["w_uq"], L["w_ukv"], tm=TM)
        o, lse = flash_fwd(q, k, v, tq=2 * TQ)
        yb, merged, x1 = merge_fwd(o, ya, z, x[0] if i == 0 else xs, L["w_b"], L["w_o"], tm=TM)
        g_ffn = L["norm_ffn"]
        if i == 0:
            got = exchange_wait(start_b1, o, name="gather_wait_b1")
            starts["b3"] = exchange_start([send_form(n, 1) for n in LATER], [], x1,
                                          name="gather_start_b3")
            g_ffn = behind(g_ffn, starts["b3"])
        else:
            got = exchange_wait(starts["b3"], o, name="gather_wait_b3")
        gathered.update({(n, i): a for n, a in zip(LATER, got)})
        L.update(channel_weights(i))
        gpre, upre, x2 = ffn_fwd(x1, g_ffn, L["w_gate"], L["w_up"], L["w_down"], tm=TM)
        logits, pe, x3 = ple_fwd(x2, p[i, 0], L["norm_ple"], L["w_pg"], L["w_ple"], tm=TM)
        layers.append(L)
        saved.append(dict(x=x[0] if i == 0 else xs, z=z, pooled=pooled, ms=ms, ya=ya, q=q, k=k,
                          v=v, o=o, lse=lse, yb=yb, merged=merged, x1=x1, gpre=gpre, upre=upre,
                          x2=x2, logits=logits, pe=pe))
        xs = x3

    loss_part, dx, d_final = loss_head(xs, final_norm[None], loss_target[0], tm=TM)

    grads = {n: [None] * DEPTH for n in REPLICATED if n != "final_norm"}
    pieces = {n: [None] * DEPTH for n in names}
    tn_mm = functools.partial(matmul_tn, tn=1024, tk=1024)
    row_pieces = lambda a: a.reshape(N_DEV, -1, a.shape[-1])
    MIDDLE = ("w_o", "w_b", "w_uq", "w_ukv")
    LAST = ("w_a", "w_in")
    scatters = []
    for i in reversed(range(DEPTH)):
        L, S = layers[i], saved[i]
        g_ple = L["norm_ple"] if not scatters else behind(L["norm_ple"], scatters[-1][0])
        dl, dpe, hn, dx2, d_norm_ple = ple_bwd(dx, S["logits"], S["pe"], S["x2"], g_ple,
                                               L["w_pg"], tm=TM)
        grads["norm_ple"][i] = d_norm_ple[0]
        pieces["w_ple_gate"][i] = row_pieces(tn_mm(hn, dl, name="dw_ple_gate"))
        pieces["w_ple"][i] = _cols_to_pieces(tn_mm(p[i, 0], dpe, name="dw_ple"))

        act, dgp, dup, h2, dx1, d_norm_ffn = ffn_bwd(dx2, S["x1"], L["norm_ffn"], S["gpre"],
                                                     S["upre"], L["w_gate"], L["w_up"],
                                                     L["w_down"], tm=TM)
        grads["norm_ffn"][i] = d_norm_ffn[0]
        pieces["w_down"][i] = matmul_tn_pieces(act, dx2, tk=512, name="dw_down")
        pieces["w_gate"][i] = matmul_tn_pieces(h2, dgp, tk=512, name="dw_gate")
        pieces["w_up"][i] = matmul_tn_pieces(h2, dup, tk=512, name="dw_up")
        scatters.append((exchange_start([], [pieces[n][i] for n in LATER], None,
                                        name=f"scatter_start_{i}a"), LATER, i))

        dya, dyb, do, dga, dgb = merge_bwd(dx1, S["ya"], S["yb"], S["z"], L["w_o"], L["w_b"],
                                           scatters[-1][0]["token"], tm=TM)
        pieces["w_o"][i] = row_pieces(tn_mm(S["merged"], dx1, name="dw_o"))
        pieces["w_b"][i] = row_pieces(tn_mm(S["o"], dyb, name="dw_b"))

        TQB = min(2 * TQ, do.shape[0])
        per_tile = lambda a: a.reshape(N_HEADS, -1, 1, TQB)
        delta = attn_delta(do, S["o"], tq=TQ)
        dq, dk, dv = flash_bwd(S["q"], S["k"], S["v"], do, per_tile(S["lse"]), per_tile(delta),
                               tq=TQB, tk=2 * TQ)
        dcq, dckv, dkr, dqf, dkvf, qn, kvn, d_q_norm, d_kv_norm = mla_bwd(
            dq, dk, dv, S["z"], cos_t, s1_t, s2_t, L["q_norm"], L["kv_norm"], L["w_uq"],
            L["w_ukv"], tm=TM)
        grads["q_norm"][i] = d_q_norm[0]
        grads["kv_norm"][i] = d_kv_norm[0]
        d_w_uq = tn_mm(qn, dqf, name="dw_uq")
        pieces["w_uq"][i] = d_w_uq.reshape(N_DEV, -1, N_HEADS, HEAD_PAD)[..., :QK_HEAD]
        pieces["w_ukv"][i] = tn_mm(kvn, dkvf, name="dw_ukv").reshape(
            N_DEV, -1, N_HEADS, QK_NOPE + V_HEAD)
        scatters.append((exchange_start([], [pieces[n][i] for n in MIDDLE], None,
                                        name=f"scatter_start_{i}b"), MIDDLE, i))

        dpc, d_pool_scale, d_w_pool = pool_bwd_a(dya, S["pooled"], L["w_a"], L["w_pool"],
                                                 behind(L["pool_scale"], scatters[-1][0]), tm=TM)
        grads["pool_scale"][i] = d_pool_scale[0]
        grads["w_pool"][i] = d_w_pool
        pieces["w_a"][i] = _cols_to_pieces(tn_mm(S["ms"], dya, name="dw_a"))
        du = pool_bwd_b(dpc, low_t, up_t, tm=TM)

        segs = [(dga, OFF_GA), (dgb, OFF_GB), (du, OFF_U), (dcq, OFF_CQ), (dckv, OFF_CKV),
                (dkr, OFF_KR)]
        dx, h, d_norm_mix = in_bwd(dx1, S["x"], L["norm_mix"], segs, L["w_in"], tm=TM)
        grads["norm_mix"][i] = d_norm_mix[0]
        pieces["w_in"][i] = split_dw_in(dw_in_proj(h, segs, tk=512), row_map, tr=512)
        if i == 1:
            scatters.append((exchange_start([], [pieces[n][i] for n in LAST], None,
                                            name="scatter_start_1c"), LAST, i))

    small = {n: jnp.stack(g) for n, g in grads.items()}
    small["final_norm"] = d_final[0]
    rep = jnp.concatenate([small[n].reshape(-1) for n in REPLICATED] + [loss_part.reshape(-1)])
    n_small = -(-rep.shape[0] // (8 * FLAT_COLS)) * (8 * FLAT_COLS)

    def small_flat(parts):
        flat = jnp.concatenate([a.reshape(-1) for a in parts])
        return jnp.pad(flat, (0, n_small - flat.shape[0])).reshape(-1, FLAT_COLS)

    start_end = exchange_start([small_flat([rep])], [pieces[n][0] for n in LAST], None,
                               name="scatter_start_0c")
    parts = {}
    for start, group, i in scatters:
        tag = {LATER: "a", MIDDLE: "b", LAST: "c"}[group]
        got = exchange_wait(start, start_end["token"], name=f"scatter_wait_{i}{tag}")
        parts.update({(n, i): a for n, a in zip(group, got)})
    for i in range(DEPTH):
        parts["w_down", i] = parts["w_down", i][:, :FF_PIECE]

    results = {}
    got_end = None
    for n in [n for n in names if n not in LAST] + list(LAST):
        if n in LAST and got_end is None:
            got_end = exchange_wait(start_end, out[0], name="scatter_wait_0c")
            parts.update({(n_, 0): a for n_, a in zip(LAST, got_end[1:])})
        w = shard[n]
        rows2d = lambda a: a.reshape(-1, w.shape[-1])
        w2 = rows2d(w)
        per_layer = [parts[n, i].reshape(N_DEV, w2.shape[0] // DEPTH, -1) for i in range(DEPTH)]
        out = reduce_adamw(per_layer, w2, rows2d(given["m_" + n]), rows2d(given["v_" + n]),
                           name="adamw_" + n)
        results[n] = [a.reshape(w.shape) for a in out]

    out = reduce_adamw([got_end[0]], *[small_flat([given[pre + n] for n in REPLICATED])
                                      for pre in ("", "m_", "v_")], name="adamw_small")
    loss = None
    for n in REPLICATED:
        results[n] = []
    for a in out:
        a = a.reshape(-1)
        off = 0
        for n in REPLICATED:
            size = int(np.prod(shard[n].shape))
            results[n].append(a[off:off + size].reshape(shard[n].shape))
            off += size
        if loss is None:
            loss = a[off]
    return (loss, dx[None], *[results[n][0] for n in WEIGHTS], *[results[n][1] for n in WEIGHTS],
            *[results[n][2] for n in WEIGHTS], *[results[n][3] for n in WEIGHTS])
```

```python
import functools

import numpy as np
import jax
import jax.numpy as jnp
from jax import lax
from jax.experimental import pallas as pl
from jax.experimental.pallas import tpu as pltpu

F32 = jnp.float32
BF16 = jnp.bfloat16

D_MODEL = 1024
DEPTH = 2
PLE_DIM = 256
POOL_WINDOWS = (2, 4, 8, 16)
POOL_GROUP = 128
POOL_WIDTH = 512
N_HEADS = 8
Q_LORA = 512
KV_LORA = 256
QK_NOPE = 128
QK_ROPE = 64
QK_HEAD = 192
V_HEAD = 128
HEAD_PAD = 256
D_FF = 2816
ROPE_THETA = 10000.0
EPS = 1e-6
IN_WIDTH = 3392
ATTN_SCALE = QK_HEAD ** -0.5

OFF_GA, OFF_GB, OFF_U, OFF_CQ, OFF_CKV, OFF_KR = 0, 1024, 2048, 2560, 3072, 3328
IN_PAD = 3456
W_IN_PIECE = IN_WIDTH // 8
PIECE_PAD = 512
FF_PIECE = D_FF // 8
FF_PAD = 384
FF_PAIRS = 4
FF_STEP = 2 * FF_PAD

ADAM_LR = 0.001
ADAM_B1 = 0.9
ADAM_B2 = 0.999
ADAM_EPS = 1e-08
ADAM_WD = 0.01
ADAM_STEP = 10

N_DEV = 8
LANES = 128
CHUNK = 128
VMEM_LIMIT = 56 * 1024 * 1024

SHARDED = (("w_in", 2), ("w_uq", 1), ("w_ukv", 1), ("w_a", 2), ("w_b", 1), ("w_o", 1),
           ("w_gate", 2), ("w_up", 2), ("w_down", 1), ("w_ple_gate", 1), ("w_ple", 2))
REPLICATED = ("norm_mix", "w_pool", "pool_scale", "q_norm", "kv_norm", "norm_ffn", "norm_ple",
              "final_norm")
WEIGHTS = ("norm_mix", "w_in", "w_pool", "pool_scale", "q_norm", "kv_norm", "w_uq", "w_ukv",
           "w_a", "w_b", "w_o", "norm_ffn", "w_gate", "w_up", "w_down", "norm_ple",
           "w_ple_gate", "w_ple", "final_norm")
FLAT_COLS = 1024
FLAT_ROW_BLOCK = 192


def _params(*sem):
    return pltpu.CompilerParams(dimension_semantics=sem, vmem_limit_bytes=VMEM_LIMIT)


def _dot(a, b):
    return jnp.dot(a, b, preferred_element_type=F32)


def _dot_nt(a, b):
    return lax.dot_general(a, b, (((1,), (1,)), ((), ())), preferred_element_type=F32)


def _dot_tn(a, b):
    return lax.dot_general(a, b, (((0,), (0,)), ((), ())), preferred_element_type=F32)


def _rms_fwd(x, g):
    r = lax.rsqrt(jnp.mean(x * x, axis=-1, keepdims=True) + EPS)
    return x * r * g


def _rms_bwd(x, g, dy):
    r = lax.rsqrt(jnp.mean(x * x, axis=-1, keepdims=True) + EPS)
    xr = x * r
    gy = dy * g
    dx = r * (gy - xr * jnp.mean(gy * xr, axis=-1, keepdims=True))
    return dx, dy * xr


def _sigmoid(x):
    return 1.0 / (1.0 + jnp.exp(-x))


def _accum_rows(ref, rows):
    ref[...] += jnp.sum(rows, axis=0, keepdims=True)


def _band(band, x):
    h1 = x.astype(BF16)
    r1 = x - h1.astype(F32)
    h2 = r1.astype(BF16)
    h3 = (r1 - h2.astype(F32)).astype(BF16)
    return _dot(band, h1) + _dot(band, h2) + _dot(band, h3)


def _rope(x, c, s1, s2, sign):
    return x * c + sign * (pltpu.roll(x, 96, 1) * s1 + pltpu.roll(x, 32, 1) * s2)


def _full(shape):
    n = len(shape)
    return pl.BlockSpec(shape, lambda *_: (0,) * n)


def matmul_tn(a, b, *, tn, tk, name, tm=1024):
    T, M = a.shape
    N = b.shape[1]
    tm, tn, tk = min(tm, M), min(tn, N), min(tk, T)
    nk = T // tk

    def body(a_ref, b_ref, o_ref, acc_ref):
        k = pl.program_id(2)

        @pl.when(k == 0)
        def _():
            acc_ref[...] = jnp.zeros_like(acc_ref)

        acc_ref[...] += _dot_tn(a_ref[...].astype(BF16), b_ref[...].astype(BF16))

        @pl.when(k == nk - 1)
        def _():
            o_ref[...] = acc_ref[...].astype(BF16)

    return pl.pallas_call(
        body, name=name, grid=(M // tm, N // tn, nk),
        in_specs=[pl.BlockSpec((tk, tm), lambda i, j, k: (k, i)),
                  pl.BlockSpec((tk, tn), lambda i, j, k: (k, j))],
        out_specs=pl.BlockSpec((tm, tn), lambda i, j, k: (i, j)),
        out_shape=jax.ShapeDtypeStruct((M, N), BF16),
        scratch_shapes=[pltpu.VMEM((tm, tn), F32)],
        compiler_params=_params("parallel", "parallel", "arbitrary"),
    )(a, b)


def matmul_tn_pieces(a, b, *, tk, name):
    a3 = a.ndim == 3
    T = a.shape[-2]
    m, n = a.shape[-1], b.shape[-1]
    tk = min(tk, T)
    nk = T // tk
    out = (N_DEV, FF_PAD, n) if a3 else (N_DEV, m, FF_PAD)

    def body(a_ref, b_ref, o_ref, acc_ref):
        k = pl.program_id(0)

        @pl.when(k == 0)
        def _():
            acc_ref[...] = jnp.zeros_like(acc_ref)

        whole = (b_ref if a3 else a_ref)[...].astype(BF16)
        for j in range(FF_PAIRS):
            if a3:
                acc_ref[j] += _dot_tn(a_ref[j].astype(BF16), whole)
            else:
                acc_ref[j] += _dot_tn(whole, b_ref[j].astype(BF16))

        @pl.when(k == nk - 1)
        def _():
            for j in range(FF_PAIRS):
                for half in range(2):
                    cut = slice(half * FF_PAD, (half + 1) * FF_PAD)
                    piece = acc_ref[j, cut, :] if a3 else acc_ref[j, :, cut]
                    o_ref[2 * j + half] = piece.astype(BF16)

    pairs = lambda w: pl.BlockSpec((FF_PAIRS, tk, w), lambda k: (0, k, 0))
    whole = lambda w: pl.BlockSpec((tk, w), lambda k: (k, 0))
    return pl.pallas_call(
        body, name=name, grid=(nk,),
        in_specs=[pairs(m) if a3 else whole(m), whole(n) if a3 else pairs(n)],
        out_specs=_full(out),
        out_shape=jax.ShapeDtypeStruct(out, BF16),
        scratch_shapes=[pltpu.VMEM((FF_PAIRS, m, n), F32)],
        compiler_params=_params("arbitrary"),
    )(a, b)


def dw_in_proj(h, segs, *, tk):
    T = h.shape[0]
    tk = min(tk, T)
    nk = T // tk
    widths = [s.shape[1] for s, _ in segs]
    offs = [o for _, o in segs]

    def body(*refs):
        h_ref, seg_refs, o_ref, acc_ref = refs[0], refs[1:-2], refs[-2], refs[-1]
        k = pl.program_id(0)

        @pl.when(k == 0)
        def _():
            acc_ref[...] = jnp.zeros_like(acc_ref)

        hv = h_ref[...]
        for s_ref, off, w in zip(seg_refs, offs, widths):
            acc_ref[:, off:off + w] += _dot_tn(hv, s_ref[...])

        @pl.when(k == nk - 1)
        def _():
            o_ref[...] = acc_ref[...].astype(BF16)

    return pl.pallas_call(
        body, name="dw_in", grid=(nk,),
        in_specs=[pl.BlockSpec((tk, D_MODEL), lambda k: (k, 0))]
        + [pl.BlockSpec((tk, w), lambda k: (k, 0)) for w in widths],
        out_specs=_full((D_MODEL, IN_PAD)),
        out_shape=jax.ShapeDtypeStruct((D_MODEL, IN_PAD), BF16),
        scratch_shapes=[pltpu.VMEM((D_MODEL, IN_PAD), F32)],
        compiler_params=_params("arbitrary"),
    )(h, *[s for s, _ in segs])


def _piece_row_map():
    src = np.full(IN_PAD, -1, np.int64)
    for orig, pad, width in ((0, OFF_U, 512), (512, OFF_CQ, 512), (1024, OFF_CKV, 256),
                             (1280, OFF_KR, 64), (1344, OFF_GA, 1024), (2368, OFF_GB, 1024)):
        src[pad:pad + width] = np.arange(orig, orig + width)
    rows = np.where(src >= 0, (src // W_IN_PIECE) * PIECE_PAD + src % W_IN_PIECE, -1)
    return rows.astype(np.int32)[None, :]


def _selector(j, map_ref, b):
    rid = j * PIECE_PAD + lax.broadcasted_iota(jnp.int32, (PIECE_PAD, LANES), 0)
    return jnp.where(rid == map_ref[:, b * LANES:(b + 1) * LANES], 1.0, 0.0).astype(BF16)


def _block_pieces():
    rows = _piece_row_map()[0].reshape(-1, LANES)
    return [sorted({int(r) // PIECE_PAD for r in blk if r >= 0}) for blk in rows]


def assemble_w_in(pieces, row_map, *, tr):
    feeds = _block_pieces()

    def body(p_ref, map_ref, o_ref):
        for b, js in enumerate(feeds):
            acc = jnp.zeros((tr, LANES), F32)
            for j in js:
                acc = acc + _dot(p_ref[j], _selector(j, map_ref, b))
            o_ref[:, b * LANES:(b + 1) * LANES] = acc.astype(BF16)

    return pl.pallas_call(
        body, name="assemble_w_in", grid=(D_MODEL // tr,),
        in_specs=[pl.BlockSpec((N_DEV, tr, PIECE_PAD), lambda i: (0, i, 0)), _full((1, IN_PAD))],
        out_specs=pl.BlockSpec((tr, IN_PAD), lambda i: (i, 0)),
        out_shape=jax.ShapeDtypeStruct((D_MODEL, IN_PAD), BF16),
        compiler_params=_params("parallel"),
    )(pieces, row_map)


def split_dw_in(dwp, row_map, *, tr):
    feeds = _block_pieces()

    def body(d_ref, map_ref, o_ref):
        for j in range(N_DEV):
            acc = jnp.zeros((tr, PIECE_PAD), F32)
            for b, js in enumerate(feeds):
                if j in js:
                    acc = acc + _dot_nt(d_ref[:, b * LANES:(b + 1) * LANES],
                                        _selector(j, map_ref, b))
            o_ref[j] = acc.astype(BF16)

    return pl.pallas_call(
        body, name="split_dw_in", grid=(D_MODEL // tr,),
        in_specs=[pl.BlockSpec((tr, IN_PAD), lambda i: (i, 0)), _full((1, IN_PAD))],
        out_specs=pl.BlockSpec((N_DEV, tr, PIECE_PAD), lambda i: (0, i, 0)),
        out_shape=jax.ShapeDtypeStruct((N_DEV, D_MODEL, PIECE_PAD), BF16),
        compiler_params=_params("parallel"),
    )(dwp, row_map)


def rms_matmul(x, g, w, *, tm, tn, name):
    T, D = x.shape
    N = w.shape[1]
    tm = min(tm, T)

    def body(x_ref, g_ref, w_ref, o_ref, h_ref):
        @pl.when(pl.program_id(1) == 0)
        def _():
            h_ref[...] = _rms_fwd(x_ref[...], g_ref[...]).astype(BF16)

        o_ref[...] = _dot(h_ref[...], w_ref[...])

    return pl.pallas_call(
        body, name=name, grid=(T // tm, N // tn),
        in_specs=[pl.BlockSpec((tm, D), lambda i, j: (i, 0)), _full((1, D)),
                  pl.BlockSpec((D, tn), lambda i, j: (0, j))],
        out_specs=pl.BlockSpec((tm, tn), lambda i, j: (i, j)),
        out_shape=jax.ShapeDtypeStruct((T, N), F32),
        scratch_shapes=[pltpu.VMEM((tm, D), BF16)],
        compiler_params=_params("parallel", "arbitrary"),
    )(x, g, w)


def _band_matrices():
    s = np.arange(CHUNK)[:, None]
    t = np.arange(CHUNK)[None, :]
    low = np.stack([((s - t >= 0) & (s - t < w)) for w in POOL_WINDOWS]).astype(np.float32)
    up = np.stack([(t > s + CHUNK - w) for w in POOL_WINDOWS]).astype(np.float32)
    return low, up


def _window_count(row0, g):
    t = row0 + lax.broadcasted_iota(jnp.int32, (CHUNK, 1), 0)
    return jnp.minimum(t + 1, POOL_WINDOWS[g]).astype(F32)


def pool_fwd(z, low, up, w_pool, pool_scale, w_a, *, tm):
    T = z.shape[0]
    tm = min(tm, T)
    nch = tm // CHUNK
    ublk = OFF_U // POOL_WIDTH

    def body(u_ref, halo_ref, low_ref, up_ref, wp_ref, sc_ref, wa_ref, pooled_ref, ms_ref, ya_ref):
        i = pl.program_id(0)
        for c in range(nch):
            rows = slice(c * CHUNK, (c + 1) * CHUNK)
            for g in range(4):
                cols = slice(g * POOL_GROUP, (g + 1) * POOL_GROUP)
                cur = u_ref[rows, cols]
                if c == 0:
                    prev = jnp.where(i > 0, halo_ref[:, cols], 0.0)
                else:
                    prev = u_ref[(c - 1) * CHUNK:c * CHUNK, cols]
                s = _band(low_ref[g], cur) + _band(up_ref[g], prev)
                pooled = (s / _window_count(i * tm + c * CHUNK, g) - cur).astype(BF16)
                pooled_ref[rows, cols] = pooled
                ms_ref[rows, cols] = (_dot(pooled, wp_ref[g]) * sc_ref[:, cols]).astype(BF16)
        ya_ref[...] = _dot(ms_ref[...], wa_ref[...])

    return pl.pallas_call(
        body, name="pool_fwd", grid=(T // tm,),
        in_specs=[pl.BlockSpec((tm, POOL_WIDTH), lambda i: (i, ublk)),
                  pl.BlockSpec((CHUNK, POOL_WIDTH), lambda i: (jnp.maximum(i * nch - 1, 0), ublk)),
                  _full((4, CHUNK, CHUNK)), _full((4, CHUNK, CHUNK)),
                  _full((4, POOL_GROUP, POOL_GROUP)), _full((1, POOL_WIDTH)),
                  _full((POOL_WIDTH, D_MODEL))],
        out_specs=[pl.BlockSpec((tm, POOL_WIDTH), lambda i: (i, 0)),
                   pl.BlockSpec((tm, POOL_WIDTH), lambda i: (i, 0)),
                   pl.BlockSpec((tm, D_MODEL), lambda i: (i, 0))],
        out_shape=[jax.ShapeDtypeStruct((T, POOL_WIDTH), BF16),
                   jax.ShapeDtypeStruct((T, POOL_WIDTH), BF16),
                   jax.ShapeDtypeStruct((T, D_MODEL), F32)],
        compiler_params=_params("parallel"),
    )(z, z, low, up, w_pool, pool_scale, w_a)


def pool_bwd_a(dya, pooled, w_a, w_pool, pool_scale, *, tm):
    T = dya.shape[0]
    tm = min(tm, T)
    nch = tm // CHUNK

    def body(dya_ref, pooled_ref, wa_ref, wp_ref, sc_ref, dpc_ref, dsc_ref, dwp_ref):
        i = pl.program_id(0)

        @pl.when(i == 0)
        def _():
            dsc_ref[...] = jnp.zeros_like(dsc_ref)
            dwp_ref[...] = jnp.zeros_like(dwp_ref)

        dms = _dot_nt(dya_ref[...], wa_ref[...])
        for g in range(4):
            cols = slice(g * POOL_GROUP, (g + 1) * POOL_GROUP)
            pg = pooled_ref[:, cols]
            dmg = dms[:, cols]
            mixed = _dot(pg, wp_ref[g])
            dsc_ref[:, cols] += jnp.sum(dmg * mixed, axis=0, keepdims=True)
            dmixed = (dmg * sc_ref[:, cols]).astype(BF16)
            dwp_ref[g] += _dot_tn(pg, dmixed)
            dpooled = _dot_nt(dmixed, wp_ref[g])
            for c in range(nch):
                rows = slice(c * CHUNK, (c + 1) * CHUNK)
                dpc_ref[rows, cols] = dpooled[rows] / _window_count(i * tm + c * CHUNK, g)

    return pl.pallas_call(
        body, name="pool_bwd_a", grid=(T // tm,),
        in_specs=[pl.BlockSpec((tm, D_MODEL), lambda i: (i, 0)),
                  pl.BlockSpec((tm, POOL_WIDTH), lambda i: (i, 0)),
                  _full((POOL_WIDTH, D_MODEL)), _full((4, POOL_GROUP, POOL_GROUP)),
                  _full((1, POOL_WIDTH))],
        out_specs=[pl.BlockSpec((tm, POOL_WIDTH), lambda i: (i, 0)), _full((1, POOL_WIDTH)),
                   _full((4, POOL_GROUP, POOL_GROUP))],
        out_shape=[jax.ShapeDtypeStruct((T, POOL_WIDTH), F32),
                   jax.ShapeDtypeStruct((1, POOL_WIDTH), F32),
                   jax.ShapeDtypeStruct((4, POOL_GROUP, POOL_GROUP), F32)],
        compiler_params=_params("arbitrary"),
    )(dya, pooled, w_a, w_pool, pool_scale)


def pool_bwd_b(dpc, low_t, up_t, *, tm):
    T = dpc.shape[0]
    tm = min(tm, T)
    nch = tm // CHUNK
    last_chunk = T // CHUNK - 1

    def body(d_ref, halo_ref, low_ref, up_ref, du_ref):
        i = pl.program_id(0)
        for c in range(nch):
            rows = slice(c * CHUNK, (c + 1) * CHUNK)
            for g in range(4):
                cols = slice(g * POOL_GROUP, (g + 1) * POOL_GROUP)
                cur = d_ref[rows, cols]
                if c == nch - 1:
                    nxt = jnp.where(i < pl.num_programs(0) - 1, halo_ref[:, cols], 0.0)
                else:
                    nxt = d_ref[(c + 1) * CHUNK:(c + 2) * CHUNK, cols]
                s = _band(low_ref[g], cur) + _band(up_ref[g], nxt)
                du_ref[rows, cols] = (s - cur * _window_count(i * tm + c * CHUNK, g)).astype(BF16)

    return pl.pallas_call(
        body, name="pool_bwd_b", grid=(T // tm,),
        in_specs=[pl.BlockSpec((tm, POOL_WIDTH), lambda i: (i, 0)),
                  pl.BlockSpec((CHUNK, POOL_WIDTH),
                               lambda i: (jnp.minimum((i + 1) * nch, last_chunk), 0)),
                  _full((4, CHUNK, CHUNK)), _full((4, CHUNK, CHUNK))],
        out_specs=pl.BlockSpec((tm, POOL_WIDTH), lambda i: (i, 0)),
        out_shape=jax.ShapeDtypeStruct((T, POOL_WIDTH), BF16),
        compiler_params=_params("parallel"),
    )(dpc, dpc, low_t, up_t)


def mla_prep(z, cos_t, s1_t, s2_t, q_norm, kv_norm, w_uq, w_ukv, *, tm):
    T = z.shape[0]
    tm = min(tm, T)

    def body(cq_ref, ckv_ref, kr_ref, c_ref, s1_ref, s2_ref, qg_ref, kvg_ref, wuq_ref, wukv_ref,
             q_ref, k_ref, v_ref):
        c, s1, s2 = c_ref[...], s1_ref[...], s2_ref[...]
        qn = _rms_fwd(cq_ref[...], qg_ref[...]).astype(BF16)
        q = _dot(qn, wuq_ref[...])
        kvn = _rms_fwd(ckv_ref[...], kvg_ref[...]).astype(BF16)
        kv = _dot(kvn, wukv_ref[...])
        kpe = _rope(kr_ref[...], c, s1, s2, 1.0).astype(BF16)
        for h in range(N_HEADS):
            o = h * HEAD_PAD
            q_ref[h, :, 0:QK_NOPE] = (q[:, o:o + QK_NOPE] * ATTN_SCALE).astype(BF16)
            q_ref[h, :, QK_NOPE:HEAD_PAD] = (
                _rope(q[:, o + QK_NOPE:o + HEAD_PAD], c, s1, s2, 1.0) * ATTN_SCALE).astype(BF16)
            k_ref[h, :, 0:QK_NOPE] = kv[:, o:o + QK_NOPE].astype(BF16)
            k_ref[h, :, QK_NOPE:HEAD_PAD] = kpe
            v_ref[h] = kv[:, o + QK_NOPE:o + HEAD_PAD].astype(BF16)

    tok = lambda w: pl.BlockSpec((tm, w), lambda i: (i, 0))
    return pl.pallas_call(
        body, name="mla_prep", grid=(T // tm,),
        in_specs=[pl.BlockSpec((tm, Q_LORA), lambda i: (i, OFF_CQ // Q_LORA)),
                  pl.BlockSpec((tm, KV_LORA), lambda i: (i, OFF_CKV // KV_LORA)),
                  pl.BlockSpec((tm, LANES), lambda i: (i, OFF_KR // LANES)),
                  tok(LANES), tok(LANES), tok(LANES),
                  _full((1, Q_LORA)), _full((1, KV_LORA)),
                  _full((Q_LORA, N_HEADS * HEAD_PAD)), _full((KV_LORA, N_HEADS * HEAD_PAD))],
        out_specs=[pl.BlockSpec((N_HEADS, tm, HEAD_PAD), lambda i: (0, i, 0)),
                   pl.BlockSpec((N_HEADS, tm, HEAD_PAD), lambda i: (0, i, 0)),
                   pl.BlockSpec((N_HEADS, tm, V_HEAD), lambda i: (0, i, 0))],
        out_shape=[jax.ShapeDtypeStruct((N_HEADS, T, HEAD_PAD), BF16),
                   jax.ShapeDtypeStruct((N_HEADS, T, HEAD_PAD), BF16),
                   jax.ShapeDtypeStruct((N_HEADS, T, V_HEAD), BF16)],
        compiler_params=_params("parallel"),
    )(z, z, z, cos_t, s1_t, s2_t, q_norm, kv_norm, w_uq, w_ukv)


def _row_vector(col, n):
    return jnp.transpose(jnp.broadcast_to(col, (n, LANES)))[0:1, :]


def flash_fwd(q, k, v, *, tq):
    H, T, _ = q.shape
    tq = min(tq, T)
    nq = T // tq
    neg = -1e30

    def body(q_ref, k_ref, v_ref, o_ref, lse_ref):
        qi = pl.program_id(1)
        qb = q_ref[0]

        def tile(j, carry, masked):
            m, l, acc = carry
            rows = pl.ds(pl.multiple_of(j * tq, tq), tq)
            s = _dot_nt(qb, k_ref[0, rows, :])
            if masked:
                r = lax.broadcasted_iota(jnp.int32, (tq, tq), 0)
                c = lax.broadcasted_iota(jnp.int32, (tq, tq), 1)
                s = jnp.where(c <= r, s, neg)
            m_new = jnp.maximum(m, jnp.max(s, axis=-1, keepdims=True))
            alpha = jnp.exp(m - m_new)
            p = jnp.exp(s - m_new)
            l = alpha * l + jnp.sum(p, axis=-1, keepdims=True)
            acc = alpha * acc + _dot(p.astype(BF16), v_ref[0, rows, :])
            return m_new, l, acc

        init = (jnp.full((tq, 1), neg, F32), jnp.zeros((tq, 1), F32), jnp.zeros((tq, V_HEAD), F32))
        carry = lax.fori_loop(0, qi, lambda j, cr: tile(j, cr, False), init)
        m, l, acc = tile(qi, carry, True)
        o_ref[...] = (acc / l).astype(BF16)
        lse_ref[0, 0] = _row_vector(m + jnp.log(l), tq)

    return pl.pallas_call(
        body, name="flash_fwd", grid=(H, nq),
        in_specs=[pl.BlockSpec((1, tq, HEAD_PAD), lambda h, i: (h, i, 0)),
                  pl.BlockSpec((1, T, HEAD_PAD), lambda h, i: (h, 0, 0)),
                  pl.BlockSpec((1, T, V_HEAD), lambda h, i: (h, 0, 0))],
        out_specs=[pl.BlockSpec((tq, V_HEAD), lambda h, i: (i, h)),
                   pl.BlockSpec((1, 1, 1, tq), lambda h, i: (h, i, 0, 0))],
        out_shape=[jax.ShapeDtypeStruct((T, H * V_HEAD), BF16),
                   jax.ShapeDtypeStruct((H, nq, 1, tq), F32)],
        compiler_params=_params("parallel", "arbitrary"),
    )(q, k, v)


def attn_delta(do, o, *, tq):
    T = do.shape[0]
    tq = min(tq, T)
    nq = T // tq
    group = max(g for g in (1, 2, 4) if nq % g == 0)

    def body(do_ref, o_ref, d_ref):
        for b in range(group):
            rows = slice(b * tq, (b + 1) * tq)
            prod = do_ref[rows, :].astype(F32) * o_ref[rows, :].astype(F32)
            d_ref[0, b] = jnp.sum(jnp.transpose(prod), axis=0, keepdims=True)

    return pl.pallas_call(
        body, name="attn_delta", grid=(N_HEADS, nq // group),
        in_specs=[pl.BlockSpec((group * tq, V_HEAD), lambda h, i: (i, h)),
                  pl.BlockSpec((group * tq, V_HEAD), lambda h, i: (i, h))],
        out_specs=pl.BlockSpec((1, group, 1, tq), lambda h, i: (h, i, 0, 0)),
        out_shape=jax.ShapeDtypeStruct((N_HEADS, nq, 1, tq), F32),
        compiler_params=_params("parallel", "parallel"),
    )(do, o)


def flash_bwd(q, k, v, do, lse, delta, *, tq, tk):
    H, T, _ = q.shape
    tq, tk = min(tq, T), min(tk, T)
    nq, nk = T // tq, T // tk
    neg = -1e30

    def body(q_ref, do_ref, lse_ref, dl_ref, k_ref, v_ref, dq_ref, dk_ref, dv_ref, dq_acc):
        j = pl.program_id(1)

        @pl.when(j == 0)
        def _():
            dq_acc[...] = jnp.zeros_like(dq_acc)

        kb = k_ref[0]
        vb = v_ref[0]

        def tile(i, carry, masked):
            dk, dv = carry
            rows = pl.ds(pl.multiple_of(i * tq, tq), tq)
            qb = q_ref[0, rows, :]
            dob = do_ref[rows, :]
            st = _dot_nt(kb, qb)
            if masked:
                key = j * tk + lax.broadcasted_iota(jnp.int32, (tk, tq), 0)
                query = i * tq + lax.broadcasted_iota(jnp.int32, (tk, tq), 1)
                st = jnp.where(key <= query, st, neg)
            pt = jnp.exp(st - lse_ref[0, i])
            dv = dv + _dot(pt.astype(BF16), dob)
            dpt = _dot_nt(vb, dob)
            dst = (pt * (dpt - dl_ref[0, i])).astype(BF16)
            dk = dk + _dot(dst, qb)
            dq_acc[rows, :] += _dot_tn(dst, kb)
            return dk, dv

        first = (j * tk) // tq
        carry = tile(first, (jnp.zeros((tk, HEAD_PAD), F32), jnp.zeros((tk, V_HEAD), F32)), True)
        dk, dv = lax.fori_loop(first + 1, nq, lambda i, cr: tile(i, cr, False), carry)
        dk_ref[0] = dk.astype(BF16)
        dv_ref[0] = dv.astype(BF16)

        @pl.when(j == nk - 1)
        def _():
            dq_ref[0] = dq_acc[...].astype(BF16)

    return pl.pallas_call(
        body, name="flash_bwd", grid=(H, nk),
        in_specs=[pl.BlockSpec((1, T, HEAD_PAD), lambda h, j: (h, 0, 0)),
                  pl.BlockSpec((T, V_HEAD), lambda h, j: (0, h)),
                  pl.BlockSpec((1, nq, 1, tq), lambda h, j: (h, 0, 0, 0)),
                  pl.BlockSpec((1, nq, 1, tq), lambda h, j: (h, 0, 0, 0)),
                  pl.BlockSpec((1, tk, HEAD_PAD), lambda h, j: (h, j, 0)),
                  pl.BlockSpec((1, tk, V_HEAD), lambda h, j: (h, j, 0))],
        out_specs=[pl.BlockSpec((1, T, HEAD_PAD), lambda h, j: (h, 0, 0)),
                   pl.BlockSpec((1, tk, HEAD_PAD), lambda h, j: (h, j, 0)),
                   pl.BlockSpec((1, tk, V_HEAD), lambda h, j: (h, j, 0))],
        out_shape=[jax.ShapeDtypeStruct((H, T, HEAD_PAD), BF16),
                   jax.ShapeDtypeStruct((H, T, HEAD_PAD), BF16),
                   jax.ShapeDtypeStruct((H, T, V_HEAD), BF16)],
        scratch_shapes=[pltpu.VMEM((T, HEAD_PAD), F32)],
        compiler_params=_params("parallel", "arbitrary"),
    )(q, do, lse, delta, k, v)


def mla_bwd(dq, dk, dv, z, cos_t, s1_t, s2_t, q_norm, kv_norm, w_uq, w_ukv, *, tm):
    T = z.shape[0]
    tm = min(tm, T)
    HW = N_HEADS * HEAD_PAD

    def body(dq_ref, dk_ref, dv_ref, cq_ref, ckv_ref, c_ref, s1_ref, s2_ref, qg_ref, kvg_ref,
             wuq_ref, wukv_ref, dcq_ref, dckv_ref, dkr_ref, dqf_ref, dkvf_ref, qn_ref, kvn_ref,
             dqg_ref, dkvg_ref):
        @pl.when(pl.program_id(0) == 0)
        def _():
            dqg_ref[...] = jnp.zeros_like(dqg_ref)
            dkvg_ref[...] = jnp.zeros_like(dkvg_ref)

        c, s1, s2 = c_ref[...], s1_ref[...], s2_ref[...]
        dkpe = jnp.zeros((tm, LANES), F32)
        for h in range(N_HEADS):
            o = h * HEAD_PAD
            dqf_ref[:, o:o + QK_NOPE] = (
                dq_ref[h, :, 0:QK_NOPE].astype(F32) * ATTN_SCALE).astype(BF16)
            dqf_ref[:, o + QK_NOPE:o + HEAD_PAD] = (
                _rope(dq_ref[h, :, QK_NOPE:HEAD_PAD].astype(F32), c, s1, s2, -1.0)
                * ATTN_SCALE).astype(BF16)
            dkvf_ref[:, o:o + QK_NOPE] = dk_ref[h, :, 0:QK_NOPE]
            dkvf_ref[:, o + QK_NOPE:o + HEAD_PAD] = dv_ref[h]
            dkpe = dkpe + dk_ref[h, :, QK_NOPE:HEAD_PAD].astype(F32)
        dkr_ref[...] = _rope(dkpe, c, s1, s2, -1.0).astype(BF16)

        cq = cq_ref[...]
        qn_ref[...] = _rms_fwd(cq, qg_ref[...]).astype(BF16)
        dcq, dgrows = _rms_bwd(cq, qg_ref[...], _dot_nt(dqf_ref[...], wuq_ref[...]))
        dcq_ref[...] = dcq.astype(BF16)
        _accum_rows(dqg_ref, dgrows)

        ckv = ckv_ref[...]
        kvn_ref[...] = _rms_fwd(ckv, kvg_ref[...]).astype(BF16)
        dckv, dgrows = _rms_bwd(ckv, kvg_ref[...], _dot_nt(dkvf_ref[...], wukv_ref[...]))
        dckv_ref[...] = dckv.astype(BF16)
        _accum_rows(dkvg_ref, dgrows)

    tok = lambda w: pl.BlockSpec((tm, w), lambda i: (i, 0))
    head = lambda w: pl.BlockSpec((N_HEADS, tm, w), lambda i: (0, i, 0))
    return pl.pallas_call(
        body, name="mla_bwd", grid=(T // tm,),
        in_specs=[head(HEAD_PAD), head(HEAD_PAD), head(V_HEAD),
                  pl.BlockSpec((tm, Q_LORA), lambda i: (i, OFF_CQ // Q_LORA)),
                  pl.BlockSpec((tm, KV_LORA), lambda i: (i, OFF_CKV // KV_LORA)),
                  tok(LANES), tok(LANES), tok(LANES),
                  _full((1, Q_LORA)), _full((1, KV_LORA)),
                  _full((Q_LORA, HW)), _full((KV_LORA, HW))],
        out_specs=[tok(Q_LORA), tok(KV_LORA), tok(LANES), tok(HW), tok(HW), tok(Q_LORA),
                   tok(KV_LORA), _full((1, Q_LORA)), _full((1, KV_LORA))],
        out_shape=[jax.ShapeDtypeStruct((T, Q_LORA), BF16),
                   jax.ShapeDtypeStruct((T, KV_LORA), BF16),
                   jax.ShapeDtypeStruct((T, LANES), BF16),
                   jax.ShapeDtypeStruct((T, HW), BF16),
                   jax.ShapeDtypeStruct((T, HW), BF16),
                   jax.ShapeDtypeStruct((T, Q_LORA), BF16),
                   jax.ShapeDtypeStruct((T, KV_LORA), BF16),
                   jax.ShapeDtypeStruct((1, Q_LORA), F32),
                   jax.ShapeDtypeStruct((1, KV_LORA), F32)],
        compiler_params=_params("arbitrary"),
    )(dq, dk, dv, z, z, cos_t, s1_t, s2_t, q_norm, kv_norm, w_uq, w_ukv)


def merge_fwd(o, ya, z, x, w_b, w_o, *, tm):
    T = x.shape[0]
    tm = min(tm, T)

    def body(o_ref, ya_ref, ga_ref, gb_ref, x_ref, wb_ref, wo_ref, yb_ref, mg_ref, x1_ref):
        yb = _dot(o_ref[...], wb_ref[...])
        yb_ref[...] = yb
        merged = (_sigmoid(ga_ref[...]) * ya_ref[...] + _sigmoid(gb_ref[...]) * yb).astype(BF16)
        mg_ref[...] = merged
        x1_ref[...] = x_ref[...] + _dot(merged, wo_ref[...])

    tok = pl.BlockSpec((tm, D_MODEL), lambda i: (i, 0))
    return pl.pallas_call(
        body, name="merge_fwd", grid=(T // tm,),
        in_specs=[tok, tok, pl.BlockSpec((tm, D_MODEL), lambda i: (i, OFF_GA // D_MODEL)),
                  pl.BlockSpec((tm, D_MODEL), lambda i: (i, OFF_GB // D_MODEL)), tok,
                  _full((D_MODEL, D_MODEL)), _full((D_MODEL, D_MODEL))],
        out_specs=[tok, tok, tok],
        out_shape=[jax.ShapeDtypeStruct((T, D_MODEL), F32),
                   jax.ShapeDtypeStruct((T, D_MODEL), BF16),
                   jax.ShapeDtypeStruct((T, D_MODEL), F32)],
        compiler_params=_params("parallel"),
    )(o, ya, z, z, x, w_b, w_o)


def merge_bwd(dx1, ya, yb, z, w_o, w_b, dep, *, tm):
    T = dx1.shape[0]
    tm = min(tm, T)

    def body(dx_ref, ya_ref, yb_ref, ga_ref, gb_ref, wo_ref, wb_ref, dep_ref, dya_ref, dyb_ref,
             do_ref, dga_ref, dgb_ref):
        dm = _dot_nt(dx_ref[...].astype(BF16), wo_ref[...])
        sa = _sigmoid(ga_ref[...])
        sb = _sigmoid(gb_ref[...])
        dya_ref[...] = (dm * sa).astype(BF16)
        dyb = (dm * sb).astype(BF16)
        dyb_ref[...] = dyb
        dga_ref[...] = (dm * ya_ref[...] * sa * (1.0 - sa)).astype(BF16)
        dgb_ref[...] = (dm * yb_ref[...] * sb * (1.0 - sb)).astype(BF16)
        do_ref[...] = _dot_nt(dyb, wb_ref[...]).astype(BF16)

    tok = pl.BlockSpec((tm, D_MODEL), lambda i: (i, 0))
    return pl.pallas_call(
        body, name="merge_bwd", grid=(T // tm,),
        in_specs=[tok, tok, tok, pl.BlockSpec((tm, D_MODEL), lambda i: (i, OFF_GA // D_MODEL)),
                  pl.BlockSpec((tm, D_MODEL), lambda i: (i, OFF_GB // D_MODEL)),
                  _full((D_MODEL, D_MODEL)), _full((D_MODEL, D_MODEL)),
                  pl.BlockSpec(memory_space=pl.ANY)],
        out_specs=[tok] * 5,
        out_shape=[jax.ShapeDtypeStruct((T, D_MODEL), BF16)] * 5,
        compiler_params=_params("parallel"),
    )(dx1, ya, yb, z, z, w_o, w_b, dep)


def in_bwd(dx1, x, g, segs, w_in, *, tm):
    T = x.shape[0]
    tm = min(tm, T)
    widths = [s.shape[1] for s, _ in segs]
    offs = [o for _, o in segs]
    n = len(segs)

    def body(*refs):
        dx1_ref, x_ref, g_ref = refs[:3]
        seg_refs = refs[3:3 + n]
        w_ref = refs[3 + n]
        dx_ref, h_ref, dg_ref = refs[4 + n:]

        @pl.when(pl.program_id(0) == 0)
        def _():
            dg_ref[...] = jnp.zeros_like(dg_ref)

        dh = jnp.zeros((tm, D_MODEL), F32)
        for s_ref, off, w in zip(seg_refs, offs, widths):
            dh = dh + _dot_nt(s_ref[...], w_ref[:, off:off + w])
        xv = x_ref[...]
        h_ref[...] = _rms_fwd(xv, g_ref[...]).astype(BF16)
        dxn, dgrows = _rms_bwd(xv, g_ref[...], dh)
        dx_ref[...] = dx1_ref[...] + dxn
        _accum_rows(dg_ref, dgrows)

    tok = pl.BlockSpec((tm, D_MODEL), lambda i: (i, 0))
    return pl.pallas_call(
        body, name="in_bwd", grid=(T // tm,),
        in_specs=[tok, tok, _full((1, D_MODEL))]
        + [pl.BlockSpec((tm, w), lambda i: (i, 0)) for w in widths] + [_full((D_MODEL, IN_PAD))],
        out_specs=[tok, tok, _full((1, D_MODEL))],
        out_shape=[jax.ShapeDtypeStruct((T, D_MODEL), F32),
                   jax.ShapeDtypeStruct((T, D_MODEL), BF16),
                   jax.ShapeDtypeStruct((1, D_MODEL), F32)],
        compiler_params=_params("arbitrary"),
    )(dx1, x, g, *[s for s, _ in segs], w_in)


def _pair_cols(w_ref):
    return jnp.concatenate([w_ref[0], w_ref[1]], axis=1)


def _pair_rows(w_ref):
    return jnp.concatenate([w_ref[0], w_ref[1]], axis=0)


def ffn_fwd(x1, g, w_gate, w_up, w_down, *, tm):
    T = x1.shape[0]
    tm = min(tm, T)

    def body(x_ref, g_ref, wg_ref, wu_ref, wd_ref, gp_ref, up_ref, x2_ref, h_ref, acc_ref):
        f = pl.program_id(1)

        @pl.when(f == 0)
        def _():
            h_ref[...] = _rms_fwd(x_ref[...], g_ref[...]).astype(BF16)
            acc_ref[...] = jnp.zeros_like(acc_ref)

        gp = _dot(h_ref[...], _pair_cols(wg_ref))
        up = _dot(h_ref[...], _pair_cols(wu_ref))
        gp_ref[0] = gp
        up_ref[0] = up
        act = (gp * _sigmoid(gp) * up).astype(BF16)
        acc_ref[...] += _dot(act, _pair_rows(wd_ref))

        @pl.when(f == FF_PAIRS - 1)
        def _():
            x2_ref[...] = x_ref[...] + acc_ref[...]

    tok = pl.BlockSpec((tm, D_MODEL), lambda i, f: (i, 0))
    ff = pl.BlockSpec((1, tm, FF_STEP), lambda i, f: (f, i, 0))
    w_col = pl.BlockSpec((2, D_MODEL, FF_PAD), lambda i, f: (f, 0, 0))
    w_row = pl.BlockSpec((2, FF_PAD, D_MODEL), lambda i, f: (f, 0, 0))
    return pl.pallas_call(
        body, name="ffn_fwd", grid=(T // tm, FF_PAIRS),
        in_specs=[tok, _full((1, D_MODEL)), w_col, w_col, w_row],
        out_specs=[ff, ff, tok],
        out_shape=[jax.ShapeDtypeStruct((FF_PAIRS, T, FF_STEP), F32),
                   jax.ShapeDtypeStruct((FF_PAIRS, T, FF_STEP), F32),
                   jax.ShapeDtypeStruct((T, D_MODEL), F32)],
        scratch_shapes=[pltpu.VMEM((tm, D_MODEL), BF16), pltpu.VMEM((tm, D_MODEL), F32)],
        compiler_params=_params("parallel", "arbitrary"),
    )(x1, g, w_gate, w_up, w_down)


def ffn_bwd(dx2, x1, g, gpre, upre, w_gate, w_up, w_down, *, tm):
    T = x1.shape[0]
    tm = min(tm, T)

    def body(dx_ref, x_ref, g_ref, gp_ref, up_ref, wg_ref, wu_ref, wd_ref,
             act_ref, dg_ref, du_ref, h_ref, dx1_ref, dgain_ref, dxb_ref, acc_ref):
        f = pl.program_id(1)

        @pl.when((pl.program_id(0) == 0) & (f == 0))
        def _():
            dgain_ref[...] = jnp.zeros_like(dgain_ref)

        @pl.when(f == 0)
        def _():
            dxb_ref[...] = dx_ref[...].astype(BF16)
            acc_ref[...] = jnp.zeros_like(acc_ref)

        gp = gp_ref[0]
        up = up_ref[0]
        dact = _dot_nt(dxb_ref[...], _pair_rows(wd_ref))
        sg = _sigmoid(gp)
        silu = gp * sg
        act_ref[0] = (silu * up).astype(BF16)
        dgp = (dact * up * (sg * (1.0 + gp * (1.0 - sg)))).astype(BF16)
        dup = (dact * silu).astype(BF16)
        dg_ref[0] = dgp
        du_ref[0] = dup
        acc_ref[...] += _dot_nt(dgp, _pair_cols(wg_ref)) + _dot_nt(dup, _pair_cols(wu_ref))

        @pl.when(f == FF_PAIRS - 1)
        def _():
            xv = x_ref[...]
            h_ref[...] = _rms_fwd(xv, g_ref[...]).astype(BF16)
            dxn, dgrows = _rms_bwd(xv, g_ref[...], acc_ref[...])
            dx1_ref[...] = dx_ref[...] + dxn
            _accum_rows(dgain_ref, dgrows)

    tok = pl.BlockSpec((tm, D_MODEL), lambda i, f: (i, 0))
    ff = pl.BlockSpec((1, tm, FF_STEP), lambda i, f: (f, i, 0))
    w_col = pl.BlockSpec((2, D_MODEL, FF_PAD), lambda i, f: (f, 0, 0))
    w_row = pl.BlockSpec((2, FF_PAD, D_MODEL), lambda i, f: (f, 0, 0))
    return pl.pallas_call(
        body, name="ffn_bwd", grid=(T // tm, FF_PAIRS),
        in_specs=[tok, tok, _full((1, D_MODEL)), ff, ff, w_col, w_col, w_row],
        out_specs=[ff, ff, ff, tok, tok, _full((1, D_MODEL))],
        out_shape=[jax.ShapeDtypeStruct((FF_PAIRS, T, FF_STEP), BF16)] * 3
        + [jax.ShapeDtypeStruct((T, D_MODEL), BF16), jax.ShapeDtypeStruct((T, D_MODEL), F32),
           jax.ShapeDtypeStruct((1, D_MODEL), F32)],
        scratch_shapes=[pltpu.VMEM((tm, D_MODEL), BF16), pltpu.VMEM((tm, D_MODEL), F32)],
        compiler_params=_params("arbitrary", "arbitrary"),
    )(dx2, x1, g, gpre, upre, w_gate, w_up, w_down)


def ple_fwd(x2, p, g, w_pg, w_ple, *, tm):
    T = x2.shape[0]
    tm = min(tm, T)

    def body(x_ref, p_ref, g_ref, wpg_ref, wple_ref, l_ref, pe_ref, x3_ref):
        xv = x_ref[...]
        logits = _dot(_rms_fwd(xv, g_ref[...]).astype(BF16), wpg_ref[...])
        pe = _dot(p_ref[...].astype(BF16), wple_ref[...])
        l_ref[...] = logits
        pe_ref[...] = pe
        x3_ref[...] = xv + _sigmoid(logits) * pe

    tok = pl.BlockSpec((tm, D_MODEL), lambda i: (i, 0))
    return pl.pallas_call(
        body, name="ple_fwd", grid=(T // tm,),
        in_specs=[tok, pl.BlockSpec((tm, PLE_DIM), lambda i: (i, 0)), _full((1, D_MODEL)),
                  _full((D_MODEL, D_MODEL)), _full((PLE_DIM, D_MODEL))],
        out_specs=[tok, tok, tok],
        out_shape=[jax.ShapeDtypeStruct((T, D_MODEL), F32)] * 3,
        compiler_params=_params("parallel"),
    )(x2, p, g, w_pg, w_ple)


def ple_bwd(dx3, logits, pe, x2, g, w_pg, *, tm):
    T = x2.shape[0]
    tm = min(tm, T)

    def body(dx_ref, l_ref, pe_ref, x_ref, g_ref, wpg_ref, dl_ref, dpe_ref, h_ref, dx2_ref,
             dg_ref):
        @pl.when(pl.program_id(0) == 0)
        def _():
            dg_ref[...] = jnp.zeros_like(dg_ref)

        dx = dx_ref[...]
        s = _sigmoid(l_ref[...])
        dpe_ref[...] = (dx * s).astype(BF16)
        dl = (dx * pe_ref[...] * s * (1.0 - s)).astype(BF16)
        dl_ref[...] = dl
        xv = x_ref[...]
        h_ref[...] = _rms_fwd(xv, g_ref[...]).astype(BF16)
        dxn, dgrows = _rms_bwd(xv, g_ref[...], _dot_nt(dl, wpg_ref[...]))
        dx2_ref[...] = dx + dxn
        _accum_rows(dg_ref, dgrows)

    tok = pl.BlockSpec((tm, D_MODEL), lambda i: (i, 0))
    return pl.pallas_call(
        body, name="ple_bwd", grid=(T // tm,),
        in_specs=[tok, tok, tok, tok, _full((1, D_MODEL)), _full((D_MODEL, D_MODEL))],
        out_specs=[tok, tok, tok, tok, _full((1, D_MODEL))],
        out_shape=[jax.ShapeDtypeStruct((T, D_MODEL), BF16)] * 3
        + [jax.ShapeDtypeStruct((T, D_MODEL), F32), jax.ShapeDtypeStruct((1, D_MODEL), F32)],
        compiler_params=_params("arbitrary"),
    )(dx3, logits, pe, x2, g, w_pg)


def loss_head(x, g, target, *, tm):
    T = x.shape[0]
    tm = min(tm, T)

    def body(x_ref, g_ref, t_ref, loss_ref, dx_ref, dg_ref):
        @pl.when(pl.program_id(0) == 0)
        def _():
            loss_ref[...] = jnp.zeros_like(loss_ref)
            dg_ref[...] = jnp.zeros_like(dg_ref)

        xv = x_ref[...]
        err = _rms_fwd(xv, g_ref[...]) - t_ref[...]
        part = 0.5 * jnp.sum(jnp.mean(err * err, axis=-1, keepdims=True), axis=0, keepdims=True)
        lane = lax.broadcasted_iota(jnp.int32, (1, LANES), 1)
        loss_ref[...] += jnp.where(lane == 0, part, 0.0)
        dxn, dgrows = _rms_bwd(xv, g_ref[...], err * (1.0 / D_MODEL))
        dx_ref[...] = dxn
        _accum_rows(dg_ref, dgrows)

    tok = pl.BlockSpec((tm, D_MODEL), lambda i: (i, 0))
    return pl.pallas_call(
        body, name="loss_head", grid=(T // tm,),
        in_specs=[tok, _full((1, D_MODEL)), tok],
        out_specs=[_full((1, LANES)), tok, _full((1, D_MODEL))],
        out_shape=[jax.ShapeDtypeStruct((1, LANES), F32), jax.ShapeDtypeStruct((T, D_MODEL), F32),
                   jax.ShapeDtypeStruct((1, D_MODEL), F32)],
        compiler_params=_params("arbitrary"),
    )(x, g, target)


def _peers():
    x, y, c = lax.axis_index("x"), lax.axis_index("y"), lax.axis_index("c")
    me = 4 * x + 2 * y + c
    out = []
    for d in range(1, N_DEV):
        px = 1 - x if d & 4 else x
        py = 1 - y if d & 2 else y
        pc = 1 - c if d & 1 else c
        out.append(((px, py, pc), 4 * px + 2 * py + pc))
    return me, out


_HBM = pl.BlockSpec(memory_space=pltpu.HBM)
_SEM = pl.BlockSpec(memory_space=pltpu.SEMAPHORE)
_ANY = pl.BlockSpec(memory_space=pl.ANY)
_DATAFLOW = pltpu.SideEffectType.DATAFLOW_SIDE_EFFECTING


def _exchange_copy(t, d, pos, idx, me, src_ref, scatter, land_ref, send_sems, recv_sems):
    k = t * (N_DEV - 1) + d
    return pltpu.make_async_remote_copy(
        src_ref=src_ref.at[idx] if scatter else src_ref, dst_ref=land_ref.at[me],
        send_sem=send_sems.at[k], recv_sem=recv_sems.at[k],
        device_id=pos, device_id_type=pl.DeviceIdType.MESH)


def _own_slot_copy(t, me, src_ref, scatter, land_ref, local_sems):
    return pltpu.make_async_copy(src_ref.at[me] if scatter else src_ref, land_ref.at[me],
                                 local_sems.at[t])


def exchange_start(gathers, scatters, dep, *, name):
    srcs = [pltpu.with_memory_space_constraint(a, pltpu.HBM) for a in list(gathers) + list(scatters)]
    kinds = [False] * len(gathers) + [True] * len(scatters)
    lands = [pltpu.with_memory_space_constraint(
        lax.empty((N_DEV,) + (a.shape[1:] if sc else a.shape), a.dtype), pltpu.HBM)
        for a, sc in zip(srcs, kinds)]
    n = len(srcs)

    def body(*refs):
        src_refs, land_refs = refs[:n], refs[n:2 * n]
        send_sems, recv_sems, local_sems = refs[-2 * n - 4:-2 * n - 1]
        token = refs[-1]
        me, peers = _peers()
        for t in range(n):
            _own_slot_copy(t, me, src_refs[t], kinds[t], land_refs[t], local_sems).start()
        for t in range(n):
            for d, (pos, idx) in enumerate(peers):
                _exchange_copy(t, d, pos, idx, me, src_refs[t], kinds[t], land_refs[t],
                               send_sems, recv_sems).start()
        token[...] = jnp.zeros_like(token)

    sem = pltpu.SemaphoreType.DMA((n * (N_DEV - 1),))
    thru = [pltpu.HBM(a.shape, a.dtype) for a in srcs + lands]
    outs = pl.pallas_call(
        body, name=name,
        in_specs=[_HBM] * (2 * n) + ([_ANY] if dep is not None else []),
        out_specs=[_SEM] * 3 + [_HBM] * (2 * n) + [pl.BlockSpec(memory_space=pltpu.VMEM)],
        out_shape=[sem, sem, pltpu.SemaphoreType.DMA((n,))] + thru
        + [jax.ShapeDtypeStruct((8, LANES), F32)],
        input_output_aliases={i: 3 + i for i in range(2 * n)},
        compiler_params=pltpu.CompilerParams(has_side_effects=_DATAFLOW),
    )(*srcs, *lands, *([dep] if dep is not None else []))
    return dict(sems=outs[:3], srcs=outs[3:3 + n], lands=outs[3 + n:3 + 2 * n], token=outs[-1],
                kinds=kinds)


def exchange_wait(handle, after, *, name):
    kinds = handle["kinds"]
    n = len(kinds)

    def body(*refs):
        src_refs, land_refs = refs[:n], refs[n:2 * n]
        send_sems, recv_sems, local_sems = refs[2 * n:2 * n + 3]
        me, peers = _peers()
        for t in range(n):
            _own_slot_copy(t, me, src_refs[t], kinds[t], land_refs[t], local_sems).wait()
            for d, (pos, idx) in enumerate(peers):
                cp = _exchange_copy(t, d, pos, idx, me, src_refs[t], kinds[t], land_refs[t],
                                    send_sems, recv_sems)
                cp.wait_send()
                cp.wait_recv()

    arrays = list(handle["srcs"]) + list(handle["lands"])
    outs = pl.pallas_call(
        body, name=name,
        in_specs=[_HBM] * (2 * n) + [_SEM] * 3 + ([_ANY] if after is not None else []),
        out_specs=[_HBM] * (2 * n),
        out_shape=[pltpu.HBM(a.shape, a.dtype) for a in arrays],
        input_output_aliases={i: i for i in range(2 * n)},
        compiler_params=pltpu.CompilerParams(has_side_effects=_DATAFLOW),
    )(*arrays, *handle["sems"], *([after] if after is not None else []))
    return outs[n:]


def _row_block(rows, cols):
    best = None
    for rb in range(16, rows + 1, 16):
        if rows % rb == 0 and rb * cols <= 256 * 1024:
            best = rb
    return best or rows


def reduce_adamw(parts, w, m, v, *, name):
    nl = len(parts)
    C = w.shape[1]
    R, cp = parts[0].shape[1:]
    rb = _row_block(R, cp)
    nb = R // rb

    def body(*refs):
        p_refs = refs[:nl]
        w_ref, m_ref, v_ref, g_ref, d_ref, m2_ref, v2_ref = refs[nl:]

        def total(p_ref):
            g = p_ref[0, :, 0:C].astype(F32)
            for k in range(1, N_DEV):
                g = g + p_ref[k, :, 0:C].astype(F32)
            return g

        g = total(p_refs[0])
        for layer in range(1, nl):
            g = jnp.where(pl.program_id(0) == layer, total(p_refs[layer]), g)
        g_ref[...] = g
        m2 = ADAM_B1 * m_ref[...] + (1.0 - ADAM_B1) * g
        v2 = ADAM_B2 * v_ref[...] + (1.0 - ADAM_B2) * (g * g)
        m2_ref[...] = m2
        v2_ref[...] = v2
        m_hat = m2 / (1.0 - ADAM_B1 ** ADAM_STEP)
        v_hat = v2 / (1.0 - ADAM_B2 ** ADAM_STEP)
        d_ref[...] = -ADAM_LR * (m_hat / (jnp.sqrt(v_hat) + ADAM_EPS) + ADAM_WD * w_ref[...])

    def part_spec(layer):
        return pl.BlockSpec((N_DEV, rb, cp), lambda l, i: (0, jnp.where(l == layer, i, 0), 0))

    blk = pl.BlockSpec((rb, C), lambda l, i: (l * nb + i, 0))
    return pl.pallas_call(
        body, name=name, grid=(nl, nb),
        in_specs=[part_spec(layer) for layer in range(nl)] + [blk, blk, blk],
        out_specs=[blk] * 4,
        out_shape=[jax.ShapeDtypeStruct((nl * R, C), F32)] * 4,
        compiler_params=_params("arbitrary", "arbitrary"),
    )(*parts, w, m, v)


def _cols_to_pieces(full):
    r = full.shape[0]
    return jnp.transpose(full.reshape(r, N_DEV, -1), (1, 0, 2))


def _pieces_to_cols(pieces):
    r = pieces.shape[1]
    return jnp.transpose(pieces, (1, 0, 2)).reshape(r, -1)


def _rope_tables(positions):
    inv_freq = 1.0 / (ROPE_THETA ** (jnp.arange(0, QK_ROPE, 2, dtype=F32) / QK_ROPE))
    ang = positions.astype(F32)[:, None] * inv_freq
    cos, sin = jnp.cos(ang), jnp.sin(ang)
    zero = jnp.zeros_like(cos)
    return (jnp.concatenate([cos, cos, zero, zero], axis=1),
            jnp.concatenate([-sin, zero, zero, zero], axis=1),
            jnp.concatenate([zero, sin, zero, zero], axis=1))


def kernel(x, p, positions, norm_mix, w_in, w_pool, pool_scale, q_norm, kv_norm, w_uq, w_ukv, w_a, w_b, w_o, norm_ffn, w_gate, w_up, w_down, norm_ple, w_ple_gate, w_ple, final_norm, loss_target, m_norm_mix, m_w_in, m_w_pool, m_pool_scale, m_q_norm, m_kv_norm, m_w_uq, m_w_ukv, m_w_a, m_w_b, m_w_o, m_norm_ffn, m_w_gate, m_w_up, m_w_down, m_norm_ple, m_w_ple_gate, m_w_ple, m_final_norm, v_norm_mix, v_w_in, v_w_pool, v_pool_scale, v_q_norm, v_kv_norm, v_w_uq, v_w_ukv, v_w_a, v_w_b, v_w_o, v_norm_ffn, v_w_gate, v_w_up, v_w_down, v_norm_ple, v_w_ple_gate, v_w_ple, v_final_norm):
    given = dict(locals())
    shard = {n: given[n] for n in WEIGHTS}
    TM = 512
    TQ = 512

    def send_form(n, i):
        w = shard[n][i].astype(BF16)
        if n == "w_in":
            return jnp.pad(w, ((0, 0), (0, PIECE_PAD - W_IN_PIECE)))
        if n == "w_uq":
            w = jnp.pad(w, ((0, 0), (0, 0), (0, HEAD_PAD - QK_HEAD)))
        if n in ("w_uq", "w_ukv"):
            return w.reshape(-1, N_HEADS * HEAD_PAD)
        if n in ("w_gate", "w_up"):
            return jnp.pad(w, ((0, 0), (0, FF_PAD - FF_PIECE)))
        if n == "w_down":
            return jnp.pad(w, ((0, FF_PAD - FF_PIECE), (0, 0)))
        return w

    names = [n for n, _ in SHARDED]
    FIRST = ("w_in", "w_uq", "w_ukv", "w_a", "w_b", "w_o")
    LATER = ("w_gate", "w_up", "w_down", "w_ple_gate", "w_ple")
    start_a0 = exchange_start([send_form("w_in", 0)], [], None, name="gather_start_a0")
    got_a0 = exchange_wait(start_a0, None, name="gather_wait_a0")
    start_a1 = exchange_start([send_form(n, 0) for n in FIRST[1:]], [], got_a0[0],
                              name="gather_start_a1")
    gathered = {("w_in", 0): got_a0[0]}
    row_map = jnp.asarray(_piece_row_map())

    low_np, up_np = _band_matrices()
    low, up = jnp.asarray(low_np, BF16), jnp.asarray(up_np, BF16)
    low_t = jnp.asarray(low_np.transpose(0, 2, 1), BF16)
    up_t = jnp.asarray(up_np.transpose(0, 2, 1), BF16)
    cos_t, s1_t, s2_t = _rope_tables(positions[0])

    def mix_weights(i):
        g = lambda n: gathered[n, i]
        return dict(
            w_uq=g("w_uq").reshape(Q_LORA, N_HEADS * HEAD_PAD),
            w_ukv=g("w_ukv").reshape(KV_LORA, N_HEADS * HEAD_PAD),
            w_a=_pieces_to_cols(g("w_a")), w_b=g("w_b").reshape(D_MODEL, D_MODEL),
            w_o=g("w_o").reshape(D_MODEL, D_MODEL),
            w_pool=w_pool[i].astype(BF16),
            pool_scale=pool_scale[i][None], norm_mix=norm_mix[i][None], q_norm=q_norm[i][None],
            kv_norm=kv_norm[i][None], norm_ffn=norm_ffn[i][None], norm_ple=norm_ple[i][None])

    def channel_weights(i):
        g = lambda n: gathered[n, i]
        return dict(w_gate=g("w_gate"), w_up=g("w_up"), w_down=g("w_down"),
                    w_pg=g("w_ple_gate").reshape(D_MODEL, D_MODEL),
                    w_ple=_pieces_to_cols(g("w_ple")))

    def behind(gain, start):
        return gain + start["token"][0, 0]

    layers, saved = [], []
    starts = {}
    for i in range(DEPTH):
        if i == 1:
            gathered.update({(n, 1): a for n, a in zip(
                FIRST, exchange_wait(starts["b2"], saved[0]["logits"], name="gather_wait_b2"))})
        w_in_i = assemble_w_in(gathered["w_in", i], row_map, tr=512)
        g_mix = norm_mix[i][None]
        z = rms_matmul(x[0] if i == 0 else xs, behind(g_mix, start_a1) if i == 0 else g_mix,
                       w_in_i, tm=TM, tn=IN_PAD, name="in_proj")
        if i == 0:
            got_a1 = exchange_wait(start_a1, z, name="gather_wait_a1")
            gathered.update({(n, 0): a for n, a in zip(FIRST[1:], got_a1)})
            start_b1 = exchange_start([send_form(n, 0) for n in LATER], [], got_a1[0],
                                      name="gather_start_b1")
        L = dict(mix_weights(i), w_in=w_in_i)
        g_pool = behind(L["pool_scale"], start_b1) if i == 0 else L["pool_scale"]
        pooled, ms, ya = pool_fwd(z, low, up, L["w_pool"], g_pool, L["w_a"], tm=TM)
        g_q = L["q_norm"]
        if i == 0:
            starts["b2"] = exchange_start([send_form(n, 1) for n in FIRST], [], ya,
                                          name="gather_start_b2")
            g_q = behind(g_q, starts["b2"])
        q, k, v = mla_prep(z, cos_t, s1_t, s2_t, g_q, L["kv_norm"], L["w_uq"], L["w_ukv"], tm=TM)
        o, lse = flash_fwd(q, k, v, tq=2 * TQ)
        yb, merged, x1 = merge_fwd(o, ya, z, x[0] if i == 0 else xs, L["w_b"], L["w_o"], tm=TM)
        g_ffn = L["norm_ffn"]
        if i == 0:
            got = exchange_wait(start_b1, o, name="gather_wait_b1")
            starts["b3"] = exchange_start([send_form(n, 1) for n in LATER], [], x1,
                                          name="gather_start_b3")
            g_ffn = behind(g_ffn, starts["b3"])
        else:
            got = exchange_wait(starts["b3"], o, name="gather_wait_b3")
        gathered.update({(n, i): a for n, a in zip(LATER, got)})
        L.update(channel_weights(i))
        gpre, upre, x2 = ffn_fwd(x1, g_ffn, L["w_gate"], L["w_up"], L["w_down"], tm=2 * TM)
        logits, pe, x3 = ple_fwd(x2, p[i, 0], L["norm_ple"], L["w_pg"], L["w_ple"], tm=TM)
        layers.append(L)
        saved.append(dict(x=x[0] if i == 0 else xs, z=z, pooled=pooled, ms=ms, ya=ya, q=q, k=k,
                          v=v, o=o, lse=lse, yb=yb, merged=merged, x1=x1, gpre=gpre, upre=upre,
                          x2=x2, logits=logits, pe=pe))
        xs = x3

    loss_part, dx, d_final = loss_head(xs, final_norm[None], loss_target[0], tm=TM)

    grads = {n: [None] * DEPTH for n in REPLICATED if n != "final_norm"}
    pieces = {n: [None] * DEPTH for n in names}
    tn_mm = functools.partial(matmul_tn, tn=1024, tk=1024)
    row_pieces = lambda a: a.reshape(N_DEV, -1, a.shape[-1])
    MIDDLE = ("w_o", "w_b", "w_uq", "w_ukv")
    LAST = ("w_a", "w_in")
    scatters = []
    for i in reversed(range(DEPTH)):
        L, S = layers[i], saved[i]
        g_ple = L["norm_ple"] if not scatters else behind(L["norm_ple"], scatters[-1][0])
        dl, dpe, hn, dx2, d_norm_ple = ple_bwd(dx, S["logits"], S["pe"], S["x2"], g_ple,
                                               L["w_pg"], tm=TM)
        grads["norm_ple"][i] = d_norm_ple[0]
        pieces["w_ple_gate"][i] = row_pieces(tn_mm(hn, dl, name="dw_ple_gate"))
        pieces["w_ple"][i] = _cols_to_pieces(tn_mm(p[i, 0], dpe, name="dw_ple"))

        act, dgp, dup, h2, dx1, d_norm_ffn = ffn_bwd(dx2, S["x1"], L["norm_ffn"], S["gpre"],
                                                     S["upre"], L["w_gate"], L["w_up"],
                                                     L["w_down"], tm=TM)
        grads["norm_ffn"][i] = d_norm_ffn[0]
        pieces["w_down"][i] = matmul_tn_pieces(act, dx2, tk=512, name="dw_down")
        pieces["w_gate"][i] = matmul_tn_pieces(h2, dgp, tk=512, name="dw_gate")
        pieces["w_up"][i] = matmul_tn_pieces(h2, dup, tk=512, name="dw_up")
        scatters.append((exchange_start([], [pieces[n][i] for n in LATER], None,
                                        name=f"scatter_start_{i}a"), LATER, i))

        dya, dyb, do, dga, dgb = merge_bwd(dx1, S["ya"], S["yb"], S["z"], L["w_o"], L["w_b"],
                                           scatters[-1][0]["token"], tm=TM)
        pieces["w_o"][i] = row_pieces(tn_mm(S["merged"], dx1, name="dw_o"))
        pieces["w_b"][i] = row_pieces(tn_mm(S["o"], dyb, name="dw_b"))

        TQB = min(2 * TQ, do.shape[0])
        per_tile = lambda a: a.reshape(N_HEADS, -1, 1, TQB)
        delta = attn_delta(do, S["o"], tq=TQ)
        dq, dk, dv = flash_bwd(S["q"], S["k"], S["v"], do, per_tile(S["lse"]), per_tile(delta),
                               tq=TQB, tk=2 * TQ)
        dcq, dckv, dkr, dqf, dkvf, qn, kvn, d_q_norm, d_kv_norm = mla_bwd(
            dq, dk, dv, S["z"], cos_t, s1_t, s2_t, L["q_norm"], L["kv_norm"], L["w_uq"],
            L["w_ukv"], tm=TM)
        grads["q_norm"][i] = d_q_norm[0]
        grads["kv_norm"][i] = d_kv_norm[0]
        d_w_uq = tn_mm(qn, dqf, name="dw_uq")
        pieces["w_uq"][i] = d_w_uq.reshape(N_DEV, -1, N_HEADS, HEAD_PAD)[..., :QK_HEAD]
        pieces["w_ukv"][i] = tn_mm(kvn, dkvf, name="dw_ukv").reshape(
            N_DEV, -1, N_HEADS, QK_NOPE + V_HEAD)
        scatters.append((exchange_start([], [pieces[n][i] for n in MIDDLE], None,
                                        name=f"scatter_start_{i}b"), MIDDLE, i))

        dpc, d_pool_scale, d_w_pool = pool_bwd_a(dya, S["pooled"], L["w_a"], L["w_pool"],
                                                 behind(L["pool_scale"], scatters[-1][0]), tm=TM)
        grads["pool_scale"][i] = d_pool_scale[0]
        grads["w_pool"][i] = d_w_pool
        pieces["w_a"][i] = _cols_to_pieces(tn_mm(S["ms"], dya, name="dw_a"))
        du = pool_bwd_b(dpc, low_t, up_t, tm=TM)

        segs = [(dga, OFF_GA), (dgb, OFF_GB), (du, OFF_U), (dcq, OFF_CQ), (dckv, OFF_CKV),
                (dkr, OFF_KR)]
        dx, h, d_norm_mix = in_bwd(dx1, S["x"], L["norm_mix"], segs, L["w_in"], tm=TM)
        grads["norm_mix"][i] = d_norm_mix[0]
        pieces["w_in"][i] = split_dw_in(dw_in_proj(h, segs, tk=512), row_map, tr=512)
        if i == 1:
            scatters.append((exchange_start([], [pieces[n][i] for n in LAST], None,
                                            name="scatter_start_1c"), LAST, i))

    small = {n: jnp.stack(g) for n, g in grads.items()}
    small["final_norm"] = d_final[0]
    rep = jnp.concatenate([small[n].reshape(-1) for n in REPLICATED] + [loss_part.reshape(-1)])
    n_small = -(-rep.shape[0] // (8 * FLAT_COLS)) * (8 * FLAT_COLS)

    def small_flat(parts):
        flat = jnp.concatenate([a.reshape(-1) for a in parts])
        return jnp.pad(flat, (0, n_small - flat.shape[0])).reshape(-1, FLAT_COLS)

    start_end = exchange_start([small_flat([rep])], [pieces[n][0] for n in LAST], None,
                               name="scatter_start_0c")
    parts = {}
    for start, group, i in scatters:
        tag = {LATER: "a", MIDDLE: "b", LAST: "c"}[group]
        got = exchange_wait(start, start_end["token"], name=f"scatter_wait_{i}{tag}")
        parts.update({(n, i): a for n, a in zip(group, got)})
    for i in range(DEPTH):
        parts["w_down", i] = parts["w_down", i][:, :FF_PIECE]

    results = {}
    got_end = None
    for n in [n for n in names if n not in LAST] + list(LAST):
        if n in LAST and got_end is None:
            got_end = exchange_wait(start_end, out[0], name="scatter_wait_0c")
            parts.update({(n_, 0): a for n_, a in zip(LAST, got_end[1:])})
        w = shard[n]
        rows2d = lambda a: a.reshape(-1, w.shape[-1])
        w2 = rows2d(w)
        per_layer = [parts[n, i].reshape(N_DEV, w2.shape[0] // DEPTH, -1) for i in range(DEPTH)]
        out = reduce_adamw(per_layer, w2, rows2d(given["m_" + n]), rows2d(given["v_" + n]),
                           name="adamw_" + n)
        results[n] = [a.reshape(w.shape) for a in out]

    out = reduce_adamw([got_end[0]], *[small_flat([given[pre + n] for n in REPLICATED])
                                      for pre in ("", "m_", "v_")], name="adamw_small")
    loss = None
    for n in REPLICATED:
        results[n] = []
    for a in out:
        a = a.reshape(-1)
        off = 0
        for n in REPLICATED:
            size = int(np.prod(shard[n].shape))
            results[n].append(a[off:off + size].reshape(shard[n].shape))
            off += size
        if loss is None:
            loss = a[off]
    return (loss, dx[None], *[results[n][0] for n in WEIGHTS], *[results[n][1] for n in WEIGHTS],
            *[results[n][2] for n in WEIGHTS], *[results[n][3] for n in WEIGHTS])
```
